```python
import jax, jax.numpy as jnp
from jax import lax
import numpy as np

D_MODEL = 1024
BATCH = 4
SEQ = 4096
DEPTH = 2
DEC_BATCH = 128
DEC_SEQ = 8
PAST_LEN = 8192
PAGE_SIZE = 128

N_EVEN = (DEPTH + 1) // 2
N_ODD = DEPTH // 2
HEAD_DIM = 64
A_WIDTH = D_MODEL // 2
B_WIDTH = D_MODEL // 2
C_WIDTH = D_MODEL // 2
D_WIDTH = D_MODEL // 2
CONV_A = 3
CONV_B = 31
POOL_WINDOWS = (2, 4, 8, 16)
POOL_GROUPS = len(POOL_WINDOWS)
POOL_GROUP_W = C_WIDTH // POOL_GROUPS
POOL_PAST = max(POOL_WINDOWS) - 1
WINDOW = 128
D_HEADS = D_WIDTH // HEAD_DIM
D_KV_HEADS = 2
D_REP = D_HEADS // D_KV_HEADS
KV_WIDTH = D_KV_HEADS * HEAD_DIM
EVEN_SPLITS = (A_WIDTH, A_WIDTH, A_WIDTH, A_WIDTH, B_WIDTH, B_WIDTH, B_WIDTH)
ODD_SPLITS = (C_WIDTH, C_WIDTH, D_WIDTH, KV_WIDTH, KV_WIDTH, D_WIDTH)
EVEN_IN = sum(EVEN_SPLITS)
ODD_IN = sum(ODD_SPLITS)
MIX_EVEN = A_WIDTH + B_WIDTH
MIX_ODD = C_WIDTH + D_WIDTH
RMS_EPS = 1e-6
LN_EPS = 1e-5

kernel_name = "hybrid_conv_pool_swa_decoder_step"


def split_cols(z, sizes):
    idx = [int(i) for i in np.cumsum(sizes)[:-1]]
    return jnp.split(z, idx, axis=-1)


def rms_norm(x, g):
    x32 = x.astype(jnp.float32)
    y = x32 * lax.rsqrt(jnp.mean(x32 * x32, axis=-1, keepdims=True) + RMS_EPS)
    return (y * g.astype(jnp.float32)).astype(x.dtype)


def layer_norm(x, g, b):
    x32 = x.astype(jnp.float32)
    mu = jnp.mean(x32, axis=-1, keepdims=True)
    xc = x32 - mu
    var = jnp.mean(xc * xc, axis=-1, keepdims=True)
    y = xc * lax.rsqrt(var + LN_EPS) * g.astype(jnp.float32) + b.astype(jnp.float32)
    return y.astype(x.dtype)


def adaln(c, w_mod, b_mod):
    mod = jax.nn.silu(c) @ w_mod + b_mod
    return jnp.split(mod[:, None, :], 3, axis=-1)


def causal_dwconv(u, past, w):
    full = jnp.concatenate([past.astype(u.dtype), u], axis=1)
    y = lax.conv_general_dilated(full, w[:, None, :].astype(u.dtype), window_strides=(1,), padding="VALID",
                                 dimension_numbers=("NWC", "WIO", "NWC"), feature_group_count=u.shape[-1])
    return y, full[:, -(w.shape[0] - 1):]


def multiscale_pool(u, past, pool_w, pool_scale, pos0):
    n, t, ch = u.shape
    full = jnp.concatenate([past.astype(u.dtype), u], axis=1)
    csum = jnp.cumsum(full.astype(jnp.float32), axis=1)
    csum = jnp.concatenate([jnp.zeros((n, 1, ch), jnp.float32), csum], axis=1)
    end = csum[:, POOL_PAST + 1:]
    pos = pos0 + jnp.arange(t)
    u32 = u.astype(jnp.float32)
    outs = []
    for g, w in enumerate(POOL_WINDOWS):
        sl = slice(g * POOL_GROUP_W, (g + 1) * POOL_GROUP_W)
        start = csum[:, POOL_PAST + 1 - w:POOL_PAST + 1 - w + t, sl]
        cnt = jnp.minimum(w, pos + 1).astype(jnp.float32)[None, :, None]
        outs.append((end[..., sl] - start) / cnt - u32[..., sl])
    pooled = jnp.stack(outs, axis=2)
    mixed = jnp.einsum("ntgc,gcd->ntgd", pooled, pool_w.astype(jnp.float32)).reshape(n, t, ch)
    return (mixed * pool_scale.astype(jnp.float32)).astype(u.dtype), full[:, -POOL_PAST:]


def window_attention(q, k, v, past_k, past_v, sinks, pos0):
    n, t, _, hd = q.shape
    nb = -(-t // WINDOW)
    tp = nb * WINDOW
    kf = jnp.concatenate([past_k.astype(k.dtype), k], axis=1)
    vf = jnp.concatenate([past_v.astype(v.dtype), v], axis=1)
    new_k, new_v = kf[:, -WINDOW:], vf[:, -WINDOW:]
    padk = ((0, 0), (0, tp - t), (0, 0), (0, 0))
    kb = jnp.pad(kf, padk).reshape(n, nb + 1, WINDOW, D_KV_HEADS, hd)
    vb = jnp.pad(vf, padk).reshape(n, nb + 1, WINDOW, D_KV_HEADS, hd)
    kband = jnp.concatenate([kb[:, :-1], kb[:, 1:]], axis=2).astype(jnp.float32)
    vband = jnp.concatenate([vb[:, :-1], vb[:, 1:]], axis=2).astype(jnp.float32)
    qb = jnp.pad(q, padk).reshape(n, nb, WINDOW, D_KV_HEADS, D_REP, hd).astype(jnp.float32)
    s = jnp.einsum("nbqgrd,nbkgd->nbgrqk", qb, kband) * (hd ** -0.5)
    qi = jnp.arange(tp).reshape(nb, WINDOW)
    kj = jnp.arange(nb)[:, None] * WINDOW + jnp.arange(2 * WINDOW)[None, :]
    t_abs = pos0 + qi
    s_abs = pos0 - WINDOW + kj
    dist = t_abs[:, :, None] - s_abs[:, None, :]
    valid = (dist >= 0) & (dist < WINDOW) & (s_abs[:, None, :] >= 0) & (kj[:, None, :] < WINDOW + t)
    slopes = jnp.asarray(2.0 ** (-8.0 * np.arange(1, D_HEADS + 1) / D_HEADS), jnp.float32)
    slopes = slopes.reshape(1, 1, D_KV_HEADS, D_REP, 1, 1)
    s = s - slopes * dist.astype(jnp.float32)[None, :, None, None]
    s = jnp.where(valid[None, :, None, None], s, -jnp.inf)
    sink = sinks.astype(jnp.float32).reshape(1, 1, D_KV_HEADS, D_REP, 1, 1)
    m = jnp.maximum(jnp.max(s, axis=-1, keepdims=True), sink)
    p = jnp.exp(s - m)
    p = p / (jnp.sum(p, axis=-1, keepdims=True) + jnp.exp(sink - m))
    o = jnp.einsum("nbgrqk,nbkgd->nbqgrd", p, vband).reshape(n, tp, D_HEADS * hd)[:, :t]
    return o.astype(q.dtype), new_k, new_v


def even_layer(x, c, st_a, st_b, w_mod, b_mod, g_pre, g_post, w_in, conv_a_w, conv_b_w, conv_b_b,
               ln_g, ln_b, w_out):
    shift, scale, gate = adaln(c, w_mod, b_mod)
    h = rms_norm(x, g_pre) * (1 + scale) + shift
    a_x, a_b, a_c, a_gate, b_val, b_glu, b_gate = split_cols(h @ w_in, EVEN_SPLITS)
    ya, new_a = causal_dwconv(a_c * a_x, st_a, conv_a_w)
    ya = a_b * ya * jax.nn.silu(a_gate)
    ub = b_val * jax.nn.sigmoid(b_glu)
    yb, new_b = causal_dwconv(ub, st_b, conv_b_w)
    yb = jax.nn.silu(layer_norm(yb + conv_b_b, ln_g, ln_b)) * jax.nn.silu(b_gate)
    mix = jnp.concatenate([ya, yb], axis=-1) @ w_out
    return x + gate * rms_norm(mix, g_post), new_a, new_b


def odd_layer(x, c, st_c, st_k, st_v, pos0, w_mod, b_mod, g_pre, g_post, w_in, pool_w, pool_scale,
              sinks, w_out):
    n, t, _ = x.shape
    shift, scale, gate = adaln(c, w_mod, b_mod)
    h = rms_norm(x, g_pre) * (1 + scale) + shift
    c_u, c_gate, q, k, v, d_gate = split_cols(h @ w_in, ODD_SPLITS)
    yc, new_c = multiscale_pool(c_u, st_c, pool_w, pool_scale, pos0)
    yc = yc * jax.nn.silu(c_gate)
    yd, new_k, new_v = window_attention(q.reshape(n, t, D_HEADS, HEAD_DIM),
                                        k.reshape(n, t, D_KV_HEADS, HEAD_DIM),
                                        v.reshape(n, t, D_KV_HEADS, HEAD_DIM), st_k, st_v, sinks, pos0)
    yd = yd * jax.nn.silu(d_gate)
    mix = jnp.concatenate([yc, yd], axis=-1) @ w_out
    return x + gate * rms_norm(mix, g_post), new_c, new_k, new_v


def trunk(x, c, st_a, st_b, st_c, st_k, st_v, pos0, we, wo):
    na, nbs, nc, nk, nv = [], [], [], [], []
    for layer in range(DEPTH):
        i = layer // 2
        if layer % 2 == 0:
            x, sa, sb = even_layer(x, c, st_a[i], st_b[i], *[w[i] for w in we])
            na.append(sa)
            nbs.append(sb)
        else:
            x, sc, sk, sv = odd_layer(x, c, st_c[i], st_k[i], st_v[i], pos0, *[w[i] for w in wo])
            nc.append(sc)
            nk.append(sk)
            nv.append(sv)
    return x, jnp.stack(na), jnp.stack(nbs), jnp.stack(nc), jnp.stack(nk), jnp.stack(nv)


def setup_inputs(seed: int = 0) -> dict:
    key = jax.random.key(seed)
    ks = iter(jax.random.split(key, 40))

    def nrm(shape, s=1.0):
        return s * jax.random.normal(next(ks), shape, jnp.float32)

    d = D_MODEL
    return {
        "x_prompt": nrm((BATCH, SEQ, d)),
        "x_sample": nrm((DEC_BATCH, DEC_SEQ, d)),
        "state_conv_a": nrm((N_EVEN, DEC_BATCH, CONV_A - 1, A_WIDTH)),
        "state_conv_b": nrm((N_EVEN, DEC_BATCH, CONV_B - 1, B_WIDTH)),
        "state_pool_c": nrm((N_ODD, DEC_BATCH, POOL_PAST, C_WIDTH)),
        "cache_win_k": nrm((N_ODD, DEC_BATCH, WINDOW, D_KV_HEADS, HEAD_DIM)),
        "cache_win_v": nrm((N_ODD, DEC_BATCH, WINDOW, D_KV_HEADS, HEAD_DIM)),
        "c_prompt": nrm((BATCH, d)),
        "c_sample": nrm((DEC_BATCH, d)),
        "w_mod_e": nrm((N_EVEN, d, 3 * d), d ** -0.5),
        "b_mod_e": nrm((N_EVEN, 3 * d), 0.02),
        "g_pre_e": 1.0 + nrm((N_EVEN, d), 0.1),
        "g_post_e": 1.0 + nrm((N_EVEN, d), 0.1),
        "w_in_e": nrm((N_EVEN, d, EVEN_IN), d ** -0.5),
        "conv_a_w": nrm((N_EVEN, CONV_A, A_WIDTH), CONV_A ** -0.5),
        "conv_b_w": nrm((N_EVEN, CONV_B, B_WIDTH), CONV_B ** -0.5),
        "conv_b_b": nrm((N_EVEN, B_WIDTH), 0.02),
        "ln_b_g": 1.0 + nrm((N_EVEN, B_WIDTH), 0.1),
        "ln_b_b": nrm((N_EVEN, B_WIDTH), 0.02),
        "w_out_e": nrm((N_EVEN, MIX_EVEN, d), MIX_EVEN ** -0.5),
        "w_mod_o": nrm((N_ODD, d, 3 * d), d ** -0.5),
        "b_mod_o": nrm((N_ODD, 3 * d), 0.02),
        "g_pre_o": 1.0 + nrm((N_ODD, d), 0.1),
        "g_post_o": 1.0 + nrm((N_ODD, d), 0.1),
        "w_in_o": nrm((N_ODD, d, ODD_IN), d ** -0.5),
        "pool_w": nrm((N_ODD, POOL_GROUPS, POOL_GROUP_W, POOL_GROUP_W), POOL_GROUP_W ** -0.5),
        "pool_scale": 1.0 + nrm((N_ODD, C_WIDTH), 0.1),
        "sinks": nrm((N_ODD, D_HEADS)),
        "w_out_o": nrm((N_ODD, MIX_ODD, d), MIX_ODD ** -0.5),
    }


def reference(x_prompt, x_sample, state_conv_a, state_conv_b, state_pool_c, cache_win_k, cache_win_v,
              c_prompt, c_sample, w_mod_e, b_mod_e, g_pre_e, g_post_e, w_in_e, conv_a_w, conv_b_w, conv_b_b,
              ln_b_g, ln_b_b, w_out_e, w_mod_o, b_mod_o, g_pre_o, g_post_o, w_in_o, pool_w, pool_scale,
              sinks, w_out_o):
    we = (w_mod_e, b_mod_e, g_pre_e, g_post_e, w_in_e, conv_a_w, conv_b_w, conv_b_b, ln_b_g, ln_b_b, w_out_e)
    wo = (w_mod_o, b_mod_o, g_pre_o, g_post_o, w_in_o, pool_w, pool_scale, sinks, w_out_o)
    dt = x_prompt.dtype
    z_a = jnp.zeros((N_EVEN, BATCH, CONV_A - 1, A_WIDTH), dt)
    z_b = jnp.zeros((N_EVEN, BATCH, CONV_B - 1, B_WIDTH), dt)
    z_c = jnp.zeros((N_ODD, BATCH, POOL_PAST, C_WIDTH), dt)
    z_kv = jnp.zeros((N_ODD, BATCH, WINDOW, D_KV_HEADS, HEAD_DIM), dt)
    y_prompt, pa, pb, pc, pk, pv = trunk(x_prompt, c_prompt, z_a, z_b, z_c, z_kv, z_kv, 0, we, wo)
    y_sample, sa, sb, sc, sk, sv = trunk(x_sample, c_sample, state_conv_a, state_conv_b, state_pool_c,
                                         cache_win_k, cache_win_v, PAST_LEN, we, wo)
    return (y_prompt, y_sample, pa, sa, pb, sb, pc, sc, pk, sk, pv, sv)
```

```python
import functools

import jax
import jax.numpy as jnp
import numpy as np
from jax import lax
from jax.experimental import pallas as pl
from jax.experimental.pallas import tpu as pltpu

F32 = jnp.float32
BF16 = jnp.bfloat16

D_MODEL = 1024
MIX_W = 512
LANES = 128
N_LG = MIX_W // LANES
HEAD_DIM = 64
N_HEADS = 8
N_KV = 2
N_REP = N_HEADS // N_KV
KV_W = N_KV * HEAD_DIM
CONV_A = 3
CONV_B = 31
POOL_WINDOWS = (2, 4, 8, 16)
POOL_PAST = 15
WINDOW = 128
PAST_LEN = 8192
EVEN_IN = 7 * MIX_W
ODD_IN = 4 * MIX_W + 2 * KV_W
RMS_EPS = 1e-6
LN_EPS = 1e-5
HIST_A = 8
HIST_B = 32
HIST_C = 16
VMEM_LIMIT_BYTES = 56 * 1024 * 1024
ALIBI_SLOPES = tuple(float(2.0 ** (-8.0 * (h + 1) / N_HEADS)) for h in range(N_HEADS))


def _sigmoid(x):
    return 1.0 / (1.0 + jnp.exp(-x))


def _silu(x):
    return x * _sigmoid(x)


def _mod_kernel(c_ref, w_ref, b_ref, o_ref):
    s = _silu(c_ref[...]).astype(BF16)
    o_ref[...] = jnp.dot(s, w_ref[...].astype(BF16), preferred_element_type=F32) + b_ref[...]


def _adaln_mod(c_all, w_mod, b_mod):
    n = c_all.shape[0]
    tn = 512
    return pl.pallas_call(
        _mod_kernel,
        out_shape=jax.ShapeDtypeStruct((n, 3 * D_MODEL), F32),
        grid=(3 * D_MODEL // tn,),
        in_specs=[
            pl.BlockSpec((n, D_MODEL), lambda i: (0, 0)),
            pl.BlockSpec((D_MODEL, tn), lambda i: (0, i)),
            pl.BlockSpec((1, tn), lambda i: (0, i)),
        ],
        out_specs=pl.BlockSpec((n, tn), lambda i: (0, i)),
        name="adaln_mod",
    )(c_all, w_mod, b_mod.reshape(1, -1))


def _mod_rows(mod_ref, lo, r0, rc, tt):
    if tt >= rc:
        return mod_ref[r0 // tt][:, lo:lo + D_MODEL]
    parts = [jnp.broadcast_to(mod_ref[b][:, lo:lo + D_MODEL], (tt, D_MODEL))
             for b in range(r0 // tt, (r0 + rc) // tt)]
    return jnp.concatenate(parts, axis=0)


def _pre_norm(x_ref, mod_ref, g_ref, h_ref, rows, tt):
    rc = min(rows, 64)
    g = g_ref[...]
    for r0 in range(0, rows, rc):
        x = x_ref[r0:r0 + rc, :]
        ms = jnp.mean(x * x, axis=-1, keepdims=True)
        h = x * lax.rsqrt(ms + RMS_EPS) * g
        h = h * (1.0 + _mod_rows(mod_ref, D_MODEL, r0, rc, tt)) + _mod_rows(mod_ref, 0, r0, rc, tt)
        h_ref[r0:r0 + rc, :] = h.astype(BF16)


def _post_norm_residual(x_ref, mod_ref, g_ref, o_ref, y_ref, rows, tt):
    rc = min(rows, 64)
    g = g_ref[...]
    for r0 in range(0, rows, rc):
        o = o_ref[r0:r0 + rc, 0:D_MODEL]
        ms = jnp.mean(o * o, axis=-1, keepdims=True)
        n = o * lax.rsqrt(ms + RMS_EPS) * g
        y_ref[r0:r0 + rc, :] = x_ref[r0:r0 + rc, :] + _mod_rows(mod_ref, 2 * D_MODEL, r0, rc, tt) * n


def _seq_chunks(nb, tt, rc_max):
    rc = min(tt, rc_max)
    return [(b, c0, b * tt + c0, rc) for b in range(nb) for c0 in range(0, tt, rc)]


def _even_kernel(x_ref, mod_ref, sa_ref, sb_ref, gpre_ref, win_ref, cwa_ref, cwb_ref, cbb_ref,
                 lng_ref, lnb_ref, wout_ref, gpost_ref,
                 y_ref, na_ref, nb_ref,
                 h_ref, z_ref, fa_ref, fb_ref, ca_ref, cb_ref, mix_ref, *, nb, tt, multi_tile):
    rows = nb * tt
    j = pl.program_id(1)

    @pl.when(j == 0)
    def _load_state():
        for b in range(nb):
            for g in range(N_LG):
                lg = slice(LANES * g, LANES * (g + 1))
                fa_ref[b * N_LG + g, 0:HIST_A, :] = jnp.zeros((HIST_A, LANES), F32)
                fa_ref[b * N_LG + g, HIST_A - (CONV_A - 1):HIST_A, :] = sa_ref[b, :, lg]
                fb_ref[b * N_LG + g, 0:8, :] = jnp.zeros((8, LANES), F32)
                fb_ref[b * N_LG + g, HIST_B - (CONV_B - 1):HIST_B, :] = sb_ref[b, :, lg]

    _pre_norm(x_ref, mod_ref, gpre_ref, h_ref, rows, tt)

    hb = h_ref[...]
    for s in range(EVEN_IN // MIX_W):
        cols = slice(MIX_W * s, MIX_W * (s + 1))
        z_ref[:, cols] = jnp.dot(hb, win_ref[:, cols], preferred_element_type=F32)

    for b, c0, r0, rc in _seq_chunks(nb, tt, 128):
        for g in range(N_LG):
            def zc(s):
                return z_ref[r0:r0 + rc, MIX_W * s + LANES * g:MIX_W * s + LANES * (g + 1)]
            fa_ref[b * N_LG + g, HIST_A + c0:HIST_A + c0 + rc, :] = zc(2) * zc(0)
            fb_ref[b * N_LG + g, HIST_B + c0:HIST_B + c0 + rc, :] = zc(4) * _sigmoid(zc(5))

    for b, c0, r0, rc in _seq_chunks(nb, tt, 128):
        for g in range(N_LG):
            lg = slice(LANES * g, LANES * (g + 1))
            acc = None
            for k in range(CONV_A):
                off = HIST_A - (CONV_A - 1) + k + c0
                t = cwa_ref[k:k + 1, lg] * fa_ref[b * N_LG + g, off:off + rc, :]
                acc = t if acc is None else acc + t
            ca_ref[r0:r0 + rc, lg] = acc
            acc = None
            for k in range(CONV_B):
                off = HIST_B - (CONV_B - 1) + k + c0
                t = cwb_ref[k:k + 1, lg] * fb_ref[b * N_LG + g, off:off + rc, :]
                acc = t if acc is None else acc + t
            cb_ref[r0:r0 + rc, lg] = acc

    for b in range(nb):
        for g in range(N_LG):
            lg = slice(LANES * g, LANES * (g + 1))
            na_ref[b, :, lg] = fa_ref[b * N_LG + g, HIST_A + tt - (CONV_A - 1):HIST_A + tt, :]
            nb_ref[b, :, lg] = fb_ref[b * N_LG + g, HIST_B + tt - (CONV_B - 1):HIST_B + tt, :]
            if multi_tile:
                fa_ref[b * N_LG + g, 0:HIST_A, :] = fa_ref[b * N_LG + g, tt:tt + HIST_A, :]
                fb_ref[b * N_LG + g, 0:HIST_B, :] = fb_ref[b * N_LG + g, tt:tt + HIST_B, :]

    rc = min(rows, 64)
    for r0 in range(0, rows, rc):
        def zs(s):
            return z_ref[r0:r0 + rc, MIX_W * s:MIX_W * (s + 1)]
        ya = zs(1) * ca_ref[r0:r0 + rc, :] * _silu(zs(3))
        mix_ref[r0:r0 + rc, 0:MIX_W] = ya.astype(BF16)
        yb = cb_ref[r0:r0 + rc, :] + cbb_ref[...]
        mu = jnp.mean(yb, axis=-1, keepdims=True)
        yc = yb - mu
        var = jnp.mean(yc * yc, axis=-1, keepdims=True)
        ln = yc * lax.rsqrt(var + LN_EPS) * lng_ref[...] + lnb_ref[...]
        mix_ref[r0:r0 + rc, MIX_W:2 * MIX_W] = (_silu(ln) * _silu(zs(6))).astype(BF16)

    z_ref[:, 0:D_MODEL] = jnp.dot(mix_ref[...], wout_ref[...], preferred_element_type=F32)
    _post_norm_residual(x_ref, mod_ref, gpost_ref, z_ref, y_ref, rows, tt)


def _const_spec(shape):
    nd = len(shape)
    return pl.BlockSpec(shape, lambda i, j: (0,) * nd)


def _even_layer(x2d, mod, st_a, st_b, g_pre, w_in, cw_a, cw_b, cb_b, ln_g, ln_b, w_out, g_post, *, n, t, nb, tt):
    rows = nb * tt
    nt = t // tt
    kern = functools.partial(_even_kernel, nb=nb, tt=tt, multi_tile=nt > 1)
    row_spec = pl.BlockSpec((rows, D_MODEL), lambda i, j: (i * nt + j, 0))

    def seq_spec(shape):
        return pl.BlockSpec((nb,) + shape, lambda i, j: (i, 0, 0))

    return pl.pallas_call(
        kern,
        out_shape=(
            jax.ShapeDtypeStruct((n * t, D_MODEL), F32),
            jax.ShapeDtypeStruct((n, CONV_A - 1, MIX_W), F32),
            jax.ShapeDtypeStruct((n, CONV_B - 1, MIX_W), F32),
        ),
        grid=(n // nb, nt),
        in_specs=[
            row_spec,
            seq_spec((1, 3 * D_MODEL)),
            seq_spec((CONV_A - 1, MIX_W)),
            seq_spec((CONV_B - 1, MIX_W)),
            _const_spec((1, D_MODEL)),
            _const_spec((D_MODEL, EVEN_IN)),
            _const_spec((CONV_A, MIX_W)),
            _const_spec((CONV_B, MIX_W)),
            _const_spec((1, MIX_W)),
            _const_spec((1, MIX_W)),
            _const_spec((1, MIX_W)),
            _const_spec((2 * MIX_W, D_MODEL)),
            _const_spec((1, D_MODEL)),
        ],
        out_specs=(row_spec, seq_spec((CONV_A - 1, MIX_W)), seq_spec((CONV_B - 1, MIX_W))),
        scratch_shapes=[
            pltpu.VMEM((rows, D_MODEL), BF16),
            pltpu.VMEM((rows, EVEN_IN), F32),
            pltpu.VMEM((nb * N_LG, HIST_A + tt, LANES), F32),
            pltpu.VMEM((nb * N_LG, HIST_B + tt, LANES), F32),
            pltpu.VMEM((rows, MIX_W), F32),
            pltpu.VMEM((rows, MIX_W), F32),
            pltpu.VMEM((rows, 2 * MIX_W), BF16),
        ],
        compiler_params=pltpu.CompilerParams(
            dimension_semantics=("arbitrary", "arbitrary"), vmem_limit_bytes=VMEM_LIMIT_BYTES),
        name="even_layer",
    )(x2d, mod, st_a, st_b, g_pre, w_in, cw_a, cw_b, cb_b, ln_g, ln_b, w_out, g_post)


def _odd_kernel(sinks_ref, x_ref, mod_ref, sc_ref, sk_ref, sv_ref, gpre_ref, win_ref, pw_ref, ps_ref,
                wout_ref, gpost_ref,
                y_ref, nc_ref, nk_ref, nv_ref,
                h_ref, z_ref, fc_ref, kf_ref, vf_ref, pc_ref, od_ref, mix_ref, *, nb, tt, pos0, multi_tile):
    rows = nb * tt
    j = pl.program_id(1)
    q_col, k_col, v_col, dg_col = 2 * MIX_W, 3 * MIX_W, 3 * MIX_W + KV_W, 3 * MIX_W + 2 * KV_W

    @pl.when(j == 0)
    def _load_state():
        for b in range(nb):
            for g in range(N_LG):
                fc_ref[b * N_LG + g, 0:8, :] = jnp.zeros((8, LANES), F32)
                fc_ref[b * N_LG + g, HIST_C - POOL_PAST:HIST_C, :] = sc_ref[b, :, LANES * g:LANES * (g + 1)]
            kf_ref[b, 0:WINDOW, :] = sk_ref[b]
            vf_ref[b, 0:WINDOW, :] = sv_ref[b]

    _pre_norm(x_ref, mod_ref, gpre_ref, h_ref, rows, tt)

    hb = h_ref[...]
    for lo, hi in ((0, MIX_W), (MIX_W, 2 * MIX_W), (q_col, k_col), (k_col, dg_col), (dg_col, ODD_IN)):
        z_ref[:, lo:hi] = jnp.dot(hb, win_ref[:, lo:hi], preferred_element_type=F32)

    for b, c0, r0, rc in _seq_chunks(nb, tt, 128):
        for g in range(N_LG):
            fc_ref[b * N_LG + g, HIST_C + c0:HIST_C + c0 + rc, :] = z_ref[r0:r0 + rc, LANES * g:LANES * (g + 1)]
        kf_ref[b, WINDOW + c0:WINDOW + c0 + rc, :] = z_ref[r0:r0 + rc, k_col:k_col + KV_W]
        vf_ref[b, WINDOW + c0:WINDOW + c0 + rc, :] = z_ref[r0:r0 + rc, v_col:v_col + KV_W]

    for b, c0, r0, rc in _seq_chunks(nb, tt, 128):
        pos = pos0 + j * tt + c0 + lax.broadcasted_iota(jnp.int32, (rc, LANES), 0)
        for g, w in enumerate(POOL_WINDOWS):
            u = fc_ref[b * N_LG + g, HIST_C + c0:HIST_C + c0 + rc, :]
            acc = u
            for d in range(1, w):
                acc = acc + fc_ref[b * N_LG + g, HIST_C + c0 - d:HIST_C + c0 - d + rc, :]
            cnt = jnp.minimum(w, pos + 1).astype(F32)
            pc_ref[r0:r0 + rc, LANES * g:LANES * (g + 1)] = acc / cnt - u

    qs = min(tt, WINDOW)
    kw = WINDOW + qs
    qi = lax.broadcasted_iota(jnp.int32, (qs, kw), 0)
    kj = lax.broadcasted_iota(jnp.int32, (qs, kw), 1)
    dist = qi + WINDOW - kj
    dist_f = dist.astype(F32)
    for b in range(nb):
        for sb in range(tt // qs):
            r0 = b * tt + sb * qs
            p0 = pos0 + j * tt + sb * qs
            visible = (dist >= 0) & (dist < WINDOW) & (kj >= WINDOW - p0)
            kwin = kf_ref[b, sb * qs:sb * qs + kw, :].astype(BF16)
            vwin = vf_ref[b, sb * qs:sb * qs + kw, :].astype(BF16)
            for g in range(N_KV):
                kg = kwin[:, HEAD_DIM * g:HEAD_DIM * (g + 1)]
                vg = vwin[:, HEAD_DIM * g:HEAD_DIM * (g + 1)]
                for r in range(N_REP):
                    h = g * N_REP + r
                    hc = slice(q_col + HEAD_DIM * h, q_col + HEAD_DIM * (h + 1))
                    qh = (z_ref[r0:r0 + qs, hc] * (HEAD_DIM ** -0.5)).astype(BF16)
                    s = lax.dot_general(qh, kg, (((1,), (1,)), ((), ())), preferred_element_type=F32)
                    s = jnp.where(visible, s - ALIBI_SLOPES[h] * dist_f, -jnp.inf)
                    sink = sinks_ref[h]
                    m = jnp.maximum(jnp.max(s, axis=-1, keepdims=True), sink)
                    p = jnp.exp(s - m)
                    den = jnp.sum(p, axis=-1, keepdims=True) + jnp.exp(sink - m)
                    o = jnp.dot(p.astype(BF16), vg, preferred_element_type=F32)
                    od_ref[r0:r0 + qs, HEAD_DIM * h:HEAD_DIM * (h + 1)] = o / den

    for b in range(nb):
        for g in range(N_LG):
            nc_ref[b, :, LANES * g:LANES * (g + 1)] = fc_ref[b * N_LG + g, HIST_C + tt - POOL_PAST:HIST_C + tt, :]
            if multi_tile:
                fc_ref[b * N_LG + g, 0:HIST_C, :] = fc_ref[b * N_LG + g, tt:tt + HIST_C, :]
        nk_ref[b] = kf_ref[b, tt:tt + WINDOW, :]
        nv_ref[b] = vf_ref[b, tt:tt + WINDOW, :]
        if multi_tile:
            kf_ref[b, 0:WINDOW, :] = kf_ref[b, tt:tt + WINDOW, :]
            vf_ref[b, 0:WINDOW, :] = vf_ref[b, tt:tt + WINDOW, :]

    rc = min(rows, 128)
    for r0 in range(0, rows, rc):
        for g in range(N_LG):
            lg = slice(LANES * g, LANES * (g + 1))
            mixed = jnp.dot(pc_ref[r0:r0 + rc, lg].astype(BF16), pw_ref[g], preferred_element_type=F32)
            yc = mixed * ps_ref[:, lg] * _silu(z_ref[r0:r0 + rc, MIX_W + LANES * g:MIX_W + LANES * (g + 1)])
            mix_ref[r0:r0 + rc, lg] = yc.astype(BF16)
        yd = od_ref[r0:r0 + rc, :] * _silu(z_ref[r0:r0 + rc, dg_col:dg_col + MIX_W])
        mix_ref[r0:r0 + rc, MIX_W:2 * MIX_W] = yd.astype(BF16)

    z_ref[:, 0:D_MODEL] = jnp.dot(mix_ref[...], wout_ref[...], preferred_element_type=F32)
    _post_norm_residual(x_ref, mod_ref, gpost_ref, z_ref, y_ref, rows, tt)


def _odd_layer(x2d, mod, st_c, st_k, st_v, g_pre, w_in, pool_w, pool_scale, sinks, w_out, g_post,
               *, n, t, nb, tt, pos0):
    rows = nb * tt
    nt = t // tt
    kern = functools.partial(_odd_kernel, nb=nb, tt=tt, pos0=pos0, multi_tile=nt > 1)
    row_spec = pl.BlockSpec((rows, D_MODEL), lambda i, j: (i * nt + j, 0))

    def seq_spec(shape):
        return pl.BlockSpec((nb,) + shape, lambda i, j: (i, 0, 0))

    return pl.pallas_call(
        kern,
        out_shape=(
            jax.ShapeDtypeStruct((n * t, D_MODEL), F32),
            jax.ShapeDtypeStruct((n, POOL_PAST, MIX_W), F32),
            jax.ShapeDtypeStruct((n, WINDOW, KV_W), F32),
            jax.ShapeDtypeStruct((n, WINDOW, KV_W), F32),
        ),
        grid=(n // nb, nt),
        in_specs=[
            pl.BlockSpec(memory_space=pltpu.SMEM),
            row_spec,
            seq_spec((1, 3 * D_MODEL)),
            seq_spec((POOL_PAST, MIX_W)),
            seq_spec((WINDOW, KV_W)),
            seq_spec((WINDOW, KV_W)),
            _const_spec((1, D_MODEL)),
            _const_spec((D_MODEL, ODD_IN)),
            _const_spec((N_LG, LANES, LANES)),
            _const_spec((1, MIX_W)),
            _const_spec((2 * MIX_W, D_MODEL)),
            _const_spec((1, D_MODEL)),
        ],
        out_specs=(row_spec, seq_spec((POOL_PAST, MIX_W)), seq_spec((WINDOW, KV_W)), seq_spec((WINDOW, KV_W))),
        scratch_shapes=[
            pltpu.VMEM((rows, D_MODEL), BF16),
            pltpu.VMEM((rows, ODD_IN), F32),
            pltpu.VMEM((nb * N_LG, HIST_C + tt, LANES), F32),
            pltpu.VMEM((nb, WINDOW + tt, KV_W), F32),
            pltpu.VMEM((nb, WINDOW + tt, KV_W), F32),
            pltpu.VMEM((rows, MIX_W), F32),
            pltpu.VMEM((rows, MIX_W), F32),
            pltpu.VMEM((rows, 2 * MIX_W), BF16),
        ],
        compiler_params=pltpu.CompilerParams(
            dimension_semantics=("arbitrary", "arbitrary"), vmem_limit_bytes=VMEM_LIMIT_BYTES),
        name="odd_layer",
    )(sinks, x2d, mod, st_c, st_k, st_v, g_pre, w_in, pool_w, pool_scale, w_out, g_post)


def _trunk(x, mod_e, mod_o, st_a, st_b, st_c, st_k, st_v, pos0, we, wo, *, nb, tt):
    n, t, _ = x.shape
    x2d = x.reshape(n * t, D_MODEL)
    x2d, na, nbs = _even_layer(x2d, mod_e.reshape(n, 1, -1), st_a, st_b, *we, n=n, t=t, nb=nb, tt=tt)
    x2d, nc, nk, nv = _odd_layer(x2d, mod_o.reshape(n, 1, -1), st_c, st_k.reshape(n, WINDOW, KV_W),
                                 st_v.reshape(n, WINDOW, KV_W), *wo, n=n, t=t, nb=nb, tt=tt, pos0=pos0)
    kv_shape = (1, n, WINDOW, N_KV, HEAD_DIM)
    return (x2d.reshape(n, t, D_MODEL), na[None], nbs[None], nc[None],
            nk.reshape(kv_shape), nv.reshape(kv_shape))


def kernel(x_prompt, x_sample, state_conv_a, state_conv_b, state_pool_c, cache_win_k, cache_win_v, c_prompt, c_sample, w_mod_e, b_mod_e, g_pre_e, g_post_e, w_in_e, conv_a_w, conv_b_w, conv_b_b, ln_b_g, ln_b_b, w_out_e, w_mod_o, b_mod_o, g_pre_o, g_post_o, w_in_o, pool_w, pool_scale, sinks, w_out_o):
    n_p, t_p, _ = x_prompt.shape
    n_s, t_s, _ = x_sample.shape
    assert w_in_e.shape[0] == 1 and w_in_o.shape[0] == 1, "one even and one odd layer"

    c_all = jnp.concatenate([c_prompt, c_sample], axis=0)
    mod_e = _adaln_mod(c_all, w_mod_e[0], b_mod_e[0])
    mod_o = _adaln_mod(c_all, w_mod_o[0], b_mod_o[0])

    we = (g_pre_e, w_in_e[0].astype(BF16), conv_a_w[0], conv_b_w[0], conv_b_b, ln_b_g, ln_b_b,
          w_out_e[0].astype(BF16), g_post_e)
    wo = (g_pre_o, w_in_o[0].astype(BF16), pool_w[0].astype(BF16), pool_scale, sinks[0],
          w_out_o[0].astype(BF16), g_post_o)

    dt = x_prompt.dtype
    z_a = jnp.zeros((n_p, CONV_A - 1, MIX_W), dt)
    z_b = jnp.zeros((n_p, CONV_B - 1, MIX_W), dt)
    z_c = jnp.zeros((n_p, POOL_PAST, MIX_W), dt)
    z_kv = jnp.zeros((n_p, WINDOW, N_KV, HEAD_DIM), dt)
    y_p, pa, pb, pc, pk, pv = _trunk(x_prompt, mod_e[:n_p], mod_o[:n_p], z_a, z_b, z_c, z_kv, z_kv, 0,
                                     we, wo, nb=1, tt=512)
    y_s, sa, sb, sc, sk, sv = _trunk(x_sample, mod_e[n_p:], mod_o[n_p:], state_conv_a[0], state_conv_b[0],
                                     state_pool_c[0], cache_win_k[0], cache_win_v[0], PAST_LEN,
                                     we, wo, nb=32, tt=t_s)
    return (y_p, y_s, pa, sa, pb, sb, pc, sc, pk, sk, pv, sv)
```

```python
import functools

import jax
import jax.numpy as jnp
import numpy as np
from jax import lax
from jax.experimental import pallas as pl
from jax.experimental.pallas import tpu as pltpu

F32 = jnp.float32
BF16 = jnp.bfloat16

D_MODEL = 1024
MIX_W = 512
LANES = 128
N_LG = MIX_W // LANES
HEAD_DIM = 64
N_HEADS = 8
N_KV = 2
N_REP = N_HEADS // N_KV
KV_W = N_KV * HEAD_DIM
CONV_A = 3
CONV_B = 31
POOL_WINDOWS = (2, 4, 8, 16)
POOL_PAST = 15
WINDOW = 128
PAST_LEN = 8192
EVEN_IN = 7 * MIX_W
ODD_IN = 4 * MIX_W + 2 * KV_W
RMS_EPS = 1e-6
LN_EPS = 1e-5
HIST_A = 8
HIST_B = 32
HIST_C = 16
ATT_MIN_Q_ROWS = 32
VMEM_LIMIT_BYTES = 56 * 1024 * 1024
ALIBI_SLOPES = tuple(float(2.0 ** (-8.0 * (h + 1) / N_HEADS)) for h in range(N_HEADS))


def _sigmoid(x):
    return 1.0 / (1.0 + jnp.exp(-x))


def _silu(x):
    return x * _sigmoid(x)


def _mod_kernel(c_ref, w_ref, b_ref, o_ref):
    s = _silu(c_ref[...]).astype(BF16)
    o_ref[...] = jnp.dot(s, w_ref[...].astype(BF16), preferred_element_type=F32) + b_ref[...]


def _adaln_mod(c_all, w_mod, b_mod):
    n = c_all.shape[0]
    tn = 512
    return pl.pallas_call(
        _mod_kernel,
        out_shape=jax.ShapeDtypeStruct((n, 3 * D_MODEL), F32),
        grid=(3 * D_MODEL // tn,),
        in_specs=[
            pl.BlockSpec((n, D_MODEL), lambda i: (0, 0)),
            pl.BlockSpec((D_MODEL, tn), lambda i: (0, i)),
            pl.BlockSpec((1, tn), lambda i: (0, i)),
        ],
        out_specs=pl.BlockSpec((n, tn), lambda i: (0, i)),
        name="adaln_mod",
    )(c_all, w_mod, b_mod.reshape(1, -1))


def _mod_rows(mod_ref, lo, r0, rc, tt):
    if tt >= rc:
        return mod_ref[r0 // tt][:, lo:lo + D_MODEL]
    parts = [jnp.broadcast_to(mod_ref[b][:, lo:lo + D_MODEL], (tt, D_MODEL))
             for b in range(r0 // tt, (r0 + rc) // tt)]
    return jnp.concatenate(parts, axis=0)


def _pre_norm(x_ref, mod_ref, g_ref, h_ref, rows, tt):
    rc = min(rows, 64)
    g = g_ref[...]
    for r0 in range(0, rows, rc):
        x = x_ref[r0:r0 + rc, :]
        ms = jnp.mean(x * x, axis=-1, keepdims=True)
        h = x * lax.rsqrt(ms + RMS_EPS) * g
        h = h * (1.0 + _mod_rows(mod_ref, D_MODEL, r0, rc, tt)) + _mod_rows(mod_ref, 0, r0, rc, tt)
        h_ref[r0:r0 + rc, :] = h.astype(BF16)


def _post_norm_residual(x_ref, mod_ref, g_ref, o_ref, y_ref, rows, tt):
    rc = min(rows, 64)
    g = g_ref[...]
    for r0 in range(0, rows, rc):
        o = o_ref[r0:r0 + rc, 0:D_MODEL]
        ms = jnp.mean(o * o, axis=-1, keepdims=True)
        n = o * lax.rsqrt(ms + RMS_EPS) * g
        y_ref[r0:r0 + rc, :] = x_ref[r0:r0 + rc, :] + _mod_rows(mod_ref, 2 * D_MODEL, r0, rc, tt) * n


def _seq_chunks(nb, tt, rc_max):
    rc = min(tt, rc_max)
    return [(b, c0, b * tt + c0, rc) for b in range(nb) for c0 in range(0, tt, rc)]


def _even_kernel(x_ref, mod_ref, sa_ref, sb_ref, gpre_ref, win_ref, cwa_ref, cwb_ref, cbb_ref,
                 lng_ref, lnb_ref, wout_ref, gpost_ref,
                 y_ref, na_ref, nb_ref,
                 h_ref, z_ref, fa_ref, fb_ref, ca_ref, cb_ref, mix_ref, *, nb, tt, multi_tile):
    rows = nb * tt
    j = pl.program_id(1)

    @pl.when(j == 0)
    def _load_state():
        for b in range(nb):
            for g in range(N_LG):
                lg = slice(LANES * g, LANES * (g + 1))
                fa_ref[b * N_LG + g, 0:HIST_A, :] = jnp.zeros((HIST_A, LANES), F32)
                fa_ref[b * N_LG + g, HIST_A - (CONV_A - 1):HIST_A, :] = sa_ref[b, :, lg]
                fb_ref[b * N_LG + g, 0:8, :] = jnp.zeros((8, LANES), F32)
                fb_ref[b * N_LG + g, HIST_B - (CONV_B - 1):HIST_B, :] = sb_ref[b, :, lg]

    _pre_norm(x_ref, mod_ref, gpre_ref, h_ref, rows, tt)

    hb = h_ref[...]
    for s in range(EVEN_IN // MIX_W):
        cols = slice(MIX_W * s, MIX_W * (s + 1))
        z_ref[:, cols] = jnp.dot(hb, win_ref[:, cols], preferred_element_type=F32)

    for b, c0, r0, rc in _seq_chunks(nb, tt, 128):
        for g in range(N_LG):
            def zc(s):
                return z_ref[r0:r0 + rc, MIX_W * s + LANES * g:MIX_W * s + LANES * (g + 1)]
            fa_ref[b * N_LG + g, HIST_A + c0:HIST_A + c0 + rc, :] = zc(2) * zc(0)
            fb_ref[b * N_LG + g, HIST_B + c0:HIST_B + c0 + rc, :] = zc(4) * _sigmoid(zc(5))

    for b, c0, r0, rc in _seq_chunks(nb, tt, 128):
        for g in range(N_LG):
            lg = slice(LANES * g, LANES * (g + 1))
            acc = None
            for k in range(CONV_A):
                off = HIST_A - (CONV_A - 1) + k + c0
                t = cwa_ref[k:k + 1, lg] * fa_ref[b * N_LG + g, off:off + rc, :]
                acc = t if acc is None else acc + t
            ca_ref[r0:r0 + rc, lg] = acc
            acc = None
            for k in range(CONV_B):
                off = HIST_B - (CONV_B - 1) + k + c0
                t = cwb_ref[k:k + 1, lg] * fb_ref[b * N_LG + g, off:off + rc, :]
                acc = t if acc is None else acc + t
            cb_ref[r0:r0 + rc, lg] = acc

    for b in range(nb):
        for g in range(N_LG):
            lg = slice(LANES * g, LANES * (g + 1))
            na_ref[b, :, lg] = fa_ref[b * N_LG + g, HIST_A + tt - (CONV_A - 1):HIST_A + tt, :]
            nb_ref[b, :, lg] = fb_ref[b * N_LG + g, HIST_B + tt - (CONV_B - 1):HIST_B + tt, :]
            if multi_tile:
                fa_ref[b * N_LG + g, 0:HIST_A, :] = fa_ref[b * N_LG + g, tt:tt + HIST_A, :]
                fb_ref[b * N_LG + g, 0:HIST_B, :] = fb_ref[b * N_LG + g, tt:tt + HIST_B, :]

    rc = min(rows, 64)
    for r0 in range(0, rows, rc):
        def zs(s):
            return z_ref[r0:r0 + rc, MIX_W * s:MIX_W * (s + 1)]
        ya = zs(1) * ca_ref[r0:r0 + rc, :] * _silu(zs(3))
        mix_ref[r0:r0 + rc, 0:MIX_W] = ya.astype(BF16)
        yb = cb_ref[r0:r0 + rc, :] + cbb_ref[...]
        mu = jnp.mean(yb, axis=-1, keepdims=True)
        yc = yb - mu
        var = jnp.mean(yc * yc, axis=-1, keepdims=True)
        ln = yc * lax.rsqrt(var + LN_EPS) * lng_ref[...] + lnb_ref[...]
        mix_ref[r0:r0 + rc, MIX_W:2 * MIX_W] = (_silu(ln) * _silu(zs(6))).astype(BF16)

    z_ref[:, 0:D_MODEL] = jnp.dot(mix_ref[...], wout_ref[...], preferred_element_type=F32)
    _post_norm_residual(x_ref, mod_ref, gpost_ref, z_ref, y_ref, rows, tt)


def _const_spec(shape):
    nd = len(shape)
    return pl.BlockSpec(shape, lambda i, j: (0,) * nd)


def _even_layer(x2d, mod, st_a, st_b, g_pre, w_in, cw_a, cw_b, cb_b, ln_g, ln_b, w_out, g_post, *, n, t, nb, tt):
    rows = nb * tt
    nt = t // tt
    kern = functools.partial(_even_kernel, nb=nb, tt=tt, multi_tile=nt > 1)
    row_spec = pl.BlockSpec((rows, D_MODEL), lambda i, j: (i * nt + j, 0))

    def seq_spec(shape):
        return pl.BlockSpec((nb,) + shape, lambda i, j: (i, 0, 0))

    return pl.pallas_call(
        kern,
        out_shape=(
            jax.ShapeDtypeStruct((n * t, D_MODEL), F32),
            jax.ShapeDtypeStruct((n, CONV_A - 1, MIX_W), F32),
            jax.ShapeDtypeStruct((n, CONV_B - 1, MIX_W), F32),
        ),
        grid=(n // nb, nt),
        in_specs=[
            row_spec,
            seq_spec((1, 3 * D_MODEL)),
            seq_spec((CONV_A - 1, MIX_W)),
            seq_spec((CONV_B - 1, MIX_W)),
            _const_spec((1, D_MODEL)),
            _const_spec((D_MODEL, EVEN_IN)),
            _const_spec((CONV_A, MIX_W)),
            _const_spec((CONV_B, MIX_W)),
            _const_spec((1, MIX_W)),
            _const_spec((1, MIX_W)),
            _const_spec((1, MIX_W)),
            _const_spec((2 * MIX_W, D_MODEL)),
            _const_spec((1, D_MODEL)),
        ],
        out_specs=(row_spec, seq_spec((CONV_A - 1, MIX_W)), seq_spec((CONV_B - 1, MIX_W))),
        scratch_shapes=[
            pltpu.VMEM((rows, D_MODEL), BF16),
            pltpu.VMEM((rows, EVEN_IN), F32),
            pltpu.VMEM((nb * N_LG, HIST_A + tt, LANES), F32),
            pltpu.VMEM((nb * N_LG, HIST_B + tt, LANES), F32),
            pltpu.VMEM((rows, MIX_W), F32),
            pltpu.VMEM((rows, MIX_W), F32),
            pltpu.VMEM((rows, 2 * MIX_W), BF16),
        ],
        compiler_params=pltpu.CompilerParams(
            dimension_semantics=("arbitrary", "arbitrary"), vmem_limit_bytes=VMEM_LIMIT_BYTES),
        name="even_layer",
    )(x2d, mod, st_a, st_b, g_pre, w_in, cw_a, cw_b, cb_b, ln_g, ln_b, w_out, g_post)


def _odd_kernel(sinks_ref, x_ref, mod_ref, sc_ref, sk_ref, sv_ref, gpre_ref, win_ref, pw_ref, ps_ref,
                wout_ref, gpost_ref,
                y_ref, nc_ref, nk_ref, nv_ref,
                h_ref, z_ref, fc_ref, kf_ref, vf_ref, pc_ref, od_ref, mix_ref, *, nb, tt, pos0, multi_tile):
    rows = nb * tt
    j = pl.program_id(1)
    q_col, k_col, v_col, dg_col = 2 * MIX_W, 3 * MIX_W, 3 * MIX_W + KV_W, 3 * MIX_W + 2 * KV_W

    @pl.when(j == 0)
    def _load_state():
        for b in range(nb):
            for g in range(N_LG):
                fc_ref[b * N_LG + g, 0:8, :] = jnp.zeros((8, LANES), F32)
                fc_ref[b * N_LG + g, HIST_C - POOL_PAST:HIST_C, :] = sc_ref[b, :, LANES * g:LANES * (g + 1)]
            kf_ref[b, 0:WINDOW, :] = sk_ref[b]
            vf_ref[b, 0:WINDOW, :] = sv_ref[b]

    _pre_norm(x_ref, mod_ref, gpre_ref, h_ref, rows, tt)

    hb = h_ref[...]
    for lo, hi in ((0, MIX_W), (MIX_W, 2 * MIX_W), (q_col, k_col), (k_col, dg_col), (dg_col, ODD_IN)):
        z_ref[:, lo:hi] = jnp.dot(hb, win_ref[:, lo:hi], preferred_element_type=F32)

    for b, c0, r0, rc in _seq_chunks(nb, tt, 128):
        for g in range(N_LG):
            fc_ref[b * N_LG + g, HIST_C + c0:HIST_C + c0 + rc, :] = z_ref[r0:r0 + rc, LANES * g:LANES * (g + 1)]
        kf_ref[b, WINDOW + c0:WINDOW + c0 + rc, :] = z_ref[r0:r0 + rc, k_col:k_col + KV_W]
        vf_ref[b, WINDOW + c0:WINDOW + c0 + rc, :] = z_ref[r0:r0 + rc, v_col:v_col + KV_W]

    for b, c0, r0, rc in _seq_chunks(nb, tt, 128):
        pos = pos0 + j * tt + c0 + lax.broadcasted_iota(jnp.int32, (rc, LANES), 0)
        for g, w in enumerate(POOL_WINDOWS):
            u = fc_ref[b * N_LG + g, HIST_C + c0:HIST_C + c0 + rc, :]
            acc = u
            for d in range(1, w):
                acc = acc + fc_ref[b * N_LG + g, HIST_C + c0 - d:HIST_C + c0 - d + rc, :]
            cnt = jnp.minimum(w, pos + 1).astype(F32)
            pc_ref[r0:r0 + rc, LANES * g:LANES * (g + 1)] = acc / cnt - u

    qs = min(tt, WINDOW)
    gs = max(1, ATT_MIN_Q_ROWS // qs)
    kw = WINDOW + qs
    n_q, n_keys = gs * qs, gs * kw
    assert qs & (qs - 1) == 0 and gs & (gs - 1) == 0 and nb % gs == 0 and (gs == 1 or tt == qs)
    shape = (N_REP * n_q, n_keys)
    ri = lax.broadcasted_iota(jnp.int32, shape, 0)
    ci = lax.broadcasted_iota(jnp.int32, shape, 1)
    q_seq = (ri >> (qs.bit_length() - 1)) & (gs - 1)
    rep = ri >> (n_q.bit_length() - 1)
    k_seq = jnp.zeros(shape, jnp.int32)
    for s_i in range(1, gs):
        k_seq = k_seq + (ci >= s_i * kw).astype(jnp.int32)
    k_c = ci - k_seq * kw
    dist = (ri & (qs - 1)) + WINDOW - k_c
    band = (q_seq == k_seq) & (dist >= 0) & (dist < WINDOW)
    dist_f = dist.astype(F32)
    rep_col = lax.broadcasted_iota(jnp.int32, (N_REP * n_q, 1), 0) >> (n_q.bit_length() - 1)
    biases, sink_cols = [], []
    for g in range(N_KV):
        slope = jnp.full(shape, ALIBI_SLOPES[g * N_REP], F32)
        sink = jnp.full((N_REP * n_q, 1), sinks_ref[g * N_REP], F32)
        for r in range(1, N_REP):
            slope = jnp.where(rep == r, ALIBI_SLOPES[g * N_REP + r], slope)
            sink = jnp.where(rep_col == r, sinks_ref[g * N_REP + r], sink)
        biases.append(jnp.where(band, -(slope * dist_f), -jnp.inf))
        sink_cols.append(sink)
    for b0 in range(0, nb, gs):
        for sb in range(tt // qs):
            r0 = b0 * tt + sb * qs
            p0 = pos0 + j * tt + sb * qs
            in_range = k_c >= WINDOW - p0
            if gs == 1:
                kwin = kf_ref[b0, sb * qs:sb * qs + kw, :].astype(BF16)
                vwin = vf_ref[b0, sb * qs:sb * qs + kw, :].astype(BF16)
            else:
                kwin = kf_ref[b0:b0 + gs].reshape(n_keys, KV_W).astype(BF16)
                vwin = vf_ref[b0:b0 + gs].reshape(n_keys, KV_W).astype(BF16)
            for g in range(N_KV):
                heads = range(g * N_REP, (g + 1) * N_REP)
                q = jnp.concatenate(
                    [z_ref[r0:r0 + n_q, q_col + HEAD_DIM * h:q_col + HEAD_DIM * (h + 1)] for h in heads], axis=0)
                q = (q * (HEAD_DIM ** -0.5)).astype(BF16)
                s = lax.dot_general(q, kwin[:, HEAD_DIM * g:HEAD_DIM * (g + 1)], (((1,), (1,)), ((), ())),
                                    preferred_element_type=F32)
                s = s + jnp.where(in_range, biases[g], -jnp.inf)
                m = jnp.maximum(jnp.max(s, axis=-1, keepdims=True), sink_cols[g])
                p = jnp.exp(s - m)
                den = jnp.sum(p, axis=-1, keepdims=True) + jnp.exp(sink_cols[g] - m)
                o = jnp.dot(p.astype(BF16), vwin[:, HEAD_DIM * g:HEAD_DIM * (g + 1)], preferred_element_type=F32)
                o = o / den
                for r, h in enumerate(heads):
                    od_ref[r0:r0 + n_q, HEAD_DIM * h:HEAD_DIM * (h + 1)] = o[r * n_q:(r + 1) * n_q]

    for b in range(nb):
        for g in range(N_LG):
            nc_ref[b, :, LANES * g:LANES * (g + 1)] = fc_ref[b * N_LG + g, HIST_C + tt - POOL_PAST:HIST_C + tt, :]
            if multi_tile:
                fc_ref[b * N_LG + g, 0:HIST_C, :] = fc_ref[b * N_LG + g, tt:tt + HIST_C, :]
        nk_ref[b] = kf_ref[b, tt:tt + WINDOW, :]
        nv_ref[b] = vf_ref[b, tt:tt + WINDOW, :]
        if multi_tile:
            kf_ref[b, 0:WINDOW, :] = kf_ref[b, tt:tt + WINDOW, :]
            vf_ref[b, 0:WINDOW, :] = vf_ref[b, tt:tt + WINDOW, :]

    rc = min(rows, 128)
    for r0 in range(0, rows, rc):
        for g in range(N_LG):
            lg = slice(LANES * g, LANES * (g + 1))
            mixed = jnp.dot(pc_ref[r0:r0 + rc, lg].astype(BF16), pw_ref[g], preferred_element_type=F32)
            yc = mixed * ps_ref[:, lg] * _silu(z_ref[r0:r0 + rc, MIX_W + LANES * g:MIX_W + LANES * (g + 1)])
            mix_ref[r0:r0 + rc, lg] = yc.astype(BF16)
        yd = od_ref[r0:r0 + rc, :] * _silu(z_ref[r0:r0 + rc, dg_col:dg_col + MIX_W])
        mix_ref[r0:r0 + rc, MIX_W:2 * MIX_W] = yd.astype(BF16)

    z_ref[:, 0:D_MODEL] = jnp.dot(mix_ref[...], wout_ref[...], preferred_element_type=F32)
    _post_norm_residual(x_ref, mod_ref, gpost_ref, z_ref, y_ref, rows, tt)


def _odd_layer(x2d, mod, st_c, st_k, st_v, g_pre, w_in, pool_w, pool_scale, sinks, w_out, g_post,
               *, n, t, nb, tt, pos0):
    rows = nb * tt
    nt = t // tt
    kern = functools.partial(_odd_kernel, nb=nb, tt=tt, pos0=pos0, multi_tile=nt > 1)
    row_spec = pl.BlockSpec((rows, D_MODEL), lambda i, j: (i * nt + j, 0))

    def seq_spec(shape):
        return pl.BlockSpec((nb,) + shape, lambda i, j: (i, 0, 0))

    return pl.pallas_call(
        kern,
        out_shape=(
            jax.ShapeDtypeStruct((n * t, D_MODEL), F32),
            jax.ShapeDtypeStruct((n, POOL_PAST, MIX_W), F32),
            jax.ShapeDtypeStruct((n, WINDOW, KV_W), F32),
            jax.ShapeDtypeStruct((n, WINDOW, KV_W), F32),
        ),
        grid=(n // nb, nt),
        in_specs=[
            pl.BlockSpec(memory_space=pltpu.SMEM),
            row_spec,
            seq_spec((1, 3 * D_MODEL)),
            seq_spec((POOL_PAST, MIX_W)),
            seq_spec((WINDOW, KV_W)),
            seq_spec((WINDOW, KV_W)),
            _const_spec((1, D_MODEL)),
            _const_spec((D_MODEL, ODD_IN)),
            _const_spec((N_LG, LANES, LANES)),
            _const_spec((1, MIX_W)),
            _const_spec((2 * MIX_W, D_MODEL)),
            _const_spec((1, D_MODEL)),
        ],
        out_specs=(row_spec, seq_spec((POOL_PAST, MIX_W)), seq_spec((WINDOW, KV_W)), seq_spec((WINDOW, KV_W))),
        scratch_shapes=[
            pltpu.VMEM((rows, D_MODEL), BF16),
            pltpu.VMEM((rows, ODD_IN), F32),
            pltpu.VMEM((nb * N_LG, HIST_C + tt, LANES), F32),
            pltpu.VMEM((nb, WINDOW + tt, KV_W), F32),
            pltpu.VMEM((nb, WINDOW + tt, KV_W), F32),
            pltpu.VMEM((rows, MIX_W), F32),
            pltpu.VMEM((rows, MIX_W), F32),
            pltpu.VMEM((rows, 2 * MIX_W), BF16),
        ],
        compiler_params=pltpu.CompilerParams(
            dimension_semantics=("arbitrary", "arbitrary"), vmem_limit_bytes=VMEM_LIMIT_BYTES),
        name="odd_layer",
    )(sinks, x2d, mod, st_c, st_k, st_v, g_pre, w_in, pool_w, pool_scale, w_out, g_post)


def _trunk(x, mod_e, mod_o, st_a, st_b, st_c, st_k, st_v, pos0, we, wo, *, nb, tt):
    n, t, _ = x.shape
    x2d = x.reshape(n * t, D_MODEL)
    x2d, na, nbs = _even_layer(x2d, mod_e.reshape(n, 1, -1), st_a, st_b, *we, n=n, t=t, nb=nb, tt=tt)
    x2d, nc, nk, nv = _odd_layer(x2d, mod_o.reshape(n, 1, -1), st_c, st_k.reshape(n, WINDOW, KV_W),
                                 st_v.reshape(n, WINDOW, KV_W), *wo, n=n, t=t, nb=nb, tt=tt, pos0=pos0)
    kv_shape = (1, n, WINDOW, N_KV, HEAD_DIM)
    return (x2d.reshape(n, t, D_MODEL), na[None], nbs[None], nc[None],
            nk.reshape(kv_shape), nv.reshape(kv_shape))


def kernel(x_prompt, x_sample, state_conv_a, state_conv_b, state_pool_c, cache_win_k, cache_win_v, c_prompt, c_sample, w_mod_e, b_mod_e, g_pre_e, g_post_e, w_in_e, conv_a_w, conv_b_w, conv_b_b, ln_b_g, ln_b_b, w_out_e, w_mod_o, b_mod_o, g_pre_o, g_post_o, w_in_o, pool_w, pool_scale, sinks, w_out_o):
    n_p, t_p, _ = x_prompt.shape
    n_s, t_s, _ = x_sample.shape
    assert w_in_e.shape[0] == 1 and w_in_o.shape[0] == 1, "one even and one odd layer"

    c_all = jnp.concatenate([c_prompt, c_sample], axis=0)
    mod_e = _adaln_mod(c_all, w_mod_e[0], b_mod_e[0])
    mod_o = _adaln_mod(c_all, w_mod_o[0], b_mod_o[0])

    we = (g_pre_e, w_in_e[0].astype(BF16), conv_a_w[0], conv_b_w[0], conv_b_b, ln_b_g, ln_b_b,
          w_out_e[0].astype(BF16), g_post_e)
    wo = (g_pre_o, w_in_o[0].astype(BF16), pool_w[0].astype(BF16), pool_scale, sinks[0],
          w_out_o[0].astype(BF16), g_post_o)

    dt = x_prompt.dtype
    z_a = jnp.zeros((n_p, CONV_A - 1, MIX_W), dt)
    z_b = jnp.zeros((n_p, CONV_B - 1, MIX_W), dt)
    z_c = jnp.zeros((n_p, POOL_PAST, MIX_W), dt)
    z_kv = jnp.zeros((n_p, WINDOW, N_KV, HEAD_DIM), dt)
    y_p, pa, pb, pc, pk, pv = _trunk(x_prompt, mod_e[:n_p], mod_o[:n_p], z_a, z_b, z_c, z_kv, z_kv, 0,
                                     we, wo, nb=1, tt=512)
    y_s, sa, sb, sc, sk, sv = _trunk(x_sample, mod_e[n_p:], mod_o[n_p:], state_conv_a[0], state_conv_b[0],
                                     state_pool_c[0], cache_win_k[0], cache_win_v[0], PAST_LEN,
                                     we, wo, nb=32, tt=t_s)
    return (y_p, y_s, pa, sa, pb, sb, pc, sc, pk, sk, pv, sv)
```

```python
import functools

import jax
import jax.numpy as jnp
from jax import lax
from jax.experimental import pallas as pl
from jax.experimental.pallas import tpu as pltpu

F32 = jnp.float32
BF16 = jnp.bfloat16

D_MODEL = 1024
MIX_W = 512
LANES = 128
MXU_COLS = 256
N_LG = MIX_W // LANES
HEAD_DIM = 64
N_HEADS = 8
N_KV = 2
N_REP = N_HEADS // N_KV
KV_W = N_KV * HEAD_DIM
CONV_A = 3
CONV_B = 31
POOL_WINDOWS = (2, 4, 8, 16)
POOL_PAST = 15
WINDOW = 128
PAST_LEN = 8192
EVEN_IN = 7 * MIX_W
ODD_IN = 4 * MIX_W + 2 * KV_W
Q_COL, K_COL, V_COL, DG_COL = 2 * MIX_W, 3 * MIX_W, 3 * MIX_W + KV_W, 3 * MIX_W + 2 * KV_W
RMS_EPS = 1e-6
LN_EPS = 1e-5
HIST_A = 8
HIST_B = 32
HIST_C = 16
ATT_MIN_Q_ROWS = 32
ROW_CHUNK = 64
WIN_CHUNK = 128
VMEM_LIMIT_BYTES = 56 * 1024 * 1024
ALIBI_SLOPES = tuple(float(2.0 ** (-8.0 * (h + 1) / N_HEADS)) for h in range(N_HEADS))


def _sigmoid(x):
    return 1.0 / (1.0 + jnp.exp(-x))


def _silu(x):
    return x * _sigmoid(x)


def _mod_kernel(c_ref, w_ref, b_ref, o_ref):
    s = _silu(c_ref[...]).astype(BF16)
    o_ref[...] = jnp.dot(s, w_ref[...].astype(BF16), preferred_element_type=F32) + b_ref[...]


def _adaln_mod(c_all, w_mod, b_mod):
    n = c_all.shape[0]
    tn = 512
    return pl.pallas_call(
        _mod_kernel,
        out_shape=jax.ShapeDtypeStruct((n, 3 * D_MODEL), F32),
        grid=(3 * D_MODEL // tn,),
        in_specs=[
            pl.BlockSpec((n, D_MODEL), lambda i: (0, 0)),
            pl.BlockSpec((D_MODEL, tn), lambda i: (0, i)),
            pl.BlockSpec((1, tn), lambda i: (0, i)),
        ],
        out_specs=pl.BlockSpec((n, tn), lambda i: (0, i)),
        name="adaln_mod",
    )(c_all, w_mod, b_mod.reshape(1, -1))


def _mod_rows(mod_ref, seq0, col, r0, rc, tt):
    if tt >= rc:
        return mod_ref[seq0 + r0 // tt][:, col:col + D_MODEL]
    parts = [jnp.broadcast_to(mod_ref[seq0 + b][:, col:col + D_MODEL], (tt, D_MODEL))
             for b in range(r0 // tt, (r0 + rc) // tt)]
    return jnp.concatenate(parts, axis=0)


def _pre_norm(x_ref, mod_ref, seq0, g_ref, h_ref, rows, tt):
    rc = min(rows, ROW_CHUNK)
    g = g_ref[...]
    for r0 in range(0, rows, rc):
        x = x_ref[r0:r0 + rc, :]
        ms = jnp.mean(x * x, axis=-1, keepdims=True)
        h = x * lax.rsqrt(ms + RMS_EPS) * g
        h = h * (1.0 + _mod_rows(mod_ref, seq0, D_MODEL, r0, rc, tt)) + _mod_rows(mod_ref, seq0, 0, r0, rc, tt)
        h_ref[r0:r0 + rc, :] = h.astype(BF16)


def _out_proj_task(mix_ref, wout_ref, o_ref, lo, n):
    def task():
        o_ref[lo:lo + n, :] = jnp.dot(mix_ref[lo:lo + n, :], wout_ref[...], preferred_element_type=F32)
    return task


def _residual_tasks(x_ref, mod_ref, seq0, g_ref, o_ref, y_ref, lo, n, tt):
    rc = min(n, ROW_CHUNK)

    def chunk(r0):
        o = o_ref[r0:r0 + rc, :]
        ms = jnp.mean(o * o, axis=-1, keepdims=True)
        nrm = o * lax.rsqrt(ms + RMS_EPS) * g_ref[...]
        y_ref[r0:r0 + rc, :] = x_ref[r0:r0 + rc, :] + _mod_rows(mod_ref, seq0, 2 * D_MODEL, r0, rc, tt) * nrm
    return [functools.partial(chunk, r0) for r0 in range(lo, lo + n, rc)]


def _seq_chunks(lo, n, tt):
    rc = min(tt, n, WIN_CHUNK)
    return [(r0 // tt, r0 % tt, r0, rc) for r0 in range(lo, lo + n, rc)]


def _emit_merged(vpu_tasks, mxu_tasks):
    n_v, n_m = len(vpu_tasks), len(mxu_tasks)
    done = 0
    for i, task in enumerate(vpu_tasks):
        want = -(-(i + 1) * n_m // n_v)
        while done < min(want, n_m):
            mxu_tasks[done]()
            done += 1
        task()
    for task in mxu_tasks[done:]:
        task()


def _emit_pipelined(rows, in_proj_tasks, mixer_tasks, out_proj_task, residual_tasks):
    n_chunks = max(1, rows // WIN_CHUNK)
    n = rows // n_chunks
    share = -(-len(in_proj_tasks) // n_chunks)
    for c in range(n_chunks + 1):
        vpu, mxu = [], []
        if c > 0:
            mxu.append(out_proj_task((c - 1) * n, n))
        mxu += in_proj_tasks[c * share:(c + 1) * share]
        if c < n_chunks:
            vpu += mixer_tasks(c * n, n)
        if c > 0:
            vpu += residual_tasks((c - 1) * n, n)
        _emit_merged(vpu, mxu)


def _even_kernel(x_ref, mod_ref, sa_ref, sb_ref, gpre_ref, win_ref, cwa_ref, cwb_ref, cbb_ref,
                 lng_ref, lnb_ref, wout_ref, gpost_ref,
                 y_ref, na_ref, nb_ref,
                 h_ref, z_ref, o_ref, fa_ref, fb_ref, ca_ref, cb_ref, mix_ref, *, nb, tt, nt):
    rows = nb * tt
    j = pl.program_id(1)

    @pl.when(j == 0)
    def _load_state():
        for b in range(nb):
            for g in range(N_LG):
                lg = slice(LANES * g, LANES * (g + 1))
                fa_ref[b * N_LG + g, 0:HIST_A, :] = jnp.zeros((HIST_A, LANES), F32)
                fa_ref[b * N_LG + g, HIST_A - (CONV_A - 1):HIST_A, :] = sa_ref[b, :, lg]
                fb_ref[b * N_LG + g, 0:8, :] = jnp.zeros((8, LANES), F32)
                fb_ref[b * N_LG + g, HIST_B - (CONV_B - 1):HIST_B, :] = sb_ref[b, :, lg]

    _pre_norm(x_ref, mod_ref, 0, gpre_ref, h_ref, rows, tt)

    hb = h_ref[...]
    for c in range(EVEN_IN // MIX_W):
        cols = slice(MIX_W * c, MIX_W * (c + 1))
        z_ref[:, cols] = jnp.dot(hb, win_ref[:, cols], preferred_element_type=F32)

    for b, c0, r0, rc in _seq_chunks(0, rows, tt):
        for g in range(N_LG):
            def zc(c):
                return z_ref[r0:r0 + rc, MIX_W * c + LANES * g:MIX_W * c + LANES * (g + 1)]
            fa_ref[b * N_LG + g, HIST_A + c0:HIST_A + c0 + rc, :] = zc(2) * zc(0)
            fb_ref[b * N_LG + g, HIST_B + c0:HIST_B + c0 + rc, :] = zc(4) * _sigmoid(zc(5))

    for b, c0, r0, rc in _seq_chunks(0, rows, tt):
        for g in range(N_LG):
            lg = slice(LANES * g, LANES * (g + 1))
            acc = None
            for k in range(CONV_A):
                off = HIST_A - (CONV_A - 1) + k + c0
                t = cwa_ref[k:k + 1, lg] * fa_ref[b * N_LG + g, off:off + rc, :]
                acc = t if acc is None else acc + t
            ca_ref[r0:r0 + rc, lg] = acc
            acc = None
            for k in range(CONV_B):
                off = HIST_B - (CONV_B - 1) + k + c0
                t = cwb_ref[k:k + 1, lg] * fb_ref[b * N_LG + g, off:off + rc, :]
                acc = t if acc is None else acc + t
            cb_ref[r0:r0 + rc, lg] = acc

    for b in range(nb):
        for g in range(N_LG):
            lg = slice(LANES * g, LANES * (g + 1))
            na_ref[b, :, lg] = fa_ref[b * N_LG + g, HIST_A + tt - (CONV_A - 1):HIST_A + tt, :]
            nb_ref[b, :, lg] = fb_ref[b * N_LG + g, HIST_B + tt - (CONV_B - 1):HIST_B + tt, :]
            if nt > 1:
                fa_ref[b * N_LG + g, 0:HIST_A, :] = fa_ref[b * N_LG + g, tt:tt + HIST_A, :]
                fb_ref[b * N_LG + g, 0:HIST_B, :] = fb_ref[b * N_LG + g, tt:tt + HIST_B, :]

    rc = min(rows, ROW_CHUNK)
    for r0 in range(0, rows, rc):
        def zs(c):
            return z_ref[r0:r0 + rc, MIX_W * c:MIX_W * (c + 1)]
        ya = zs(1) * ca_ref[r0:r0 + rc, :] * _silu(zs(3))
        mix_ref[r0:r0 + rc, 0:MIX_W] = ya.astype(BF16)
        yb = cb_ref[r0:r0 + rc, :] + cbb_ref[...]
        mu = jnp.mean(yb, axis=-1, keepdims=True)
        yc = yb - mu
        var = jnp.mean(yc * yc, axis=-1, keepdims=True)
        ln = yc * lax.rsqrt(var + LN_EPS) * lng_ref[...] + lnb_ref[...]
        mix_ref[r0:r0 + rc, MIX_W:2 * MIX_W] = (_silu(ln) * _silu(zs(6))).astype(BF16)

    _out_proj_task(mix_ref, wout_ref, o_ref, 0, rows)()
    for task in _residual_tasks(x_ref, mod_ref, 0, gpost_ref, o_ref, y_ref, 0, rows, tt):
        task()


def _even_layer(x2d, mod, st_a, st_b, g_pre, w_in, cw_a, cw_b, cb_b, ln_g, ln_b, w_out, g_post, *, n, t, nb, tt):
    rows = nb * tt
    nt = t // tt
    kern = functools.partial(_even_kernel, nb=nb, tt=tt, nt=nt)
    row_spec = pl.BlockSpec((rows, D_MODEL), lambda i, j: (i * nt + j, 0))

    def seq_spec(shape):
        return pl.BlockSpec((nb,) + shape, lambda i, j: (i, 0, 0))

    def const_spec(shape):
        return pl.BlockSpec(shape, lambda i, j: (0,) * len(shape))

    return pl.pallas_call(
        kern,
        out_shape=(
            jax.ShapeDtypeStruct((n * t, D_MODEL), F32),
            jax.ShapeDtypeStruct((n, CONV_A - 1, MIX_W), F32),
            jax.ShapeDtypeStruct((n, CONV_B - 1, MIX_W), F32),
        ),
        grid=(n // nb, nt),
        in_specs=[
            row_spec,
            seq_spec((1, 3 * D_MODEL)),
            seq_spec((CONV_A - 1, MIX_W)),
            seq_spec((CONV_B - 1, MIX_W)),
            const_spec((1, D_MODEL)),
            const_spec((D_MODEL, EVEN_IN)),
            const_spec((CONV_A, MIX_W)),
            const_spec((CONV_B, MIX_W)),
            const_spec((1, MIX_W)),
            const_spec((1, MIX_W)),
            const_spec((1, MIX_W)),
            const_spec((2 * MIX_W, D_MODEL)),
            const_spec((1, D_MODEL)),
        ],
        out_specs=(row_spec, seq_spec((CONV_A - 1, MIX_W)), seq_spec((CONV_B - 1, MIX_W))),
        scratch_shapes=[
            pltpu.VMEM((rows, D_MODEL), BF16),
            pltpu.VMEM((rows, EVEN_IN), F32),
            pltpu.VMEM((rows, D_MODEL), F32),
            pltpu.VMEM((nb * N_LG, HIST_A + tt, LANES), F32),
            pltpu.VMEM((nb * N_LG, HIST_B + tt, LANES), F32),
            pltpu.VMEM((rows, MIX_W), F32),
            pltpu.VMEM((rows, MIX_W), F32),
            pltpu.VMEM((rows, 2 * MIX_W), BF16),
        ],
        compiler_params=pltpu.CompilerParams(
            dimension_semantics=("arbitrary", "arbitrary"), vmem_limit_bytes=VMEM_LIMIT_BYTES),
        name="even_layer",
    )(x2d, mod, st_a, st_b, g_pre, w_in, cw_a, cw_b, cb_b, ln_g, ln_b, w_out, g_post)


def _odd_kernel(sinks_ref, xa_ref, xb_ref, mod_ref, sc_ref, sk_ref, sv_ref, gpre_ref, win_ref, pw_ref, ps_ref,
                wout_ref, gpost_ref,
                y_ref, nc_ref, nk_ref, nv_ref,
                h_ref, z0_ref, z1_ref, o_ref, fc_ref, kf_ref, ks_ref, vf_ref, vs_ref, pc_ref, od_ref, mix_ref,
                *, nb, tt, nt, n_tiles, pos0):
    rows = nb * tt
    s = pl.program_id(0)
    ta = jnp.minimum(s, n_tiles - 1)
    tb = jnp.maximum(s - 1, 0)
    seq_a, seq_b, jb = (ta // nt) * nb, (tb // nt) * nb, tb % nt
    tile_pos = pos0 + jb * tt

    @pl.when(s == 0)
    def _init():
        z1_ref[...] = jnp.zeros((rows, ODD_IN), F32)

    @pl.when(jb == 0)
    def _load_state():
        for b in range(nb):
            for g in range(N_LG):
                fc_ref[b * N_LG + g, 0:8, :] = jnp.zeros((8, LANES), F32)
                fc_ref[b * N_LG + g, HIST_C - POOL_PAST:HIST_C, :] = sc_ref[b, :, LANES * g:LANES * (g + 1)]
            kf_ref[b, 0:WINDOW, :] = sk_ref[b]
            vf_ref[b, 0:WINDOW, :] = sv_ref[b]
            ks_ref[b, 0:WINDOW, :] = pltpu.roll(sk_ref[b], HEAD_DIM, axis=1)
            vs_ref[b, 0:WINDOW, :] = pltpu.roll(sv_ref[b], HEAD_DIM, axis=1)

    qs = min(tt, WINDOW)
    gs = max(1, ATT_MIN_Q_ROWS // qs)
    kw = WINDOW + qs
    n_q, n_keys = gs * qs, gs * kw
    assert N_REP == 4 and 2 * HEAD_DIM == LANES and KV_W == LANES
    assert qs & (qs - 1) == 0 and gs & (gs - 1) == 0 and nb % gs == 0 and (gs == 1 or tt == qs)
    shape = (2 * n_q, n_keys)
    ri = lax.broadcasted_iota(jnp.int32, shape, 0)
    ci = lax.broadcasted_iota(jnp.int32, shape, 1)
    q_seq = (ri >> (qs.bit_length() - 1)) & (gs - 1)
    k_seq = jnp.zeros(shape, jnp.int32)
    for s_i in range(1, gs):
        k_seq = k_seq + (ci >= s_i * kw).astype(jnp.int32)
    k_c = ci - k_seq * kw
    dist = (ri & (qs - 1)) + WINDOW - k_c
    band = (q_seq == k_seq) & (dist >= 0) & (dist < WINDOW)
    dist_f = dist.astype(F32)
    upper_grp = ri >= n_q
    upper_grp_col = lax.broadcasted_iota(jnp.int32, (2 * n_q, 1), 0) >= n_q
    low_lanes = lax.broadcasted_iota(jnp.int32, (n_keys, LANES), 1) < HEAD_DIM
    ones_cols = jnp.ones((n_keys, LANES), BF16)

    def head_of(g, grp, parity):
        return g * N_REP + 2 * grp + parity

    def alibi_bias(g, parity):
        slope = jnp.where(upper_grp, ALIBI_SLOPES[head_of(g, 1, parity)], ALIBI_SLOPES[head_of(g, 0, parity)])
        return jnp.where(band, -(slope * dist_f), -jnp.inf)

    biases = {(g, parity): alibi_bias(g, parity) for g in range(N_KV) for parity in range(2)}

    def attention(b0, sb, g):
        r0 = b0 * tt + sb * qs
        p0 = tile_pos + sb * qs
        in_range = k_c >= WINDOW - p0
        if gs == 1:
            win = (b0, slice(sb * qs, sb * qs + kw))
            k_nat, k_swp, v_nat, v_swp = kf_ref[win], ks_ref[win], vf_ref[win], vs_ref[win]
        else:
            k_nat, k_swp, v_nat, v_swp = (r[b0:b0 + gs].reshape(n_keys, KV_W)
                                          for r in (kf_ref, ks_ref, vf_ref, vs_ref))
        k_lo, k_hi = (k_nat, k_swp) if g == 0 else (k_swp, k_nat)
        v_lo, v_hi = (v_nat, v_swp) if g == 0 else (v_swp, v_nat)
        v_ext = jnp.concatenate([jnp.where(low_lanes, v_lo, v_hi).astype(BF16), ones_cols], axis=1)
        qg = jnp.concatenate([z_ref_b[r0:r0 + n_q, Q_COL + 2 * LANES * g + LANES * grp:
                                      Q_COL + 2 * LANES * g + LANES * (grp + 1)] for grp in range(2)], axis=0)
        qg = (qg * (HEAD_DIM ** -0.5)).astype(BF16)
        for parity in range(2):
            k_ext = (jnp.where(low_lanes, k_lo, 0.0) if parity == 0 else jnp.where(low_lanes, 0.0, k_hi)).astype(BF16)
            sink = jnp.where(upper_grp_col, sinks_ref[head_of(g, 1, parity)], sinks_ref[head_of(g, 0, parity)])
            sc = lax.dot_general(qg, k_ext, (((1,), (1,)), ((), ())), preferred_element_type=F32)
            sc = sc + jnp.where(in_range, biases[g, parity], -jnp.inf)
            m = jnp.maximum(jnp.max(sc, axis=-1, keepdims=True), sink)
            p = jnp.exp(sc - m)
            o_ext = jnp.dot(p.astype(BF16), v_ext, preferred_element_type=F32)
            o = o_ext[:, 0:LANES] / (o_ext[:, LANES:2 * LANES] + jnp.exp(sink - m))
            for grp in range(2):
                h = head_of(g, grp, parity)
                od_ref[r0:r0 + n_q, HEAD_DIM * h:HEAD_DIM * (h + 1)] = (
                    o[grp * n_q:(grp + 1) * n_q, HEAD_DIM * parity:HEAD_DIM * (parity + 1)])

    def window_inputs(b, c0, r0, rc):
        for g in range(N_LG):
            fc_ref[b * N_LG + g, HIST_C + c0:HIST_C + c0 + rc, :] = z_ref_b[r0:r0 + rc, LANES * g:LANES * (g + 1)]
        k = z_ref_b[r0:r0 + rc, K_COL:K_COL + KV_W]
        v = z_ref_b[r0:r0 + rc, V_COL:V_COL + KV_W]
        kf_ref[b, WINDOW + c0:WINDOW + c0 + rc, :] = k
        vf_ref[b, WINDOW + c0:WINDOW + c0 + rc, :] = v
        ks_ref[b, WINDOW + c0:WINDOW + c0 + rc, :] = pltpu.roll(k, HEAD_DIM, axis=1)
        vs_ref[b, WINDOW + c0:WINDOW + c0 + rc, :] = pltpu.roll(v, HEAD_DIM, axis=1)

    def pooling(b, c0, r0, rc):
        pos = tile_pos + c0 + lax.broadcasted_iota(jnp.int32, (rc, LANES), 0)
        for g, w in enumerate(POOL_WINDOWS):
            u = fc_ref[b * N_LG + g, HIST_C + c0:HIST_C + c0 + rc, :]
            acc = u
            for d in range(1, w):
                acc = acc + fc_ref[b * N_LG + g, HIST_C + c0 - d:HIST_C + c0 - d + rc, :]
            cnt = jnp.minimum(w, pos + 1).astype(F32)
            pc_ref[r0:r0 + rc, LANES * g:LANES * (g + 1)] = acc / cnt - u

    def gating(r0, rc):
        for g in range(N_LG):
            lg = slice(LANES * g, LANES * (g + 1))
            mixed = jnp.dot(pc_ref[r0:r0 + rc, lg].astype(BF16), pw_ref[g], preferred_element_type=F32)
            yc = mixed * ps_ref[:, lg] * _silu(z_ref_b[r0:r0 + rc, MIX_W + LANES * g:MIX_W + LANES * (g + 1)])
            mix_ref[r0:r0 + rc, lg] = yc.astype(BF16)
        yd = od_ref[r0:r0 + rc, :] * _silu(z_ref_b[r0:r0 + rc, DG_COL:DG_COL + MIX_W])
        mix_ref[r0:r0 + rc, MIX_W:2 * MIX_W] = yd.astype(BF16)

    def mixer_tasks(lo, n):
        chunks = _seq_chunks(lo, n, tt)
        if gs == 1:
            blocks = [(r0 // tt, (r0 % tt) // qs) for r0 in range(lo, lo + n, qs)]
        else:
            blocks = [(b0, 0) for b0 in range(lo // tt, (lo + n) // tt, gs)]
        return ([functools.partial(window_inputs, *ch) for ch in chunks]
                + [functools.partial(pooling, *ch) for ch in chunks]
                + [functools.partial(attention, b0, sb, g) for b0, sb in blocks for g in range(N_KV)]
                + [functools.partial(gating, lo, n)])

    def in_proj_piece(z_ref_a, c0):
        z_ref_a[:, c0:c0 + MXU_COLS] = jnp.dot(h_ref[...], win_ref[:, c0:c0 + MXU_COLS], preferred_element_type=F32)

    _pre_norm(xa_ref, mod_ref, seq_a, gpre_ref, h_ref, rows, tt)

    for parity, (z_ref_a, z_ref_b) in enumerate(((z0_ref, z1_ref), (z1_ref, z0_ref))):
        @pl.when(s % 2 == parity)
        def _step():
            _emit_pipelined(
                rows,
                [functools.partial(in_proj_piece, z_ref_a, c0) for c0 in range(0, ODD_IN, MXU_COLS)],
                mixer_tasks,
                functools.partial(_out_proj_task, mix_ref, wout_ref, o_ref),
                functools.partial(_residual_tasks, xb_ref, mod_ref, seq_b, gpost_ref, o_ref, y_ref, tt=tt))

    for b in range(nb):
        for g in range(N_LG):
            nc_ref[b, :, LANES * g:LANES * (g + 1)] = fc_ref[b * N_LG + g, HIST_C + tt - POOL_PAST:HIST_C + tt, :]
            if nt > 1:
                fc_ref[b * N_LG + g, 0:HIST_C, :] = fc_ref[b * N_LG + g, tt:tt + HIST_C, :]
        nk_ref[b] = kf_ref[b, tt:tt + WINDOW, :]
        nv_ref[b] = vf_ref[b, tt:tt + WINDOW, :]
        if nt > 1:
            for r in (kf_ref, ks_ref, vf_ref, vs_ref):
                r[b, 0:WINDOW, :] = r[b, tt:tt + WINDOW, :]


def _odd_layer(x2d, mod, st_c, st_k, st_v, g_pre, w_in, pool_w, pool_scale, sinks, w_out, g_post,
               *, n, t, nb, tt, pos0):
    rows = nb * tt
    nt = t // tt
    n_tiles = (n // nb) * nt
    kern = functools.partial(_odd_kernel, nb=nb, tt=tt, nt=nt, n_tiles=n_tiles, pos0=pos0)
    cur_rows = pl.BlockSpec((rows, D_MODEL), lambda s: (jnp.minimum(s, n_tiles - 1), 0))
    prev_rows = pl.BlockSpec((rows, D_MODEL), lambda s: (jnp.maximum(s - 1, 0), 0))

    def prev_seq(shape):
        return pl.BlockSpec((nb,) + shape, lambda s: (jnp.maximum(s - 1, 0) // nt, 0, 0))

    def const_spec(shape):
        return pl.BlockSpec(shape, lambda s: (0,) * len(shape))

    return pl.pallas_call(
        kern,
        out_shape=(
            jax.ShapeDtypeStruct((n * t, D_MODEL), F32),
            jax.ShapeDtypeStruct((n, POOL_PAST, MIX_W), F32),
            jax.ShapeDtypeStruct((n, WINDOW, KV_W), F32),
            jax.ShapeDtypeStruct((n, WINDOW, KV_W), F32),
        ),
        grid=(n_tiles + 1,),
        in_specs=[
            pl.BlockSpec(memory_space=pltpu.SMEM),
            cur_rows,
            prev_rows,
            const_spec((n, 1, 3 * D_MODEL)),
            prev_seq((POOL_PAST, MIX_W)),
            prev_seq((WINDOW, KV_W)),
            prev_seq((WINDOW, KV_W)),
            const_spec((1, D_MODEL)),
            const_spec((D_MODEL, ODD_IN)),
            const_spec((N_LG, LANES, LANES)),
            const_spec((1, MIX_W)),
            const_spec((2 * MIX_W, D_MODEL)),
            const_spec((1, D_MODEL)),
        ],
        out_specs=(prev_rows, prev_seq((POOL_PAST, MIX_W)), prev_seq((WINDOW, KV_W)), prev_seq((WINDOW, KV_W))),
        scratch_shapes=[
            pltpu.VMEM((rows, D_MODEL), BF16),
            pltpu.VMEM((rows, ODD_IN), F32),
            pltpu.VMEM((rows, ODD_IN), F32),
            pltpu.VMEM((rows, D_MODEL), F32),
            pltpu.VMEM((nb * N_LG, HIST_C + tt, LANES), F32),
            pltpu.VMEM((nb, WINDOW + tt, KV_W), F32),
            pltpu.VMEM((nb, WINDOW + tt, KV_W), F32),
            pltpu.VMEM((nb, WINDOW + tt, KV_W), F32),
            pltpu.VMEM((nb, WINDOW + tt, KV_W), F32),
            pltpu.VMEM((rows, MIX_W), F32),
            pltpu.VMEM((rows, MIX_W), F32),
            pltpu.VMEM((rows, 2 * MIX_W), BF16),
        ],
        compiler_params=pltpu.CompilerParams(
            dimension_semantics=("arbitrary",), vmem_limit_bytes=VMEM_LIMIT_BYTES),
        name="odd_layer",
    )(sinks, x2d, x2d, mod, st_c, st_k, st_v, g_pre, w_in, pool_w, pool_scale, w_out, g_post)


def _tiling(n, t):
    if t >= 512:
        return 1, 512
    return min(n, 256 // t), t


def _trunk(x, mod_e, mod_o, st_a, st_b, st_c, st_k, st_v, pos0, we, wo):
    n, t, _ = x.shape
    nb, tt = _tiling(n, t)
    x2d = x.reshape(n * t, D_MODEL)
    x2d, na, nbs = _even_layer(x2d, mod_e.reshape(n, 1, -1), st_a, st_b, *we, n=n, t=t, nb=nb, tt=tt)
    x2d, nc, nk, nv = _odd_layer(x2d, mod_o.reshape(n, 1, -1), st_c, st_k.reshape(n, WINDOW, KV_W),
                                 st_v.reshape(n, WINDOW, KV_W), *wo, n=n, t=t, nb=nb, tt=tt, pos0=pos0)
    kv_shape = (1, n, WINDOW, N_KV, HEAD_DIM)
    return (x2d.reshape(n, t, D_MODEL), na[None], nbs[None], nc[None],
            nk.reshape(kv_shape), nv.reshape(kv_shape))


def kernel(x_prompt, x_sample, state_conv_a, state_conv_b, state_pool_c, cache_win_k, cache_win_v, c_prompt, c_sample, w_mod_e, b_mod_e, g_pre_e, g_post_e, w_in_e, conv_a_w, conv_b_w, conv_b_b, ln_b_g, ln_b_b, w_out_e, w_mod_o, b_mod_o, g_pre_o, g_post_o, w_in_o, pool_w, pool_scale, sinks, w_out_o):
    n_p = x_prompt.shape[0]
    assert w_in_e.shape[0] == 1 and w_in_o.shape[0] == 1, "one even and one odd layer"

    c_all = jnp.concatenate([c_prompt, c_sample], axis=0)
    mod_e = _adaln_mod(c_all, w_mod_e[0], b_mod_e[0])
    mod_o = _adaln_mod(c_all, w_mod_o[0], b_mod_o[0])

    we = (g_pre_e, w_in_e[0].astype(BF16), conv_a_w[0], conv_b_w[0], conv_b_b, ln_b_g, ln_b_b,
          w_out_e[0].astype(BF16), g_post_e)
    wo = (g_pre_o, w_in_o[0].astype(BF16), pool_w[0].astype(BF16), pool_scale, sinks[0],
          w_out_o[0].astype(BF16), g_post_o)

    dt = x_prompt.dtype
    z_a = jnp.zeros((n_p, CONV_A - 1, MIX_W), dt)
    z_b = jnp.zeros((n_p, CONV_B - 1, MIX_W), dt)
    z_c = jnp.zeros((n_p, POOL_PAST, MIX_W), dt)
    z_kv = jnp.zeros((n_p, WINDOW, N_KV, HEAD_DIM), dt)
    y_p, pa, pb, pc, pk, pv = _trunk(x_prompt, mod_e[:n_p], mod_o[:n_p], z_a, z_b, z_c, z_kv, z_kv, 0, we, wo)
    y_s, sa, sb, sc, sk, sv = _trunk(x_sample, mod_e[n_p:], mod_o[n_p:], state_conv_a[0], state_conv_b[0],
                                     state_pool_c[0], cache_win_k[0], cache_win_v[0], PAST_LEN, we, wo)
    return (y_p, y_s, pa, sa, pb, sb, pc, sc, pk, sk, pv, sv)
```

```python
import functools

import jax
import jax.numpy as jnp
from jax import lax
from jax.experimental import pallas as pl
from jax.experimental.pallas import tpu as pltpu

F32 = jnp.float32
BF16 = jnp.bfloat16

D_MODEL = 1024
MIX_W = 512
LANES = 128
MXU_COLS = 256
N_LG = MIX_W // LANES
HEAD_DIM = 64
N_HEADS = 8
N_KV = 2
N_REP = N_HEADS // N_KV
KV_W = N_KV * HEAD_DIM
CONV_A = 3
CONV_B = 31
POOL_WINDOWS = (2, 4, 8, 16)
POOL_PAST = 15
WINDOW = 128
PAST_LEN = 8192
EVEN_IN = 7 * MIX_W
ODD_IN = 4 * MIX_W + 2 * KV_W
Q_COL, K_COL, V_COL, DG_COL = 2 * MIX_W, 3 * MIX_W, 3 * MIX_W + KV_W, 3 * MIX_W + 2 * KV_W
RMS_EPS = 1e-6
LN_EPS = 1e-5
HIST_A = 8
HIST_B = 32
HIST_C = 16
ATT_MIN_Q_ROWS = 32
ROW_CHUNK = 64
WIN_CHUNK = 128
VMEM_LIMIT_BYTES = 56 * 1024 * 1024
ALIBI_SLOPES = tuple(float(2.0 ** (-8.0 * (h + 1) / N_HEADS)) for h in range(N_HEADS))


def _sigmoid(x):
    return 1.0 / (1.0 + jnp.exp(-x))


def _silu(x):
    return x * _sigmoid(x)


def _mod_kernel(c_ref, w_ref, b_ref, o_ref):
    s = _silu(c_ref[...]).astype(BF16)
    o_ref[...] = jnp.dot(s, w_ref[...].astype(BF16), preferred_element_type=F32) + b_ref[...]


def _adaln_mod(c_all, w_mod, b_mod):
    n = c_all.shape[0]
    tn = 512
    return pl.pallas_call(
        _mod_kernel,
        out_shape=jax.ShapeDtypeStruct((n, 3 * D_MODEL), F32),
        grid=(3 * D_MODEL // tn,),
        in_specs=[
            pl.BlockSpec((n, D_MODEL), lambda i: (0, 0)),
            pl.BlockSpec((D_MODEL, tn), lambda i: (0, i)),
            pl.BlockSpec((1, tn), lambda i: (0, i)),
        ],
        out_specs=pl.BlockSpec((n, tn), lambda i: (0, i)),
        name="adaln_mod",
    )(c_all, w_mod, b_mod.reshape(1, -1))


def _mod_rows(mod_ref, seq0, col, r0, rc, tt):
    if tt >= rc:
        return mod_ref[seq0 + r0 // tt][:, col:col + D_MODEL]
    parts = [jnp.broadcast_to(mod_ref[seq0 + b][:, col:col + D_MODEL], (tt, D_MODEL))
             for b in range(r0 // tt, (r0 + rc) // tt)]
    return jnp.concatenate(parts, axis=0)


def _pre_norm(x_ref, mod_ref, seq0, g_ref, h_ref, rows, tt):
    rc = min(rows, ROW_CHUNK)
    g = g_ref[...]
    for r0 in range(0, rows, rc):
        x = x_ref[r0:r0 + rc, :]
        ms = jnp.mean(x * x, axis=-1, keepdims=True)
        h = x * lax.rsqrt(ms + RMS_EPS) * g
        h = h * (1.0 + _mod_rows(mod_ref, seq0, D_MODEL, r0, rc, tt)) + _mod_rows(mod_ref, seq0, 0, r0, rc, tt)
        h_ref[r0:r0 + rc, :] = h.astype(BF16)


def _out_proj_task(mix_ref, wout_ref, o_ref, lo, n):
    def task():
        o_ref[lo:lo + n, :] = jnp.dot(mix_ref[lo:lo + n, :], wout_ref[...], preferred_element_type=F32)
    return task


def _residual_tasks(x_ref, mod_ref, seq0, g_ref, o_ref, y_ref, lo, n, tt):
    rc = min(n, ROW_CHUNK)

    def chunk(r0):
        o = o_ref[r0:r0 + rc, :]
        ms = jnp.mean(o * o, axis=-1, keepdims=True)
        nrm = o * lax.rsqrt(ms + RMS_EPS) * g_ref[...]
        y_ref[r0:r0 + rc, :] = x_ref[r0:r0 + rc, :] + _mod_rows(mod_ref, seq0, 2 * D_MODEL, r0, rc, tt) * nrm
    return [functools.partial(chunk, r0) for r0 in range(lo, lo + n, rc)]


def _seq_chunks(lo, n, tt):
    rc = min(tt, n, WIN_CHUNK)
    return [(r0 // tt, r0 % tt, r0, rc) for r0 in range(lo, lo + n, rc)]


def _even_kernel(x_ref, mod_ref, sa_ref, sb_ref, gpre_ref, win_ref, cwa_ref, cwb_ref, cbb_ref,
                 lng_ref, lnb_ref, wout_ref, gpost_ref,
                 y_ref, na_ref, nb_ref,
                 h_ref, z_ref, o_ref, fa_ref, fb_ref, ca_ref, cb_ref, mix_ref, *, nb, tt, nt):
    rows = nb * tt
    j = pl.program_id(1)

    @pl.when(j == 0)
    def _load_state():
        for b in range(nb):
            for g in range(N_LG):
                lg = slice(LANES * g, LANES * (g + 1))
                fa_ref[b * N_LG + g, 0:HIST_A, :] = jnp.zeros((HIST_A, LANES), F32)
                fa_ref[b * N_LG + g, HIST_A - (CONV_A - 1):HIST_A, :] = sa_ref[b, :, lg]
                fb_ref[b * N_LG + g, 0:8, :] = jnp.zeros((8, LANES), F32)
                fb_ref[b * N_LG + g, HIST_B - (CONV_B - 1):HIST_B, :] = sb_ref[b, :, lg]

    _pre_norm(x_ref, mod_ref, 0, gpre_ref, h_ref, rows, tt)

    hb = h_ref[...]
    for c in range(EVEN_IN // MIX_W):
        cols = slice(MIX_W * c, MIX_W * (c + 1))
        z_ref[:, cols] = jnp.dot(hb, win_ref[:, cols], preferred_element_type=F32)

    for b, c0, r0, rc in _seq_chunks(0, rows, tt):
        for g in range(N_LG):
            def zc(c):
                return z_ref[r0:r0 + rc, MIX_W * c + LANES * g:MIX_W * c + LANES * (g + 1)]
            fa_ref[b * N_LG + g, HIST_A + c0:HIST_A + c0 + rc, :] = zc(2) * zc(0)
            fb_ref[b * N_LG + g, HIST_B + c0:HIST_B + c0 + rc, :] = zc(4) * _sigmoid(zc(5))

    for b, c0, r0, rc in _seq_chunks(0, rows, tt):
        for g in range(N_LG):
            lg = slice(LANES * g, LANES * (g + 1))
            acc = None
            for k in range(CONV_A):
                off = HIST_A - (CONV_A - 1) + k + c0
                t = cwa_ref[k:k + 1, lg] * fa_ref[b * N_LG + g, off:off + rc, :]
                acc = t if acc is None else acc + t
            ca_ref[r0:r0 + rc, lg] = acc
            acc = None
            for k in range(CONV_B):
                off = HIST_B - (CONV_B - 1) + k + c0
                t = cwb_ref[k:k + 1, lg] * fb_ref[b * N_LG + g, off:off + rc, :]
                acc = t if acc is None else acc + t
            cb_ref[r0:r0 + rc, lg] = acc

    for b in range(nb):
        for g in range(N_LG):
            lg = slice(LANES * g, LANES * (g + 1))
            na_ref[b, :, lg] = fa_ref[b * N_LG + g, HIST_A + tt - (CONV_A - 1):HIST_A + tt, :]
            nb_ref[b, :, lg] = fb_ref[b * N_LG + g, HIST_B + tt - (CONV_B - 1):HIST_B + tt, :]
            if nt > 1:
                fa_ref[b * N_LG + g, 0:HIST_A, :] = fa_ref[b * N_LG + g, tt:tt + HIST_A, :]
                fb_ref[b * N_LG + g, 0:HIST_B, :] = fb_ref[b * N_LG + g, tt:tt + HIST_B, :]

    rc = min(rows, ROW_CHUNK)
    for r0 in range(0, rows, rc):
        def zs(c):
            return z_ref[r0:r0 + rc, MIX_W * c:MIX_W * (c + 1)]
        ya = zs(1) * ca_ref[r0:r0 + rc, :] * _silu(zs(3))
        mix_ref[r0:r0 + rc, 0:MIX_W] = ya.astype(BF16)
        yb = cb_ref[r0:r0 + rc, :] + cbb_ref[...]
        mu = jnp.mean(yb, axis=-1, keepdims=True)
        yc = yb - mu
        var = jnp.mean(yc * yc, axis=-1, keepdims=True)
        ln = yc * lax.rsqrt(var + LN_EPS) * lng_ref[...] + lnb_ref[...]
        mix_ref[r0:r0 + rc, MIX_W:2 * MIX_W] = (_silu(ln) * _silu(zs(6))).astype(BF16)

    _out_proj_task(mix_ref, wout_ref, o_ref, 0, rows)()
    for task in _residual_tasks(x_ref, mod_ref, 0, gpost_ref, o_ref, y_ref, 0, rows, tt):
        task()


def _even_layer(x2d, mod, st_a, st_b, g_pre, w_in, cw_a, cw_b, cb_b, ln_g, ln_b, w_out, g_post, *, n, t, nb, tt):
    rows = nb * tt
    nt = t // tt
    kern = functools.partial(_even_kernel, nb=nb, tt=tt, nt=nt)
    row_spec = pl.BlockSpec((rows, D_MODEL), lambda i, j: (i * nt + j, 0))

    def seq_spec(shape):
        return pl.BlockSpec((nb,) + shape, lambda i, j: (i, 0, 0))

    def const_spec(shape):
        return pl.BlockSpec(shape, lambda i, j: (0,) * len(shape))

    return pl.pallas_call(
        kern,
        out_shape=(
            jax.ShapeDtypeStruct((n * t, D_MODEL), F32),
            jax.ShapeDtypeStruct((n, CONV_A - 1, MIX_W), F32),
            jax.ShapeDtypeStruct((n, CONV_B - 1, MIX_W), F32),
        ),
        grid=(n // nb, nt),
        in_specs=[
            row_spec,
            seq_spec((1, 3 * D_MODEL)),
            seq_spec((CONV_A - 1, MIX_W)),
            seq_spec((CONV_B - 1, MIX_W)),
            const_spec((1, D_MODEL)),
            const_spec((D_MODEL, EVEN_IN)),
            const_spec((CONV_A, MIX_W)),
            const_spec((CONV_B, MIX_W)),
            const_spec((1, MIX_W)),
            const_spec((1, MIX_W)),
            const_spec((1, MIX_W)),
            const_spec((2 * MIX_W, D_MODEL)),
            const_spec((1, D_MODEL)),
        ],
        out_specs=(row_spec, seq_spec((CONV_A - 1, MIX_W)), seq_spec((CONV_B - 1, MIX_W))),
        scratch_shapes=[
            pltpu.VMEM((rows, D_MODEL), BF16),
            pltpu.VMEM((rows, EVEN_IN), F32),
            pltpu.VMEM((rows, D_MODEL), F32),
            pltpu.VMEM((nb * N_LG, HIST_A + tt, LANES), F32),
            pltpu.VMEM((nb * N_LG, HIST_B + tt, LANES), F32),
            pltpu.VMEM((rows, MIX_W), F32),
            pltpu.VMEM((rows, MIX_W), F32),
            pltpu.VMEM((rows, 2 * MIX_W), BF16),
        ],
        compiler_params=pltpu.CompilerParams(
            dimension_semantics=("arbitrary", "arbitrary"), vmem_limit_bytes=VMEM_LIMIT_BYTES),
        name="even_layer",
    )(x2d, mod, st_a, st_b, g_pre, w_in, cw_a, cw_b, cb_b, ln_g, ln_b, w_out, g_post)


def _odd_kernel(sinks_ref, x_ref, mod_ref, sc_ref, sk_ref, sv_ref, gpre_ref, win_ref, pw_ref, ps_ref,
                wout_ref, gpost_ref,
                y_ref, nc_ref, nk_ref, nv_ref,
                h_ref, z_ref, o_ref, fc_ref, kf_ref, ks_ref, vf_ref, vs_ref, pc_ref, od_ref, mix_ref,
                *, nb, tt, nt, pos0):
    rows = nb * tt
    j = pl.program_id(1)
    tile_pos = pos0 + j * tt

    @pl.when(j == 0)
    def _load_state():
        for b in range(nb):
            for g in range(N_LG):
                fc_ref[b * N_LG + g, 0:8, :] = jnp.zeros((8, LANES), F32)
                fc_ref[b * N_LG + g, HIST_C - POOL_PAST:HIST_C, :] = sc_ref[b, :, LANES * g:LANES * (g + 1)]
            kf_ref[b, 0:WINDOW, :] = sk_ref[b]
            vf_ref[b, 0:WINDOW, :] = sv_ref[b]
            ks_ref[b, 0:WINDOW, :] = pltpu.roll(sk_ref[b], HEAD_DIM, axis=1)
            vs_ref[b, 0:WINDOW, :] = pltpu.roll(sv_ref[b], HEAD_DIM, axis=1)

    qs = min(tt, WINDOW)
    gs = max(1, ATT_MIN_Q_ROWS // qs)
    kw = WINDOW + qs
    n_q, n_keys = gs * qs, gs * kw
    assert N_REP == 4 and 2 * HEAD_DIM == LANES and KV_W == LANES
    assert qs & (qs - 1) == 0 and gs & (gs - 1) == 0 and nb % gs == 0 and (gs == 1 or tt == qs)
    shape = (2 * n_q, n_keys)
    ri = lax.broadcasted_iota(jnp.int32, shape, 0)
    ci = lax.broadcasted_iota(jnp.int32, shape, 1)
    q_seq = (ri >> (qs.bit_length() - 1)) & (gs - 1)
    k_seq = jnp.zeros(shape, jnp.int32)
    for s_i in range(1, gs):
        k_seq = k_seq + (ci >= s_i * kw).astype(jnp.int32)
    k_c = ci - k_seq * kw
    dist = (ri & (qs - 1)) + WINDOW - k_c
    band = (q_seq == k_seq) & (dist >= 0) & (dist < WINDOW)
    dist_f = dist.astype(F32)
    upper_grp = ri >= n_q
    upper_grp_col = lax.broadcasted_iota(jnp.int32, (2 * n_q, 1), 0) >= n_q
    low_lanes = lax.broadcasted_iota(jnp.int32, (n_keys, LANES), 1) < HEAD_DIM
    ones_cols = jnp.ones((n_keys, LANES), BF16)

    def head_of(g, grp, parity):
        return g * N_REP + 2 * grp + parity

    def alibi_bias(g, parity):
        slope = jnp.where(upper_grp, ALIBI_SLOPES[head_of(g, 1, parity)], ALIBI_SLOPES[head_of(g, 0, parity)])
        return jnp.where(band, -(slope * dist_f), -jnp.inf)

    biases = {(g, parity): alibi_bias(g, parity) for g in range(N_KV) for parity in range(2)}

    def attention(b0, sb, g):
        r0 = b0 * tt + sb * qs
        p0 = tile_pos + sb * qs
        in_range = k_c >= WINDOW - p0
        if gs == 1:
            win = (b0, slice(sb * qs, sb * qs + kw))
            k_nat, k_swp, v_nat, v_swp = kf_ref[win], ks_ref[win], vf_ref[win], vs_ref[win]
        else:
            k_nat, k_swp, v_nat, v_swp = (r[b0:b0 + gs].reshape(n_keys, KV_W)
                                          for r in (kf_ref, ks_ref, vf_ref, vs_ref))
        k_lo, k_hi = (k_nat, k_swp) if g == 0 else (k_swp, k_nat)
        v_lo, v_hi = (v_nat, v_swp) if g == 0 else (v_swp, v_nat)
        v_ext = jnp.concatenate([jnp.where(low_lanes, v_lo, v_hi).astype(BF16), ones_cols], axis=1)
        qg = jnp.concatenate([z_ref[r0:r0 + n_q, Q_COL + 2 * LANES * g + LANES * grp:
                                      Q_COL + 2 * LANES * g + LANES * (grp + 1)] for grp in range(2)], axis=0)
        qg = (qg * (HEAD_DIM ** -0.5)).astype(BF16)
        for parity in range(2):
            k_ext = (jnp.where(low_lanes, k_lo, 0.0) if parity == 0 else jnp.where(low_lanes, 0.0, k_hi)).astype(BF16)
            sink = jnp.where(upper_grp_col, sinks_ref[head_of(g, 1, parity)], sinks_ref[head_of(g, 0, parity)])
            sc = lax.dot_general(qg, k_ext, (((1,), (1,)), ((), ())), preferred_element_type=F32)
            sc = sc + jnp.where(in_range, biases[g, parity], -jnp.inf)
            m = jnp.maximum(jnp.max(sc, axis=-1, keepdims=True), sink)
            p = jnp.exp(sc - m)
            o_ext = jnp.dot(p.astype(BF16), v_ext, preferred_element_type=F32)
            o = o_ext[:, 0:LANES] / (o_ext[:, LANES:2 * LANES] + jnp.exp(sink - m))
            for grp in range(2):
                h = head_of(g, grp, parity)
                od_ref[r0:r0 + n_q, HEAD_DIM * h:HEAD_DIM * (h + 1)] = (
                    o[grp * n_q:(grp + 1) * n_q, HEAD_DIM * parity:HEAD_DIM * (parity + 1)])

    def window_inputs(b, c0, r0, rc):
        for g in range(N_LG):
            fc_ref[b * N_LG + g, HIST_C + c0:HIST_C + c0 + rc, :] = z_ref[r0:r0 + rc, LANES * g:LANES * (g + 1)]
        k = z_ref[r0:r0 + rc, K_COL:K_COL + KV_W]
        v = z_ref[r0:r0 + rc, V_COL:V_COL + KV_W]
        kf_ref[b, WINDOW + c0:WINDOW + c0 + rc, :] = k
        vf_ref[b, WINDOW + c0:WINDOW + c0 + rc, :] = v
        ks_ref[b, WINDOW + c0:WINDOW + c0 + rc, :] = pltpu.roll(k, HEAD_DIM, axis=1)
        vs_ref[b, WINDOW + c0:WINDOW + c0 + rc, :] = pltpu.roll(v, HEAD_DIM, axis=1)

    def pooling(b, c0, r0, rc):
        pos = tile_pos + c0 + lax.broadcasted_iota(jnp.int32, (rc, LANES), 0)
        for g, w in enumerate(POOL_WINDOWS):
            u = fc_ref[b * N_LG + g, HIST_C + c0:HIST_C + c0 + rc, :]
            acc = u
            for d in range(1, w):
                acc = acc + fc_ref[b * N_LG + g, HIST_C + c0 - d:HIST_C + c0 - d + rc, :]
            cnt = jnp.minimum(w, pos + 1).astype(F32)
            pc_ref[r0:r0 + rc, LANES * g:LANES * (g + 1)] = acc / cnt - u

    def gating(r0, rc):
        for g in range(N_LG):
            lg = slice(LANES * g, LANES * (g + 1))
            mixed = jnp.dot(pc_ref[r0:r0 + rc, lg].astype(BF16), pw_ref[g], preferred_element_type=F32)
            yc = mixed * ps_ref[:, lg] * _silu(z_ref[r0:r0 + rc, MIX_W + LANES * g:MIX_W + LANES * (g + 1)])
            mix_ref[r0:r0 + rc, lg] = yc.astype(BF16)
        yd = od_ref[r0:r0 + rc, :] * _silu(z_ref[r0:r0 + rc, DG_COL:DG_COL + MIX_W])
        mix_ref[r0:r0 + rc, MIX_W:2 * MIX_W] = yd.astype(BF16)

    def mixer_tasks(lo, n):
        chunks = _seq_chunks(lo, n, tt)
        if gs == 1:
            blocks = [(r0 // tt, (r0 % tt) // qs) for r0 in range(lo, lo + n, qs)]
        else:
            blocks = [(b0, 0) for b0 in range(lo // tt, (lo + n) // tt, gs)]
        return ([functools.partial(window_inputs, *ch) for ch in chunks]
                + [functools.partial(pooling, *ch) for ch in chunks]
                + [functools.partial(attention, b0, sb, g) for b0, sb in blocks for g in range(N_KV)]
                + [functools.partial(gating, lo, n)])

    _pre_norm(x_ref, mod_ref, 0, gpre_ref, h_ref, rows, tt)
    for c0 in range(0, ODD_IN, MXU_COLS):
        z_ref[:, c0:c0 + MXU_COLS] = jnp.dot(h_ref[...], win_ref[:, c0:c0 + MXU_COLS], preferred_element_type=F32)

    n_chunks = max(1, rows // WIN_CHUNK)
    n = rows // n_chunks
    for c in range(n_chunks + 1):
        if c > 0:
            _out_proj_task(mix_ref, wout_ref, o_ref, (c - 1) * n, n)()
        if c < n_chunks:
            for task in mixer_tasks(c * n, n):
                task()
        if c > 0:
            for task in _residual_tasks(x_ref, mod_ref, 0, gpost_ref, o_ref, y_ref, (c - 1) * n, n, tt):
                task()

    for b in range(nb):
        for g in range(N_LG):
            nc_ref[b, :, LANES * g:LANES * (g + 1)] = fc_ref[b * N_LG + g, HIST_C + tt - POOL_PAST:HIST_C + tt, :]
            if nt > 1:
                fc_ref[b * N_LG + g, 0:HIST_C, :] = fc_ref[b * N_LG + g, tt:tt + HIST_C, :]
        nk_ref[b] = kf_ref[b, tt:tt + WINDOW, :]
        nv_ref[b] = vf_ref[b, tt:tt + WINDOW, :]
        if nt > 1:
            for r in (kf_ref, ks_ref, vf_ref, vs_ref):
                r[b, 0:WINDOW, :] = r[b, tt:tt + WINDOW, :]


def _odd_layer(x2d, mod, st_c, st_k, st_v, g_pre, w_in, pool_w, pool_scale, sinks, w_out, g_post,
               *, n, t, nb, tt, pos0):
    rows = nb * tt
    nt = t // tt
    kern = functools.partial(_odd_kernel, nb=nb, tt=tt, nt=nt, pos0=pos0)
    row_spec = pl.BlockSpec((rows, D_MODEL), lambda i, j: (i * nt + j, 0))

    def seq_spec(shape):
        return pl.BlockSpec((nb,) + shape, lambda i, j: (i, 0, 0))

    def const_spec(shape):
        return pl.BlockSpec(shape, lambda i, j: (0,) * len(shape))

    return pl.pallas_call(
        kern,
        out_shape=(
            jax.ShapeDtypeStruct((n * t, D_MODEL), F32),
            jax.ShapeDtypeStruct((n, POOL_PAST, MIX_W), F32),
            jax.ShapeDtypeStruct((n, WINDOW, KV_W), F32),
            jax.ShapeDtypeStruct((n, WINDOW, KV_W), F32),
        ),
        grid=(n // nb, nt),
        in_specs=[
            pl.BlockSpec(memory_space=pltpu.SMEM),
            row_spec,
            seq_spec((1, 3 * D_MODEL)),
            seq_spec((POOL_PAST, MIX_W)),
            seq_spec((WINDOW, KV_W)),
            seq_spec((WINDOW, KV_W)),
            const_spec((1, D_MODEL)),
            const_spec((D_MODEL, ODD_IN)),
            const_spec((N_LG, LANES, LANES)),
            const_spec((1, MIX_W)),
            const_spec((2 * MIX_W, D_MODEL)),
            const_spec((1, D_MODEL)),
        ],
        out_specs=(row_spec, seq_spec((POOL_PAST, MIX_W)), seq_spec((WINDOW, KV_W)), seq_spec((WINDOW, KV_W))),
        scratch_shapes=[
            pltpu.VMEM((rows, D_MODEL), BF16),
            pltpu.VMEM((rows, ODD_IN), F32),
            pltpu.VMEM((rows, D_MODEL), F32),
            pltpu.VMEM((nb * N_LG, HIST_C + tt, LANES), F32),
            pltpu.VMEM((nb, WINDOW + tt, KV_W), F32),
            pltpu.VMEM((nb, WINDOW + tt, KV_W), F32),
            pltpu.VMEM((nb, WINDOW + tt, KV_W), F32),
            pltpu.VMEM((nb, WINDOW + tt, KV_W), F32),
            pltpu.VMEM((rows, MIX_W), F32),
            pltpu.VMEM((rows, MIX_W), F32),
            pltpu.VMEM((rows, 2 * MIX_W), BF16),
        ],
        compiler_params=pltpu.CompilerParams(
            dimension_semantics=("arbitrary", "arbitrary"), vmem_limit_bytes=VMEM_LIMIT_BYTES),
        name="odd_layer",
    )(sinks, x2d, mod, st_c, st_k, st_v, g_pre, w_in, pool_w, pool_scale, w_out, g_post)


def _tiling(n, t):
    if t >= 512:
        return 1, 512
    return min(n, 256 // t), t


def _trunk(x, mod_e, mod_o, st_a, st_b, st_c, st_k, st_v, pos0, we, wo):
    n, t, _ = x.shape
    nb, tt = _tiling(n, t)
    x2d = x.reshape(n * t, D_MODEL)
    x2d, na, nbs = _even_layer(x2d, mod_e.reshape(n, 1, -1), st_a, st_b, *we, n=n, t=t, nb=nb, tt=tt)
    x2d, nc, nk, nv = _odd_layer(x2d, mod_o.reshape(n, 1, -1), st_c, st_k.reshape(n, WINDOW, KV_W),
                                 st_v.reshape(n, WINDOW, KV_W), *wo, n=n, t=t, nb=nb, tt=tt, pos0=pos0)
    kv_shape = (1, n, WINDOW, N_KV, HEAD_DIM)
    return (x2d.reshape(n, t, D_MODEL), na[None], nbs[None], nc[None],
            nk.reshape(kv_shape), nv.reshape(kv_shape))


def kernel(x_prompt, x_sample, state_conv_a, state_conv_b, state_pool_c, cache_win_k, cache_win_v, c_prompt, c_sample, w_mod_e, b_mod_e, g_pre_e, g_post_e, w_in_e, conv_a_w, conv_b_w, conv_b_b, ln_b_g, ln_b_b, w_out_e, w_mod_o, b_mod_o, g_pre_o, g_post_o, w_in_o, pool_w, pool_scale, sinks, w_out_o):
    n_p = x_prompt.shape[0]
    assert w_in_e.shape[0] == 1 and w_in_o.shape[0] == 1, "one even and one odd layer"

    c_all = jnp.concatenate([c_prompt, c_sample], axis=0)
    mod_e = _adaln_mod(c_all, w_mod_e[0], b_mod_e[0])
    mod_o = _adaln_mod(c_all, w_mod_o[0], b_mod_o[0])

    we = (g_pre_e, w_in_e[0].astype(BF16), conv_a_w[0], conv_b_w[0], conv_b_b, ln_b_g, ln_b_b,
          w_out_e[0].astype(BF16), g_post_e)
    wo = (g_pre_o, w_in_o[0].astype(BF16), pool_w[0].astype(BF16), pool_scale, sinks[0],
          w_out_o[0].astype(BF16), g_post_o)

    dt = x_prompt.dtype
    z_a = jnp.zeros((n_p, CONV_A - 1, MIX_W), dt)
    z_b = jnp.zeros((n_p, CONV_B - 1, MIX_W), dt)
    z_c = jnp.zeros((n_p, POOL_PAST, MIX_W), dt)
    z_kv = jnp.zeros((n_p, WINDOW, N_KV, HEAD_DIM), dt)
    y_p, pa, pb, pc, pk, pv = _trunk(x_prompt, mod_e[:n_p], mod_o[:n_p], z_a, z_b, z_c, z_kv, z_kv, 0, we, wo)
    y_s, sa, sb, sc, sk, sv = _trunk(x_sample, mod_e[n_p:], mod_o[n_p:], state_conv_a[0], state_conv_b[0],
                                     state_pool_c[0], cache_win_k[0], cache_win_v[0], PAST_LEN, we, wo)
    return (y_p, y_s, pa, sa, pb, sb, pc, sc, pk, sk, pv, sv)
```

```python
import functools

import jax
import jax.numpy as jnp
from jax import lax
from jax.experimental import pallas as pl
from jax.experimental.pallas import tpu as pltpu

F32 = jnp.float32
BF16 = jnp.bfloat16

D_MODEL = 1024
MIX_W = 512
LANES = 128
MXU_COLS = 256
N_LG = MIX_W // LANES
HEAD_DIM = 64
N_HEADS = 8
N_KV = 2
N_REP = N_HEADS // N_KV
KV_W = N_KV * HEAD_DIM
CONV_A = 3
CONV_B = 31
POOL_WINDOWS = (2, 4, 8, 16)
POOL_PAST = 15
WINDOW = 128
PAST_LEN = 8192
EVEN_IN = 7 * MIX_W
ODD_IN = 4 * MIX_W + 2 * KV_W
Q_COL, K_COL, V_COL, DG_COL = 2 * MIX_W, 3 * MIX_W, 3 * MIX_W + KV_W, 3 * MIX_W + 2 * KV_W
RMS_EPS = 1e-6
LN_EPS = 1e-5
HIST_A = 8
HIST_B = 32
HIST_C = 16
ATT_MIN_Q_ROWS = 32
ROW_CHUNK = 64
WIN_CHUNK = 128
MOD_K_BLOCK = 256
VMEM_LIMIT_BYTES = 56 * 1024 * 1024
ALIBI_SLOPES = tuple(float(2.0 ** (-8.0 * (h + 1) / N_HEADS)) for h in range(N_HEADS))


def _sigmoid(x):
    return 1.0 / (1.0 + jnp.exp(-x))


def _silu(x):
    return x * _sigmoid(x)


def _mod_kernel(c_ref, w_ref, b_ref, o_ref):
    k = pl.program_id(0)
    part = jnp.dot(_silu(c_ref[...]).astype(BF16), w_ref[...].astype(BF16), preferred_element_type=F32)

    @pl.when(k == 0)
    def _first():
        o_ref[...] = part + b_ref[...]

    @pl.when(k > 0)
    def _rest():
        o_ref[...] += part


def _adaln_mod(c_all, w_mod, b_mod):
    n = c_all.shape[0]
    return pl.pallas_call(
        _mod_kernel,
        out_shape=jax.ShapeDtypeStruct((n, 3 * D_MODEL), F32),
        grid=(D_MODEL // MOD_K_BLOCK,),
        in_specs=[
            pl.BlockSpec((n, MOD_K_BLOCK), lambda k: (0, k)),
            pl.BlockSpec((MOD_K_BLOCK, 3 * D_MODEL), lambda k: (k, 0)),
            pl.BlockSpec((1, 3 * D_MODEL), lambda k: (0, 0)),
        ],
        out_specs=pl.BlockSpec((n, 3 * D_MODEL), lambda k: (0, 0)),
        compiler_params=pltpu.CompilerParams(dimension_semantics=("arbitrary",)),
        name="adaln_mod",
    )(c_all, w_mod, b_mod.reshape(1, -1))


def _mod_rows(mod_ref, seq0, col, r0, rc, tt):
    if tt >= rc:
        return mod_ref[pl.ds(seq0 + r0 // tt, 1), col:col + D_MODEL]
    parts = [jnp.broadcast_to(mod_ref[pl.ds(seq0 + b, 1), col:col + D_MODEL], (tt, D_MODEL))
             for b in range(r0 // tt, (r0 + rc) // tt)]
    return jnp.concatenate(parts, axis=0)


def _pre_norm(x_ref, mod_ref, seq0, g_ref, h_ref, rows, tt):
    rc = min(rows, ROW_CHUNK)
    g = g_ref[...]
    for r0 in range(0, rows, rc):
        x = x_ref[r0:r0 + rc, :]
        ms = jnp.mean(x * x, axis=-1, keepdims=True)
        h = x * lax.rsqrt(ms + RMS_EPS) * g
        h = h * (1.0 + _mod_rows(mod_ref, seq0, D_MODEL, r0, rc, tt)) + _mod_rows(mod_ref, seq0, 0, r0, rc, tt)
        h_ref[r0:r0 + rc, :] = h.astype(BF16)


def _out_proj_task(mix_ref, wout_ref, o_ref, lo, n):
    def task():
        o_ref[lo:lo + n, :] = jnp.dot(mix_ref[lo:lo + n, :], wout_ref[...], preferred_element_type=F32)
    return task


def _residual_tasks(x_ref, mod_ref, seq0, g_ref, o_ref, y_ref, lo, n, tt):
    rc = min(n, ROW_CHUNK)

    def chunk(r0):
        o = o_ref[r0:r0 + rc, :]
        ms = jnp.mean(o * o, axis=-1, keepdims=True)
        nrm = o * lax.rsqrt(ms + RMS_EPS) * g_ref[...]
        y_ref[r0:r0 + rc, :] = x_ref[r0:r0 + rc, :] + _mod_rows(mod_ref, seq0, 2 * D_MODEL, r0, rc, tt) * nrm
    return [functools.partial(chunk, r0) for r0 in range(lo, lo + n, rc)]


def _seq_chunks(lo, n, tt):
    rc = min(tt, n, WIN_CHUNK)
    return [(r0 // tt, r0 % tt, r0, rc) for r0 in range(lo, lo + n, rc)]


def _even_kernel(x_ref, mod_ref, sa_ref, sb_ref, gpre_ref, win_ref, cwa_ref, cwb_ref, cbb_ref,
                 lng_ref, lnb_ref, wout_ref, gpost_ref,
                 y_ref, na_ref, nb_ref,
                 h_ref, z_ref, o_ref, fa_ref, fb_ref, ca_ref, cb_ref, mix_ref, *, nb, tt, nt, seq_base):
    rows = nb * tt
    j = pl.program_id(1)
    seq0 = seq_base + pl.program_id(0) * nb

    @pl.when(j == 0)
    def _load_state():
        for b in range(nb):
            for g in range(N_LG):
                lg = slice(LANES * g, LANES * (g + 1))
                fa_ref[b * N_LG + g, 0:HIST_A, :] = jnp.zeros((HIST_A, LANES), F32)
                fa_ref[b * N_LG + g, HIST_A - (CONV_A - 1):HIST_A, :] = sa_ref[b, :, lg]
                fb_ref[b * N_LG + g, 0:8, :] = jnp.zeros((8, LANES), F32)
                fb_ref[b * N_LG + g, HIST_B - (CONV_B - 1):HIST_B, :] = sb_ref[b, :, lg]

    _pre_norm(x_ref, mod_ref, seq0, gpre_ref, h_ref, rows, tt)

    hb = h_ref[...]
    for c in range(EVEN_IN // MIX_W):
        cols = slice(MIX_W * c, MIX_W * (c + 1))
        z_ref[:, cols] = jnp.dot(hb, win_ref[:, cols], preferred_element_type=F32)

    for b, c0, r0, rc in _seq_chunks(0, rows, tt):
        for g in range(N_LG):
            def zc(c):
                return z_ref[r0:r0 + rc, MIX_W * c + LANES * g:MIX_W * c + LANES * (g + 1)]
            fa_ref[b * N_LG + g, HIST_A + c0:HIST_A + c0 + rc, :] = zc(2) * zc(0)
            fb_ref[b * N_LG + g, HIST_B + c0:HIST_B + c0 + rc, :] = zc(4) * _sigmoid(zc(5))

    for b, c0, r0, rc in _seq_chunks(0, rows, tt):
        for g in range(N_LG):
            lg = slice(LANES * g, LANES * (g + 1))
            acc = None
            for k in range(CONV_A):
                off = HIST_A - (CONV_A - 1) + k + c0
                t = cwa_ref[k:k + 1, lg] * fa_ref[b * N_LG + g, off:off + rc, :]
                acc = t if acc is None else acc + t
            ca_ref[r0:r0 + rc, lg] = acc
            acc = None
            for k in range(CONV_B):
                off = HIST_B - (CONV_B - 1) + k + c0
                t = cwb_ref[k:k + 1, lg] * fb_ref[b * N_LG + g, off:off + rc, :]
                acc = t if acc is None else acc + t
            cb_ref[r0:r0 + rc, lg] = acc

    for b in range(nb):
        for g in range(N_LG):
            lg = slice(LANES * g, LANES * (g + 1))
            na_ref[b, :, lg] = fa_ref[b * N_LG + g, HIST_A + tt - (CONV_A - 1):HIST_A + tt, :]
            nb_ref[b, :, lg] = fb_ref[b * N_LG + g, HIST_B + tt - (CONV_B - 1):HIST_B + tt, :]
            if nt > 1:
                fa_ref[b * N_LG + g, 0:HIST_A, :] = fa_ref[b * N_LG + g, tt:tt + HIST_A, :]
                fb_ref[b * N_LG + g, 0:HIST_B, :] = fb_ref[b * N_LG + g, tt:tt + HIST_B, :]

    rc = min(rows, ROW_CHUNK)
    for r0 in range(0, rows, rc):
        def zs(c):
            return z_ref[r0:r0 + rc, MIX_W * c:MIX_W * (c + 1)]
        ya = zs(1) * ca_ref[r0:r0 + rc, :] * _silu(zs(3))
        mix_ref[r0:r0 + rc, 0:MIX_W] = ya.astype(BF16)
        yb = cb_ref[r0:r0 + rc, :] + cbb_ref[...]
        mu = jnp.mean(yb, axis=-1, keepdims=True)
        yc = yb - mu
        var = jnp.mean(yc * yc, axis=-1, keepdims=True)
        ln = yc * lax.rsqrt(var + LN_EPS) * lng_ref[...] + lnb_ref[...]
        mix_ref[r0:r0 + rc, MIX_W:2 * MIX_W] = (_silu(ln) * _silu(zs(6))).astype(BF16)

    _out_proj_task(mix_ref, wout_ref, o_ref, 0, rows)()
    for task in _residual_tasks(x_ref, mod_ref, seq0, gpost_ref, o_ref, y_ref, 0, rows, tt):
        task()


def _even_layer(x2d, mod, st_a, st_b, g_pre, w_in, cw_a, cw_b, cb_b, ln_g, ln_b, w_out, g_post,
                *, n, t, nb, tt, seq_base):
    rows = nb * tt
    nt = t // tt
    kern = functools.partial(_even_kernel, nb=nb, tt=tt, nt=nt, seq_base=seq_base)
    row_spec = pl.BlockSpec((rows, D_MODEL), lambda i, j: (i * nt + j, 0))

    def seq_spec(shape):
        return pl.BlockSpec((nb,) + shape, lambda i, j: (i, 0, 0))

    def const_spec(shape):
        return pl.BlockSpec(shape, lambda i, j: (0,) * len(shape))

    return pl.pallas_call(
        kern,
        out_shape=(
            jax.ShapeDtypeStruct((n * t, D_MODEL), F32),
            jax.ShapeDtypeStruct((n, CONV_A - 1, MIX_W), F32),
            jax.ShapeDtypeStruct((n, CONV_B - 1, MIX_W), F32),
        ),
        grid=(n // nb, nt),
        in_specs=[
            row_spec,
            const_spec(mod.shape),
            seq_spec((CONV_A - 1, MIX_W)),
            seq_spec((CONV_B - 1, MIX_W)),
            const_spec((1, D_MODEL)),
            const_spec((D_MODEL, EVEN_IN)),
            const_spec((CONV_A, MIX_W)),
            const_spec((CONV_B, MIX_W)),
            const_spec((1, MIX_W)),
            const_spec((1, MIX_W)),
            const_spec((1, MIX_W)),
            const_spec((2 * MIX_W, D_MODEL)),
            const_spec((1, D_MODEL)),
        ],
        out_specs=(row_spec, seq_spec((CONV_A - 1, MIX_W)), seq_spec((CONV_B - 1, MIX_W))),
        scratch_shapes=[
            pltpu.VMEM((rows, D_MODEL), BF16),
            pltpu.VMEM((rows, EVEN_IN), F32),
            pltpu.VMEM((rows, D_MODEL), F32),
            pltpu.VMEM((nb * N_LG, HIST_A + tt, LANES), F32),
            pltpu.VMEM((nb * N_LG, HIST_B + tt, LANES), F32),
            pltpu.VMEM((rows, MIX_W), F32),
            pltpu.VMEM((rows, MIX_W), F32),
            pltpu.VMEM((rows, 2 * MIX_W), BF16),
        ],
        compiler_params=pltpu.CompilerParams(
            dimension_semantics=("arbitrary", "arbitrary"), vmem_limit_bytes=VMEM_LIMIT_BYTES),
        name="even_layer",
    )(x2d, mod, st_a, st_b, g_pre, w_in, cw_a, cw_b, cb_b, ln_g, ln_b, w_out, g_post)


def _odd_kernel(sinks_ref, x_ref, mod_ref, sc_ref, sk_ref, sv_ref, gpre_ref, win_ref, pw_ref, ps_ref,
                wout_ref, gpost_ref,
                y_ref, nc_ref, nk_ref, nv_ref,
                h_ref, z_ref, o_ref, fc_ref, kf_ref, ks_ref, vf_ref, vs_ref, pc_ref, od_ref, mix_ref,
                *, nb, tt, nt, pos0, seq_base):
    rows = nb * tt
    j = pl.program_id(1)
    seq0 = seq_base + pl.program_id(0) * nb
    tile_pos = pos0 + j * tt

    @pl.when(j == 0)
    def _load_state():
        for b in range(nb):
            for g in range(N_LG):
                fc_ref[b * N_LG + g, 0:8, :] = jnp.zeros((8, LANES), F32)
                fc_ref[b * N_LG + g, HIST_C - POOL_PAST:HIST_C, :] = sc_ref[b, :, LANES * g:LANES * (g + 1)]
            kf_ref[b, 0:WINDOW, :] = sk_ref[b]
            vf_ref[b, 0:WINDOW, :] = sv_ref[b]
            ks_ref[b, 0:WINDOW, :] = pltpu.roll(sk_ref[b], HEAD_DIM, axis=1)
            vs_ref[b, 0:WINDOW, :] = pltpu.roll(sv_ref[b], HEAD_DIM, axis=1)

    qs = min(tt, WINDOW)
    gs = max(1, ATT_MIN_Q_ROWS // qs)
    kw = WINDOW + qs
    n_q, n_keys = gs * qs, gs * kw
    assert N_REP == 4 and 2 * HEAD_DIM == LANES and KV_W == LANES
    assert qs & (qs - 1) == 0 and gs & (gs - 1) == 0 and nb % gs == 0 and (gs == 1 or tt == qs)
    shape = (2 * n_q, n_keys)
    ri = lax.broadcasted_iota(jnp.int32, shape, 0)
    ci = lax.broadcasted_iota(jnp.int32, shape, 1)
    q_seq = (ri >> (qs.bit_length() - 1)) & (gs - 1)
    k_seq = jnp.zeros(shape, jnp.int32)
    for s_i in range(1, gs):
        k_seq = k_seq + (ci >= s_i * kw).astype(jnp.int32)
    k_c = ci - k_seq * kw
    dist = (ri & (qs - 1)) + WINDOW - k_c
    band = (q_seq == k_seq) & (dist >= 0) & (dist < WINDOW)
    dist_f = dist.astype(F32)
    upper_grp = ri >= n_q
    upper_grp_col = lax.broadcasted_iota(jnp.int32, (2 * n_q, 1), 0) >= n_q
    low_lanes = lax.broadcasted_iota(jnp.int32, (n_keys, LANES), 1) < HEAD_DIM
    ones_cols = jnp.ones((n_keys, LANES), BF16)

    def head_of(g, grp, parity):
        return g * N_REP + 2 * grp + parity

    def alibi_bias(g, parity):
        slope = jnp.where(upper_grp, ALIBI_SLOPES[head_of(g, 1, parity)], ALIBI_SLOPES[head_of(g, 0, parity)])
        return jnp.where(band, -(slope * dist_f), -jnp.inf)

    biases = {(g, parity): alibi_bias(g, parity) for g in range(N_KV) for parity in range(2)}

    def attention(b0, sb, g):
        r0 = b0 * tt + sb * qs
        p0 = tile_pos + sb * qs
        in_range = k_c >= WINDOW - p0
        if gs == 1:
            win = (b0, slice(sb * qs, sb * qs + kw))
            k_nat, k_swp, v_nat, v_swp = kf_ref[win], ks_ref[win], vf_ref[win], vs_ref[win]
        else:
            k_nat, k_swp, v_nat, v_swp = (r[b0:b0 + gs].reshape(n_keys, KV_W)
                                          for r in (kf_ref, ks_ref, vf_ref, vs_ref))
        k_lo, k_hi = (k_nat, k_swp) if g == 0 else (k_swp, k_nat)
        v_lo, v_hi = (v_nat, v_swp) if g == 0 else (v_swp, v_nat)
        v_ext = jnp.concatenate([jnp.where(low_lanes, v_lo, v_hi).astype(BF16), ones_cols], axis=1)
        qg = jnp.concatenate([z_ref[r0:r0 + n_q, Q_COL + 2 * LANES * g + LANES * grp:
                                      Q_COL + 2 * LANES * g + LANES * (grp + 1)] for grp in range(2)], axis=0)
        qg = (qg * (HEAD_DIM ** -0.5)).astype(BF16)
        for parity in range(2):
            k_ext = (jnp.where(low_lanes, k_lo, 0.0) if parity == 0 else jnp.where(low_lanes, 0.0, k_hi)).astype(BF16)
            sink = jnp.where(upper_grp_col, sinks_ref[head_of(g, 1, parity)], sinks_ref[head_of(g, 0, parity)])
            sc = lax.dot_general(qg, k_ext, (((1,), (1,)), ((), ())), preferred_element_type=F32)
            sc = sc + jnp.where(in_range, biases[g, parity], -jnp.inf)
            m = jnp.maximum(jnp.max(sc, axis=-1, keepdims=True), sink)
            p = jnp.exp(sc - m)
            o_ext = jnp.dot(p.astype(BF16), v_ext, preferred_element_type=F32)
            o = o_ext[:, 0:LANES] / (o_ext[:, LANES:2 * LANES] + jnp.exp(sink - m))
            for grp in range(2):
                h = head_of(g, grp, parity)
                od_ref[r0:r0 + n_q, HEAD_DIM * h:HEAD_DIM * (h + 1)] = (
                    o[grp * n_q:(grp + 1) * n_q, HEAD_DIM * parity:HEAD_DIM * (parity + 1)])

    def window_inputs(b, c0, r0, rc):
        for g in range(N_LG):
            fc_ref[b * N_LG + g, HIST_C + c0:HIST_C + c0 + rc, :] = z_ref[r0:r0 + rc, LANES * g:LANES * (g + 1)]
        k = z_ref[r0:r0 + rc, K_COL:K_COL + KV_W]
        v = z_ref[r0:r0 + rc, V_COL:V_COL + KV_W]
        kf_ref[b, WINDOW + c0:WINDOW + c0 + rc, :] = k
        vf_ref[b, WINDOW + c0:WINDOW + c0 + rc, :] = v
        ks_ref[b, WINDOW + c0:WINDOW + c0 + rc, :] = pltpu.roll(k, HEAD_DIM, axis=1)
        vs_ref[b, WINDOW + c0:WINDOW + c0 + rc, :] = pltpu.roll(v, HEAD_DIM, axis=1)

    def pooling(b, c0, r0, rc):
        pos = tile_pos + c0 + lax.broadcasted_iota(jnp.int32, (rc, LANES), 0)
        for g, w in enumerate(POOL_WINDOWS):
            u = fc_ref[b * N_LG + g, HIST_C + c0:HIST_C + c0 + rc, :]
            acc = u
            for d in range(1, w):
                acc = acc + fc_ref[b * N_LG + g, HIST_C + c0 - d:HIST_C + c0 - d + rc, :]
            cnt = jnp.minimum(w, pos + 1).astype(F32)
            pc_ref[r0:r0 + rc, LANES * g:LANES * (g + 1)] = acc / cnt - u

    def gating(r0, rc):
        for g in range(N_LG):
            lg = slice(LANES * g, LANES * (g + 1))
            mixed = jnp.dot(pc_ref[r0:r0 + rc, lg].astype(BF16), pw_ref[g], preferred_element_type=F32)
            yc = mixed * ps_ref[:, lg] * _silu(z_ref[r0:r0 + rc, MIX_W + LANES * g:MIX_W + LANES * (g + 1)])
            mix_ref[r0:r0 + rc, lg] = yc.astype(BF16)
        yd = od_ref[r0:r0 + rc, :] * _silu(z_ref[r0:r0 + rc, DG_COL:DG_COL + MIX_W])
        mix_ref[r0:r0 + rc, MIX_W:2 * MIX_W] = yd.astype(BF16)

    def mixer_tasks(lo, n):
        chunks = _seq_chunks(lo, n, tt)
        if gs == 1:
            blocks = [(r0 // tt, (r0 % tt) // qs) for r0 in range(lo, lo + n, qs)]
        else:
            blocks = [(b0, 0) for b0 in range(lo // tt, (lo + n) // tt, gs)]
        return ([functools.partial(window_inputs, *ch) for ch in chunks]
                + [functools.partial(pooling, *ch) for ch in chunks]
                + [functools.partial(attention, b0, sb, g) for b0, sb in blocks for g in range(N_KV)]
                + [functools.partial(gating, lo, n)])

    _pre_norm(x_ref, mod_ref, seq0, gpre_ref, h_ref, rows, tt)
    for c0 in range(0, ODD_IN, MXU_COLS):
        z_ref[:, c0:c0 + MXU_COLS] = jnp.dot(h_ref[...], win_ref[:, c0:c0 + MXU_COLS], preferred_element_type=F32)

    n_chunks = max(1, rows // WIN_CHUNK)
    n = rows // n_chunks
    for c in range(n_chunks + 1):
        if c > 0:
            _out_proj_task(mix_ref, wout_ref, o_ref, (c - 1) * n, n)()
        if c < n_chunks:
            for task in mixer_tasks(c * n, n):
                task()
        if c > 0:
            for task in _residual_tasks(x_ref, mod_ref, seq0, gpost_ref, o_ref, y_ref, (c - 1) * n, n, tt):
                task()

    for b in range(nb):
        for g in range(N_LG):
            nc_ref[b, :, LANES * g:LANES * (g + 1)] = fc_ref[b * N_LG + g, HIST_C + tt - POOL_PAST:HIST_C + tt, :]
            if nt > 1:
                fc_ref[b * N_LG + g, 0:HIST_C, :] = fc_ref[b * N_LG + g, tt:tt + HIST_C, :]
        nk_ref[b] = kf_ref[b, tt:tt + WINDOW, :]
        nv_ref[b] = vf_ref[b, tt:tt + WINDOW, :]
        if nt > 1:
            for r in (kf_ref, ks_ref, vf_ref, vs_ref):
                r[b, 0:WINDOW, :] = r[b, tt:tt + WINDOW, :]


def _odd_layer(x2d, mod, st_c, st_k, st_v, g_pre, w_in, pool_w, pool_scale, sinks, w_out, g_post,
               *, n, t, nb, tt, pos0, seq_base):
    rows = nb * tt
    nt = t // tt
    kern = functools.partial(_odd_kernel, nb=nb, tt=tt, nt=nt, pos0=pos0, seq_base=seq_base)
    row_spec = pl.BlockSpec((rows, D_MODEL), lambda i, j: (i * nt + j, 0))

    def seq_spec(shape):
        return pl.BlockSpec((nb,) + shape, lambda i, j: (i, 0, 0))

    def const_spec(shape):
        return pl.BlockSpec(shape, lambda i, j: (0,) * len(shape))

    return pl.pallas_call(
        kern,
        out_shape=(
            jax.ShapeDtypeStruct((n * t, D_MODEL), F32),
            jax.ShapeDtypeStruct((n, POOL_PAST, MIX_W), F32),
            jax.ShapeDtypeStruct((n, WINDOW, KV_W), F32),
            jax.ShapeDtypeStruct((n, WINDOW, KV_W), F32),
        ),
        grid=(n // nb, nt),
        in_specs=[
            pl.BlockSpec(memory_space=pltpu.SMEM),
            row_spec,
            const_spec(mod.shape),
            seq_spec((POOL_PAST, MIX_W)),
            seq_spec((WINDOW, KV_W)),
            seq_spec((WINDOW, KV_W)),
            const_spec((1, D_MODEL)),
            const_spec((D_MODEL, ODD_IN)),
            const_spec((N_LG, LANES, LANES)),
            const_spec((1, MIX_W)),
            const_spec((2 * MIX_W, D_MODEL)),
            const_spec((1, D_MODEL)),
        ],
        out_specs=(row_spec, seq_spec((POOL_PAST, MIX_W)), seq_spec((WINDOW, KV_W)), seq_spec((WINDOW, KV_W))),
        scratch_shapes=[
            pltpu.VMEM((rows, D_MODEL), BF16),
            pltpu.VMEM((rows, ODD_IN), F32),
            pltpu.VMEM((rows, D_MODEL), F32),
            pltpu.VMEM((nb * N_LG, HIST_C + tt, LANES), F32),
            pltpu.VMEM((nb, WINDOW + tt, KV_W), F32),
            pltpu.VMEM((nb, WINDOW + tt, KV_W), F32),
            pltpu.VMEM((nb, WINDOW + tt, KV_W), F32),
            pltpu.VMEM((nb, WINDOW + tt, KV_W), F32),
            pltpu.VMEM((rows, MIX_W), F32),
            pltpu.VMEM((rows, MIX_W), F32),
            pltpu.VMEM((rows, 2 * MIX_W), BF16),
        ],
        compiler_params=pltpu.CompilerParams(
            dimension_semantics=("arbitrary", "arbitrary"), vmem_limit_bytes=VMEM_LIMIT_BYTES),
        name="odd_layer",
    )(sinks, x2d, mod, st_c, st_k, st_v, g_pre, w_in, pool_w, pool_scale, w_out, g_post)


def _tiling(n, t):
    if t >= 512:
        return 1, 512
    return min(n, 256 // t), t


def _trunk(x, mod_e, mod_o, seq_base, st_a, st_b, st_c, st_k, st_v, pos0, we, wo):
    n, t, _ = x.shape
    nb, tt = _tiling(n, t)
    x2d = x.reshape(n * t, D_MODEL)
    x2d, na, nbs = _even_layer(x2d, mod_e, st_a, st_b, *we, n=n, t=t, nb=nb, tt=tt, seq_base=seq_base)
    x2d, nc, nk, nv = _odd_layer(x2d, mod_o, st_c, st_k.reshape(n, WINDOW, KV_W),
                                 st_v.reshape(n, WINDOW, KV_W), *wo, n=n, t=t, nb=nb, tt=tt, pos0=pos0,
                                 seq_base=seq_base)
    kv_shape = (1, n, WINDOW, N_KV, HEAD_DIM)
    return (x2d.reshape(n, t, D_MODEL), na[None], nbs[None], nc[None],
            nk.reshape(kv_shape), nv.reshape(kv_shape))


def kernel(x_prompt, x_sample, state_conv_a, state_conv_b, state_pool_c, cache_win_k, cache_win_v, c_prompt, c_sample, w_mod_e, b_mod_e, g_pre_e, g_post_e, w_in_e, conv_a_w, conv_b_w, conv_b_b, ln_b_g, ln_b_b, w_out_e, w_mod_o, b_mod_o, g_pre_o, g_post_o, w_in_o, pool_w, pool_scale, sinks, w_out_o):
    n_p = x_prompt.shape[0]
    assert w_in_e.shape[0] == 1 and w_in_o.shape[0] == 1, "one even and one odd layer"

    c_all = jnp.concatenate([c_prompt, c_sample], axis=0)
    mod_e = _adaln_mod(c_all, w_mod_e[0], b_mod_e[0])
    mod_o = _adaln_mod(c_all, w_mod_o[0], b_mod_o[0])

    we = (g_pre_e, w_in_e[0].astype(BF16), conv_a_w[0], conv_b_w[0], conv_b_b, ln_b_g, ln_b_b,
          w_out_e[0].astype(BF16), g_post_e)
    wo = (g_pre_o, w_in_o[0].astype(BF16), pool_w[0].astype(BF16), pool_scale, sinks[0],
          w_out_o[0].astype(BF16), g_post_o)

    dt = x_prompt.dtype
    z_a = jnp.zeros((n_p, CONV_A - 1, MIX_W), dt)
    z_b = jnp.zeros((n_p, CONV_B - 1, MIX_W), dt)
    z_c = jnp.zeros((n_p, POOL_PAST, MIX_W), dt)
    z_kv = jnp.zeros((n_p, WINDOW, N_KV, HEAD_DIM), dt)
    y_p, pa, pb, pc, pk, pv = _trunk(x_prompt, mod_e, mod_o, 0, z_a, z_b, z_c, z_kv, z_kv, 0, we, wo)
    y_s, sa, sb, sc, sk, sv = _trunk(x_sample, mod_e, mod_o, n_p, state_conv_a[0], state_conv_b[0],
                                     state_pool_c[0], cache_win_k[0], cache_win_v[0], PAST_LEN, we, wo)
    return (y_p, y_s, pa, sa, pb, sb, pc, sc, pk, sk, pv, sv)
```

```python
import functools

import jax
import jax.numpy as jnp
from jax import lax
from jax.experimental import pallas as pl
from jax.experimental.pallas import tpu as pltpu

F32 = jnp.float32
BF16 = jnp.bfloat16

D_MODEL = 1024
MIX_W = 512
LANES = 128
MXU_COLS = 256
N_LG = MIX_W // LANES
HEAD_DIM = 64
N_HEADS = 8
N_KV = 2
N_REP = N_HEADS // N_KV
KV_W = N_KV * HEAD_DIM
CONV_A = 3
CONV_B = 31
POOL_WINDOWS = (2, 4, 8, 16)
POOL_PAST = 15
WINDOW = 128
PAST_LEN = 8192
EVEN_IN = 7 * MIX_W
ODD_IN = 4 * MIX_W + 2 * KV_W
Q_COL, K_COL, V_COL, DG_COL = 2 * MIX_W, 3 * MIX_W, 3 * MIX_W + KV_W, 3 * MIX_W + 2 * KV_W
RMS_EPS = 1e-6
LN_EPS = 1e-5
HIST_A = 8
HIST_B = 32
HIST_C = 16
ATT_MIN_Q_ROWS = 32
ROW_CHUNK = 64
WIN_CHUNK = 128
MOD_K_BLOCK = 256
ROW_PAD = LANES
VMEM_LIMIT_BYTES = 56 * 1024 * 1024
NEG_LOG2_E = -1.4426950408889634
ALIBI_SLOPES = tuple(float(2.0 ** (-8.0 * (h + 1) / N_HEADS)) for h in range(N_HEADS))


def _sigmoid(x):
    return 1.0 / (1.0 + jnp.exp2(x * NEG_LOG2_E))


def _silu(x):
    return x * _sigmoid(x)


def _mod_kernel(c_ref, w_ref, b_ref, o_ref):
    k = pl.program_id(0)
    part = jnp.dot(_silu(c_ref[...]).astype(BF16), w_ref[...].astype(BF16), preferred_element_type=F32)

    @pl.when(k == 0)
    def _first():
        o_ref[...] = part + b_ref[...]

    @pl.when(k > 0)
    def _rest():
        o_ref[...] += part


def _adaln_mod(c_all, w_mod, b_mod):
    n = c_all.shape[0]
    return pl.pallas_call(
        _mod_kernel,
        out_shape=jax.ShapeDtypeStruct((n, 3 * D_MODEL), F32),
        grid=(D_MODEL // MOD_K_BLOCK,),
        in_specs=[
            pl.BlockSpec((n, MOD_K_BLOCK), lambda k: (0, k)),
            pl.BlockSpec((MOD_K_BLOCK, 3 * D_MODEL), lambda k: (k, 0)),
            pl.BlockSpec((1, 3 * D_MODEL), lambda k: (0, 0)),
        ],
        out_specs=pl.BlockSpec((n, 3 * D_MODEL), lambda k: (0, 0)),
        compiler_params=pltpu.CompilerParams(dimension_semantics=("arbitrary",)),
        name="adaln_mod",
    )(c_all, w_mod, b_mod.reshape(1, -1))


def _mod_rows(mod_ref, seq0, col, r0, rc, tt):
    if tt >= rc:
        return mod_ref[pl.ds(seq0 + r0 // tt, 1), col:col + D_MODEL]
    parts = [jnp.broadcast_to(mod_ref[pl.ds(seq0 + b, 1), col:col + D_MODEL], (tt, D_MODEL))
             for b in range(r0 // tt, (r0 + rc) // tt)]
    return jnp.concatenate(parts, axis=0)


def _pre_norm(x_ref, mod_ref, seq0, g_ref, h_ref, rows, tt):
    rc = min(rows, ROW_CHUNK)
    for r0 in range(0, rows, rc):
        x = x_ref[r0:r0 + rc, :]
        ms = jnp.mean(x * x, axis=-1, keepdims=True)
        gain = g_ref[...] * (1.0 + _mod_rows(mod_ref, seq0, D_MODEL, r0, rc, tt))
        h = x * lax.rsqrt(ms + RMS_EPS) * gain + _mod_rows(mod_ref, seq0, 0, r0, rc, tt)
        h_ref[r0:r0 + rc, :] = h.astype(BF16)


def _out_proj_task(mix_ref, wout_ref, o_ref, lo, n):
    def task():
        o_ref[lo:lo + n, :] = jnp.dot(mix_ref[lo:lo + n, :], wout_ref[...], preferred_element_type=F32)
    return task


def _residual_tasks(x_ref, mod_ref, seq0, g_ref, o_ref, y_ref, lo, n, tt):
    rc = min(n, ROW_CHUNK)

    def chunk(r0):
        o = o_ref[r0:r0 + rc, :]
        ms = jnp.mean(o * o, axis=-1, keepdims=True)
        gain = g_ref[...] * _mod_rows(mod_ref, seq0, 2 * D_MODEL, r0, rc, tt)
        y_ref[r0:r0 + rc, :] = x_ref[r0:r0 + rc, :] + o * lax.rsqrt(ms + RMS_EPS) * gain
    return [functools.partial(chunk, r0) for r0 in range(lo, lo + n, rc)]


def _seq_chunks(lo, n, tt):
    rc = min(tt, n, WIN_CHUNK)
    return [(r0 // tt, r0 % tt, r0, rc) for r0 in range(lo, lo + n, rc)]


def _even_kernel(x_ref, mod_ref, sa_ref, sb_ref, gpre_ref, win_ref, cwa_ref, cwb_ref, cbb_ref,
                 lng_ref, lnb_ref, wout_ref, gpost_ref,
                 y_ref, na_ref, nb_ref,
                 h_ref, z_ref, o_ref, fa_ref, fb_ref, ca_ref, cb_ref, mix_ref, *, nb, tt, nt, seq_base):
    rows = nb * tt
    j = pl.program_id(1)
    seq0 = seq_base + pl.program_id(0) * nb

    @pl.when(j == 0)
    def _load_state():
        for b in range(nb):
            for g in range(N_LG):
                lg = slice(LANES * g, LANES * (g + 1))
                fa_ref[b * N_LG + g, 0:HIST_A, :] = jnp.zeros((HIST_A, LANES), F32)
                fa_ref[b * N_LG + g, HIST_A - (CONV_A - 1):HIST_A, :] = sa_ref[b, :, lg]
                fb_ref[b * N_LG + g, 0:8, :] = jnp.zeros((8, LANES), F32)
                fb_ref[b * N_LG + g, HIST_B - (CONV_B - 1):HIST_B, :] = sb_ref[b, :, lg]

    _pre_norm(x_ref, mod_ref, seq0, gpre_ref, h_ref, rows, tt)

    hb = h_ref[...]
    for c in range(EVEN_IN // MIX_W):
        cols = slice(MIX_W * c, MIX_W * (c + 1))
        z_ref[:, cols] = jnp.dot(hb, win_ref[:, cols], preferred_element_type=F32)

    for b, c0, r0, rc in _seq_chunks(0, rows, tt):
        for g in range(N_LG):
            def zc(c):
                return z_ref[r0:r0 + rc, MIX_W * c + LANES * g:MIX_W * c + LANES * (g + 1)]
            fa_ref[b * N_LG + g, HIST_A + c0:HIST_A + c0 + rc, :] = zc(2) * zc(0)
            fb_ref[b * N_LG + g, HIST_B + c0:HIST_B + c0 + rc, :] = zc(4) * _sigmoid(zc(5))

    for b, c0, r0, rc in _seq_chunks(0, rows, tt):
        for g in range(N_LG):
            lg = slice(LANES * g, LANES * (g + 1))
            acc = None
            for k in range(CONV_A):
                off = HIST_A - (CONV_A - 1) + k + c0
                t = cwa_ref[k:k + 1, lg] * fa_ref[b * N_LG + g, off:off + rc, :]
                acc = t if acc is None else acc + t
            ca_ref[r0:r0 + rc, lg] = acc
            acc = None
            for k in range(CONV_B):
                off = HIST_B - (CONV_B - 1) + k + c0
                t = cwb_ref[k:k + 1, lg] * fb_ref[b * N_LG + g, off:off + rc, :]
                acc = t if acc is None else acc + t
            cb_ref[r0:r0 + rc, lg] = acc

    for b in range(nb):
        for g in range(N_LG):
            lg = slice(LANES * g, LANES * (g + 1))
            na_ref[b, :, lg] = fa_ref[b * N_LG + g, HIST_A + tt - (CONV_A - 1):HIST_A + tt, :]
            nb_ref[b, :, lg] = fb_ref[b * N_LG + g, HIST_B + tt - (CONV_B - 1):HIST_B + tt, :]
            if nt > 1:
                fa_ref[b * N_LG + g, 0:HIST_A, :] = fa_ref[b * N_LG + g, tt:tt + HIST_A, :]
                fb_ref[b * N_LG + g, 0:HIST_B, :] = fb_ref[b * N_LG + g, tt:tt + HIST_B, :]

    rc = min(rows, ROW_CHUNK)
    for r0 in range(0, rows, rc):
        def zs(c):
            return z_ref[r0:r0 + rc, MIX_W * c:MIX_W * (c + 1)]
        ya = zs(1) * ca_ref[r0:r0 + rc, 0:MIX_W] * _silu(zs(3))
        mix_ref[r0:r0 + rc, 0:MIX_W] = ya.astype(BF16)
        yb = cb_ref[r0:r0 + rc, 0:MIX_W] + cbb_ref[...]
        mu = jnp.mean(yb, axis=-1, keepdims=True)
        yc = yb - mu
        var = jnp.mean(yc * yc, axis=-1, keepdims=True)
        ln = yc * lax.rsqrt(var + LN_EPS) * lng_ref[...] + lnb_ref[...]
        mix_ref[r0:r0 + rc, MIX_W:2 * MIX_W] = (_silu(ln) * _silu(zs(6))).astype(BF16)

    _out_proj_task(mix_ref, wout_ref, o_ref, 0, rows)()
    for task in _residual_tasks(x_ref, mod_ref, seq0, gpost_ref, o_ref, y_ref, 0, rows, tt):
        task()


def _even_layer(x2d, mod, st_a, st_b, g_pre, w_in, cw_a, cw_b, cb_b, ln_g, ln_b, w_out, g_post,
                *, n, t, nb, tt, seq_base):
    rows = nb * tt
    nt = t // tt
    kern = functools.partial(_even_kernel, nb=nb, tt=tt, nt=nt, seq_base=seq_base)
    row_spec = pl.BlockSpec((rows, D_MODEL), lambda i, j: (i * nt + j, 0))

    def seq_spec(shape):
        return pl.BlockSpec((nb,) + shape, lambda i, j: (i, 0, 0))

    def const_spec(shape):
        return pl.BlockSpec(shape, lambda i, j: (0,) * len(shape))

    return pl.pallas_call(
        kern,
        out_shape=(
            jax.ShapeDtypeStruct((n * t, D_MODEL), F32),
            jax.ShapeDtypeStruct((n, CONV_A - 1, MIX_W), F32),
            jax.ShapeDtypeStruct((n, CONV_B - 1, MIX_W), F32),
        ),
        grid=(n // nb, nt),
        in_specs=[
            row_spec,
            const_spec(mod.shape),
            seq_spec((CONV_A - 1, MIX_W)),
            seq_spec((CONV_B - 1, MIX_W)),
            const_spec((1, D_MODEL)),
            const_spec((D_MODEL, EVEN_IN)),
            const_spec((CONV_A, MIX_W)),
            const_spec((CONV_B, MIX_W)),
            const_spec((1, MIX_W)),
            const_spec((1, MIX_W)),
            const_spec((1, MIX_W)),
            const_spec((2 * MIX_W, D_MODEL)),
            const_spec((1, D_MODEL)),
        ],
        out_specs=(row_spec, seq_spec((CONV_A - 1, MIX_W)), seq_spec((CONV_B - 1, MIX_W))),
        scratch_shapes=[
            pltpu.VMEM((rows, D_MODEL), BF16),
            pltpu.VMEM((rows, EVEN_IN + ROW_PAD), F32),
            pltpu.VMEM((rows, D_MODEL), F32),
            pltpu.VMEM((nb * N_LG, HIST_A + tt, LANES), F32),
            pltpu.VMEM((nb * N_LG, HIST_B + tt, LANES), F32),
            pltpu.VMEM((rows, MIX_W + ROW_PAD), F32),
            pltpu.VMEM((rows, MIX_W + ROW_PAD), F32),
            pltpu.VMEM((rows, 2 * MIX_W), BF16),
        ],
        compiler_params=pltpu.CompilerParams(
            dimension_semantics=("arbitrary", "arbitrary"), vmem_limit_bytes=VMEM_LIMIT_BYTES),
        name="even_layer",
    )(x2d, mod, st_a, st_b, g_pre, w_in, cw_a, cw_b, cb_b, ln_g, ln_b, w_out, g_post)


def _odd_kernel(sinks_ref, x_ref, mod_ref, sc_ref, sk_ref, sv_ref, gpre_ref, win_ref, pw_ref, ps_ref,
                wout_ref, gpost_ref,
                y_ref, nc_ref, nk_ref, nv_ref,
                h_ref, z_ref, o_ref, fc_ref, kf_ref, ks_ref, vf_ref, vs_ref, pc_ref, od_ref, mix_ref,
                *, nb, tt, nt, pos0, seq_base):
    rows = nb * tt
    j = pl.program_id(1)
    seq0 = seq_base + pl.program_id(0) * nb
    tile_pos = pos0 + j * tt

    @pl.when(j == 0)
    def _load_state():
        for b in range(nb):
            for g in range(N_LG):
                fc_ref[b * N_LG + g, 0:8, :] = jnp.zeros((8, LANES), F32)
                fc_ref[b * N_LG + g, HIST_C - POOL_PAST:HIST_C, :] = sc_ref[b, :, LANES * g:LANES * (g + 1)]
            kf_ref[b, 0:WINDOW, :] = sk_ref[b]
            vf_ref[b, 0:WINDOW, :] = sv_ref[b]
            ks_ref[b, 0:WINDOW, :] = pltpu.roll(sk_ref[b], HEAD_DIM, axis=1)
            vs_ref[b, 0:WINDOW, :] = pltpu.roll(sv_ref[b], HEAD_DIM, axis=1)

    qs = min(tt, WINDOW)
    gs = max(1, ATT_MIN_Q_ROWS // qs)
    kw = WINDOW + qs
    n_q, n_keys = gs * qs, gs * kw
    assert N_REP == 4 and 2 * HEAD_DIM == LANES and KV_W == LANES
    assert qs & (qs - 1) == 0 and gs & (gs - 1) == 0 and nb % gs == 0 and (gs == 1 or tt == qs)
    shape = (2 * n_q, n_keys)
    ri = lax.broadcasted_iota(jnp.int32, shape, 0)
    ci = lax.broadcasted_iota(jnp.int32, shape, 1)
    q_seq = (ri >> (qs.bit_length() - 1)) & (gs - 1)
    k_seq = jnp.zeros(shape, jnp.int32)
    for s_i in range(1, gs):
        k_seq = k_seq + (ci >= s_i * kw).astype(jnp.int32)
    k_c = ci - k_seq * kw
    dist = (ri & (qs - 1)) + WINDOW - k_c
    band = (q_seq == k_seq) & (dist >= 0) & (dist < WINDOW)
    dist_f = dist.astype(F32)
    upper_grp = ri >= n_q
    upper_grp_col = lax.broadcasted_iota(jnp.int32, (2 * n_q, 1), 0) >= n_q
    low_lanes = lax.broadcasted_iota(jnp.int32, (n_keys, LANES), 1) < HEAD_DIM
    ones_cols = jnp.ones((n_keys, LANES), BF16)

    def head_of(g, grp, parity):
        return g * N_REP + 2 * grp + parity

    def alibi_bias(g, parity):
        slope = jnp.where(upper_grp, ALIBI_SLOPES[head_of(g, 1, parity)], ALIBI_SLOPES[head_of(g, 0, parity)])
        return jnp.where(band, -(slope * dist_f), -jnp.inf)

    biases = {(g, parity): alibi_bias(g, parity) for g in range(N_KV) for parity in range(2)}

    def attention(b0, sb, g):
        r0 = b0 * tt + sb * qs
        p0 = tile_pos + sb * qs
        in_range = k_c >= WINDOW - p0
        if gs == 1:
            win = (b0, slice(sb * qs, sb * qs + kw))
            k_nat, k_swp, v_nat, v_swp = kf_ref[win], ks_ref[win], vf_ref[win], vs_ref[win]
        else:
            k_nat, k_swp, v_nat, v_swp = (r[b0:b0 + gs].reshape(n_keys, KV_W)
                                          for r in (kf_ref, ks_ref, vf_ref, vs_ref))
        k_lo, k_hi = (k_nat, k_swp) if g == 0 else (k_swp, k_nat)
        v_lo, v_hi = (v_nat, v_swp) if g == 0 else (v_swp, v_nat)
        v_ext = jnp.concatenate([jnp.where(low_lanes, v_lo, v_hi).astype(BF16), ones_cols], axis=1)
        qg = jnp.concatenate([z_ref[r0:r0 + n_q, Q_COL + 2 * LANES * g + LANES * grp:
                                      Q_COL + 2 * LANES * g + LANES * (grp + 1)] for grp in range(2)], axis=0)
        qg = (qg * (HEAD_DIM ** -0.5)).astype(BF16)
        for parity in range(2):
            k_ext = (jnp.where(low_lanes, k_lo, 0.0) if parity == 0 else jnp.where(low_lanes, 0.0, k_hi)).astype(BF16)
            sink = jnp.where(upper_grp_col, sinks_ref[head_of(g, 1, parity)], sinks_ref[head_of(g, 0, parity)])
            sc = lax.dot_general(qg, k_ext, (((1,), (1,)), ((), ())), preferred_element_type=F32)
            sc = sc + jnp.where(in_range, biases[g, parity], -jnp.inf)
            m = jnp.maximum(jnp.max(sc, axis=-1, keepdims=True), sink)
            p = jnp.exp(sc - m)
            o_ext = jnp.dot(p.astype(BF16), v_ext, preferred_element_type=F32)
            o = o_ext[:, 0:LANES] / (o_ext[:, LANES:2 * LANES] + jnp.exp(sink - m))
            for grp in range(2):
                h = head_of(g, grp, parity)
                od_ref[r0:r0 + n_q, HEAD_DIM * h:HEAD_DIM * (h + 1)] = (
                    o[grp * n_q:(grp + 1) * n_q, HEAD_DIM * parity:HEAD_DIM * (parity + 1)])

    def window_inputs(b, c0, r0, rc):
        for g in range(N_LG):
            fc_ref[b * N_LG + g, HIST_C + c0:HIST_C + c0 + rc, :] = z_ref[r0:r0 + rc, LANES * g:LANES * (g + 1)]
        k = z_ref[r0:r0 + rc, K_COL:K_COL + KV_W]
        v = z_ref[r0:r0 + rc, V_COL:V_COL + KV_W]
        kf_ref[b, WINDOW + c0:WINDOW + c0 + rc, :] = k
        vf_ref[b, WINDOW + c0:WINDOW + c0 + rc, :] = v
        ks_ref[b, WINDOW + c0:WINDOW + c0 + rc, :] = pltpu.roll(k, HEAD_DIM, axis=1)
        vs_ref[b, WINDOW + c0:WINDOW + c0 + rc, :] = pltpu.roll(v, HEAD_DIM, axis=1)

    def pooling(b, c0, r0, rc):
        pos = tile_pos + c0 + lax.broadcasted_iota(jnp.int32, (rc, LANES), 0)
        for g, w in enumerate(POOL_WINDOWS):
            u = fc_ref[b * N_LG + g, HIST_C + c0:HIST_C + c0 + rc, :]
            acc = u
            for d in range(1, w):
                acc = acc + fc_ref[b * N_LG + g, HIST_C + c0 - d:HIST_C + c0 - d + rc, :]
            cnt = jnp.minimum(w, pos + 1).astype(F32)
            pc_ref[r0:r0 + rc, LANES * g:LANES * (g + 1)] = acc / cnt - u

    def gating(r0, rc):
        for g in range(N_LG):
            lg = slice(LANES * g, LANES * (g + 1))
            mixed = jnp.dot(pc_ref[r0:r0 + rc, lg].astype(BF16), pw_ref[g], preferred_element_type=F32)
            yc = mixed * ps_ref[:, lg] * _silu(z_ref[r0:r0 + rc, MIX_W + LANES * g:MIX_W + LANES * (g + 1)])
            mix_ref[r0:r0 + rc, lg] = yc.astype(BF16)
        yd = od_ref[r0:r0 + rc, 0:MIX_W] * _silu(z_ref[r0:r0 + rc, DG_COL:DG_COL + MIX_W])
        mix_ref[r0:r0 + rc, MIX_W:2 * MIX_W] = yd.astype(BF16)

    def mixer_tasks(lo, n):
        chunks = _seq_chunks(lo, n, tt)
        if gs == 1:
            blocks = [(r0 // tt, (r0 % tt) // qs) for r0 in range(lo, lo + n, qs)]
        else:
            blocks = [(b0, 0) for b0 in range(lo // tt, (lo + n) // tt, gs)]
        return ([functools.partial(window_inputs, *ch) for ch in chunks]
                + [functools.partial(pooling, *ch) for ch in chunks]
                + [functools.partial(attention, b0, sb, g) for b0, sb in blocks for g in range(N_KV)]
                + [functools.partial(gating, lo, n)])

    _pre_norm(x_ref, mod_ref, seq0, gpre_ref, h_ref, rows, tt)
    for c0 in range(0, ODD_IN, MXU_COLS):
        z_ref[:, c0:c0 + MXU_COLS] = jnp.dot(h_ref[...], win_ref[:, c0:c0 + MXU_COLS], preferred_element_type=F32)

    n_chunks = max(1, rows // WIN_CHUNK)
    n = rows // n_chunks
    for c in range(n_chunks + 1):
        if c > 0:
            _out_proj_task(mix_ref, wout_ref, o_ref, (c - 1) * n, n)()
        if c < n_chunks:
            for task in mixer_tasks(c * n, n):
                task()
        if c > 0:
            for task in _residual_tasks(x_ref, mod_ref, seq0, gpost_ref, o_ref, y_ref, (c - 1) * n, n, tt):
                task()

    for b in range(nb):
        for g in range(N_LG):
            nc_ref[b, :, LANES * g:LANES * (g + 1)] = fc_ref[b * N_LG + g, HIST_C + tt - POOL_PAST:HIST_C + tt, :]
            if nt > 1:
                fc_ref[b * N_LG + g, 0:HIST_C, :] = fc_ref[b * N_LG + g, tt:tt + HIST_C, :]
        nk_ref[b] = kf_ref[b, tt:tt + WINDOW, :]
        nv_ref[b] = vf_ref[b, tt:tt + WINDOW, :]
        if nt > 1:
            for r in (kf_ref, ks_ref, vf_ref, vs_ref):
                r[b, 0:WINDOW, :] = r[b, tt:tt + WINDOW, :]


def _odd_layer(x2d, mod, st_c, st_k, st_v, g_pre, w_in, pool_w, pool_scale, sinks, w_out, g_post,
               *, n, t, nb, tt, pos0, seq_base):
    rows = nb * tt
    nt = t // tt
    kern = functools.partial(_odd_kernel, nb=nb, tt=tt, nt=nt, pos0=pos0, seq_base=seq_base)
    row_spec = pl.BlockSpec((rows, D_MODEL), lambda i, j: (i * nt + j, 0))

    def seq_spec(shape):
        return pl.BlockSpec((nb,) + shape, lambda i, j: (i, 0, 0))

    def const_spec(shape):
        return pl.BlockSpec(shape, lambda i, j: (0,) * len(shape))

    return pl.pallas_call(
        kern,
        out_shape=(
            jax.ShapeDtypeStruct((n * t, D_MODEL), F32),
            jax.ShapeDtypeStruct((n, POOL_PAST, MIX_W), F32),
            jax.ShapeDtypeStruct((n, WINDOW, KV_W), F32),
            jax.ShapeDtypeStruct((n, WINDOW, KV_W), F32),
        ),
        grid=(n // nb, nt),
        in_specs=[
            pl.BlockSpec(memory_space=pltpu.SMEM),
            row_spec,
            const_spec(mod.shape),
            seq_spec((POOL_PAST, MIX_W)),
            seq_spec((WINDOW, KV_W)),
            seq_spec((WINDOW, KV_W)),
            const_spec((1, D_MODEL)),
            const_spec((D_MODEL, ODD_IN)),
            const_spec((N_LG, LANES, LANES)),
            const_spec((1, MIX_W)),
            const_spec((2 * MIX_W, D_MODEL)),
            const_spec((1, D_MODEL)),
        ],
        out_specs=(row_spec, seq_spec((POOL_PAST, MIX_W)), seq_spec((WINDOW, KV_W)), seq_spec((WINDOW, KV_W))),
        scratch_shapes=[
            pltpu.VMEM((rows, D_MODEL), BF16),
            pltpu.VMEM((rows, ODD_IN + ROW_PAD), F32),
            pltpu.VMEM((rows, D_MODEL), F32),
            pltpu.VMEM((nb * N_LG, HIST_C + tt, LANES), F32),
            pltpu.VMEM((nb, WINDOW + tt, KV_W), F32),
            pltpu.VMEM((nb, WINDOW + tt, KV_W), F32),
            pltpu.VMEM((nb, WINDOW + tt, KV_W), F32),
            pltpu.VMEM((nb, WINDOW + tt, KV_W), F32),
            pltpu.VMEM((rows, MIX_W + ROW_PAD), F32),
            pltpu.VMEM((rows, MIX_W + ROW_PAD), F32),
            pltpu.VMEM((rows, 2 * MIX_W), BF16),
        ],
        compiler_params=pltpu.CompilerParams(
            dimension_semantics=("arbitrary", "arbitrary"), vmem_limit_bytes=VMEM_LIMIT_BYTES),
        name="odd_layer",
    )(sinks, x2d, mod, st_c, st_k, st_v, g_pre, w_in, pool_w, pool_scale, w_out, g_post)


def _tiling(n, t):
    if t >= 512:
        return 1, 512
    return min(n, 256 // t), t


def _trunk(x, mod_e, mod_o, seq_base, st_a, st_b, st_c, st_k, st_v, pos0, we, wo):
    n, t, _ = x.shape
    nb, tt = _tiling(n, t)
    x2d = x.reshape(n * t, D_MODEL)
    x2d, na, nbs = _even_layer(x2d, mod_e, st_a, st_b, *we, n=n, t=t, nb=nb, tt=tt, seq_base=seq_base)
    x2d, nc, nk, nv = _odd_layer(x2d, mod_o, st_c, st_k.reshape(n, WINDOW, KV_W),
                                 st_v.reshape(n, WINDOW, KV_W), *wo, n=n, t=t, nb=nb, tt=tt, pos0=pos0,
                                 seq_base=seq_base)
    kv_shape = (1, n, WINDOW, N_KV, HEAD_DIM)
    return (x2d.reshape(n, t, D_MODEL), na[None], nbs[None], nc[None],
            nk.reshape(kv_shape), nv.reshape(kv_shape))


def kernel(x_prompt, x_sample, state_conv_a, state_conv_b, state_pool_c, cache_win_k, cache_win_v, c_prompt, c_sample, w_mod_e, b_mod_e, g_pre_e, g_post_e, w_in_e, conv_a_w, conv_b_w, conv_b_b, ln_b_g, ln_b_b, w_out_e, w_mod_o, b_mod_o, g_pre_o, g_post_o, w_in_o, pool_w, pool_scale, sinks, w_out_o):
    n_p = x_prompt.shape[0]
    assert w_in_e.shape[0] == 1 and w_in_o.shape[0] == 1, "one even and one odd layer"

    c_all = jnp.concatenate([c_prompt, c_sample], axis=0)
    mod_e = _adaln_mod(c_all, w_mod_e[0], b_mod_e[0])
    mod_o = _adaln_mod(c_all, w_mod_o[0], b_mod_o[0])

    we = (g_pre_e, w_in_e[0].astype(BF16), conv_a_w[0], conv_b_w[0], conv_b_b, ln_b_g, ln_b_b,
          w_out_e[0].astype(BF16), g_post_e)
    wo = (g_pre_o, w_in_o[0].astype(BF16), pool_w[0].astype(BF16), pool_scale, sinks[0],
          w_out_o[0].astype(BF16), g_post_o)

    dt = x_prompt.dtype
    z_a = jnp.zeros((n_p, CONV_A - 1, MIX_W), dt)
    z_b = jnp.zeros((n_p, CONV_B - 1, MIX_W), dt)
    z_c = jnp.zeros((n_p, POOL_PAST, MIX_W), dt)
    z_kv = jnp.zeros((n_p, WINDOW, N_KV, HEAD_DIM), dt)
    y_p, pa, pb, pc, pk, pv = _trunk(x_prompt, mod_e, mod_o, 0, z_a, z_b, z_c, z_kv, z_kv, 0, we, wo)
    y_s, sa, sb, sc, sk, sv = _trunk(x_sample, mod_e, mod_o, n_p, state_conv_a[0], state_conv_b[0],
                                     state_pool_c[0], cache_win_k[0], cache_win_v[0], PAST_LEN, we, wo)
    return (y_p, y_s, pa, sa, pb, sb, pc, sc, pk, sk, pv, sv)
```

```python
import functools

import jax
import jax.numpy as jnp
from jax import lax
from jax.experimental import pallas as pl
from jax.experimental.pallas import tpu as pltpu

F32 = jnp.float32
BF16 = jnp.bfloat16

D_MODEL = 1024
MIX_W = 512
LANES = 128
MXU_COLS = 256
N_LG = MIX_W // LANES
HEAD_DIM = 64
N_HEADS = 8
N_KV = 2
N_REP = N_HEADS // N_KV
KV_W = N_KV * HEAD_DIM
CONV_A = 3
CONV_B = 31
POOL_WINDOWS = (2, 4, 8, 16)
POOL_PAST = 15
WINDOW = 128
PAST_LEN = 8192
EVEN_IN = 7 * MIX_W
ODD_IN = 4 * MIX_W + 2 * KV_W
Q_COL, K_COL, V_COL, DG_COL = 2 * MIX_W, 3 * MIX_W, 3 * MIX_W + KV_W, 3 * MIX_W + 2 * KV_W
RMS_EPS = 1e-6
LN_EPS = 1e-5
HIST_A = 8
HIST_B = 32
HIST_C = 16
ATT_MIN_Q_ROWS = 32
ROW_CHUNK = 64
WIN_CHUNK = 128
MOD_K_BLOCK = 256
ROW_PAD = LANES
VMEM_LIMIT_BYTES = 56 * 1024 * 1024
NEG_LOG2_E = -1.4426950408889634
ALIBI_SLOPES = tuple(float(2.0 ** (-8.0 * (h + 1) / N_HEADS)) for h in range(N_HEADS))


def _sigmoid(x):
    return 1.0 / (1.0 + jnp.exp2(x * NEG_LOG2_E))


def _silu(x):
    return x * _sigmoid(x)


def _mod_kernel(c_ref, w_ref, b_ref, o_ref):
    k = pl.program_id(0)
    part = jnp.dot(_silu(c_ref[...]).astype(BF16), w_ref[...].astype(BF16), preferred_element_type=F32)

    @pl.when(k == 0)
    def _first():
        o_ref[...] = part + b_ref[...]

    @pl.when(k > 0)
    def _rest():
        o_ref[...] += part


def _adaln_mod(c_all, w_mod, b_mod):
    n = c_all.shape[0]
    return pl.pallas_call(
        _mod_kernel,
        out_shape=jax.ShapeDtypeStruct((n, 3 * D_MODEL), F32),
        grid=(D_MODEL // MOD_K_BLOCK,),
        in_specs=[
            pl.BlockSpec((n, MOD_K_BLOCK), lambda k: (0, k)),
            pl.BlockSpec((MOD_K_BLOCK, 3 * D_MODEL), lambda k: (k, 0)),
            pl.BlockSpec((1, 3 * D_MODEL), lambda k: (0, 0)),
        ],
        out_specs=pl.BlockSpec((n, 3 * D_MODEL), lambda k: (0, 0)),
        compiler_params=pltpu.CompilerParams(dimension_semantics=("arbitrary",)),
        name="adaln_mod",
    )(c_all, w_mod, b_mod.reshape(1, -1))


def _mod_rows(mod_ref, seq0, col, r0, rc, tt):
    if tt >= rc:
        return mod_ref[pl.ds(seq0 + r0 // tt, 1), col:col + D_MODEL]
    parts = [jnp.broadcast_to(mod_ref[pl.ds(seq0 + b, 1), col:col + D_MODEL], (tt, D_MODEL))
             for b in range(r0 // tt, (r0 + rc) // tt)]
    return jnp.concatenate(parts, axis=0)


def _pre_norm(x_ref, mod_ref, seq0, g_ref, h_ref, rows, tt):
    rc = min(rows, ROW_CHUNK)
    for r0 in range(0, rows, rc):
        x = x_ref[r0:r0 + rc, :]
        ms = jnp.mean(x * x, axis=-1, keepdims=True)
        gain = g_ref[...] * (1.0 + _mod_rows(mod_ref, seq0, D_MODEL, r0, rc, tt))
        h = x * lax.rsqrt(ms + RMS_EPS) * gain + _mod_rows(mod_ref, seq0, 0, r0, rc, tt)
        h_ref[r0:r0 + rc, :] = h.astype(BF16)


def _out_proj_task(mix_ref, wout_ref, o_ref, lo, n):
    def task():
        o_ref[lo:lo + n, :] = jnp.dot(mix_ref[lo:lo + n, :], wout_ref[...], preferred_element_type=F32)
    return task


def _residual_tasks(x_ref, mod_ref, seq0, g_ref, o_ref, y_ref, lo, n, tt):
    rc = min(n, ROW_CHUNK)

    def chunk(r0):
        o = o_ref[r0:r0 + rc, :]
        ms = jnp.mean(o * o, axis=-1, keepdims=True)
        gain = g_ref[...] * _mod_rows(mod_ref, seq0, 2 * D_MODEL, r0, rc, tt)
        y_ref[r0:r0 + rc, :] = x_ref[r0:r0 + rc, :] + o * lax.rsqrt(ms + RMS_EPS) * gain
    return [functools.partial(chunk, r0) for r0 in range(lo, lo + n, rc)]


def _seq_chunks(lo, n, tt):
    rc = min(tt, n, WIN_CHUNK)
    return [(r0 // tt, r0 % tt, r0, rc) for r0 in range(lo, lo + n, rc)]


def _even_gating(z_ref, ca_ref, cb_ref, cbb_ref, lng_ref, lnb_ref, mix_ref, rows):
    rc = min(rows, ROW_CHUNK)
    for r0 in range(0, rows, rc):
        def zs(c):
            return z_ref[r0:r0 + rc, MIX_W * c:MIX_W * (c + 1)]

        def all_lanes(c_ref):
            return jnp.concatenate([c_ref[g, r0:r0 + rc, :] for g in range(N_LG)], axis=1)
        ya = zs(1) * all_lanes(ca_ref) * _silu(zs(3))
        mix_ref[r0:r0 + rc, 0:MIX_W] = ya.astype(BF16)
        yb = all_lanes(cb_ref) + cbb_ref[...]
        mu = jnp.mean(yb, axis=-1, keepdims=True)
        yc = yb - mu
        var = jnp.mean(yc * yc, axis=-1, keepdims=True)
        ln = yc * lax.rsqrt(var + LN_EPS) * lng_ref[...] + lnb_ref[...]
        mix_ref[r0:r0 + rc, MIX_W:2 * MIX_W] = (_silu(ln) * _silu(zs(6))).astype(BF16)


def _even_kernel(x_ref, mod_ref, sa_ref, sb_ref, gpre_ref, win_ref, cwa_ref, cwb_ref, cbb_ref,
                 lng_ref, lnb_ref, wout_ref, gpost_ref,
                 y_ref, na_ref, nb_ref,
                 h_ref, z_ref, o_ref, fa_ref, fb_ref, ca_ref, cb_ref, mix_ref,
                 *, nb, tt, nt, seq_base, time_major_state):
    rows = nb * tt
    j = pl.program_id(1)
    seq0 = seq_base + pl.program_id(0) * nb
    assert not time_major_state or nt == 1

    @pl.when(j == 0)
    def _load_state():
        for b in range(0 if time_major_state else nb):
            for g in range(N_LG):
                lg = slice(LANES * g, LANES * (g + 1))
                fa_ref[b * N_LG + g, 0:HIST_A, :] = jnp.zeros((HIST_A, LANES), F32)
                fa_ref[b * N_LG + g, HIST_A - (CONV_A - 1):HIST_A, :] = sa_ref[b, :, lg]
                fb_ref[b * N_LG + g, 0:8, :] = jnp.zeros((8, LANES), F32)
                fb_ref[b * N_LG + g, HIST_B - (CONV_B - 1):HIST_B, :] = sb_ref[b, :, lg]

    _pre_norm(x_ref, mod_ref, seq0, gpre_ref, h_ref, rows, tt)

    for c0 in range(0, EVEN_IN, MXU_COLS):
        z_ref[:, c0:c0 + MXU_COLS] = jnp.dot(h_ref[...], win_ref[:, c0:c0 + MXU_COLS], preferred_element_type=F32)

    if time_major_state:
        rc = min(rows, WIN_CHUNK)
        for r0 in range(0, rows, rc):
            for g in range(N_LG):
                def zc(c):
                    return z_ref[r0:r0 + rc, MIX_W * c + LANES * g:MIX_W * c + LANES * (g + 1)]
                fa_ref[g, r0:r0 + rc, :] = zc(2) * zc(0)
                fb_ref[g, r0:r0 + rc, :] = zc(4) * _sigmoid(zc(5))

        def conv(w_ref, state_ref, u_ref, out_ref, new_state_ref, taps):
            hist = taps - 1

            def slab(j_in, g):
                if j_in < hist:
                    return state_ref[j_in, :, LANES * g:LANES * (g + 1)]
                return u_ref[g, pl.ds(j_in - hist, nb, stride=tt), :]

            for t in range(tt):
                for g in range(N_LG):
                    acc = None
                    for k in range(taps):
                        term = w_ref[k:k + 1, LANES * g:LANES * (g + 1)] * slab(t + k, g)
                        acc = term if acc is None else acc + term
                    out_ref[g, pl.ds(t, nb, stride=tt), :] = acc
            for j_out in range(hist):
                for g in range(N_LG):
                    new_state_ref[j_out, :, LANES * g:LANES * (g + 1)] = slab(j_out + tt, g)

        conv(cwa_ref, sa_ref, fa_ref, ca_ref, na_ref, CONV_A)
        conv(cwb_ref, sb_ref, fb_ref, cb_ref, nb_ref, CONV_B)

    for b, c0, r0, rc in ([] if time_major_state else _seq_chunks(0, rows, tt)):
        for g in range(N_LG):
            def zc(c):
                return z_ref[r0:r0 + rc, MIX_W * c + LANES * g:MIX_W * c + LANES * (g + 1)]
            fa_ref[b * N_LG + g, HIST_A + c0:HIST_A + c0 + rc, :] = zc(2) * zc(0)
            fb_ref[b * N_LG + g, HIST_B + c0:HIST_B + c0 + rc, :] = zc(4) * _sigmoid(zc(5))

    for b, c0, r0, rc in ([] if time_major_state else _seq_chunks(0, rows, tt)):
        for g in range(N_LG):
            lg = slice(LANES * g, LANES * (g + 1))
            acc = None
            for k in range(CONV_A):
                off = HIST_A - (CONV_A - 1) + k + c0
                t = cwa_ref[k:k + 1, lg] * fa_ref[b * N_LG + g, off:off + rc, :]
                acc = t if acc is None else acc + t
            ca_ref[g, r0:r0 + rc, :] = acc
            acc = None
            for k in range(CONV_B):
                off = HIST_B - (CONV_B - 1) + k + c0
                t = cwb_ref[k:k + 1, lg] * fb_ref[b * N_LG + g, off:off + rc, :]
                acc = t if acc is None else acc + t
            cb_ref[g, r0:r0 + rc, :] = acc

    for b in range(0 if time_major_state else nb):
        for g in range(N_LG):
            lg = slice(LANES * g, LANES * (g + 1))
            na_ref[b, :, lg] = fa_ref[b * N_LG + g, HIST_A + tt - (CONV_A - 1):HIST_A + tt, :]
            nb_ref[b, :, lg] = fb_ref[b * N_LG + g, HIST_B + tt - (CONV_B - 1):HIST_B + tt, :]
            if nt > 1:
                fa_ref[b * N_LG + g, 0:HIST_A, :] = fa_ref[b * N_LG + g, tt:tt + HIST_A, :]
                fb_ref[b * N_LG + g, 0:HIST_B, :] = fb_ref[b * N_LG + g, tt:tt + HIST_B, :]

    _even_gating(z_ref, ca_ref, cb_ref, cbb_ref, lng_ref, lnb_ref, mix_ref, rows)
    _out_proj_task(mix_ref, wout_ref, o_ref, 0, rows)()
    for task in _residual_tasks(x_ref, mod_ref, seq0, gpost_ref, o_ref, y_ref, 0, rows, tt):
        task()


def _even_layer(x2d, mod, st_a, st_b, g_pre, w_in, cw_a, cw_b, cb_b, ln_g, ln_b, w_out, g_post,
                *, n, t, nb, tt, seq_base, time_major_state):
    rows = nb * tt
    nt = t // tt
    kern = functools.partial(_even_kernel, nb=nb, tt=tt, nt=nt, seq_base=seq_base,
                             time_major_state=time_major_state)
    row_spec = pl.BlockSpec((rows, D_MODEL), lambda i, j: (i * nt + j, 0))

    def seq_spec(shape):
        if time_major_state:
            return pl.BlockSpec((shape[0], nb, shape[1]), lambda i, j: (0, i, 0))
        return pl.BlockSpec((nb,) + shape, lambda i, j: (i, 0, 0))

    def window_scratch(hist):
        if time_major_state:
            return pltpu.VMEM((N_LG, rows, LANES), F32)
        return pltpu.VMEM((nb * N_LG, hist + tt, LANES), F32)

    def const_spec(shape):
        return pl.BlockSpec(shape, lambda i, j: (0,) * len(shape))

    return pl.pallas_call(
        kern,
        out_shape=(
            jax.ShapeDtypeStruct((n * t, D_MODEL), F32),
            jax.ShapeDtypeStruct(st_a.shape, F32),
            jax.ShapeDtypeStruct(st_b.shape, F32),
        ),
        grid=(n // nb, nt),
        in_specs=[
            row_spec,
            const_spec(mod.shape),
            seq_spec((CONV_A - 1, MIX_W)),
            seq_spec((CONV_B - 1, MIX_W)),
            const_spec((1, D_MODEL)),
            const_spec((D_MODEL, EVEN_IN)),
            const_spec((CONV_A, MIX_W)),
            const_spec((CONV_B, MIX_W)),
            const_spec((1, MIX_W)),
            const_spec((1, MIX_W)),
            const_spec((1, MIX_W)),
            const_spec((2 * MIX_W, D_MODEL)),
            const_spec((1, D_MODEL)),
        ],
        out_specs=(row_spec, seq_spec((CONV_A - 1, MIX_W)), seq_spec((CONV_B - 1, MIX_W))),
        scratch_shapes=[
            pltpu.VMEM((rows, D_MODEL), BF16),
            pltpu.VMEM((rows, EVEN_IN + ROW_PAD), F32),
            pltpu.VMEM((rows, D_MODEL), F32),
            window_scratch(HIST_A),
            window_scratch(HIST_B),
            pltpu.VMEM((N_LG, rows, LANES), F32),
            pltpu.VMEM((N_LG, rows, LANES), F32),
            pltpu.VMEM((rows, 2 * MIX_W), BF16),
        ],
        compiler_params=pltpu.CompilerParams(
            dimension_semantics=("arbitrary", "arbitrary"), vmem_limit_bytes=VMEM_LIMIT_BYTES),
        name="even_layer",
    )(x2d, mod, st_a, st_b, g_pre, w_in, cw_a, cw_b, cb_b, ln_g, ln_b, w_out, g_post)


def _odd_kernel(sinks_ref, x_ref, mod_ref, sc_ref, sk_ref, sv_ref, gpre_ref, win_ref, pw_ref, ps_ref,
                wout_ref, gpost_ref,
                y_ref, nc_ref, nk_ref, nv_ref,
                h_ref, z_ref, o_ref, fc_ref, kf_ref, ks_ref, vf_ref, vs_ref, pc_ref, od_ref, mix_ref,
                *, nb, tt, nt, pos0, seq_base, time_major_pool):
    rows = nb * tt
    j = pl.program_id(1)
    seq0 = seq_base + pl.program_id(0) * nb
    tile_pos = pos0 + j * tt
    assert not time_major_pool or nt == 1

    @pl.when(j == 0)
    def _load_state():
        for b in range(nb):
            for g in range(0 if time_major_pool else N_LG):
                fc_ref[b * N_LG + g, 0:8, :] = jnp.zeros((8, LANES), F32)
                fc_ref[b * N_LG + g, HIST_C - POOL_PAST:HIST_C, :] = sc_ref[b, :, LANES * g:LANES * (g + 1)]
            kf_ref[b, 0:WINDOW, :] = sk_ref[b]
            vf_ref[b, 0:WINDOW, :] = sv_ref[b]
            ks_ref[b, 0:WINDOW, :] = pltpu.roll(sk_ref[b], HEAD_DIM, axis=1)
            vs_ref[b, 0:WINDOW, :] = pltpu.roll(sv_ref[b], HEAD_DIM, axis=1)

    qs = min(tt, WINDOW)
    gs = max(1, ATT_MIN_Q_ROWS // qs)
    kw = WINDOW + qs
    n_q, n_keys = gs * qs, gs * kw
    assert N_REP == 4 and 2 * HEAD_DIM == LANES and KV_W == LANES
    assert qs & (qs - 1) == 0 and gs & (gs - 1) == 0 and nb % gs == 0 and (gs == 1 or tt == qs)
    shape = (2 * n_q, n_keys)
    ri = lax.broadcasted_iota(jnp.int32, shape, 0)
    ci = lax.broadcasted_iota(jnp.int32, shape, 1)
    q_seq = (ri >> (qs.bit_length() - 1)) & (gs - 1)
    k_seq = jnp.zeros(shape, jnp.int32)
    for s_i in range(1, gs):
        k_seq = k_seq + (ci >= s_i * kw).astype(jnp.int32)
    k_c = ci - k_seq * kw
    dist = (ri & (qs - 1)) + WINDOW - k_c
    band = (q_seq == k_seq) & (dist >= 0) & (dist < WINDOW)
    dist_f = dist.astype(F32)
    upper_grp = ri >= n_q
    upper_grp_col = lax.broadcasted_iota(jnp.int32, (2 * n_q, 1), 0) >= n_q
    low_lanes = lax.broadcasted_iota(jnp.int32, (n_keys, LANES), 1) < HEAD_DIM
    ones_cols = jnp.ones((n_keys, LANES), BF16)

    def head_of(g, grp, parity):
        return g * N_REP + 2 * grp + parity

    def alibi_bias(g, parity):
        slope = jnp.where(upper_grp, ALIBI_SLOPES[head_of(g, 1, parity)], ALIBI_SLOPES[head_of(g, 0, parity)])
        return jnp.where(band, -(slope * dist_f), -jnp.inf)

    biases = {(g, parity): alibi_bias(g, parity) for g in range(N_KV) for parity in range(2)}

    def attention(b0, sb, g):
        r0 = b0 * tt + sb * qs
        p0 = tile_pos + sb * qs
        in_range = k_c >= WINDOW - p0
        if gs == 1:
            win = (b0, slice(sb * qs, sb * qs + kw))
            k_nat, k_swp, v_nat, v_swp = kf_ref[win], ks_ref[win], vf_ref[win], vs_ref[win]
        else:
            k_nat, k_swp, v_nat, v_swp = (r[b0:b0 + gs].reshape(n_keys, KV_W)
                                          for r in (kf_ref, ks_ref, vf_ref, vs_ref))
        k_lo, k_hi = (k_nat, k_swp) if g == 0 else (k_swp, k_nat)
        v_lo, v_hi = (v_nat, v_swp) if g == 0 else (v_swp, v_nat)
        v_ext = jnp.concatenate([jnp.where(low_lanes, v_lo, v_hi).astype(BF16), ones_cols], axis=1)
        qg = jnp.concatenate([z_ref[r0:r0 + n_q, Q_COL + 2 * LANES * g + LANES * grp:
                                      Q_COL + 2 * LANES * g + LANES * (grp + 1)] for grp in range(2)], axis=0)
        qg = (qg * (HEAD_DIM ** -0.5)).astype(BF16)
        for parity in range(2):
            k_ext = (jnp.where(low_lanes, k_lo, 0.0) if parity == 0 else jnp.where(low_lanes, 0.0, k_hi)).astype(BF16)
            sink = jnp.where(upper_grp_col, sinks_ref[head_of(g, 1, parity)], sinks_ref[head_of(g, 0, parity)])
            sc = lax.dot_general(qg, k_ext, (((1,), (1,)), ((), ())), preferred_element_type=F32)
            sc = sc + jnp.where(in_range, biases[g, parity], -jnp.inf)
            m = jnp.maximum(jnp.max(sc, axis=-1, keepdims=True), sink)
            p = jnp.exp(sc - m)
            o_ext = jnp.dot(p.astype(BF16), v_ext, preferred_element_type=F32)
            o = o_ext[:, 0:LANES] / (o_ext[:, LANES:2 * LANES] + jnp.exp(sink - m))
            for grp in range(2):
                h = head_of(g, grp, parity)
                od_ref[r0:r0 + n_q, HEAD_DIM * h:HEAD_DIM * (h + 1)] = (
                    o[grp * n_q:(grp + 1) * n_q, HEAD_DIM * parity:HEAD_DIM * (parity + 1)])

    def window_inputs(b, c0, r0, rc):
        for g in range(0 if time_major_pool else N_LG):
            fc_ref[b * N_LG + g, HIST_C + c0:HIST_C + c0 + rc, :] = z_ref[r0:r0 + rc, LANES * g:LANES * (g + 1)]
        k = z_ref[r0:r0 + rc, K_COL:K_COL + KV_W]
        v = z_ref[r0:r0 + rc, V_COL:V_COL + KV_W]
        kf_ref[b, WINDOW + c0:WINDOW + c0 + rc, :] = k
        vf_ref[b, WINDOW + c0:WINDOW + c0 + rc, :] = v
        ks_ref[b, WINDOW + c0:WINDOW + c0 + rc, :] = pltpu.roll(k, HEAD_DIM, axis=1)
        vs_ref[b, WINDOW + c0:WINDOW + c0 + rc, :] = pltpu.roll(v, HEAD_DIM, axis=1)

    def pooling(b, c0, r0, rc):
        pos = tile_pos + c0 + lax.broadcasted_iota(jnp.int32, (rc, LANES), 0)
        for g, w in enumerate(POOL_WINDOWS):
            u = fc_ref[b * N_LG + g, HIST_C + c0:HIST_C + c0 + rc, :]
            acc = u
            for d in range(1, w):
                acc = acc + fc_ref[b * N_LG + g, HIST_C + c0 - d:HIST_C + c0 - d + rc, :]
            cnt = jnp.minimum(w, pos + 1).astype(F32)
            pc_ref[g, r0:r0 + rc, :] = acc / cnt - u

    def pooling_time_major(lo, n):
        b_lo, n_seq = lo // tt, n // tt

        for g in range(N_LG):
            fc_ref[g, lo:lo + n, :] = z_ref[lo:lo + n, LANES * g:LANES * (g + 1)]

        def slab(j_in, g):
            if j_in < POOL_PAST:
                return sc_ref[j_in, b_lo:b_lo + n_seq, LANES * g:LANES * (g + 1)]
            return fc_ref[g, pl.ds(lo + j_in - POOL_PAST, n_seq, stride=tt), :]

        for t in range(tt):
            for g, w in enumerate(POOL_WINDOWS):
                u = slab(POOL_PAST + t, g)
                acc = u
                for d in range(1, w):
                    acc = acc + slab(POOL_PAST + t - d, g)
                pc_ref[g, pl.ds(lo + t, n_seq, stride=tt), :] = acc / float(min(w, pos0 + t + 1)) - u
        for j_out in range(POOL_PAST):
            for g in range(N_LG):
                nc_ref[j_out, b_lo:b_lo + n_seq, LANES * g:LANES * (g + 1)] = slab(j_out + tt, g)

    def gating(r0, rc):
        for g in range(N_LG):
            lg = slice(LANES * g, LANES * (g + 1))
            mixed = jnp.dot(pc_ref[g, r0:r0 + rc, :].astype(BF16), pw_ref[g], preferred_element_type=F32)
            yc = mixed * ps_ref[:, lg] * _silu(z_ref[r0:r0 + rc, MIX_W + LANES * g:MIX_W + LANES * (g + 1)])
            mix_ref[r0:r0 + rc, lg] = yc.astype(BF16)
        yd = od_ref[r0:r0 + rc, 0:MIX_W] * _silu(z_ref[r0:r0 + rc, DG_COL:DG_COL + MIX_W])
        mix_ref[r0:r0 + rc, MIX_W:2 * MIX_W] = yd.astype(BF16)

    def mixer_tasks(lo, n):
        chunks = _seq_chunks(lo, n, tt)
        if gs == 1:
            blocks = [(r0 // tt, (r0 % tt) // qs) for r0 in range(lo, lo + n, qs)]
        else:
            blocks = [(b0, 0) for b0 in range(lo // tt, (lo + n) // tt, gs)]
        pool_tasks = ([functools.partial(pooling_time_major, lo, n)] if time_major_pool
                      else [functools.partial(pooling, *ch) for ch in chunks])
        return ([functools.partial(window_inputs, *ch) for ch in chunks]
                + pool_tasks
                + [functools.partial(attention, b0, sb, g) for b0, sb in blocks for g in range(N_KV)]
                + [functools.partial(gating, lo, n)])

    _pre_norm(x_ref, mod_ref, seq0, gpre_ref, h_ref, rows, tt)
    for c0 in range(0, ODD_IN, MXU_COLS):
        z_ref[:, c0:c0 + MXU_COLS] = jnp.dot(h_ref[...], win_ref[:, c0:c0 + MXU_COLS], preferred_element_type=F32)

    n_chunks = max(1, rows // WIN_CHUNK)
    n = rows // n_chunks
    for c in range(n_chunks + 1):
        if c > 0:
            _out_proj_task(mix_ref, wout_ref, o_ref, (c - 1) * n, n)()
        if c < n_chunks:
            for task in mixer_tasks(c * n, n):
                task()
        if c > 0:
            for task in _residual_tasks(x_ref, mod_ref, seq0, gpost_ref, o_ref, y_ref, (c - 1) * n, n, tt):
                task()

    for b in range(nb):
        for g in range(0 if time_major_pool else N_LG):
            nc_ref[b, :, LANES * g:LANES * (g + 1)] = fc_ref[b * N_LG + g, HIST_C + tt - POOL_PAST:HIST_C + tt, :]
            if nt > 1:
                fc_ref[b * N_LG + g, 0:HIST_C, :] = fc_ref[b * N_LG + g, tt:tt + HIST_C, :]
        nk_ref[b] = kf_ref[b, tt:tt + WINDOW, :]
        nv_ref[b] = vf_ref[b, tt:tt + WINDOW, :]
        if nt > 1:
            for r in (kf_ref, ks_ref, vf_ref, vs_ref):
                r[b, 0:WINDOW, :] = r[b, tt:tt + WINDOW, :]


def _odd_layer(x2d, mod, st_c, st_k, st_v, g_pre, w_in, pool_w, pool_scale, sinks, w_out, g_post,
               *, n, t, nb, tt, pos0, seq_base, time_major_pool):
    rows = nb * tt
    nt = t // tt
    kern = functools.partial(_odd_kernel, nb=nb, tt=tt, nt=nt, pos0=pos0, seq_base=seq_base,
                             time_major_pool=time_major_pool)
    row_spec = pl.BlockSpec((rows, D_MODEL), lambda i, j: (i * nt + j, 0))

    def seq_spec(shape):
        return pl.BlockSpec((nb,) + shape, lambda i, j: (i, 0, 0))

    def const_spec(shape):
        return pl.BlockSpec(shape, lambda i, j: (0,) * len(shape))

    if time_major_pool:
        pool_spec = pl.BlockSpec((POOL_PAST, nb, MIX_W), lambda i, j: (0, i, 0))
    else:
        pool_spec = seq_spec((POOL_PAST, MIX_W))

    return pl.pallas_call(
        kern,
        out_shape=(
            jax.ShapeDtypeStruct((n * t, D_MODEL), F32),
            jax.ShapeDtypeStruct(st_c.shape, F32),
            jax.ShapeDtypeStruct((n, WINDOW, KV_W), F32),
            jax.ShapeDtypeStruct((n, WINDOW, KV_W), F32),
        ),
        grid=(n // nb, nt),
        in_specs=[
            pl.BlockSpec(memory_space=pltpu.SMEM),
            row_spec,
            const_spec(mod.shape),
            pool_spec,
            seq_spec((WINDOW, KV_W)),
            seq_spec((WINDOW, KV_W)),
            const_spec((1, D_MODEL)),
            const_spec((D_MODEL, ODD_IN)),
            const_spec((N_LG, LANES, LANES)),
            const_spec((1, MIX_W)),
            const_spec((2 * MIX_W, D_MODEL)),
            const_spec((1, D_MODEL)),
        ],
        out_specs=(row_spec, pool_spec, seq_spec((WINDOW, KV_W)), seq_spec((WINDOW, KV_W))),
        scratch_shapes=[
            pltpu.VMEM((rows, D_MODEL), BF16),
            pltpu.VMEM((rows, ODD_IN + ROW_PAD), F32),
            pltpu.VMEM((rows, D_MODEL), F32),
            pltpu.VMEM((N_LG, rows, LANES) if time_major_pool else (nb * N_LG, HIST_C + tt, LANES), F32),
            pltpu.VMEM((nb, WINDOW + tt, KV_W), F32),
            pltpu.VMEM((nb, WINDOW + tt, KV_W), F32),
            pltpu.VMEM((nb, WINDOW + tt, KV_W), F32),
            pltpu.VMEM((nb, WINDOW + tt, KV_W), F32),
            pltpu.VMEM((N_LG, rows, LANES), F32),
            pltpu.VMEM((rows, MIX_W + ROW_PAD), F32),
            pltpu.VMEM((rows, 2 * MIX_W), BF16),
        ],
        compiler_params=pltpu.CompilerParams(
            dimension_semantics=("arbitrary", "arbitrary"), vmem_limit_bytes=VMEM_LIMIT_BYTES),
        name="odd_layer",
    )(sinks, x2d, mod, st_c, st_k, st_v, g_pre, w_in, pool_w, pool_scale, w_out, g_post)


def _tiling(n, t):
    if t >= 512:
        return 1, 512
    return min(n, 256 // t), t


def _trunk(x, mod_e, mod_o, seq_base, st_a, st_b, st_c, st_k, st_v, pos0, we, wo):
    n, t, _ = x.shape
    nb, tt = _tiling(n, t)
    short = t == tt and nb > 1

    def swap(st):
        return jnp.transpose(st, (1, 0, 2)) if short else st

    x2d, na, nbs = _even_layer(x.reshape(n * t, D_MODEL), mod_e, swap(st_a), swap(st_b), *we, n=n, t=t, nb=nb,
                               tt=tt, seq_base=seq_base, time_major_state=short)
    x2d, nc, nk, nv = _odd_layer(x2d, mod_o, swap(st_c), st_k.reshape(n, WINDOW, KV_W),
                                 st_v.reshape(n, WINDOW, KV_W), *wo, n=n, t=t, nb=nb, tt=tt, pos0=pos0,
                                 seq_base=seq_base, time_major_pool=short)
    na, nbs, nc = swap(na), swap(nbs), swap(nc)
    kv_shape = (1, n, WINDOW, N_KV, HEAD_DIM)
    return (x2d.reshape(n, t, D_MODEL), na[None], nbs[None], nc[None],
            nk.reshape(kv_shape), nv.reshape(kv_shape))


def kernel(x_prompt, x_sample, state_conv_a, state_conv_b, state_pool_c, cache_win_k, cache_win_v, c_prompt, c_sample, w_mod_e, b_mod_e, g_pre_e, g_post_e, w_in_e, conv_a_w, conv_b_w, conv_b_b, ln_b_g, ln_b_b, w_out_e, w_mod_o, b_mod_o, g_pre_o, g_post_o, w_in_o, pool_w, pool_scale, sinks, w_out_o):
    n_p = x_prompt.shape[0]
    assert w_in_e.shape[0] == 1 and w_in_o.shape[0] == 1, "one even and one odd layer"

    c_all = jnp.concatenate([c_prompt, c_sample], axis=0)
    mod_e = _adaln_mod(c_all, w_mod_e[0], b_mod_e[0])
    mod_o = _adaln_mod(c_all, w_mod_o[0], b_mod_o[0])

    we = (g_pre_e, w_in_e[0].astype(BF16), conv_a_w[0], conv_b_w[0], conv_b_b, ln_b_g, ln_b_b,
          w_out_e[0].astype(BF16), g_post_e)
    wo = (g_pre_o, w_in_o[0].astype(BF16), pool_w[0].astype(BF16), pool_scale, sinks[0],
          w_out_o[0].astype(BF16), g_post_o)

    dt = x_prompt.dtype
    z_a = jnp.zeros((n_p, CONV_A - 1, MIX_W), dt)
    z_b = jnp.zeros((n_p, CONV_B - 1, MIX_W), dt)
    z_c = jnp.zeros((n_p, POOL_PAST, MIX_W), dt)
    z_kv = jnp.zeros((n_p, WINDOW, N_KV, HEAD_DIM), dt)
    y_p, pa, pb, pc, pk, pv = _trunk(x_prompt, mod_e, mod_o, 0, z_a, z_b, z_c, z_kv, z_kv, 0, we, wo)
    y_s, sa, sb, sc, sk, sv = _trunk(x_sample, mod_e, mod_o, n_p, state_conv_a[0], state_conv_b[0],
                                     state_pool_c[0], cache_win_k[0], cache_win_v[0], PAST_LEN, we, wo)
    return (y_p, y_s, pa, sa, pb, sb, pc, sc, pk, sk, pv, sv)
```

```python
import functools

import jax
import jax.numpy as jnp
from jax import lax
from jax.experimental import pallas as pl
from jax.experimental.pallas import tpu as pltpu

F32 = jnp.float32
BF16 = jnp.bfloat16

D_MODEL = 1024
MIX_W = 512
LANES = 128
MXU_COLS = 256
N_LG = MIX_W // LANES
HEAD_DIM = 64
N_HEADS = 8
N_KV = 2
N_REP = N_HEADS // N_KV
KV_W = N_KV * HEAD_DIM
CONV_A = 3
CONV_B = 31
POOL_WINDOWS = (2, 4, 8, 16)
POOL_PAST = 15
WINDOW = 128
PAST_LEN = 8192
EVEN_IN = 7 * MIX_W
ODD_IN = 4 * MIX_W + 2 * KV_W
Q_COL, K_COL, V_COL, DG_COL = 2 * MIX_W, 3 * MIX_W, 3 * MIX_W + KV_W, 3 * MIX_W + 2 * KV_W
RMS_EPS = 1e-6
LN_EPS = 1e-5
HIST_A = 8
HIST_B = 32
HIST_C = 16
ATT_MIN_Q_ROWS = 32
TILE_ROWS = 1024
SHORT_TILE_ROWS = 256
ROW_CHUNK = 64
WIN_CHUNK = 128
MOD_K_BLOCK = 256
ROW_PAD = LANES
VMEM_LIMIT_BYTES = 56 * 1024 * 1024
NEG_LOG2_E = -1.4426950408889634
ALIBI_SLOPES = tuple(float(2.0 ** (-8.0 * (h + 1) / N_HEADS)) for h in range(N_HEADS))


def _sigmoid(x):
    return 1.0 / (1.0 + jnp.exp2(x * NEG_LOG2_E))


def _silu(x):
    return x * _sigmoid(x)


def _mod_kernel(c_ref, w_ref, b_ref, o_ref):
    k = pl.program_id(0)
    part = jnp.dot(_silu(c_ref[...]).astype(BF16), w_ref[...].astype(BF16), preferred_element_type=F32)

    @pl.when(k == 0)
    def _first():
        o_ref[...] = part + b_ref[...]

    @pl.when(k > 0)
    def _rest():
        o_ref[...] += part


def _adaln_mod(c_all, w_mod, b_mod):
    n = c_all.shape[0]
    return pl.pallas_call(
        _mod_kernel,
        out_shape=jax.ShapeDtypeStruct((n, 3 * D_MODEL), F32),
        grid=(D_MODEL // MOD_K_BLOCK,),
        in_specs=[
            pl.BlockSpec((n, MOD_K_BLOCK), lambda k: (0, k)),
            pl.BlockSpec((MOD_K_BLOCK, 3 * D_MODEL), lambda k: (k, 0)),
            pl.BlockSpec((1, 3 * D_MODEL), lambda k: (0, 0)),
        ],
        out_specs=pl.BlockSpec((n, 3 * D_MODEL), lambda k: (0, 0)),
        compiler_params=pltpu.CompilerParams(dimension_semantics=("arbitrary",)),
        name="adaln_mod",
    )(c_all, w_mod, b_mod.reshape(1, -1))


def _mod_rows(mod_ref, seq0, col, r0, rc, tt):
    if tt >= rc:
        return mod_ref[pl.ds(seq0 + r0 // tt, 1), col:col + D_MODEL]
    parts = [jnp.broadcast_to(mod_ref[pl.ds(seq0 + b, 1), col:col + D_MODEL], (tt, D_MODEL))
             for b in range(r0 // tt, (r0 + rc) // tt)]
    return jnp.concatenate(parts, axis=0)


def _pre_norm(x_ref, mod_ref, seq0, g_ref, h_ref, rows, tt):
    rc = min(rows, ROW_CHUNK)
    for r0 in range(0, rows, rc):
        x = x_ref[r0:r0 + rc, :]
        ms = jnp.mean(x * x, axis=-1, keepdims=True)
        gain = g_ref[...] * (1.0 + _mod_rows(mod_ref, seq0, D_MODEL, r0, rc, tt))
        h = x * lax.rsqrt(ms + RMS_EPS) * gain + _mod_rows(mod_ref, seq0, 0, r0, rc, tt)
        h_ref[r0:r0 + rc, :] = h.astype(BF16)


def _out_proj_task(mix_ref, wout_ref, o_ref, lo, n):
    def task():
        o_ref[lo:lo + n, :] = jnp.dot(mix_ref[lo:lo + n, :], wout_ref[...], preferred_element_type=F32)
    return task


def _residual_tasks(x_ref, mod_ref, seq0, g_ref, o_ref, y_ref, lo, n, tt):
    rc = min(n, ROW_CHUNK)

    def chunk(r0):
        o = o_ref[r0:r0 + rc, :]
        ms = jnp.mean(o * o, axis=-1, keepdims=True)
        gain = g_ref[...] * _mod_rows(mod_ref, seq0, 2 * D_MODEL, r0, rc, tt)
        y_ref[r0:r0 + rc, :] = x_ref[r0:r0 + rc, :] + o * lax.rsqrt(ms + RMS_EPS) * gain
    return [functools.partial(chunk, r0) for r0 in range(lo, lo + n, rc)]


def _seq_chunks(lo, n, tt):
    rc = min(tt, n, WIN_CHUNK)
    return [(r0 // tt, r0 % tt, r0, rc) for r0 in range(lo, lo + n, rc)]


def _even_gating(z_ref, ca_ref, cb_ref, cbb_ref, lng_ref, lnb_ref, mix_ref, rows):
    rc = min(rows, ROW_CHUNK)
    for r0 in range(0, rows, rc):
        def zs(c):
            return z_ref[r0:r0 + rc, MIX_W * c:MIX_W * (c + 1)]

        def all_lanes(c_ref):
            return jnp.concatenate([c_ref[g, r0:r0 + rc, :] for g in range(N_LG)], axis=1)
        ya = zs(1) * all_lanes(ca_ref) * _silu(zs(3))
        mix_ref[r0:r0 + rc, 0:MIX_W] = ya.astype(BF16)
        yb = all_lanes(cb_ref) + cbb_ref[...]
        mu = jnp.mean(yb, axis=-1, keepdims=True)
        yc = yb - mu
        var = jnp.mean(yc * yc, axis=-1, keepdims=True)
        ln = yc * lax.rsqrt(var + LN_EPS) * lng_ref[...] + lnb_ref[...]
        mix_ref[r0:r0 + rc, MIX_W:2 * MIX_W] = (_silu(ln) * _silu(zs(6))).astype(BF16)


def _even_kernel(x_ref, mod_ref, sa_ref, sb_ref, gpre_ref, win_ref, cwa_ref, cwb_ref, cbb_ref,
                 lng_ref, lnb_ref, wout_ref, gpost_ref,
                 y_ref, na_ref, nb_ref,
                 h_ref, z_ref, o_ref, fa_ref, fb_ref, ca_ref, cb_ref, mix_ref,
                 *, nb, tt, nt, seq_base, time_major_state):
    rows = nb * tt
    j = pl.program_id(1)
    seq0 = seq_base + pl.program_id(0) * nb
    assert not time_major_state or nt == 1

    @pl.when(j == 0)
    def _load_state():
        for b in range(0 if time_major_state else nb):
            for g in range(N_LG):
                lg = slice(LANES * g, LANES * (g + 1))
                fa_ref[b * N_LG + g, 0:HIST_A, :] = jnp.zeros((HIST_A, LANES), F32)
                fa_ref[b * N_LG + g, HIST_A - (CONV_A - 1):HIST_A, :] = sa_ref[b, :, lg]
                fb_ref[b * N_LG + g, 0:8, :] = jnp.zeros((8, LANES), F32)
                fb_ref[b * N_LG + g, HIST_B - (CONV_B - 1):HIST_B, :] = sb_ref[b, :, lg]

    _pre_norm(x_ref, mod_ref, seq0, gpre_ref, h_ref, rows, tt)

    for c0 in range(0, EVEN_IN, MXU_COLS):
        z_ref[:, c0:c0 + MXU_COLS] = jnp.dot(h_ref[...], win_ref[:, c0:c0 + MXU_COLS], preferred_element_type=F32)

    if time_major_state:
        rc = min(rows, WIN_CHUNK)
        for r0 in range(0, rows, rc):
            for g in range(N_LG):
                def zc(c):
                    return z_ref[r0:r0 + rc, MIX_W * c + LANES * g:MIX_W * c + LANES * (g + 1)]
                fa_ref[g, r0:r0 + rc, :] = zc(2) * zc(0)
                fb_ref[g, r0:r0 + rc, :] = zc(4) * _sigmoid(zc(5))

        def conv(w_ref, state_ref, u_ref, out_ref, new_state_ref, taps):
            hist = taps - 1

            def slab(j_in, g):
                if j_in < hist:
                    return state_ref[j_in, :, LANES * g:LANES * (g + 1)]
                return u_ref[g, pl.ds(j_in - hist, nb, stride=tt), :]

            for t in range(tt):
                for g in range(N_LG):
                    acc = None
                    for k in range(taps):
                        term = w_ref[k:k + 1, LANES * g:LANES * (g + 1)] * slab(t + k, g)
                        acc = term if acc is None else acc + term
                    out_ref[g, pl.ds(t, nb, stride=tt), :] = acc
            for j_out in range(hist):
                for g in range(N_LG):
                    new_state_ref[j_out, :, LANES * g:LANES * (g + 1)] = slab(j_out + tt, g)

        conv(cwa_ref, sa_ref, fa_ref, ca_ref, na_ref, CONV_A)
        conv(cwb_ref, sb_ref, fb_ref, cb_ref, nb_ref, CONV_B)

    for b, c0, r0, rc in ([] if time_major_state else _seq_chunks(0, rows, tt)):
        for g in range(N_LG):
            def zc(c):
                return z_ref[r0:r0 + rc, MIX_W * c + LANES * g:MIX_W * c + LANES * (g + 1)]
            fa_ref[b * N_LG + g, HIST_A + c0:HIST_A + c0 + rc, :] = zc(2) * zc(0)
            fb_ref[b * N_LG + g, HIST_B + c0:HIST_B + c0 + rc, :] = zc(4) * _sigmoid(zc(5))

    for b, c0, r0, rc in ([] if time_major_state else _seq_chunks(0, rows, tt)):
        for g in range(N_LG):
            lg = slice(LANES * g, LANES * (g + 1))
            acc = None
            for k in range(CONV_A):
                off = HIST_A - (CONV_A - 1) + k + c0
                t = cwa_ref[k:k + 1, lg] * fa_ref[b * N_LG + g, off:off + rc, :]
                acc = t if acc is None else acc + t
            ca_ref[g, r0:r0 + rc, :] = acc
            acc = None
            for k in range(CONV_B):
                off = HIST_B - (CONV_B - 1) + k + c0
                t = cwb_ref[k:k + 1, lg] * fb_ref[b * N_LG + g, off:off + rc, :]
                acc = t if acc is None else acc + t
            cb_ref[g, r0:r0 + rc, :] = acc

    for b in range(0 if time_major_state else nb):
        for g in range(N_LG):
            lg = slice(LANES * g, LANES * (g + 1))
            na_ref[b, :, lg] = fa_ref[b * N_LG + g, HIST_A + tt - (CONV_A - 1):HIST_A + tt, :]
            nb_ref[b, :, lg] = fb_ref[b * N_LG + g, HIST_B + tt - (CONV_B - 1):HIST_B + tt, :]
            if nt > 1:
                fa_ref[b * N_LG + g, 0:HIST_A, :] = fa_ref[b * N_LG + g, tt:tt + HIST_A, :]
                fb_ref[b * N_LG + g, 0:HIST_B, :] = fb_ref[b * N_LG + g, tt:tt + HIST_B, :]

    _even_gating(z_ref, ca_ref, cb_ref, cbb_ref, lng_ref, lnb_ref, mix_ref, rows)
    _out_proj_task(mix_ref, wout_ref, o_ref, 0, rows)()
    for task in _residual_tasks(x_ref, mod_ref, seq0, gpost_ref, o_ref, y_ref, 0, rows, tt):
        task()


def _even_layer(x2d, mod, st_a, st_b, g_pre, w_in, cw_a, cw_b, cb_b, ln_g, ln_b, w_out, g_post,
                *, n, t, nb, tt, seq_base, time_major_state):
    rows = nb * tt
    nt = t // tt
    kern = functools.partial(_even_kernel, nb=nb, tt=tt, nt=nt, seq_base=seq_base,
                             time_major_state=time_major_state)
    row_spec = pl.BlockSpec((rows, D_MODEL), lambda i, j: (i * nt + j, 0))

    def seq_spec(shape):
        if time_major_state:
            return pl.BlockSpec((shape[0], nb, shape[1]), lambda i, j: (0, i, 0))
        return pl.BlockSpec((nb,) + shape, lambda i, j: (i, 0, 0))

    def window_scratch(hist):
        if time_major_state:
            return pltpu.VMEM((N_LG, rows, LANES), F32)
        return pltpu.VMEM((nb * N_LG, hist + tt, LANES), F32)

    def const_spec(shape):
        return pl.BlockSpec(shape, lambda i, j: (0,) * len(shape))

    return pl.pallas_call(
        kern,
        out_shape=(
            jax.ShapeDtypeStruct((n * t, D_MODEL), F32),
            jax.ShapeDtypeStruct(st_a.shape, F32),
            jax.ShapeDtypeStruct(st_b.shape, F32),
        ),
        grid=(n // nb, nt),
        in_specs=[
            row_spec,
            const_spec(mod.shape),
            seq_spec((CONV_A - 1, MIX_W)),
            seq_spec((CONV_B - 1, MIX_W)),
            const_spec((1, D_MODEL)),
            const_spec((D_MODEL, EVEN_IN)),
            const_spec((CONV_A, MIX_W)),
            const_spec((CONV_B, MIX_W)),
            const_spec((1, MIX_W)),
            const_spec((1, MIX_W)),
            const_spec((1, MIX_W)),
            const_spec((2 * MIX_W, D_MODEL)),
            const_spec((1, D_MODEL)),
        ],
        out_specs=(row_spec, seq_spec((CONV_A - 1, MIX_W)), seq_spec((CONV_B - 1, MIX_W))),
        scratch_shapes=[
            pltpu.VMEM((rows, D_MODEL), BF16),
            pltpu.VMEM((rows, EVEN_IN + ROW_PAD), F32),
            pltpu.VMEM((rows, D_MODEL), F32),
            window_scratch(HIST_A),
            window_scratch(HIST_B),
            pltpu.VMEM((N_LG, rows, LANES), F32),
            pltpu.VMEM((N_LG, rows, LANES), F32),
            pltpu.VMEM((rows, 2 * MIX_W), BF16),
        ],
        compiler_params=pltpu.CompilerParams(
            dimension_semantics=("arbitrary", "arbitrary"), vmem_limit_bytes=VMEM_LIMIT_BYTES),
        name="even_layer",
    )(x2d, mod, st_a, st_b, g_pre, w_in, cw_a, cw_b, cb_b, ln_g, ln_b, w_out, g_post)


def _odd_kernel(sinks_ref, x_ref, mod_ref, sc_ref, sk_ref, sv_ref, gpre_ref, win_ref, pw_ref, ps_ref,
                wout_ref, gpost_ref,
                y_ref, nc_ref, nk_ref, nv_ref,
                h_ref, z_ref, o_ref, fc_ref, kf_ref, ks_ref, vf_ref, vs_ref, pc_ref, od_ref, mix_ref,
                *, nb, tt, nt, pos0, seq_base, time_major_pool):
    rows = nb * tt
    j = pl.program_id(1)
    seq0 = seq_base + pl.program_id(0) * nb
    tile_pos = pos0 + j * tt
    assert not time_major_pool or nt == 1

    @pl.when(j == 0)
    def _load_state():
        for b in range(nb):
            for g in range(0 if time_major_pool else N_LG):
                fc_ref[b * N_LG + g, 0:8, :] = jnp.zeros((8, LANES), F32)
                fc_ref[b * N_LG + g, HIST_C - POOL_PAST:HIST_C, :] = sc_ref[b, :, LANES * g:LANES * (g + 1)]
            kf_ref[b, 0:WINDOW, :] = sk_ref[b]
            vf_ref[b, 0:WINDOW, :] = sv_ref[b]
            ks_ref[b, 0:WINDOW, :] = pltpu.roll(sk_ref[b], HEAD_DIM, axis=1)
            vs_ref[b, 0:WINDOW, :] = pltpu.roll(sv_ref[b], HEAD_DIM, axis=1)

    qs = min(tt, WINDOW)
    gs = max(1, ATT_MIN_Q_ROWS // qs)
    kw = WINDOW + qs
    n_q, n_keys = gs * qs, gs * kw
    assert N_REP == 4 and 2 * HEAD_DIM == LANES and KV_W == LANES
    assert qs & (qs - 1) == 0 and gs & (gs - 1) == 0 and nb % gs == 0 and (gs == 1 or tt == qs)
    shape = (2 * n_q, n_keys)
    ri = lax.broadcasted_iota(jnp.int32, shape, 0)
    ci = lax.broadcasted_iota(jnp.int32, shape, 1)
    q_seq = (ri >> (qs.bit_length() - 1)) & (gs - 1)
    k_seq = jnp.zeros(shape, jnp.int32)
    for s_i in range(1, gs):
        k_seq = k_seq + (ci >= s_i * kw).astype(jnp.int32)
    k_c = ci - k_seq * kw
    dist = (ri & (qs - 1)) + WINDOW - k_c
    band = (q_seq == k_seq) & (dist >= 0) & (dist < WINDOW)
    dist_f = dist.astype(F32)
    upper_grp = ri >= n_q
    upper_grp_col = lax.broadcasted_iota(jnp.int32, (2 * n_q, 1), 0) >= n_q
    low_lanes = lax.broadcasted_iota(jnp.int32, (n_keys, LANES), 1) < HEAD_DIM
    ones_cols = jnp.ones((n_keys, LANES), BF16)

    def head_of(g, grp, parity):
        return g * N_REP + 2 * grp + parity

    def alibi_bias(g, parity):
        slope = jnp.where(upper_grp, ALIBI_SLOPES[head_of(g, 1, parity)], ALIBI_SLOPES[head_of(g, 0, parity)])
        return jnp.where(band, -(slope * dist_f), -jnp.inf)

    biases = {(g, parity): alibi_bias(g, parity) for g in range(N_KV) for parity in range(2)}

    def attention(b0, sb, g):
        r0 = b0 * tt + sb * qs
        p0 = tile_pos + sb * qs
        in_range = (k_c >= WINDOW - p0) if pos0 + sb * qs < WINDOW else None
        if gs == 1:
            win = (b0, slice(sb * qs, sb * qs + kw))
            k_nat, k_swp, v_nat, v_swp = kf_ref[win], ks_ref[win], vf_ref[win], vs_ref[win]
        else:
            k_nat, k_swp, v_nat, v_swp = (r[b0:b0 + gs].reshape(n_keys, KV_W)
                                          for r in (kf_ref, ks_ref, vf_ref, vs_ref))
        k_lo, k_hi = (k_nat, k_swp) if g == 0 else (k_swp, k_nat)
        v_lo, v_hi = (v_nat, v_swp) if g == 0 else (v_swp, v_nat)
        v_ext = jnp.concatenate([jnp.where(low_lanes, v_lo, v_hi).astype(BF16), ones_cols], axis=1)
        qg = jnp.concatenate([z_ref[r0:r0 + n_q, Q_COL + 2 * LANES * g + LANES * grp:
                                    Q_COL + 2 * LANES * g + LANES * (grp + 1)] for grp in range(2)], axis=0)
        qg = (qg * (HEAD_DIM ** -0.5)).astype(BF16)
        for parity in range(2):
            k_ext = (jnp.where(low_lanes, k_lo, 0.0) if parity == 0 else jnp.where(low_lanes, 0.0, k_hi)).astype(BF16)
            sink = jnp.where(upper_grp_col, sinks_ref[head_of(g, 1, parity)], sinks_ref[head_of(g, 0, parity)])
            sc = lax.dot_general(qg, k_ext, (((1,), (1,)), ((), ())), preferred_element_type=F32)
            bias = biases[g, parity]
            sc = sc + (bias if in_range is None else jnp.where(in_range, bias, -jnp.inf))
            m = jnp.maximum(jnp.max(sc, axis=-1, keepdims=True), sink)
            p = jnp.exp(sc - m)
            o_ext = jnp.dot(p.astype(BF16), v_ext, preferred_element_type=F32)
            o = o_ext[:, 0:LANES] / (o_ext[:, LANES:2 * LANES] + jnp.exp(sink - m))
            for grp in range(2):
                h = head_of(g, grp, parity)
                od_ref[r0:r0 + n_q, HEAD_DIM * h:HEAD_DIM * (h + 1)] = (
                    o[grp * n_q:(grp + 1) * n_q, HEAD_DIM * parity:HEAD_DIM * (parity + 1)])

    def window_inputs(b, c0, r0, rc):
        for g in range(0 if time_major_pool else N_LG):
            fc_ref[b * N_LG + g, HIST_C + c0:HIST_C + c0 + rc, :] = z_ref[r0:r0 + rc, LANES * g:LANES * (g + 1)]
        k = z_ref[r0:r0 + rc, K_COL:K_COL + KV_W]
        v = z_ref[r0:r0 + rc, V_COL:V_COL + KV_W]
        kf_ref[b, WINDOW + c0:WINDOW + c0 + rc, :] = k
        vf_ref[b, WINDOW + c0:WINDOW + c0 + rc, :] = v
        ks_ref[b, WINDOW + c0:WINDOW + c0 + rc, :] = pltpu.roll(k, HEAD_DIM, axis=1)
        vs_ref[b, WINDOW + c0:WINDOW + c0 + rc, :] = pltpu.roll(v, HEAD_DIM, axis=1)

    def pooling(b, c0, r0, rc):
        pos = tile_pos + c0 + lax.broadcasted_iota(jnp.int32, (rc, LANES), 0)
        for g, w in enumerate(POOL_WINDOWS):
            u = fc_ref[b * N_LG + g, HIST_C + c0:HIST_C + c0 + rc, :]
            acc = u
            for d in range(1, w):
                acc = acc + fc_ref[b * N_LG + g, HIST_C + c0 - d:HIST_C + c0 - d + rc, :]
            cnt = jnp.minimum(w, pos + 1).astype(F32)
            pc_ref[g, r0:r0 + rc, :] = acc / cnt - u

    def pooling_time_major(lo, n):
        b_lo, n_seq = lo // tt, n // tt

        for g in range(N_LG):
            fc_ref[g, lo:lo + n, :] = z_ref[lo:lo + n, LANES * g:LANES * (g + 1)]

        def slab(j_in, g):
            if j_in < POOL_PAST:
                return sc_ref[j_in, b_lo:b_lo + n_seq, LANES * g:LANES * (g + 1)]
            return fc_ref[g, pl.ds(lo + j_in - POOL_PAST, n_seq, stride=tt), :]

        for t in range(tt):
            for g, w in enumerate(POOL_WINDOWS):
                u = slab(POOL_PAST + t, g)
                acc = u
                for d in range(1, w):
                    acc = acc + slab(POOL_PAST + t - d, g)
                pc_ref[g, pl.ds(lo + t, n_seq, stride=tt), :] = acc / float(min(w, pos0 + t + 1)) - u
        for j_out in range(POOL_PAST):
            for g in range(N_LG):
                nc_ref[j_out, b_lo:b_lo + n_seq, LANES * g:LANES * (g + 1)] = slab(j_out + tt, g)

    def gating(r0, rc):
        for g in range(N_LG):
            lg = slice(LANES * g, LANES * (g + 1))
            mixed = jnp.dot(pc_ref[g, r0:r0 + rc, :].astype(BF16), pw_ref[g], preferred_element_type=F32)
            yc = mixed * ps_ref[:, lg] * _silu(z_ref[r0:r0 + rc, MIX_W + LANES * g:MIX_W + LANES * (g + 1)])
            mix_ref[r0:r0 + rc, lg] = yc.astype(BF16)
        yd = od_ref[r0:r0 + rc, 0:MIX_W] * _silu(z_ref[r0:r0 + rc, DG_COL:DG_COL + MIX_W])
        mix_ref[r0:r0 + rc, MIX_W:2 * MIX_W] = yd.astype(BF16)

    def mixer_tasks(lo, n):
        chunks = _seq_chunks(lo, n, tt)
        if gs == 1:
            blocks = [(r0 // tt, (r0 % tt) // qs) for r0 in range(lo, lo + n, qs)]
        else:
            blocks = [(b0, 0) for b0 in range(lo // tt, (lo + n) // tt, gs)]
        pool_tasks = ([functools.partial(pooling_time_major, lo, n)] if time_major_pool
                      else [functools.partial(pooling, *ch) for ch in chunks])
        return ([functools.partial(window_inputs, *ch) for ch in chunks]
                + pool_tasks
                + [functools.partial(attention, b0, sb, g) for b0, sb in blocks for g in range(N_KV)]
                + [functools.partial(gating, lo, n)])

    _pre_norm(x_ref, mod_ref, seq0, gpre_ref, h_ref, rows, tt)
    for c0 in range(0, ODD_IN, MXU_COLS):
        z_ref[:, c0:c0 + MXU_COLS] = jnp.dot(h_ref[...], win_ref[:, c0:c0 + MXU_COLS], preferred_element_type=F32)

    n_chunks = max(1, rows // WIN_CHUNK)
    n = rows // n_chunks
    for c in range(n_chunks + 1):
        if c > 0:
            _out_proj_task(mix_ref, wout_ref, o_ref, (c - 1) * n, n)()
        if c < n_chunks:
            for task in mixer_tasks(c * n, n):
                task()
        if c > 0:
            for task in _residual_tasks(x_ref, mod_ref, seq0, gpost_ref, o_ref, y_ref, (c - 1) * n, n, tt):
                task()

    for b in range(nb):
        for g in range(0 if time_major_pool else N_LG):
            nc_ref[b, :, LANES * g:LANES * (g + 1)] = fc_ref[b * N_LG + g, HIST_C + tt - POOL_PAST:HIST_C + tt, :]
            if nt > 1:
                fc_ref[b * N_LG + g, 0:HIST_C, :] = fc_ref[b * N_LG + g, tt:tt + HIST_C, :]
        nk_ref[b] = kf_ref[b, tt:tt + WINDOW, :]
        nv_ref[b] = vf_ref[b, tt:tt + WINDOW, :]
        if nt > 1:
            for r in (kf_ref, ks_ref, vf_ref, vs_ref):
                r[b, 0:WINDOW, :] = r[b, tt:tt + WINDOW, :]


def _odd_layer(x2d, mod, st_c, st_k, st_v, g_pre, w_in, pool_w, pool_scale, sinks, w_out, g_post,
               *, n, t, nb, tt, pos0, seq_base, time_major_pool):
    rows = nb * tt
    nt = t // tt
    kern = functools.partial(_odd_kernel, nb=nb, tt=tt, nt=nt, pos0=pos0, seq_base=seq_base,
                             time_major_pool=time_major_pool)
    row_spec = pl.BlockSpec((rows, D_MODEL), lambda i, j: (i * nt + j, 0))

    def seq_spec(shape):
        return pl.BlockSpec((nb,) + shape, lambda i, j: (i, 0, 0))

    def const_spec(shape):
        return pl.BlockSpec(shape, lambda i, j: (0,) * len(shape))

    if time_major_pool:
        pool_spec = pl.BlockSpec((POOL_PAST, nb, MIX_W), lambda i, j: (0, i, 0))
    else:
        pool_spec = seq_spec((POOL_PAST, MIX_W))

    return pl.pallas_call(
        kern,
        out_shape=(
            jax.ShapeDtypeStruct((n * t, D_MODEL), F32),
            jax.ShapeDtypeStruct(st_c.shape, F32),
            jax.ShapeDtypeStruct((n, WINDOW, KV_W), F32),
            jax.ShapeDtypeStruct((n, WINDOW, KV_W), F32),
        ),
        grid=(n // nb, nt),
        in_specs=[
            pl.BlockSpec(memory_space=pltpu.SMEM),
            row_spec,
            const_spec(mod.shape),
            pool_spec,
            seq_spec((WINDOW, KV_W)),
            seq_spec((WINDOW, KV_W)),
            const_spec((1, D_MODEL)),
            const_spec((D_MODEL, ODD_IN)),
            const_spec((N_LG, LANES, LANES)),
            const_spec((1, MIX_W)),
            const_spec((2 * MIX_W, D_MODEL)),
            const_spec((1, D_MODEL)),
        ],
        out_specs=(row_spec, pool_spec, seq_spec((WINDOW, KV_W)), seq_spec((WINDOW, KV_W))),
        scratch_shapes=[
            pltpu.VMEM((rows, D_MODEL), BF16),
            pltpu.VMEM((rows, ODD_IN + ROW_PAD), F32),
            pltpu.VMEM((rows, D_MODEL), F32),
            pltpu.VMEM((N_LG, rows, LANES) if time_major_pool else (nb * N_LG, HIST_C + tt, LANES), F32),
            pltpu.VMEM((nb, WINDOW + tt, KV_W), F32),
            pltpu.VMEM((nb, WINDOW + tt, KV_W), F32),
            pltpu.VMEM((nb, WINDOW + tt, KV_W), F32),
            pltpu.VMEM((nb, WINDOW + tt, KV_W), F32),
            pltpu.VMEM((N_LG, rows, LANES), F32),
            pltpu.VMEM((rows, MIX_W + ROW_PAD), F32),
            pltpu.VMEM((rows, 2 * MIX_W), BF16),
        ],
        compiler_params=pltpu.CompilerParams(
            dimension_semantics=("arbitrary", "arbitrary"), vmem_limit_bytes=VMEM_LIMIT_BYTES),
        name="odd_layer",
    )(sinks, x2d, mod, st_c, st_k, st_v, g_pre, w_in, pool_w, pool_scale, w_out, g_post)


def _tiling(n, t):
    if t >= TILE_ROWS:
        return 1, TILE_ROWS
    return min(n, SHORT_TILE_ROWS // t), t


def _trunk(x, mod_e, mod_o, seq_base, st_a, st_b, st_c, st_k, st_v, pos0, we, wo):
    n, t, _ = x.shape
    nb, tt = _tiling(n, t)
    short = t == tt and nb > 1

    def swap(st):
        return jnp.transpose(st, (1, 0, 2)) if short else st

    x2d, na, nbs = _even_layer(x.reshape(n * t, D_MODEL), mod_e, swap(st_a), swap(st_b), *we, n=n, t=t, nb=nb,
                               tt=tt, seq_base=seq_base, time_major_state=short)
    x2d, nc, nk, nv = _odd_layer(x2d, mod_o, swap(st_c), st_k.reshape(n, WINDOW, KV_W),
                                 st_v.reshape(n, WINDOW, KV_W), *wo, n=n, t=t, nb=nb, tt=tt, pos0=pos0,
                                 seq_base=seq_base, time_major_pool=short)
    na, nbs, nc = swap(na), swap(nbs), swap(nc)
    kv_shape = (1, n, WINDOW, N_KV, HEAD_DIM)
    return (x2d.reshape(n, t, D_MODEL), na[None], nbs[None], nc[None],
            nk.reshape(kv_shape), nv.reshape(kv_shape))


def kernel(x_prompt, x_sample, state_conv_a, state_conv_b, state_pool_c, cache_win_k, cache_win_v, c_prompt, c_sample, w_mod_e, b_mod_e, g_pre_e, g_post_e, w_in_e, conv_a_w, conv_b_w, conv_b_b, ln_b_g, ln_b_b, w_out_e, w_mod_o, b_mod_o, g_pre_o, g_post_o, w_in_o, pool_w, pool_scale, sinks, w_out_o):
    n_p = x_prompt.shape[0]
    assert w_in_e.shape[0] == 1 and w_in_o.shape[0] == 1, "one even and one odd layer"

    c_all = jnp.concatenate([c_prompt, c_sample], axis=0)
    mod_e = _adaln_mod(c_all, w_mod_e[0], b_mod_e[0])
    mod_o = _adaln_mod(c_all, w_mod_o[0], b_mod_o[0])

    we = (g_pre_e, w_in_e[0].astype(BF16), conv_a_w[0], conv_b_w[0], conv_b_b, ln_b_g, ln_b_b,
          w_out_e[0].astype(BF16), g_post_e)
    wo = (g_pre_o, w_in_o[0].astype(BF16), pool_w[0].astype(BF16), pool_scale, sinks[0],
          w_out_o[0].astype(BF16), g_post_o)

    dt = x_prompt.dtype
    z_a = jnp.zeros((n_p, CONV_A - 1, MIX_W), dt)
    z_b = jnp.zeros((n_p, CONV_B - 1, MIX_W), dt)
    z_c = jnp.zeros((n_p, POOL_PAST, MIX_W), dt)
    z_kv = jnp.zeros((n_p, WINDOW, N_KV, HEAD_DIM), dt)
    y_p, pa, pb, pc, pk, pv = _trunk(x_prompt, mod_e, mod_o, 0, z_a, z_b, z_c, z_kv, z_kv, 0, we, wo)
    y_s, sa, sb, sc, sk, sv = _trunk(x_sample, mod_e, mod_o, n_p, state_conv_a[0], state_conv_b[0],
                                     state_pool_c[0], cache_win_k[0], cache_win_v[0], PAST_LEN, we, wo)
    return (y_p, y_s, pa, sa, pb, sb, pc, sc, pk, sk, pv, sv)
```

```python
import functools

import jax
import jax.numpy as jnp
from jax import lax
from jax.experimental import pallas as pl
from jax.experimental.pallas import tpu as pltpu

F32 = jnp.float32
BF16 = jnp.bfloat16

D_MODEL = 1024
MIX_W = 512
LANES = 128
MXU_COLS = 256
N_LG = MIX_W // LANES
HEAD_DIM = 64
N_HEADS = 8
N_KV = 2
N_REP = N_HEADS // N_KV
KV_W = N_KV * HEAD_DIM
CONV_A = 3
CONV_B = 31
POOL_WINDOWS = (2, 4, 8, 16)
POOL_PAST = 15
WINDOW = 128
PAST_LEN = 8192
EVEN_IN = 7 * MIX_W
ODD_IN = 4 * MIX_W + 2 * KV_W
Q_COL, K_COL, V_COL, DG_COL = 2 * MIX_W, 3 * MIX_W, 3 * MIX_W + KV_W, 3 * MIX_W + 2 * KV_W
RMS_EPS = 1e-6
LN_EPS = 1e-5
HIST_A = 8
HIST_B = 32
HIST_C = 16
ATT_MIN_Q_ROWS = 32
TILE_ROWS = 1024
SHORT_TILE_ROWS = 256
ROW_CHUNK = 64
WIN_CHUNK = 128
MOD_K_BLOCK = 256
ROW_PAD = LANES
VMEM_LIMIT_BYTES = 56 * 1024 * 1024
NEG_LOG2_E = -1.4426950408889634
ALIBI_SLOPES = tuple(float(2.0 ** (-8.0 * (h + 1) / N_HEADS)) for h in range(N_HEADS))


def _sigmoid(x):
    return 1.0 / (1.0 + jnp.exp2(x * NEG_LOG2_E))


def _silu(x):
    return x * _sigmoid(x)


def _mod_kernel(c_ref, w_ref, b_ref, o_ref):
    k = pl.program_id(0)
    part = jnp.dot(_silu(c_ref[...]).astype(BF16), w_ref[...].astype(BF16), preferred_element_type=F32)

    @pl.when(k == 0)
    def _first():
        o_ref[...] = part + b_ref[...]

    @pl.when(k > 0)
    def _rest():
        o_ref[...] += part


def _adaln_mod(c_all, w_mod, b_mod):
    n = c_all.shape[0]
    return pl.pallas_call(
        _mod_kernel,
        out_shape=jax.ShapeDtypeStruct((n, 3 * D_MODEL), F32),
        grid=(D_MODEL // MOD_K_BLOCK,),
        in_specs=[
            pl.BlockSpec((n, MOD_K_BLOCK), lambda k: (0, k)),
            pl.BlockSpec((MOD_K_BLOCK, 3 * D_MODEL), lambda k: (k, 0)),
            pl.BlockSpec((1, 3 * D_MODEL), lambda k: (0, 0)),
        ],
        out_specs=pl.BlockSpec((n, 3 * D_MODEL), lambda k: (0, 0)),
        compiler_params=pltpu.CompilerParams(dimension_semantics=("arbitrary",)),
        name="adaln_mod",
    )(c_all, w_mod, b_mod.reshape(1, -1))


def _mod_rows(mod_ref, seq0, col, r0, rc, tt):
    if tt >= rc:
        return mod_ref[pl.ds(seq0 + r0 // tt, 1), col:col + D_MODEL]
    parts = [jnp.broadcast_to(mod_ref[pl.ds(seq0 + b, 1), col:col + D_MODEL], (tt, D_MODEL))
             for b in range(r0 // tt, (r0 + rc) // tt)]
    return jnp.concatenate(parts, axis=0)


def _pre_norm(x_ref, mod_ref, seq0, g_ref, h_ref, rows, tt):
    rc = min(rows, ROW_CHUNK)
    for r0 in range(0, rows, rc):
        x = x_ref[r0:r0 + rc, :]
        ms = jnp.mean(x * x, axis=-1, keepdims=True)
        gain = g_ref[...] * (1.0 + _mod_rows(mod_ref, seq0, D_MODEL, r0, rc, tt))
        h = x * lax.rsqrt(ms + RMS_EPS) * gain + _mod_rows(mod_ref, seq0, 0, r0, rc, tt)
        h_ref[r0:r0 + rc, :] = h.astype(BF16)


def _out_proj_task(mix_ref, wout_ref, o_ref, lo, n):
    def task():
        o_ref[lo:lo + n, :] = jnp.dot(mix_ref[lo:lo + n, :], wout_ref[...], preferred_element_type=F32)
    return task


def _residual_tasks(x_ref, mod_ref, seq0, g_ref, o_ref, y_ref, lo, n, tt):
    rc = min(n, ROW_CHUNK)

    def chunk(r0):
        o = o_ref[r0:r0 + rc, :]
        ms = jnp.mean(o * o, axis=-1, keepdims=True)
        gain = g_ref[...] * _mod_rows(mod_ref, seq0, 2 * D_MODEL, r0, rc, tt)
        y_ref[r0:r0 + rc, :] = x_ref[r0:r0 + rc, :] + o * lax.rsqrt(ms + RMS_EPS) * gain
    return [functools.partial(chunk, r0) for r0 in range(lo, lo + n, rc)]


def _seq_chunks(lo, n, tt):
    rc = min(tt, n, WIN_CHUNK)
    return [(r0 // tt, r0 % tt, r0, rc) for r0 in range(lo, lo + n, rc)]


def _even_gating(z_ref, ca_ref, cb_ref, cbb_ref, lng_ref, lnb_ref, mix_ref, rows):
    rc = min(rows, ROW_CHUNK)
    for r0 in range(0, rows, rc):
        def zs(c):
            return z_ref[r0:r0 + rc, MIX_W * c:MIX_W * (c + 1)]

        def all_lanes(c_ref):
            return jnp.concatenate([c_ref[g, r0:r0 + rc, :] for g in range(N_LG)], axis=1)
        ya = zs(1) * all_lanes(ca_ref) * _silu(zs(3))
        mix_ref[r0:r0 + rc, 0:MIX_W] = ya.astype(BF16)
        yb = all_lanes(cb_ref) + cbb_ref[...]
        mu = jnp.mean(yb, axis=-1, keepdims=True)
        yc = yb - mu
        var = jnp.mean(yc * yc, axis=-1, keepdims=True)
        ln = yc * lax.rsqrt(var + LN_EPS) * lng_ref[...] + lnb_ref[...]
        mix_ref[r0:r0 + rc, MIX_W:2 * MIX_W] = (_silu(ln) * _silu(zs(6))).astype(BF16)


def _even_kernel(x_ref, mod_ref, sa_ref, sb_ref, gpre_ref, win_ref, cwa_ref, cwb_ref, cbb_ref,
                 lng_ref, lnb_ref, wout_ref, gpost_ref,
                 y_ref, na_ref, nb_ref,
                 h_ref, z_ref, o_ref, fa_ref, fb_ref, ca_ref, cb_ref, mix_ref,
                 *, nb, tt, nt, seq_base, time_major_state):
    rows = nb * tt
    j = pl.program_id(1)
    seq0 = seq_base + pl.program_id(0) * nb
    assert not time_major_state or nt == 1

    @pl.when(j == 0)
    def _load_state():
        for b in range(0 if time_major_state else nb):
            for g in range(N_LG):
                lg = slice(LANES * g, LANES * (g + 1))
                fa_ref[b * N_LG + g, 0:HIST_A, :] = jnp.zeros((HIST_A, LANES), F32)
                fa_ref[b * N_LG + g, HIST_A - (CONV_A - 1):HIST_A, :] = sa_ref[b, :, lg]
                fb_ref[b * N_LG + g, 0:8, :] = jnp.zeros((8, LANES), F32)
                fb_ref[b * N_LG + g, HIST_B - (CONV_B - 1):HIST_B, :] = sb_ref[b, :, lg]

    _pre_norm(x_ref, mod_ref, seq0, gpre_ref, h_ref, rows, tt)

    for c0 in range(0, EVEN_IN, MXU_COLS):
        z_ref[:, c0:c0 + MXU_COLS] = jnp.dot(h_ref[...], win_ref[:, c0:c0 + MXU_COLS], preferred_element_type=F32)

    if time_major_state:
        rc = min(rows, WIN_CHUNK)
        for r0 in range(0, rows, rc):
            for g in range(N_LG):
                def zc(c):
                    return z_ref[r0:r0 + rc, MIX_W * c + LANES * g:MIX_W * c + LANES * (g + 1)]
                fa_ref[g, r0:r0 + rc, :] = zc(2) * zc(0)
                fb_ref[g, r0:r0 + rc, :] = zc(4) * _sigmoid(zc(5))

        def conv(w_ref, state_ref, u_ref, out_ref, new_state_ref, taps):
            hist = taps - 1

            def slab(j_in, g):
                if j_in < hist:
                    return state_ref[j_in, :, LANES * g:LANES * (g + 1)]
                return u_ref[g, pl.ds(j_in - hist, nb, stride=tt), :]

            for t in range(tt):
                for g in range(N_LG):
                    acc = None
                    for k in range(taps):
                        term = w_ref[k:k + 1, LANES * g:LANES * (g + 1)] * slab(t + k, g)
                        acc = term if acc is None else acc + term
                    out_ref[g, pl.ds(t, nb, stride=tt), :] = acc
            for j_out in range(hist):
                for g in range(N_LG):
                    new_state_ref[j_out, :, LANES * g:LANES * (g + 1)] = slab(j_out + tt, g)

        conv(cwa_ref, sa_ref, fa_ref, ca_ref, na_ref, CONV_A)
        conv(cwb_ref, sb_ref, fb_ref, cb_ref, nb_ref, CONV_B)

    for b, c0, r0, rc in ([] if time_major_state else _seq_chunks(0, rows, tt)):
        for g in range(N_LG):
            def zc(c):
                return z_ref[r0:r0 + rc, MIX_W * c + LANES * g:MIX_W * c + LANES * (g + 1)]
            fa_ref[b * N_LG + g, HIST_A + c0:HIST_A + c0 + rc, :] = zc(2) * zc(0)
            fb_ref[b * N_LG + g, HIST_B + c0:HIST_B + c0 + rc, :] = zc(4) * _sigmoid(zc(5))

    for b, c0, r0, rc in ([] if time_major_state else _seq_chunks(0, rows, tt)):
        for g in range(N_LG):
            lg = slice(LANES * g, LANES * (g + 1))
            acc = None
            for k in range(CONV_A):
                off = HIST_A - (CONV_A - 1) + k + c0
                t = cwa_ref[k:k + 1, lg] * fa_ref[b * N_LG + g, off:off + rc, :]
                acc = t if acc is None else acc + t
            ca_ref[g, r0:r0 + rc, :] = acc
            acc = None
            for k in range(CONV_B):
                off = HIST_B - (CONV_B - 1) + k + c0
                t = cwb_ref[k:k + 1, lg] * fb_ref[b * N_LG + g, off:off + rc, :]
                acc = t if acc is None else acc + t
            cb_ref[g, r0:r0 + rc, :] = acc

    for b in range(0 if time_major_state else nb):
        for g in range(N_LG):
            lg = slice(LANES * g, LANES * (g + 1))
            na_ref[b, :, lg] = fa_ref[b * N_LG + g, HIST_A + tt - (CONV_A - 1):HIST_A + tt, :]
            nb_ref[b, :, lg] = fb_ref[b * N_LG + g, HIST_B + tt - (CONV_B - 1):HIST_B + tt, :]
            if nt > 1:
                fa_ref[b * N_LG + g, 0:HIST_A, :] = fa_ref[b * N_LG + g, tt:tt + HIST_A, :]
                fb_ref[b * N_LG + g, 0:HIST_B, :] = fb_ref[b * N_LG + g, tt:tt + HIST_B, :]

    _even_gating(z_ref, ca_ref, cb_ref, cbb_ref, lng_ref, lnb_ref, mix_ref, rows)
    _out_proj_task(mix_ref, wout_ref, o_ref, 0, rows)()
    for task in _residual_tasks(x_ref, mod_ref, seq0, gpost_ref, o_ref, y_ref, 0, rows, tt):
        task()


def _even_layer(x2d, mod, st_a, st_b, g_pre, w_in, cw_a, cw_b, cb_b, ln_g, ln_b, w_out, g_post,
                *, n, t, nb, tt, seq_base, time_major_state):
    rows = nb * tt
    nt = t // tt
    kern = functools.partial(_even_kernel, nb=nb, tt=tt, nt=nt, seq_base=seq_base,
                             time_major_state=time_major_state)
    row_spec = pl.BlockSpec((rows, D_MODEL), lambda i, j: (i * nt + j, 0))

    def seq_spec(shape):
        if time_major_state:
            return pl.BlockSpec((shape[0], nb, shape[1]), lambda i, j: (0, i, 0))
        return pl.BlockSpec((nb,) + shape, lambda i, j: (i, 0, 0))

    def window_scratch(hist):
        if time_major_state:
            return pltpu.VMEM((N_LG, rows, LANES), F32)
        return pltpu.VMEM((nb * N_LG, hist + tt, LANES), F32)

    def const_spec(shape):
        return pl.BlockSpec(shape, lambda i, j: (0,) * len(shape))

    return pl.pallas_call(
        kern,
        out_shape=(
            jax.ShapeDtypeStruct((n * t, D_MODEL), F32),
            jax.ShapeDtypeStruct(st_a.shape, F32),
            jax.ShapeDtypeStruct(st_b.shape, F32),
        ),
        grid=(n // nb, nt),
        in_specs=[
            row_spec,
            const_spec(mod.shape),
            seq_spec((CONV_A - 1, MIX_W)),
            seq_spec((CONV_B - 1, MIX_W)),
            const_spec((1, D_MODEL)),
            const_spec((D_MODEL, EVEN_IN)),
            const_spec((CONV_A, MIX_W)),
            const_spec((CONV_B, MIX_W)),
            const_spec((1, MIX_W)),
            const_spec((1, MIX_W)),
            const_spec((1, MIX_W)),
            const_spec((2 * MIX_W, D_MODEL)),
            const_spec((1, D_MODEL)),
        ],
        out_specs=(row_spec, seq_spec((CONV_A - 1, MIX_W)), seq_spec((CONV_B - 1, MIX_W))),
        scratch_shapes=[
            pltpu.VMEM((rows, D_MODEL), BF16),
            pltpu.VMEM((rows, EVEN_IN + ROW_PAD), F32),
            pltpu.VMEM((rows, D_MODEL), F32),
            window_scratch(HIST_A),
            window_scratch(HIST_B),
            pltpu.VMEM((N_LG, rows, LANES), F32),
            pltpu.VMEM((N_LG, rows, LANES), F32),
            pltpu.VMEM((rows, 2 * MIX_W), BF16),
        ],
        compiler_params=pltpu.CompilerParams(
            dimension_semantics=("arbitrary", "arbitrary"), vmem_limit_bytes=VMEM_LIMIT_BYTES),
        name="even_layer",
    )(x2d, mod, st_a, st_b, g_pre, w_in, cw_a, cw_b, cb_b, ln_g, ln_b, w_out, g_post)


def _odd_kernel(sinks_ref, x_ref, mod_ref, sc_ref, sk_ref, sv_ref, gpre_ref, win_ref, pw_ref, ps_ref,
                wout_ref, gpost_ref,
                y_ref, nc_ref, nk_ref, nv_ref,
                h_ref, z_ref, o_ref, fc_ref, kf_ref, ks_ref, vf_ref, vs_ref, pc_ref, od_ref, mix_ref,
                *, nb, tt, nt, pos0, seq_base, time_major_pool):
    rows = nb * tt
    j = pl.program_id(1)
    seq0 = seq_base + pl.program_id(0) * nb
    tile_pos = pos0 + j * tt
    assert not time_major_pool or nt == 1

    @pl.when(j == 0)
    def _load_state():
        for b in range(nb):
            for g in range(0 if time_major_pool else N_LG):
                fc_ref[b * N_LG + g, 0:8, :] = jnp.zeros((8, LANES), F32)
                fc_ref[b * N_LG + g, HIST_C - POOL_PAST:HIST_C, :] = sc_ref[b, :, LANES * g:LANES * (g + 1)]
            k_past, v_past = sk_ref[b].T, sv_ref[b].T
            kf_ref[b, 0:WINDOW, :] = k_past
            vf_ref[b, 0:WINDOW, :] = v_past
            ks_ref[b, 0:WINDOW, :] = pltpu.roll(k_past, HEAD_DIM, axis=1)
            vs_ref[b, 0:WINDOW, :] = pltpu.roll(v_past, HEAD_DIM, axis=1)

    qs = min(tt, WINDOW)
    gs = max(1, ATT_MIN_Q_ROWS // qs)
    kw = WINDOW + qs
    n_q, n_keys = gs * qs, gs * kw
    assert N_REP == 4 and 2 * HEAD_DIM == LANES and KV_W == LANES
    assert qs & (qs - 1) == 0 and gs & (gs - 1) == 0 and nb % gs == 0 and (gs == 1 or tt == qs)
    shape = (2 * n_q, n_keys)
    ri = lax.broadcasted_iota(jnp.int32, shape, 0)
    ci = lax.broadcasted_iota(jnp.int32, shape, 1)
    q_seq = (ri >> (qs.bit_length() - 1)) & (gs - 1)
    k_seq = jnp.zeros(shape, jnp.int32)
    for s_i in range(1, gs):
        k_seq = k_seq + (ci >= s_i * kw).astype(jnp.int32)
    k_c = ci - k_seq * kw
    dist = (ri & (qs - 1)) + WINDOW - k_c
    band = (q_seq == k_seq) & (dist >= 0) & (dist < WINDOW)
    dist_f = dist.astype(F32)
    upper_grp = ri >= n_q
    upper_grp_col = lax.broadcasted_iota(jnp.int32, (2 * n_q, 1), 0) >= n_q
    low_lanes = lax.broadcasted_iota(jnp.int32, (n_keys, LANES), 1) < HEAD_DIM
    ones_cols = jnp.ones((n_keys, LANES), BF16)

    def head_of(g, grp, parity):
        return g * N_REP + 2 * grp + parity

    def alibi_bias(g, parity):
        slope = jnp.where(upper_grp, ALIBI_SLOPES[head_of(g, 1, parity)], ALIBI_SLOPES[head_of(g, 0, parity)])
        return jnp.where(band, -(slope * dist_f), -jnp.inf)

    biases = {(g, parity): alibi_bias(g, parity) for g in range(N_KV) for parity in range(2)}

    def attention(b0, sb, g):
        r0 = b0 * tt + sb * qs
        p0 = tile_pos + sb * qs
        in_range = (k_c >= WINDOW - p0) if pos0 + sb * qs < WINDOW else None
        if gs == 1:
            win = (b0, slice(sb * qs, sb * qs + kw))
            k_nat, k_swp, v_nat, v_swp = kf_ref[win], ks_ref[win], vf_ref[win], vs_ref[win]
        else:
            k_nat, k_swp, v_nat, v_swp = (r[b0:b0 + gs].reshape(n_keys, KV_W)
                                          for r in (kf_ref, ks_ref, vf_ref, vs_ref))
        k_lo, k_hi = (k_nat, k_swp) if g == 0 else (k_swp, k_nat)
        v_lo, v_hi = (v_nat, v_swp) if g == 0 else (v_swp, v_nat)
        v_ext = jnp.concatenate([jnp.where(low_lanes, v_lo, v_hi).astype(BF16), ones_cols], axis=1)
        qg = jnp.concatenate([z_ref[r0:r0 + n_q, Q_COL + 2 * LANES * g + LANES * grp:
                                    Q_COL + 2 * LANES * g + LANES * (grp + 1)] for grp in range(2)], axis=0)
        qg = (qg * (HEAD_DIM ** -0.5)).astype(BF16)
        for parity in range(2):
            k_ext = (jnp.where(low_lanes, k_lo, 0.0) if parity == 0 else jnp.where(low_lanes, 0.0, k_hi)).astype(BF16)
            sink = jnp.where(upper_grp_col, sinks_ref[head_of(g, 1, parity)], sinks_ref[head_of(g, 0, parity)])
            sc = lax.dot_general(qg, k_ext, (((1,), (1,)), ((), ())), preferred_element_type=F32)
            bias = biases[g, parity]
            sc = sc + (bias if in_range is None else jnp.where(in_range, bias, -jnp.inf))
            m = jnp.maximum(jnp.max(sc, axis=-1, keepdims=True), sink)
            p = jnp.exp(sc - m)
            o_ext = jnp.dot(p.astype(BF16), v_ext, preferred_element_type=F32)
            o = o_ext[:, 0:LANES] / (o_ext[:, LANES:2 * LANES] + jnp.exp(sink - m))
            for grp in range(2):
                h = head_of(g, grp, parity)
                od_ref[r0:r0 + n_q, HEAD_DIM * h:HEAD_DIM * (h + 1)] = (
                    o[grp * n_q:(grp + 1) * n_q, HEAD_DIM * parity:HEAD_DIM * (parity + 1)])

    def window_inputs(b, c0, r0, rc):
        for g in range(0 if time_major_pool else N_LG):
            fc_ref[b * N_LG + g, HIST_C + c0:HIST_C + c0 + rc, :] = z_ref[r0:r0 + rc, LANES * g:LANES * (g + 1)]
        k = z_ref[r0:r0 + rc, K_COL:K_COL + KV_W]
        v = z_ref[r0:r0 + rc, V_COL:V_COL + KV_W]
        kf_ref[b, WINDOW + c0:WINDOW + c0 + rc, :] = k
        vf_ref[b, WINDOW + c0:WINDOW + c0 + rc, :] = v
        ks_ref[b, WINDOW + c0:WINDOW + c0 + rc, :] = pltpu.roll(k, HEAD_DIM, axis=1)
        vs_ref[b, WINDOW + c0:WINDOW + c0 + rc, :] = pltpu.roll(v, HEAD_DIM, axis=1)

    def pooling(b, c0, r0, rc):
        pos = tile_pos + c0 + lax.broadcasted_iota(jnp.int32, (rc, LANES), 0)
        for g, w in enumerate(POOL_WINDOWS):
            u = fc_ref[b * N_LG + g, HIST_C + c0:HIST_C + c0 + rc, :]
            acc = u
            for d in range(1, w):
                acc = acc + fc_ref[b * N_LG + g, HIST_C + c0 - d:HIST_C + c0 - d + rc, :]
            cnt = jnp.minimum(w, pos + 1).astype(F32)
            pc_ref[g, r0:r0 + rc, :] = acc / cnt - u

    def pooling_time_major(lo, n):
        b_lo, n_seq = lo // tt, n // tt

        for g in range(N_LG):
            fc_ref[g, lo:lo + n, :] = z_ref[lo:lo + n, LANES * g:LANES * (g + 1)]

        def slab(j_in, g):
            if j_in < POOL_PAST:
                return sc_ref[j_in, b_lo:b_lo + n_seq, LANES * g:LANES * (g + 1)]
            return fc_ref[g, pl.ds(lo + j_in - POOL_PAST, n_seq, stride=tt), :]

        for t in range(tt):
            for g, w in enumerate(POOL_WINDOWS):
                u = slab(POOL_PAST + t, g)
                acc = u
                for d in range(1, w):
                    acc = acc + slab(POOL_PAST + t - d, g)
                pc_ref[g, pl.ds(lo + t, n_seq, stride=tt), :] = acc / float(min(w, pos0 + t + 1)) - u
        for j_out in range(POOL_PAST):
            for g in range(N_LG):
                nc_ref[j_out, b_lo:b_lo + n_seq, LANES * g:LANES * (g + 1)] = slab(j_out + tt, g)

    def gating(r0, rc):
        for g in range(N_LG):
            lg = slice(LANES * g, LANES * (g + 1))
            mixed = jnp.dot(pc_ref[g, r0:r0 + rc, :].astype(BF16), pw_ref[g], preferred_element_type=F32)
            yc = mixed * ps_ref[:, lg] * _silu(z_ref[r0:r0 + rc, MIX_W + LANES * g:MIX_W + LANES * (g + 1)])
            mix_ref[r0:r0 + rc, lg] = yc.astype(BF16)
        yd = od_ref[r0:r0 + rc, 0:MIX_W] * _silu(z_ref[r0:r0 + rc, DG_COL:DG_COL + MIX_W])
        mix_ref[r0:r0 + rc, MIX_W:2 * MIX_W] = yd.astype(BF16)

    def mixer_tasks(lo, n):
        chunks = _seq_chunks(lo, n, tt)
        if gs == 1:
            blocks = [(r0 // tt, (r0 % tt) // qs) for r0 in range(lo, lo + n, qs)]
        else:
            blocks = [(b0, 0) for b0 in range(lo // tt, (lo + n) // tt, gs)]
        pool_tasks = ([functools.partial(pooling_time_major, lo, n)] if time_major_pool
                      else [functools.partial(pooling, *ch) for ch in chunks])
        return ([functools.partial(window_inputs, *ch) for ch in chunks]
                + pool_tasks
                + [functools.partial(attention, b0, sb, g) for b0, sb in blocks for g in range(N_KV)]
                + [functools.partial(gating, lo, n)])

    _pre_norm(x_ref, mod_ref, seq0, gpre_ref, h_ref, rows, tt)
    for c0 in range(0, ODD_IN, MXU_COLS):
        z_ref[:, c0:c0 + MXU_COLS] = jnp.dot(h_ref[...], win_ref[:, c0:c0 + MXU_COLS], preferred_element_type=F32)

    n_chunks = max(1, rows // WIN_CHUNK)
    n = rows // n_chunks
    for c in range(n_chunks + 1):
        if c > 0:
            _out_proj_task(mix_ref, wout_ref, o_ref, (c - 1) * n, n)()
        if c < n_chunks:
            for task in mixer_tasks(c * n, n):
                task()
        if c > 0:
            for task in _residual_tasks(x_ref, mod_ref, seq0, gpost_ref, o_ref, y_ref, (c - 1) * n, n, tt):
                task()

    for b in range(nb):
        for g in range(0 if time_major_pool else N_LG):
            nc_ref[b, :, LANES * g:LANES * (g + 1)] = fc_ref[b * N_LG + g, HIST_C + tt - POOL_PAST:HIST_C + tt, :]
            if nt > 1:
                fc_ref[b * N_LG + g, 0:HIST_C, :] = fc_ref[b * N_LG + g, tt:tt + HIST_C, :]
        nk_ref[b] = kf_ref[b, tt:tt + WINDOW, :]
        nv_ref[b] = vf_ref[b, tt:tt + WINDOW, :]
        if nt > 1:
            for r in (kf_ref, ks_ref, vf_ref, vs_ref):
                r[b, 0:WINDOW, :] = r[b, tt:tt + WINDOW, :]


def _odd_layer(x2d, mod, st_c, st_k, st_v, g_pre, w_in, pool_w, pool_scale, sinks, w_out, g_post,
               *, n, t, nb, tt, pos0, seq_base, time_major_pool):
    rows = nb * tt
    nt = t // tt
    kern = functools.partial(_odd_kernel, nb=nb, tt=tt, nt=nt, pos0=pos0, seq_base=seq_base,
                             time_major_pool=time_major_pool)
    row_spec = pl.BlockSpec((rows, D_MODEL), lambda i, j: (i * nt + j, 0))

    def seq_spec(shape):
        return pl.BlockSpec((nb,) + shape, lambda i, j: (i, 0, 0))

    def const_spec(shape):
        return pl.BlockSpec(shape, lambda i, j: (0,) * len(shape))

    if time_major_pool:
        pool_spec = pl.BlockSpec((POOL_PAST, nb, MIX_W), lambda i, j: (0, i, 0))
    else:
        pool_spec = seq_spec((POOL_PAST, MIX_W))

    return pl.pallas_call(
        kern,
        out_shape=(
            jax.ShapeDtypeStruct((n * t, D_MODEL), F32),
            jax.ShapeDtypeStruct(st_c.shape, F32),
            jax.ShapeDtypeStruct((n, WINDOW, KV_W), F32),
            jax.ShapeDtypeStruct((n, WINDOW, KV_W), F32),
        ),
        grid=(n // nb, nt),
        in_specs=[
            pl.BlockSpec(memory_space=pltpu.SMEM),
            row_spec,
            const_spec(mod.shape),
            pool_spec,
            seq_spec((KV_W, WINDOW)),
            seq_spec((KV_W, WINDOW)),
            const_spec((1, D_MODEL)),
            const_spec((D_MODEL, ODD_IN)),
            const_spec((N_LG, LANES, LANES)),
            const_spec((1, MIX_W)),
            const_spec((2 * MIX_W, D_MODEL)),
            const_spec((1, D_MODEL)),
        ],
        out_specs=(row_spec, pool_spec, seq_spec((WINDOW, KV_W)), seq_spec((WINDOW, KV_W))),
        scratch_shapes=[
            pltpu.VMEM((rows, D_MODEL), BF16),
            pltpu.VMEM((rows, ODD_IN + ROW_PAD), F32),
            pltpu.VMEM((rows, D_MODEL), F32),
            pltpu.VMEM((N_LG, rows, LANES) if time_major_pool else (nb * N_LG, HIST_C + tt, LANES), F32),
            pltpu.VMEM((nb, WINDOW + tt, KV_W), F32),
            pltpu.VMEM((nb, WINDOW + tt, KV_W), F32),
            pltpu.VMEM((nb, WINDOW + tt, KV_W), F32),
            pltpu.VMEM((nb, WINDOW + tt, KV_W), F32),
            pltpu.VMEM((N_LG, rows, LANES), F32),
            pltpu.VMEM((rows, MIX_W + ROW_PAD), F32),
            pltpu.VMEM((rows, 2 * MIX_W), BF16),
        ],
        compiler_params=pltpu.CompilerParams(
            dimension_semantics=("arbitrary", "arbitrary"), vmem_limit_bytes=VMEM_LIMIT_BYTES),
        name="odd_layer",
    )(sinks, x2d, mod, st_c, st_k, st_v, g_pre, w_in, pool_w, pool_scale, w_out, g_post)


def _tiling(n, t):
    if t >= TILE_ROWS:
        return 1, TILE_ROWS
    return min(n, SHORT_TILE_ROWS // t), t


def _trunk(x, mod_e, mod_o, seq_base, st_a, st_b, st_c, st_k, st_v, pos0, we, wo):
    n, t, _ = x.shape
    nb, tt = _tiling(n, t)
    short = t == tt and nb > 1

    def swap(st):
        return jnp.transpose(st, (1, 0, 2)) if short else st

    x2d, na, nbs = _even_layer(x.reshape(n * t, D_MODEL), mod_e, swap(st_a), swap(st_b), *we, n=n, t=t, nb=nb,
                               tt=tt, seq_base=seq_base, time_major_state=short)
    def channel_major(cache):
        return jnp.transpose(cache, (0, 2, 3, 1)).reshape(n, KV_W, WINDOW)

    x2d, nc, nk, nv = _odd_layer(x2d, mod_o, swap(st_c), channel_major(st_k), channel_major(st_v), *wo,
                                 n=n, t=t, nb=nb, tt=tt, pos0=pos0, seq_base=seq_base, time_major_pool=short)
    na, nbs, nc = swap(na), swap(nbs), swap(nc)
    kv_shape = (1, n, WINDOW, N_KV, HEAD_DIM)
    return (x2d.reshape(n, t, D_MODEL), na[None], nbs[None], nc[None],
            nk.reshape(kv_shape), nv.reshape(kv_shape))


def kernel(x_prompt, x_sample, state_conv_a, state_conv_b, state_pool_c, cache_win_k, cache_win_v, c_prompt, c_sample, w_mod_e, b_mod_e, g_pre_e, g_post_e, w_in_e, conv_a_w, conv_b_w, conv_b_b, ln_b_g, ln_b_b, w_out_e, w_mod_o, b_mod_o, g_pre_o, g_post_o, w_in_o, pool_w, pool_scale, sinks, w_out_o):
    n_p = x_prompt.shape[0]
    assert w_in_e.shape[0] == 1 and w_in_o.shape[0] == 1, "one even and one odd layer"

    c_all = jnp.concatenate([c_prompt, c_sample], axis=0)
    mod_e = _adaln_mod(c_all, w_mod_e[0], b_mod_e[0])
    mod_o = _adaln_mod(c_all, w_mod_o[0], b_mod_o[0])

    we = (g_pre_e, w_in_e[0].astype(BF16), conv_a_w[0], conv_b_w[0], conv_b_b, ln_b_g, ln_b_b,
          w_out_e[0].astype(BF16), g_post_e)
    wo = (g_pre_o, w_in_o[0].astype(BF16), pool_w[0].astype(BF16), pool_scale, sinks[0],
          w_out_o[0].astype(BF16), g_post_o)

    dt = x_prompt.dtype
    z_a = jnp.zeros((n_p, CONV_A - 1, MIX_W), dt)
    z_b = jnp.zeros((n_p, CONV_B - 1, MIX_W), dt)
    z_c = jnp.zeros((n_p, POOL_PAST, MIX_W), dt)
    z_kv = jnp.zeros((n_p, WINDOW, N_KV, HEAD_DIM), dt)
    y_p, pa, pb, pc, pk, pv = _trunk(x_prompt, mod_e, mod_o, 0, z_a, z_b, z_c, z_kv, z_kv, 0, we, wo)
    y_s, sa, sb, sc, sk, sv = _trunk(x_sample, mod_e, mod_o, n_p, state_conv_a[0], state_conv_b[0],
                                     state_pool_c[0], cache_win_k[0], cache_win_v[0], PAST_LEN, we, wo)
    return (y_p, y_s, pa, sa, pb, sb, pc, sc, pk, sk, pv, sv)
```

```python
import functools

import jax
import jax.numpy as jnp
from jax import lax
from jax.experimental import pallas as pl
from jax.experimental.pallas import tpu as pltpu

F32 = jnp.float32
BF16 = jnp.bfloat16

D_MODEL = 1024
MIX_W = 512
LANES = 128
MXU_COLS = 256
N_LG = MIX_W // LANES
HEAD_DIM = 64
N_HEADS = 8
N_KV = 2
N_REP = N_HEADS // N_KV
KV_W = N_KV * HEAD_DIM
CONV_A = 3
CONV_B = 31
POOL_WINDOWS = (2, 4, 8, 16)
POOL_PAST = 15
WINDOW = 128
PAST_LEN = 8192
EVEN_IN = 7 * MIX_W
ODD_IN = 4 * MIX_W + 2 * KV_W
Q_COL, K_COL, V_COL, DG_COL = 2 * MIX_W, 3 * MIX_W, 3 * MIX_W + KV_W, 3 * MIX_W + 2 * KV_W
RMS_EPS = 1e-6
LN_EPS = 1e-5
HIST_A = 8
HIST_B = 32
HIST_C = 16
ATT_MIN_Q_ROWS = 32
TILE_ROWS = 1024
SHORT_TILE_ROWS = 256
ROW_CHUNK = 64
WIN_CHUNK = 128
MOD_K_BLOCK = 256
ROW_PAD = LANES
VMEM_LIMIT_BYTES = 56 * 1024 * 1024
NEG_LOG2_E = -1.4426950408889634
ALIBI_SLOPES = tuple(float(2.0 ** (-8.0 * (h + 1) / N_HEADS)) for h in range(N_HEADS))


def _sigmoid(x):
    return 1.0 / (1.0 + jnp.exp2(x * NEG_LOG2_E))


def _silu(x):
    return x * _sigmoid(x)


def _mod_kernel(c_ref, w_ref, b_ref, o_ref):
    k = pl.program_id(0)
    part = jnp.dot(_silu(c_ref[...]).astype(BF16), w_ref[...].astype(BF16), preferred_element_type=F32)

    @pl.when(k == 0)
    def _first():
        o_ref[...] = part + b_ref[...]

    @pl.when(k > 0)
    def _rest():
        o_ref[...] += part


def _adaln_mod(c_all, w_mod, b_mod):
    n = c_all.shape[0]
    return pl.pallas_call(
        _mod_kernel,
        out_shape=jax.ShapeDtypeStruct((n, 3 * D_MODEL), F32),
        grid=(D_MODEL // MOD_K_BLOCK,),
        in_specs=[
            pl.BlockSpec((n, MOD_K_BLOCK), lambda k: (0, k)),
            pl.BlockSpec((MOD_K_BLOCK, 3 * D_MODEL), lambda k: (k, 0)),
            pl.BlockSpec((1, 3 * D_MODEL), lambda k: (0, 0)),
        ],
        out_specs=pl.BlockSpec((n, 3 * D_MODEL), lambda k: (0, 0)),
        compiler_params=pltpu.CompilerParams(dimension_semantics=("arbitrary",)),
        name="adaln_mod",
    )(c_all, w_mod, b_mod.reshape(1, -1))


def _mod_rows(mod_ref, seq0, col, r0, rc, tt):
    if tt >= rc:
        return mod_ref[pl.ds(seq0 + r0 // tt, 1), col:col + D_MODEL]
    parts = [jnp.broadcast_to(mod_ref[pl.ds(seq0 + b, 1), col:col + D_MODEL], (tt, D_MODEL))
             for b in range(r0 // tt, (r0 + rc) // tt)]
    return jnp.concatenate(parts, axis=0)


def _pre_norm(x_ref, mod_ref, seq0, g_ref, h_ref, rows, tt):
    rc = min(rows, ROW_CHUNK)
    for r0 in range(0, rows, rc):
        x = x_ref[r0:r0 + rc, :]
        ms = jnp.mean(x * x, axis=-1, keepdims=True)
        gain = g_ref[...] * (1.0 + _mod_rows(mod_ref, seq0, D_MODEL, r0, rc, tt))
        h = x * lax.rsqrt(ms + RMS_EPS) * gain + _mod_rows(mod_ref, seq0, 0, r0, rc, tt)
        h_ref[r0:r0 + rc, :] = h.astype(BF16)


def _out_proj_task(mix_ref, wout_ref, o_ref, lo, n):
    def task():
        o_ref[lo:lo + n, :] = jnp.dot(mix_ref[lo:lo + n, :], wout_ref[...], preferred_element_type=F32)
    return task


def _residual_tasks(x_ref, mod_ref, seq0, g_ref, o_ref, y_ref, lo, n, tt):
    rc = min(n, ROW_CHUNK)

    def chunk(r0):
        o = o_ref[r0:r0 + rc, :]
        ms = jnp.mean(o * o, axis=-1, keepdims=True)
        gain = g_ref[...] * _mod_rows(mod_ref, seq0, 2 * D_MODEL, r0, rc, tt)
        y_ref[r0:r0 + rc, :] = x_ref[r0:r0 + rc, :] + o * lax.rsqrt(ms + RMS_EPS) * gain
    return [functools.partial(chunk, r0) for r0 in range(lo, lo + n, rc)]


def _seq_chunks(lo, n, tt):
    rc = min(tt, n, WIN_CHUNK)
    return [(r0 // tt, r0 % tt, r0, rc) for r0 in range(lo, lo + n, rc)]


def _even_gating(z_ref, ca_ref, cb_ref, cbb_ref, lng_ref, lnb_ref, mix_ref, rows):
    rc = min(rows, ROW_CHUNK)
    for r0 in range(0, rows, rc):
        def zs(c):
            return z_ref[r0:r0 + rc, MIX_W * c:MIX_W * (c + 1)]

        def all_lanes(c_ref):
            return jnp.concatenate([c_ref[g, r0:r0 + rc, :] for g in range(N_LG)], axis=1)
        ya = zs(1) * all_lanes(ca_ref) * _silu(zs(3))
        mix_ref[r0:r0 + rc, 0:MIX_W] = ya.astype(BF16)
        yb = all_lanes(cb_ref) + cbb_ref[...]
        mu = jnp.mean(yb, axis=-1, keepdims=True)
        yc = yb - mu
        var = jnp.mean(yc * yc, axis=-1, keepdims=True)
        ln = yc * lax.rsqrt(var + LN_EPS) * lng_ref[...] + lnb_ref[...]
        mix_ref[r0:r0 + rc, MIX_W:2 * MIX_W] = (_silu(ln) * _silu(zs(6))).astype(BF16)


def _even_kernel(x_ref, mod_ref, sa_ref, sb_ref, gpre_ref, win_ref, cwa_ref, cwb_ref, cbb_ref,
                 lng_ref, lnb_ref, wout_ref, gpost_ref,
                 y_ref, na_ref, nb_ref,
                 h_ref, z_ref, o_ref, fa_ref, fb_ref, ca_ref, cb_ref, mix_ref,
                 *, nb, tt, nt, seq_base, time_major_state):
    rows = nb * tt
    j = pl.program_id(1)
    seq0 = seq_base + pl.program_id(0) * nb
    assert not time_major_state or nt == 1

    @pl.when(j == 0)
    def _load_state():
        for b in range(0 if time_major_state else nb):
            for g in range(N_LG):
                lg = slice(LANES * g, LANES * (g + 1))
                fa_ref[b * N_LG + g, 0:HIST_A, :] = jnp.zeros((HIST_A, LANES), F32)
                fa_ref[b * N_LG + g, HIST_A - (CONV_A - 1):HIST_A, :] = sa_ref[b, :, lg]
                fb_ref[b * N_LG + g, 0:8, :] = jnp.zeros((8, LANES), F32)
                fb_ref[b * N_LG + g, HIST_B - (CONV_B - 1):HIST_B, :] = sb_ref[b, :, lg]

    _pre_norm(x_ref, mod_ref, seq0, gpre_ref, h_ref, rows, tt)

    split_order = (4, 5, 0, 2, 1, 3, 6)
    for c0 in [MIX_W * s + d for s in split_order for d in range(0, MIX_W, MXU_COLS)]:
        z_ref[:, c0:c0 + MXU_COLS] = jnp.dot(h_ref[...], win_ref[:, c0:c0 + MXU_COLS], preferred_element_type=F32)

    if time_major_state:
        rc = min(rows, WIN_CHUNK)
        for r0 in range(0, rows, rc):
            for g in range(N_LG):
                def zc(c):
                    return z_ref[r0:r0 + rc, MIX_W * c + LANES * g:MIX_W * c + LANES * (g + 1)]
                fa_ref[g, r0:r0 + rc, :] = zc(2) * zc(0)
                fb_ref[g, r0:r0 + rc, :] = zc(4) * _sigmoid(zc(5))

        def conv(w_ref, state_ref, u_ref, out_ref, new_state_ref, taps):
            hist = taps - 1

            def slab(j_in, g):
                if j_in < hist:
                    return state_ref[j_in, :, LANES * g:LANES * (g + 1)]
                return u_ref[g, pl.ds(j_in - hist, nb, stride=tt), :]

            for t in range(tt):
                for g in range(N_LG):
                    acc = None
                    for k in range(taps):
                        term = w_ref[k:k + 1, LANES * g:LANES * (g + 1)] * slab(t + k, g)
                        acc = term if acc is None else acc + term
                    out_ref[g, pl.ds(t, nb, stride=tt), :] = acc
            for j_out in range(hist):
                for g in range(N_LG):
                    new_state_ref[j_out, :, LANES * g:LANES * (g + 1)] = slab(j_out + tt, g)

        conv(cwa_ref, sa_ref, fa_ref, ca_ref, na_ref, CONV_A)
        conv(cwb_ref, sb_ref, fb_ref, cb_ref, nb_ref, CONV_B)

    for b, c0, r0, rc in ([] if time_major_state else _seq_chunks(0, rows, tt)):
        for g in range(N_LG):
            def zc(c):
                return z_ref[r0:r0 + rc, MIX_W * c + LANES * g:MIX_W * c + LANES * (g + 1)]
            fa_ref[b * N_LG + g, HIST_A + c0:HIST_A + c0 + rc, :] = zc(2) * zc(0)
            fb_ref[b * N_LG + g, HIST_B + c0:HIST_B + c0 + rc, :] = zc(4) * _sigmoid(zc(5))

    for b, c0, r0, rc in ([] if time_major_state else _seq_chunks(0, rows, tt)):
        for g in range(N_LG):
            lg = slice(LANES * g, LANES * (g + 1))
            acc = None
            for k in range(CONV_A):
                off = HIST_A - (CONV_A - 1) + k + c0
                t = cwa_ref[k:k + 1, lg] * fa_ref[b * N_LG + g, off:off + rc, :]
                acc = t if acc is None else acc + t
            ca_ref[g, r0:r0 + rc, :] = acc
            acc = None
            for k in range(CONV_B):
                off = HIST_B - (CONV_B - 1) + k + c0
                t = cwb_ref[k:k + 1, lg] * fb_ref[b * N_LG + g, off:off + rc, :]
                acc = t if acc is None else acc + t
            cb_ref[g, r0:r0 + rc, :] = acc

    for b in range(0 if time_major_state else nb):
        for g in range(N_LG):
            lg = slice(LANES * g, LANES * (g + 1))
            na_ref[b, :, lg] = fa_ref[b * N_LG + g, HIST_A + tt - (CONV_A - 1):HIST_A + tt, :]
            nb_ref[b, :, lg] = fb_ref[b * N_LG + g, HIST_B + tt - (CONV_B - 1):HIST_B + tt, :]
            if nt > 1:
                fa_ref[b * N_LG + g, 0:HIST_A, :] = fa_ref[b * N_LG + g, tt:tt + HIST_A, :]
                fb_ref[b * N_LG + g, 0:HIST_B, :] = fb_ref[b * N_LG + g, tt:tt + HIST_B, :]

    _even_gating(z_ref, ca_ref, cb_ref, cbb_ref, lng_ref, lnb_ref, mix_ref, rows)
    _out_proj_task(mix_ref, wout_ref, o_ref, 0, rows)()
    for task in _residual_tasks(x_ref, mod_ref, seq0, gpost_ref, o_ref, y_ref, 0, rows, tt):
        task()


def _even_layer(x2d, mod, st_a, st_b, g_pre, w_in, cw_a, cw_b, cb_b, ln_g, ln_b, w_out, g_post,
                *, n, t, nb, tt, seq_base, time_major_state):
    rows = nb * tt
    nt = t // tt
    kern = functools.partial(_even_kernel, nb=nb, tt=tt, nt=nt, seq_base=seq_base,
                             time_major_state=time_major_state)
    row_spec = pl.BlockSpec((rows, D_MODEL), lambda i, j: (i * nt + j, 0))

    def seq_spec(shape):
        if time_major_state:
            return pl.BlockSpec((shape[0], nb, shape[1]), lambda i, j: (0, i, 0))
        return pl.BlockSpec((nb,) + shape, lambda i, j: (i, 0, 0))

    def window_scratch(hist):
        if time_major_state:
            return pltpu.VMEM((N_LG, rows, LANES), F32)
        return pltpu.VMEM((nb * N_LG, hist + tt, LANES), F32)

    def const_spec(shape):
        return pl.BlockSpec(shape, lambda i, j: (0,) * len(shape))

    return pl.pallas_call(
        kern,
        out_shape=(
            jax.ShapeDtypeStruct((n * t, D_MODEL), F32),
            jax.ShapeDtypeStruct(st_a.shape, F32),
            jax.ShapeDtypeStruct(st_b.shape, F32),
        ),
        grid=(n // nb, nt),
        in_specs=[
            row_spec,
            const_spec(mod.shape),
            seq_spec((CONV_A - 1, MIX_W)),
            seq_spec((CONV_B - 1, MIX_W)),
            const_spec((1, D_MODEL)),
            const_spec((D_MODEL, EVEN_IN)),
            const_spec((CONV_A, MIX_W)),
            const_spec((CONV_B, MIX_W)),
            const_spec((1, MIX_W)),
            const_spec((1, MIX_W)),
            const_spec((1, MIX_W)),
            const_spec((2 * MIX_W, D_MODEL)),
            const_spec((1, D_MODEL)),
        ],
        out_specs=(row_spec, seq_spec((CONV_A - 1, MIX_W)), seq_spec((CONV_B - 1, MIX_W))),
        scratch_shapes=[
            pltpu.VMEM((rows, D_MODEL), BF16),
            pltpu.VMEM((rows, EVEN_IN + ROW_PAD), F32),
            pltpu.VMEM((rows, D_MODEL), F32),
            window_scratch(HIST_A),
            window_scratch(HIST_B),
            pltpu.VMEM((N_LG, rows, LANES), F32),
            pltpu.VMEM((N_LG, rows, LANES), F32),
            pltpu.VMEM((rows, 2 * MIX_W), BF16),
        ],
        compiler_params=pltpu.CompilerParams(
            dimension_semantics=("arbitrary", "arbitrary"), vmem_limit_bytes=VMEM_LIMIT_BYTES),
        name="even_layer",
    )(x2d, mod, st_a, st_b, g_pre, w_in, cw_a, cw_b, cb_b, ln_g, ln_b, w_out, g_post)


def _odd_kernel(sinks_ref, x_ref, mod_ref, sc_ref, sk_ref, sv_ref, gpre_ref, win_ref, pw_ref, ps_ref,
                wout_ref, gpost_ref,
                y_ref, nc_ref, nk_ref, nv_ref,
                h_ref, z_ref, o_ref, fc_ref, kf_ref, ks_ref, vf_ref, vs_ref, pc_ref, od_ref, mix_ref,
                *, nb, tt, nt, pos0, seq_base, time_major_pool):
    rows = nb * tt
    j = pl.program_id(1)
    seq0 = seq_base + pl.program_id(0) * nb
    tile_pos = pos0 + j * tt
    assert not time_major_pool or nt == 1

    @pl.when(j == 0)
    def _load_state():
        for b in range(nb):
            for g in range(0 if time_major_pool else N_LG):
                fc_ref[b * N_LG + g, 0:8, :] = jnp.zeros((8, LANES), F32)
                fc_ref[b * N_LG + g, HIST_C - POOL_PAST:HIST_C, :] = sc_ref[b, :, LANES * g:LANES * (g + 1)]
            k_past, v_past = sk_ref[b].T, sv_ref[b].T
            kf_ref[b, 0:WINDOW, :] = k_past
            vf_ref[b, 0:WINDOW, :] = v_past
            ks_ref[b, 0:WINDOW, :] = pltpu.roll(k_past, HEAD_DIM, axis=1)
            vs_ref[b, 0:WINDOW, :] = pltpu.roll(v_past, HEAD_DIM, axis=1)

    qs = min(tt, WINDOW)
    gs = max(1, ATT_MIN_Q_ROWS // qs)
    kw = WINDOW + qs
    n_q, n_keys = gs * qs, gs * kw
    assert N_REP == 4 and 2 * HEAD_DIM == LANES and KV_W == LANES
    assert qs & (qs - 1) == 0 and gs & (gs - 1) == 0 and nb % gs == 0 and (gs == 1 or tt == qs)
    shape = (2 * n_q, n_keys)
    ri = lax.broadcasted_iota(jnp.int32, shape, 0)
    ci = lax.broadcasted_iota(jnp.int32, shape, 1)
    q_seq = (ri >> (qs.bit_length() - 1)) & (gs - 1)
    k_seq = jnp.zeros(shape, jnp.int32)
    for s_i in range(1, gs):
        k_seq = k_seq + (ci >= s_i * kw).astype(jnp.int32)
    k_c = ci - k_seq * kw
    dist = (ri & (qs - 1)) + WINDOW - k_c
    band = (q_seq == k_seq) & (dist >= 0) & (dist < WINDOW)
    dist_f = dist.astype(F32)
    upper_grp = ri >= n_q
    upper_grp_col = lax.broadcasted_iota(jnp.int32, (2 * n_q, 1), 0) >= n_q
    low_lanes = lax.broadcasted_iota(jnp.int32, (n_keys, LANES), 1) < HEAD_DIM
    ones_cols = jnp.ones((n_keys, LANES), BF16)

    def head_of(g, grp, parity):
        return g * N_REP + 2 * grp + parity

    def alibi_bias(g, parity):
        slope = jnp.where(upper_grp, ALIBI_SLOPES[head_of(g, 1, parity)], ALIBI_SLOPES[head_of(g, 0, parity)])
        return jnp.where(band, -(slope * dist_f), -jnp.inf)

    biases = {(g, parity): alibi_bias(g, parity) for g in range(N_KV) for parity in range(2)}

    def attention(b0, sb, g):
        r0 = b0 * tt + sb * qs
        p0 = tile_pos + sb * qs
        in_range = (k_c >= WINDOW - p0) if pos0 + sb * qs < WINDOW else None
        if gs == 1:
            win = (b0, slice(sb * qs, sb * qs + kw))
            k_nat, k_swp, v_nat, v_swp = kf_ref[win], ks_ref[win], vf_ref[win], vs_ref[win]
        else:
            k_nat, k_swp, v_nat, v_swp = (r[b0:b0 + gs].reshape(n_keys, KV_W)
                                          for r in (kf_ref, ks_ref, vf_ref, vs_ref))
        k_lo, k_hi = (k_nat, k_swp) if g == 0 else (k_swp, k_nat)
        v_lo, v_hi = (v_nat, v_swp) if g == 0 else (v_swp, v_nat)
        v_ext = jnp.concatenate([jnp.where(low_lanes, v_lo, v_hi).astype(BF16), ones_cols], axis=1)
        qg = jnp.concatenate([z_ref[r0:r0 + n_q, Q_COL + 2 * LANES * g + LANES * grp:
                                    Q_COL + 2 * LANES * g + LANES * (grp + 1)] for grp in range(2)], axis=0)
        qg = (qg * (HEAD_DIM ** -0.5)).astype(BF16)
        for parity in range(2):
            k_ext = (jnp.where(low_lanes, k_lo, 0.0) if parity == 0 else jnp.where(low_lanes, 0.0, k_hi)).astype(BF16)
            sink = jnp.where(upper_grp_col, sinks_ref[head_of(g, 1, parity)], sinks_ref[head_of(g, 0, parity)])
            sc = lax.dot_general(qg, k_ext, (((1,), (1,)), ((), ())), preferred_element_type=F32)
            bias = biases[g, parity]
            sc = sc + (bias if in_range is None else jnp.where(in_range, bias, -jnp.inf))
            m = jnp.maximum(jnp.max(sc, axis=-1, keepdims=True), sink)
            p = jnp.exp(sc - m)
            o_ext = jnp.dot(p.astype(BF16), v_ext, preferred_element_type=F32)
            o = o_ext[:, 0:LANES] / (o_ext[:, LANES:2 * LANES] + jnp.exp(sink - m))
            for grp in range(2):
                h = head_of(g, grp, parity)
                od_ref[r0:r0 + n_q, HEAD_DIM * h:HEAD_DIM * (h + 1)] = (
                    o[grp * n_q:(grp + 1) * n_q, HEAD_DIM * parity:HEAD_DIM * (parity + 1)])

    def window_inputs(b, c0, r0, rc):
        for g in range(0 if time_major_pool else N_LG):
            fc_ref[b * N_LG + g, HIST_C + c0:HIST_C + c0 + rc, :] = z_ref[r0:r0 + rc, LANES * g:LANES * (g + 1)]
        k = z_ref[r0:r0 + rc, K_COL:K_COL + KV_W]
        v = z_ref[r0:r0 + rc, V_COL:V_COL + KV_W]
        kf_ref[b, WINDOW + c0:WINDOW + c0 + rc, :] = k
        vf_ref[b, WINDOW + c0:WINDOW + c0 + rc, :] = v
        ks_ref[b, WINDOW + c0:WINDOW + c0 + rc, :] = pltpu.roll(k, HEAD_DIM, axis=1)
        vs_ref[b, WINDOW + c0:WINDOW + c0 + rc, :] = pltpu.roll(v, HEAD_DIM, axis=1)

    def pooling(b, c0, r0, rc):
        pos = tile_pos + c0 + lax.broadcasted_iota(jnp.int32, (rc, LANES), 0)
        for g, w in enumerate(POOL_WINDOWS):
            u = fc_ref[b * N_LG + g, HIST_C + c0:HIST_C + c0 + rc, :]
            acc = u
            for d in range(1, w):
                acc = acc + fc_ref[b * N_LG + g, HIST_C + c0 - d:HIST_C + c0 - d + rc, :]
            cnt = jnp.minimum(w, pos + 1).astype(F32)
            pc_ref[g, r0:r0 + rc, :] = acc / cnt - u

    def pooling_time_major(lo, n):
        b_lo, n_seq = lo // tt, n // tt

        for g in range(N_LG):
            fc_ref[g, lo:lo + n, :] = z_ref[lo:lo + n, LANES * g:LANES * (g + 1)]

        def slab(j_in, g):
            if j_in < POOL_PAST:
                return sc_ref[j_in, b_lo:b_lo + n_seq, LANES * g:LANES * (g + 1)]
            return fc_ref[g, pl.ds(lo + j_in - POOL_PAST, n_seq, stride=tt), :]

        for t in range(tt):
            for g, w in enumerate(POOL_WINDOWS):
                u = slab(POOL_PAST + t, g)
                acc = u
                for d in range(1, w):
                    acc = acc + slab(POOL_PAST + t - d, g)
                pc_ref[g, pl.ds(lo + t, n_seq, stride=tt), :] = acc / float(min(w, pos0 + t + 1)) - u
        for j_out in range(POOL_PAST):
            for g in range(N_LG):
                nc_ref[j_out, b_lo:b_lo + n_seq, LANES * g:LANES * (g + 1)] = slab(j_out + tt, g)

    def gating(r0, rc):
        for g in range(N_LG):
            lg = slice(LANES * g, LANES * (g + 1))
            mixed = jnp.dot(pc_ref[g, r0:r0 + rc, :].astype(BF16), pw_ref[g], preferred_element_type=F32)
            yc = mixed * ps_ref[:, lg] * _silu(z_ref[r0:r0 + rc, MIX_W + LANES * g:MIX_W + LANES * (g + 1)])
            mix_ref[r0:r0 + rc, lg] = yc.astype(BF16)
        yd = od_ref[r0:r0 + rc, 0:MIX_W] * _silu(z_ref[r0:r0 + rc, DG_COL:DG_COL + MIX_W])
        mix_ref[r0:r0 + rc, MIX_W:2 * MIX_W] = yd.astype(BF16)

    def mixer_tasks(lo, n):
        chunks = _seq_chunks(lo, n, tt)
        if gs == 1:
            blocks = [(r0 // tt, (r0 % tt) // qs) for r0 in range(lo, lo + n, qs)]
        else:
            blocks = [(b0, 0) for b0 in range(lo // tt, (lo + n) // tt, gs)]
        pool_tasks = ([functools.partial(pooling_time_major, lo, n)] if time_major_pool
                      else [functools.partial(pooling, *ch) for ch in chunks])
        return ([functools.partial(window_inputs, *ch) for ch in chunks]
                + pool_tasks
                + [functools.partial(attention, b0, sb, g) for b0, sb in blocks for g in range(N_KV)]
                + [functools.partial(gating, lo, n)])

    _pre_norm(x_ref, mod_ref, seq0, gpre_ref, h_ref, rows, tt)
    for c0 in range(0, ODD_IN, MXU_COLS):
        z_ref[:, c0:c0 + MXU_COLS] = jnp.dot(h_ref[...], win_ref[:, c0:c0 + MXU_COLS], preferred_element_type=F32)

    n_chunks = max(1, rows // WIN_CHUNK)
    n = rows // n_chunks
    for c in range(n_chunks + 1):
        if c > 0:
            _out_proj_task(mix_ref, wout_ref, o_ref, (c - 1) * n, n)()
        if c < n_chunks:
            for task in mixer_tasks(c * n, n):
                task()
        if c > 0:
            for task in _residual_tasks(x_ref, mod_ref, seq0, gpost_ref, o_ref, y_ref, (c - 1) * n, n, tt):
                task()

    for b in range(nb):
        for g in range(0 if time_major_pool else N_LG):
            nc_ref[b, :, LANES * g:LANES * (g + 1)] = fc_ref[b * N_LG + g, HIST_C + tt - POOL_PAST:HIST_C + tt, :]
            if nt > 1:
                fc_ref[b * N_LG + g, 0:HIST_C, :] = fc_ref[b * N_LG + g, tt:tt + HIST_C, :]
        nk_ref[b] = kf_ref[b, tt:tt + WINDOW, :]
        nv_ref[b] = vf_ref[b, tt:tt + WINDOW, :]
        if nt > 1:
            for r in (kf_ref, ks_ref, vf_ref, vs_ref):
                r[b, 0:WINDOW, :] = r[b, tt:tt + WINDOW, :]


def _odd_layer(x2d, mod, st_c, st_k, st_v, g_pre, w_in, pool_w, pool_scale, sinks, w_out, g_post,
               *, n, t, nb, tt, pos0, seq_base, time_major_pool):
    rows = nb * tt
    nt = t // tt
    kern = functools.partial(_odd_kernel, nb=nb, tt=tt, nt=nt, pos0=pos0, seq_base=seq_base,
                             time_major_pool=time_major_pool)
    row_spec = pl.BlockSpec((rows, D_MODEL), lambda i, j: (i * nt + j, 0))

    def seq_spec(shape):
        return pl.BlockSpec((nb,) + shape, lambda i, j: (i, 0, 0))

    def const_spec(shape):
        return pl.BlockSpec(shape, lambda i, j: (0,) * len(shape))

    if time_major_pool:
        pool_spec = pl.BlockSpec((POOL_PAST, nb, MIX_W), lambda i, j: (0, i, 0))
    else:
        pool_spec = seq_spec((POOL_PAST, MIX_W))

    return pl.pallas_call(
        kern,
        out_shape=(
            jax.ShapeDtypeStruct((n * t, D_MODEL), F32),
            jax.ShapeDtypeStruct(st_c.shape, F32),
            jax.ShapeDtypeStruct((n, WINDOW, KV_W), F32),
            jax.ShapeDtypeStruct((n, WINDOW, KV_W), F32),
        ),
        grid=(n // nb, nt),
        in_specs=[
            pl.BlockSpec(memory_space=pltpu.SMEM),
            row_spec,
            const_spec(mod.shape),
            pool_spec,
            seq_spec((KV_W, WINDOW)),
            seq_spec((KV_W, WINDOW)),
            const_spec((1, D_MODEL)),
            const_spec((D_MODEL, ODD_IN)),
            const_spec((N_LG, LANES, LANES)),
            const_spec((1, MIX_W)),
            const_spec((2 * MIX_W, D_MODEL)),
            const_spec((1, D_MODEL)),
        ],
        out_specs=(row_spec, pool_spec, seq_spec((WINDOW, KV_W)), seq_spec((WINDOW, KV_W))),
        scratch_shapes=[
            pltpu.VMEM((rows, D_MODEL), BF16),
            pltpu.VMEM((rows, ODD_IN + ROW_PAD), F32),
            pltpu.VMEM((rows, D_MODEL), F32),
            pltpu.VMEM((N_LG, rows, LANES) if time_major_pool else (nb * N_LG, HIST_C + tt, LANES), F32),
            pltpu.VMEM((nb, WINDOW + tt, KV_W), F32),
            pltpu.VMEM((nb, WINDOW + tt, KV_W), F32),
            pltpu.VMEM((nb, WINDOW + tt, KV_W), F32),
            pltpu.VMEM((nb, WINDOW + tt, KV_W), F32),
            pltpu.VMEM((N_LG, rows, LANES), F32),
            pltpu.VMEM((rows, MIX_W + ROW_PAD), F32),
            pltpu.VMEM((rows, 2 * MIX_W), BF16),
        ],
        compiler_params=pltpu.CompilerParams(
            dimension_semantics=("arbitrary", "arbitrary"), vmem_limit_bytes=VMEM_LIMIT_BYTES),
        name="odd_layer",
    )(sinks, x2d, mod, st_c, st_k, st_v, g_pre, w_in, pool_w, pool_scale, w_out, g_post)


def _tiling(n, t):
    if t >= TILE_ROWS:
        return 1, TILE_ROWS
    return min(n, SHORT_TILE_ROWS // t), t


def _trunk(x, mod_e, mod_o, seq_base, st_a, st_b, st_c, st_k, st_v, pos0, we, wo):
    n, t, _ = x.shape
    nb, tt = _tiling(n, t)
    short = t == tt and nb > 1

    def swap(st):
        return jnp.transpose(st, (1, 0, 2)) if short else st

    x2d, na, nbs = _even_layer(x.reshape(n * t, D_MODEL), mod_e, swap(st_a), swap(st_b), *we, n=n, t=t, nb=nb,
                               tt=tt, seq_base=seq_base, time_major_state=short)
    def channel_major(cache):
        return jnp.transpose(cache, (0, 2, 3, 1)).reshape(n, KV_W, WINDOW)

    x2d, nc, nk, nv = _odd_layer(x2d, mod_o, swap(st_c), channel_major(st_k), channel_major(st_v), *wo,
                                 n=n, t=t, nb=nb, tt=tt, pos0=pos0, seq_base=seq_base, time_major_pool=short)
    na, nbs, nc = swap(na), swap(nbs), swap(nc)
    kv_shape = (1, n, WINDOW, N_KV, HEAD_DIM)
    return (x2d.reshape(n, t, D_MODEL), na[None], nbs[None], nc[None],
            nk.reshape(kv_shape), nv.reshape(kv_shape))


def kernel(x_prompt, x_sample, state_conv_a, state_conv_b, state_pool_c, cache_win_k, cache_win_v, c_prompt, c_sample, w_mod_e, b_mod_e, g_pre_e, g_post_e, w_in_e, conv_a_w, conv_b_w, conv_b_b, ln_b_g, ln_b_b, w_out_e, w_mod_o, b_mod_o, g_pre_o, g_post_o, w_in_o, pool_w, pool_scale, sinks, w_out_o):
    n_p = x_prompt.shape[0]
    assert w_in_e.shape[0] == 1 and w_in_o.shape[0] == 1, "one even and one odd layer"

    c_all = jnp.concatenate([c_prompt, c_sample], axis=0)
    mod_e = _adaln_mod(c_all, w_mod_e[0], b_mod_e[0])
    mod_o = _adaln_mod(c_all, w_mod_o[0], b_mod_o[0])

    we = (g_pre_e, w_in_e[0].astype(BF16), conv_a_w[0], conv_b_w[0], conv_b_b, ln_b_g, ln_b_b,
          w_out_e[0].astype(BF16), g_post_e)
    wo = (g_pre_o, w_in_o[0].astype(BF16), pool_w[0].astype(BF16), pool_scale, sinks[0],
          w_out_o[0].astype(BF16), g_post_o)

    dt = x_prompt.dtype
    z_a = jnp.zeros((n_p, CONV_A - 1, MIX_W), dt)
    z_b = jnp.zeros((n_p, CONV_B - 1, MIX_W), dt)
    z_c = jnp.zeros((n_p, POOL_PAST, MIX_W), dt)
    z_kv = jnp.zeros((n_p, WINDOW, N_KV, HEAD_DIM), dt)
    y_p, pa, pb, pc, pk, pv = _trunk(x_prompt, mod_e, mod_o, 0, z_a, z_b, z_c, z_kv, z_kv, 0, we, wo)
    y_s, sa, sb, sc, sk, sv = _trunk(x_sample, mod_e, mod_o, n_p, state_conv_a[0], state_conv_b[0],
                                     state_pool_c[0], cache_win_k[0], cache_win_v[0], PAST_LEN, we, wo)
    return (y_p, y_s, pa, sa, pb, sb, pc, sc, pk, sk, pv, sv)
```

```python
import functools

import jax
import jax.numpy as jnp
from jax import lax
from jax.experimental import pallas as pl
from jax.experimental.pallas import tpu as pltpu

F32 = jnp.float32
BF16 = jnp.bfloat16

D_MODEL = 1024
MIX_W = 512
LANES = 128
MXU_COLS = 256
N_LG = MIX_W // LANES
HEAD_DIM = 64
N_HEADS = 8
N_KV = 2
N_REP = N_HEADS // N_KV
KV_W = N_KV * HEAD_DIM
CONV_A = 3
CONV_B = 31
POOL_WINDOWS = (2, 4, 8, 16)
POOL_PAST = 15
WINDOW = 128
PAST_LEN = 8192
EVEN_IN = 7 * MIX_W
ODD_IN = 4 * MIX_W + 2 * KV_W
Q_COL, K_COL, V_COL, DG_COL = 2 * MIX_W, 3 * MIX_W, 3 * MIX_W + KV_W, 3 * MIX_W + 2 * KV_W
RMS_EPS = 1e-6
LN_EPS = 1e-5
HIST_A = 8
HIST_B = 32
HIST_C = 16
ATT_MIN_Q_ROWS = 32
TILE_ROWS = 1024
SHORT_TILE_ROWS = 256
ROW_CHUNK = 64
WIN_CHUNK = 128
MOD_K_BLOCK = 256
ROW_PAD = LANES
VMEM_LIMIT_BYTES = 56 * 1024 * 1024
NEG_LOG2_E = -1.4426950408889634
ALIBI_SLOPES = tuple(float(2.0 ** (-8.0 * (h + 1) / N_HEADS)) for h in range(N_HEADS))


def _sigmoid(x):
    return 1.0 / (1.0 + jnp.exp2(x * NEG_LOG2_E))


def _silu(x):
    return x * _sigmoid(x)


def _mod_kernel(c_ref, w_ref, b_ref, o_ref):
    k = pl.program_id(0)
    part = jnp.dot(_silu(c_ref[...]).astype(BF16), w_ref[...].astype(BF16), preferred_element_type=F32)

    @pl.when(k == 0)
    def _first():
        o_ref[...] = part + b_ref[...]

    @pl.when(k > 0)
    def _rest():
        o_ref[...] += part


def _adaln_mod(c_all, w_mod, b_mod):
    n = c_all.shape[0]
    return pl.pallas_call(
        _mod_kernel,
        out_shape=jax.ShapeDtypeStruct((n, 3 * D_MODEL), F32),
        grid=(D_MODEL // MOD_K_BLOCK,),
        in_specs=[
            pl.BlockSpec((n, MOD_K_BLOCK), lambda k: (0, k)),
            pl.BlockSpec((MOD_K_BLOCK, 3 * D_MODEL), lambda k: (k, 0)),
            pl.BlockSpec((1, 3 * D_MODEL), lambda k: (0, 0)),
        ],
        out_specs=pl.BlockSpec((n, 3 * D_MODEL), lambda k: (0, 0)),
        compiler_params=pltpu.CompilerParams(dimension_semantics=("arbitrary",)),
        name="adaln_mod",
    )(c_all, w_mod, b_mod.reshape(1, -1))


def _mod_rows(mod_ref, seq0, col, r0, rc, tt):
    if tt >= rc:
        return mod_ref[pl.ds(seq0 + r0 // tt, 1), col:col + D_MODEL]
    parts = [jnp.broadcast_to(mod_ref[pl.ds(seq0 + b, 1), col:col + D_MODEL], (tt, D_MODEL))
             for b in range(r0 // tt, (r0 + rc) // tt)]
    return jnp.concatenate(parts, axis=0)


def _pre_norm(x_ref, mod_ref, seq0, g_ref, h_ref, rows, tt):
    rc = min(rows, ROW_CHUNK)
    for r0 in range(0, rows, rc):
        x = x_ref[r0:r0 + rc, :]
        ms = jnp.mean(x * x, axis=-1, keepdims=True)
        gain = g_ref[...] * (1.0 + _mod_rows(mod_ref, seq0, D_MODEL, r0, rc, tt))
        h = x * lax.rsqrt(ms + RMS_EPS) * gain + _mod_rows(mod_ref, seq0, 0, r0, rc, tt)
        h_ref[r0:r0 + rc, :] = h.astype(BF16)


def _out_proj_task(mix_ref, wout_ref, o_ref, lo, n):
    def task():
        o_ref[lo:lo + n, :] = jnp.dot(mix_ref[lo:lo + n, :], wout_ref[...], preferred_element_type=F32)
    return task


def _residual_tasks(x_ref, mod_ref, seq0, g_ref, o_ref, y_ref, lo, n, tt):
    rc = min(n, ROW_CHUNK)

    def chunk(r0):
        o = o_ref[r0:r0 + rc, :]
        ms = jnp.mean(o * o, axis=-1, keepdims=True)
        gain = g_ref[...] * _mod_rows(mod_ref, seq0, 2 * D_MODEL, r0, rc, tt)
        y_ref[r0:r0 + rc, :] = x_ref[r0:r0 + rc, :] + o * lax.rsqrt(ms + RMS_EPS) * gain
    return [functools.partial(chunk, r0) for r0 in range(lo, lo + n, rc)]


def _seq_chunks(lo, n, tt):
    rc = min(tt, n, WIN_CHUNK)
    return [(r0 // tt, r0 % tt, r0, rc) for r0 in range(lo, lo + n, rc)]


def _even_gating(z_ref, ca_ref, cb_ref, cbb_ref, lng_ref, lnb_ref, mix_ref, rows):
    rc = min(rows, ROW_CHUNK)
    for r0 in range(0, rows, rc):
        def zs(c):
            return z_ref[r0:r0 + rc, MIX_W * c:MIX_W * (c + 1)]

        def all_lanes(c_ref):
            return jnp.concatenate([c_ref[g, r0:r0 + rc, :] for g in range(N_LG)], axis=1)
        ya = zs(1) * all_lanes(ca_ref) * _silu(zs(3))
        mix_ref[r0:r0 + rc, 0:MIX_W] = ya.astype(BF16)
        yb = all_lanes(cb_ref) + cbb_ref[...]
        mu = jnp.mean(yb, axis=-1, keepdims=True)
        yc = yb - mu
        var = jnp.mean(yc * yc, axis=-1, keepdims=True)
        ln = yc * lax.rsqrt(var + LN_EPS) * lng_ref[...] + lnb_ref[...]
        mix_ref[r0:r0 + rc, MIX_W:2 * MIX_W] = (_silu(ln) * _silu(zs(6))).astype(BF16)


def _even_kernel(x_ref, mod_ref, sa_ref, sb_ref, gpre_ref, win_ref, cwa_ref, cwb_ref, cbb_ref,
                 lng_ref, lnb_ref, wout_ref, gpost_ref,
                 y_ref, na_ref, nb_ref,
                 h_ref, z_ref, o_ref, fa_ref, fb_ref, ca_ref, cb_ref, mix_ref,
                 *, nb, tt, nt, seq_base, time_major_state):
    rows = nb * tt
    j = pl.program_id(1)
    seq0 = seq_base + pl.program_id(0) * nb
    assert not time_major_state or nt == 1

    @pl.when(j == 0)
    def _load_state():
        for b in range(0 if time_major_state else nb):
            for g in range(N_LG):
                lg = slice(LANES * g, LANES * (g + 1))
                fa_ref[b * N_LG + g, 0:HIST_A, :] = jnp.zeros((HIST_A, LANES), F32)
                fa_ref[b * N_LG + g, HIST_A - (CONV_A - 1):HIST_A, :] = sa_ref[b, :, lg]
                fb_ref[b * N_LG + g, 0:8, :] = jnp.zeros((8, LANES), F32)
                fb_ref[b * N_LG + g, HIST_B - (CONV_B - 1):HIST_B, :] = sb_ref[b, :, lg]

    _pre_norm(x_ref, mod_ref, seq0, gpre_ref, h_ref, rows, tt)

    split_order = (4, 5, 0, 2, 1, 3, 6)
    for c0 in [MIX_W * s + d for s in split_order for d in range(0, MIX_W, MXU_COLS)]:
        z_ref[:, c0:c0 + MXU_COLS] = jnp.dot(h_ref[...], win_ref[:, c0:c0 + MXU_COLS], preferred_element_type=F32)

    if time_major_state:
        rc = min(rows, WIN_CHUNK)
        for r0 in range(0, rows, rc):
            for g in range(N_LG):
                def zc(c):
                    return z_ref[r0:r0 + rc, MIX_W * c + LANES * g:MIX_W * c + LANES * (g + 1)]
                fa_ref[g, r0:r0 + rc, :] = zc(2) * zc(0)
                fb_ref[g, r0:r0 + rc, :] = zc(4) * _sigmoid(zc(5))

        def conv(w_ref, state_ref, u_ref, out_ref, new_state_ref, taps):
            hist = taps - 1

            def slab(j_in, g):
                if j_in < hist:
                    return state_ref[j_in, :, LANES * g:LANES * (g + 1)]
                return u_ref[g, pl.ds(j_in - hist, nb, stride=tt), :]

            for t in range(tt):
                for g in range(N_LG):
                    acc = None
                    for k in range(taps):
                        term = w_ref[k:k + 1, LANES * g:LANES * (g + 1)] * slab(t + k, g)
                        acc = term if acc is None else acc + term
                    out_ref[g, pl.ds(t, nb, stride=tt), :] = acc
            for j_out in range(hist):
                for g in range(N_LG):
                    new_state_ref[j_out, :, LANES * g:LANES * (g + 1)] = slab(j_out + tt, g)

        conv(cwa_ref, sa_ref, fa_ref, ca_ref, na_ref, CONV_A)
        conv(cwb_ref, sb_ref, fb_ref, cb_ref, nb_ref, CONV_B)

    for b, c0, r0, rc in ([] if time_major_state else _seq_chunks(0, rows, tt)):
        for g in range(N_LG):
            def zc(c):
                return z_ref[r0:r0 + rc, MIX_W * c + LANES * g:MIX_W * c + LANES * (g + 1)]
            fa_ref[b * N_LG + g, HIST_A + c0:HIST_A + c0 + rc, :] = zc(2) * zc(0)
            fb_ref[b * N_LG + g, HIST_B + c0:HIST_B + c0 + rc, :] = zc(4) * _sigmoid(zc(5))

    for b, c0, r0, rc in ([] if time_major_state else _seq_chunks(0, rows, tt)):
        for g in range(N_LG):
            lg = slice(LANES * g, LANES * (g + 1))
            acc = None
            for k in range(CONV_A):
                off = HIST_A - (CONV_A - 1) + k + c0
                t = cwa_ref[k:k + 1, lg] * fa_ref[b * N_LG + g, off:off + rc, :]
                acc = t if acc is None else acc + t
            ca_ref[g, r0:r0 + rc, :] = acc
            acc = None
            for k in range(CONV_B):
                off = HIST_B - (CONV_B - 1) + k + c0
                t = cwb_ref[k:k + 1, lg] * fb_ref[b * N_LG + g, off:off + rc, :]
                acc = t if acc is None else acc + t
            cb_ref[g, r0:r0 + rc, :] = acc

    for b in range(0 if time_major_state else nb):
        for g in range(N_LG):
            lg = slice(LANES * g, LANES * (g + 1))
            na_ref[b, :, lg] = fa_ref[b * N_LG + g, HIST_A + tt - (CONV_A - 1):HIST_A + tt, :]
            nb_ref[b, :, lg] = fb_ref[b * N_LG + g, HIST_B + tt - (CONV_B - 1):HIST_B + tt, :]
            if nt > 1:
                fa_ref[b * N_LG + g, 0:HIST_A, :] = fa_ref[b * N_LG + g, tt:tt + HIST_A, :]
                fb_ref[b * N_LG + g, 0:HIST_B, :] = fb_ref[b * N_LG + g, tt:tt + HIST_B, :]

    _even_gating(z_ref, ca_ref, cb_ref, cbb_ref, lng_ref, lnb_ref, mix_ref, rows)
    _out_proj_task(mix_ref, wout_ref, o_ref, 0, rows)()
    for task in _residual_tasks(x_ref, mod_ref, seq0, gpost_ref, o_ref, y_ref, 0, rows, tt):
        task()


def _even_layer(x2d, mod, st_a, st_b, g_pre, w_in, cw_a, cw_b, cb_b, ln_g, ln_b, w_out, g_post,
                *, n, t, nb, tt, seq_base, time_major_state):
    rows = nb * tt
    nt = t // tt
    kern = functools.partial(_even_kernel, nb=nb, tt=tt, nt=nt, seq_base=seq_base,
                             time_major_state=time_major_state)
    row_spec = pl.BlockSpec((rows, D_MODEL), lambda i, j: (i * nt + j, 0))

    def seq_spec(shape):
        if time_major_state:
            return pl.BlockSpec((shape[0], nb, shape[1]), lambda i, j: (0, i, 0))
        return pl.BlockSpec((nb,) + shape, lambda i, j: (i, 0, 0))

    def window_scratch(hist):
        if time_major_state:
            return pltpu.VMEM((N_LG, rows, LANES), F32)
        return pltpu.VMEM((nb * N_LG, hist + tt, LANES), F32)

    def const_spec(shape):
        return pl.BlockSpec(shape, lambda i, j: (0,) * len(shape))

    return pl.pallas_call(
        kern,
        out_shape=(
            jax.ShapeDtypeStruct((n * t, D_MODEL), F32),
            jax.ShapeDtypeStruct(st_a.shape, F32),
            jax.ShapeDtypeStruct(st_b.shape, F32),
        ),
        grid=(n // nb, nt),
        in_specs=[
            row_spec,
            const_spec(mod.shape),
            seq_spec((CONV_A - 1, MIX_W)),
            seq_spec((CONV_B - 1, MIX_W)),
            const_spec((1, D_MODEL)),
            const_spec((D_MODEL, EVEN_IN)),
            const_spec((CONV_A, MIX_W)),
            const_spec((CONV_B, MIX_W)),
            const_spec((1, MIX_W)),
            const_spec((1, MIX_W)),
            const_spec((1, MIX_W)),
            const_spec((2 * MIX_W, D_MODEL)),
            const_spec((1, D_MODEL)),
        ],
        out_specs=(row_spec, seq_spec((CONV_A - 1, MIX_W)), seq_spec((CONV_B - 1, MIX_W))),
        scratch_shapes=[
            pltpu.VMEM((rows, D_MODEL), BF16),
            pltpu.VMEM((rows, EVEN_IN + ROW_PAD), F32),
            pltpu.VMEM((rows, D_MODEL), F32),
            window_scratch(HIST_A),
            window_scratch(HIST_B),
            pltpu.VMEM((N_LG, rows, LANES), F32),
            pltpu.VMEM((N_LG, rows, LANES), F32),
            pltpu.VMEM((rows, 2 * MIX_W), BF16),
        ],
        compiler_params=pltpu.CompilerParams(
            dimension_semantics=("arbitrary", "arbitrary"), vmem_limit_bytes=VMEM_LIMIT_BYTES),
        name="even_layer",
    )(x2d, mod, st_a, st_b, g_pre, w_in, cw_a, cw_b, cb_b, ln_g, ln_b, w_out, g_post)


def _odd_kernel(sinks_ref, x_ref, mod_ref, sc_ref, sk_ref, sv_ref, gpre_ref, win_ref, pw_ref, ps_ref,
                wout_ref, gpost_ref,
                y_ref, nc_ref, nk_ref, nv_ref,
                h_ref, z_ref, o_ref, fc_ref, kf_ref, ks_ref, vf_ref, vs_ref, pc_ref, od_ref, mix_ref,
                *, nb, tt, nt, pos0, seq_base, time_major_pool):
    rows = nb * tt
    j = pl.program_id(1)
    seq0 = seq_base + pl.program_id(0) * nb
    tile_pos = pos0 + j * tt
    assert not time_major_pool or nt == 1

    @pl.when(j == 0)
    def _load_state():
        for b in range(nb):
            for g in range(0 if time_major_pool else N_LG):
                fc_ref[b * N_LG + g, 0:8, :] = jnp.zeros((8, LANES), F32)
                fc_ref[b * N_LG + g, HIST_C - POOL_PAST:HIST_C, :] = sc_ref[b, :, LANES * g:LANES * (g + 1)]
            k_past, v_past = sk_ref[b].T, sv_ref[b].T
            kf_ref[b, 0:WINDOW, :] = k_past
            vf_ref[b, 0:WINDOW, :] = v_past
            ks_ref[b, 0:WINDOW, :] = pltpu.roll(k_past, HEAD_DIM, axis=1)
            vs_ref[b, 0:WINDOW, :] = pltpu.roll(v_past, HEAD_DIM, axis=1)

    qs = min(tt, WINDOW)
    gs = max(1, ATT_MIN_Q_ROWS // qs)
    kw = WINDOW + qs
    n_q, n_keys = gs * qs, gs * kw
    assert N_REP == 4 and 2 * HEAD_DIM == LANES and KV_W == LANES
    assert qs & (qs - 1) == 0 and gs & (gs - 1) == 0 and nb % gs == 0 and (gs == 1 or tt == qs)
    shape = (2 * n_q, n_keys)
    ri = lax.broadcasted_iota(jnp.int32, shape, 0)
    ci = lax.broadcasted_iota(jnp.int32, shape, 1)
    q_seq = (ri >> (qs.bit_length() - 1)) & (gs - 1)
    k_seq = jnp.zeros(shape, jnp.int32)
    for s_i in range(1, gs):
        k_seq = k_seq + (ci >= s_i * kw).astype(jnp.int32)
    k_c = ci - k_seq * kw
    dist = (ri & (qs - 1)) + WINDOW - k_c
    band = (q_seq == k_seq) & (dist >= 0) & (dist < WINDOW)
    dist_f = dist.astype(F32)
    upper_grp = ri >= n_q
    upper_grp_col = lax.broadcasted_iota(jnp.int32, (2 * n_q, 1), 0) >= n_q
    low_lanes = lax.broadcasted_iota(jnp.int32, (n_keys, LANES), 1) < HEAD_DIM
    ones_cols = jnp.ones((n_keys, LANES), BF16)

    def head_of(g, grp, parity):
        return g * N_REP + 2 * grp + parity

    def alibi_bias(g, parity):
        slope = jnp.where(upper_grp, ALIBI_SLOPES[head_of(g, 1, parity)], ALIBI_SLOPES[head_of(g, 0, parity)])
        return jnp.where(band, -(slope * dist_f), -jnp.inf)

    biases = {(g, parity): alibi_bias(g, parity) for g in range(N_KV) for parity in range(2)}

    def attention(b0, sb, g):
        r0 = b0 * tt + sb * qs
        p0 = tile_pos + sb * qs
        in_range = (k_c >= WINDOW - p0) if pos0 + sb * qs < WINDOW else None
        if gs == 1:
            win = (b0, slice(sb * qs, sb * qs + kw))
            k_nat, k_swp, v_nat, v_swp = kf_ref[win], ks_ref[win], vf_ref[win], vs_ref[win]
        else:
            k_nat, k_swp, v_nat, v_swp = (r[b0:b0 + gs].reshape(n_keys, KV_W)
                                          for r in (kf_ref, ks_ref, vf_ref, vs_ref))
        k_lo, k_hi = (k_nat, k_swp) if g == 0 else (k_swp, k_nat)
        v_lo, v_hi = (v_nat, v_swp) if g == 0 else (v_swp, v_nat)
        v_ext = jnp.concatenate([jnp.where(low_lanes, v_lo, v_hi).astype(BF16), ones_cols], axis=1)
        qg = jnp.concatenate([z_ref[r0:r0 + n_q, Q_COL + 2 * LANES * g + LANES * grp:
                                    Q_COL + 2 * LANES * g + LANES * (grp + 1)] for grp in range(2)], axis=0)
        qg = (qg * (HEAD_DIM ** -0.5)).astype(BF16)
        for parity in range(2):
            k_ext = (jnp.where(low_lanes, k_lo, 0.0) if parity == 0 else jnp.where(low_lanes, 0.0, k_hi)).astype(BF16)
            sink = jnp.where(upper_grp_col, sinks_ref[head_of(g, 1, parity)], sinks_ref[head_of(g, 0, parity)])
            sc = lax.dot_general(qg, k_ext, (((1,), (1,)), ((), ())), preferred_element_type=F32)
            bias = biases[g, parity]
            sc = sc + (bias if in_range is None else jnp.where(in_range, bias, -jnp.inf))
            m = jnp.maximum(jnp.max(sc, axis=-1, keepdims=True), sink)
            p = jnp.exp(sc - m)
            o_ext = jnp.dot(p.astype(BF16), v_ext, preferred_element_type=F32)
            o = o_ext[:, 0:LANES] / (o_ext[:, LANES:2 * LANES] + jnp.exp(sink - m))
            for grp in range(2):
                h = head_of(g, grp, parity)
                od_ref[r0:r0 + n_q, HEAD_DIM * h:HEAD_DIM * (h + 1)] = (
                    o[grp * n_q:(grp + 1) * n_q, HEAD_DIM * parity:HEAD_DIM * (parity + 1)])

    def window_inputs(b, c0, r0, rc):
        for g in range(0 if time_major_pool else N_LG):
            fc_ref[b * N_LG + g, HIST_C + c0:HIST_C + c0 + rc, :] = z_ref[r0:r0 + rc, LANES * g:LANES * (g + 1)]
        k = z_ref[r0:r0 + rc, K_COL:K_COL + KV_W]
        v = z_ref[r0:r0 + rc, V_COL:V_COL + KV_W]
        kf_ref[b, WINDOW + c0:WINDOW + c0 + rc, :] = k
        vf_ref[b, WINDOW + c0:WINDOW + c0 + rc, :] = v
        ks_ref[b, WINDOW + c0:WINDOW + c0 + rc, :] = pltpu.roll(k, HEAD_DIM, axis=1)
        vs_ref[b, WINDOW + c0:WINDOW + c0 + rc, :] = pltpu.roll(v, HEAD_DIM, axis=1)

    def pooling(b, c0, r0, rc):
        pos = tile_pos + c0 + lax.broadcasted_iota(jnp.int32, (rc, LANES), 0)
        for g, w in enumerate(POOL_WINDOWS):
            u = fc_ref[b * N_LG + g, HIST_C + c0:HIST_C + c0 + rc, :]
            acc = u
            for d in range(1, w):
                acc = acc + fc_ref[b * N_LG + g, HIST_C + c0 - d:HIST_C + c0 - d + rc, :]
            cnt = jnp.minimum(w, pos + 1).astype(F32)
            pc_ref[g, r0:r0 + rc, :] = acc / cnt - u

    def pooling_time_major(lo, n):
        b_lo, n_seq = lo // tt, n // tt

        for g in range(N_LG):
            fc_ref[g, lo:lo + n, :] = z_ref[lo:lo + n, LANES * g:LANES * (g + 1)]

        def slab(j_in, g):
            if j_in < POOL_PAST:
                return sc_ref[j_in, b_lo:b_lo + n_seq, LANES * g:LANES * (g + 1)]
            return fc_ref[g, pl.ds(lo + j_in - POOL_PAST, n_seq, stride=tt), :]

        for t in range(tt):
            for g, w in enumerate(POOL_WINDOWS):
                u = slab(POOL_PAST + t, g)
                acc = u
                for d in range(1, w):
                    acc = acc + slab(POOL_PAST + t - d, g)
                pc_ref[g, pl.ds(lo + t, n_seq, stride=tt), :] = acc / float(min(w, pos0 + t + 1)) - u
        for j_out in range(POOL_PAST):
            for g in range(N_LG):
                nc_ref[j_out, b_lo:b_lo + n_seq, LANES * g:LANES * (g + 1)] = slab(j_out + tt, g)

    def gating(r0, rc):
        for g in range(N_LG):
            lg = slice(LANES * g, LANES * (g + 1))
            mixed = jnp.dot(pc_ref[g, r0:r0 + rc, :].astype(BF16), pw_ref[g], preferred_element_type=F32)
            yc = mixed * ps_ref[:, lg] * _silu(z_ref[r0:r0 + rc, MIX_W + LANES * g:MIX_W + LANES * (g + 1)])
            mix_ref[r0:r0 + rc, lg] = yc.astype(BF16)
        yd = od_ref[r0:r0 + rc, 0:MIX_W] * _silu(z_ref[r0:r0 + rc, DG_COL:DG_COL + MIX_W])
        mix_ref[r0:r0 + rc, MIX_W:2 * MIX_W] = yd.astype(BF16)

    def mixer_tasks(lo, n):
        chunks = _seq_chunks(lo, n, tt)
        if gs == 1:
            blocks = [(r0 // tt, (r0 % tt) // qs) for r0 in range(lo, lo + n, qs)]
        else:
            blocks = [(b0, 0) for b0 in range(lo // tt, (lo + n) // tt, gs)]
        pool_tasks = ([functools.partial(pooling_time_major, lo, n)] if time_major_pool
                      else [functools.partial(pooling, *ch) for ch in chunks])
        return ([functools.partial(window_inputs, *ch) for ch in chunks]
                + pool_tasks
                + [functools.partial(attention, b0, sb, g) for b0, sb in blocks for g in range(N_KV)]
                + [functools.partial(gating, lo, n)])

    _pre_norm(x_ref, mod_ref, seq0, gpre_ref, h_ref, rows, tt)
    for c0 in range(0, ODD_IN, MXU_COLS):
        z_ref[:, c0:c0 + MXU_COLS] = jnp.dot(h_ref[...], win_ref[:, c0:c0 + MXU_COLS], preferred_element_type=F32)

    n_chunks = max(1, rows // WIN_CHUNK)
    n = rows // n_chunks
    for c in range(n_chunks + 1):
        if c > 0:
            _out_proj_task(mix_ref, wout_ref, o_ref, (c - 1) * n, n)()
        if c < n_chunks:
            for task in mixer_tasks(c * n, n):
                task()
        if c > 0:
            for task in _residual_tasks(x_ref, mod_ref, seq0, gpost_ref, o_ref, y_ref, (c - 1) * n, n, tt):
                task()

    for b in range(nb):
        for g in range(0 if time_major_pool else N_LG):
            nc_ref[b, :, LANES * g:LANES * (g + 1)] = fc_ref[b * N_LG + g, HIST_C + tt - POOL_PAST:HIST_C + tt, :]
            if nt > 1:
                fc_ref[b * N_LG + g, 0:HIST_C, :] = fc_ref[b * N_LG + g, tt:tt + HIST_C, :]
        nk_ref[b] = kf_ref[b, tt:tt + WINDOW, :].T
        nv_ref[b] = vf_ref[b, tt:tt + WINDOW, :].T
        if nt > 1:
            for r in (kf_ref, ks_ref, vf_ref, vs_ref):
                r[b, 0:WINDOW, :] = r[b, tt:tt + WINDOW, :]


def _odd_layer(x2d, mod, st_c, st_k, st_v, g_pre, w_in, pool_w, pool_scale, sinks, w_out, g_post,
               *, n, t, nb, tt, pos0, seq_base, time_major_pool):
    rows = nb * tt
    nt = t // tt
    kern = functools.partial(_odd_kernel, nb=nb, tt=tt, nt=nt, pos0=pos0, seq_base=seq_base,
                             time_major_pool=time_major_pool)
    row_spec = pl.BlockSpec((rows, D_MODEL), lambda i, j: (i * nt + j, 0))

    def seq_spec(shape):
        return pl.BlockSpec((nb,) + shape, lambda i, j: (i, 0, 0))

    def const_spec(shape):
        return pl.BlockSpec(shape, lambda i, j: (0,) * len(shape))

    if time_major_pool:
        pool_spec = pl.BlockSpec((POOL_PAST, nb, MIX_W), lambda i, j: (0, i, 0))
    else:
        pool_spec = seq_spec((POOL_PAST, MIX_W))

    return pl.pallas_call(
        kern,
        out_shape=(
            jax.ShapeDtypeStruct((n * t, D_MODEL), F32),
            jax.ShapeDtypeStruct(st_c.shape, F32),
            jax.ShapeDtypeStruct((n, KV_W, WINDOW), F32),
            jax.ShapeDtypeStruct((n, KV_W, WINDOW), F32),
        ),
        grid=(n // nb, nt),
        in_specs=[
            pl.BlockSpec(memory_space=pltpu.SMEM),
            row_spec,
            const_spec(mod.shape),
            pool_spec,
            seq_spec((KV_W, WINDOW)),
            seq_spec((KV_W, WINDOW)),
            const_spec((1, D_MODEL)),
            const_spec((D_MODEL, ODD_IN)),
            const_spec((N_LG, LANES, LANES)),
            const_spec((1, MIX_W)),
            const_spec((2 * MIX_W, D_MODEL)),
            const_spec((1, D_MODEL)),
        ],
        out_specs=(row_spec, pool_spec, seq_spec((KV_W, WINDOW)), seq_spec((KV_W, WINDOW))),
        scratch_shapes=[
            pltpu.VMEM((rows, D_MODEL), BF16),
            pltpu.VMEM((rows, ODD_IN + ROW_PAD), F32),
            pltpu.VMEM((rows, D_MODEL), F32),
            pltpu.VMEM((N_LG, rows, LANES) if time_major_pool else (nb * N_LG, HIST_C + tt, LANES), F32),
            pltpu.VMEM((nb, WINDOW + tt, KV_W), F32),
            pltpu.VMEM((nb, WINDOW + tt, KV_W), F32),
            pltpu.VMEM((nb, WINDOW + tt, KV_W), F32),
            pltpu.VMEM((nb, WINDOW + tt, KV_W), F32),
            pltpu.VMEM((N_LG, rows, LANES), F32),
            pltpu.VMEM((rows, MIX_W + ROW_PAD), F32),
            pltpu.VMEM((rows, 2 * MIX_W), BF16),
        ],
        compiler_params=pltpu.CompilerParams(
            dimension_semantics=("arbitrary", "arbitrary"), vmem_limit_bytes=VMEM_LIMIT_BYTES),
        name="odd_layer",
    )(sinks, x2d, mod, st_c, st_k, st_v, g_pre, w_in, pool_w, pool_scale, w_out, g_post)


def _tiling(n, t):
    if t >= TILE_ROWS:
        return 1, TILE_ROWS
    return min(n, SHORT_TILE_ROWS // t), t


def _trunk(x, mod_e, mod_o, seq_base, st_a, st_b, st_c, st_k, st_v, pos0, we, wo):
    n, t, _ = x.shape
    nb, tt = _tiling(n, t)
    short = t == tt and nb > 1

    def swap(st):
        return jnp.transpose(st, (1, 0, 2)) if short else st

    x2d, na, nbs = _even_layer(x.reshape(n * t, D_MODEL), mod_e, swap(st_a), swap(st_b), *we, n=n, t=t, nb=nb,
                               tt=tt, seq_base=seq_base, time_major_state=short)
    def channel_major(cache):
        return jnp.transpose(cache, (0, 2, 3, 1)).reshape(n, KV_W, WINDOW)

    def position_major(cache):
        return jnp.transpose(cache.reshape(n, N_KV, HEAD_DIM, WINDOW), (0, 3, 1, 2))

    x2d, nc, nk, nv = _odd_layer(x2d, mod_o, swap(st_c), channel_major(st_k), channel_major(st_v), *wo,
                                 n=n, t=t, nb=nb, tt=tt, pos0=pos0, seq_base=seq_base, time_major_pool=short)
    na, nbs, nc = swap(na), swap(nbs), swap(nc)
    return (x2d.reshape(n, t, D_MODEL), na[None], nbs[None], nc[None],
            position_major(nk)[None], position_major(nv)[None])


def kernel(x_prompt, x_sample, state_conv_a, state_conv_b, state_pool_c, cache_win_k, cache_win_v, c_prompt, c_sample, w_mod_e, b_mod_e, g_pre_e, g_post_e, w_in_e, conv_a_w, conv_b_w, conv_b_b, ln_b_g, ln_b_b, w_out_e, w_mod_o, b_mod_o, g_pre_o, g_post_o, w_in_o, pool_w, pool_scale, sinks, w_out_o):
    n_p = x_prompt.shape[0]
    assert w_in_e.shape[0] == 1 and w_in_o.shape[0] == 1, "one even and one odd layer"

    c_all = jnp.concatenate([c_prompt, c_sample], axis=0)
    mod_e = _adaln_mod(c_all, w_mod_e[0], b_mod_e[0])
    mod_o = _adaln_mod(c_all, w_mod_o[0], b_mod_o[0])

    we = (g_pre_e, w_in_e[0].astype(BF16), conv_a_w[0], conv_b_w[0], conv_b_b, ln_b_g, ln_b_b,
          w_out_e[0].astype(BF16), g_post_e)
    wo = (g_pre_o, w_in_o[0].astype(BF16), pool_w[0].astype(BF16), pool_scale, sinks[0],
          w_out_o[0].astype(BF16), g_post_o)

    dt = x_prompt.dtype
    z_a = jnp.zeros((n_p, CONV_A - 1, MIX_W), dt)
    z_b = jnp.zeros((n_p, CONV_B - 1, MIX_W), dt)
    z_c = jnp.zeros((n_p, POOL_PAST, MIX_W), dt)
    z_kv = jnp.zeros((n_p, WINDOW, N_KV, HEAD_DIM), dt)
    y_p, pa, pb, pc, pk, pv = _trunk(x_prompt, mod_e, mod_o, 0, z_a, z_b, z_c, z_kv, z_kv, 0, we, wo)
    y_s, sa, sb, sc, sk, sv = _trunk(x_sample, mod_e, mod_o, n_p, state_conv_a[0], state_conv_b[0],
                                     state_pool_c[0], cache_win_k[0], cache_win_v[0], PAST_LEN, we, wo)
    return (y_p, y_s, pa, sa, pb, sb, pc, sc, pk, sk, pv, sv)
```

```python
import functools

import jax
import jax.numpy as jnp
from jax import lax
from jax.experimental import pallas as pl
from jax.experimental.pallas import tpu as pltpu

F32 = jnp.float32
BF16 = jnp.bfloat16

D_MODEL = 1024
MIX_W = 512
LANES = 128
MXU_COLS = 256
N_LG = MIX_W // LANES
HEAD_DIM = 64
N_HEADS = 8
N_KV = 2
N_REP = N_HEADS // N_KV
KV_W = N_KV * HEAD_DIM
CONV_A = 3
CONV_B = 31
POOL_WINDOWS = (2, 4, 8, 16)
POOL_PAST = 15
WINDOW = 128
PAST_LEN = 8192
EVEN_IN = 7 * MIX_W
ODD_IN = 4 * MIX_W + 2 * KV_W
Q_COL, K_COL, V_COL, DG_COL = 2 * MIX_W, 3 * MIX_W, 3 * MIX_W + KV_W, 3 * MIX_W + 2 * KV_W
RMS_EPS = 1e-6
LN_EPS = 1e-5
HIST_A = 8
HIST_B = 32
HIST_C = 16
ATT_MIN_Q_ROWS = 32
TILE_ROWS = 1024
SHORT_TILE_ROWS = 256
ROW_CHUNK = 64
WIN_CHUNK = 128
MOD_K_BLOCK = 256
ROW_PAD = LANES
VMEM_LIMIT_BYTES = 56 * 1024 * 1024
NEG_LOG2_E = -1.4426950408889634
ALIBI_SLOPES = tuple(float(2.0 ** (-8.0 * (h + 1) / N_HEADS)) for h in range(N_HEADS))


def _sigmoid(x):
    return 1.0 / (1.0 + jnp.exp2(x * NEG_LOG2_E))


def _silu(x):
    return x * _sigmoid(x)


def _mod_kernel(cp_ref, cs_ref, w_ref, b_ref, op_ref, os_ref):
    k = pl.program_id(0)
    n_s = cs_ref.shape[0]
    c = jnp.concatenate([cs_ref[...], cp_ref[...]], axis=0)
    part = jnp.dot(_silu(c).astype(BF16), w_ref[...].astype(BF16), preferred_element_type=F32)
    for o_ref, rows in ((os_ref, slice(0, n_s)), (op_ref, slice(n_s, c.shape[0]))):
        @pl.when(k == 0)
        def _first():
            o_ref[...] = part[rows] + b_ref[...]

        @pl.when(k > 0)
        def _rest():
            o_ref[...] += part[rows]


def _adaln_mod(c_prompt, c_sample, w_mod, b_mod):
    def c_spec(c):
        return pl.BlockSpec((c.shape[0], MOD_K_BLOCK), lambda k: (0, k))

    def out_spec(c):
        return pl.BlockSpec((c.shape[0], 3 * D_MODEL), lambda k: (0, 0))

    return pl.pallas_call(
        _mod_kernel,
        out_shape=tuple(jax.ShapeDtypeStruct((c.shape[0], 3 * D_MODEL), F32) for c in (c_prompt, c_sample)),
        grid=(D_MODEL // MOD_K_BLOCK,),
        in_specs=[
            c_spec(c_prompt),
            c_spec(c_sample),
            pl.BlockSpec((MOD_K_BLOCK, 3 * D_MODEL), lambda k: (k, 0)),
            pl.BlockSpec((1, 3 * D_MODEL), lambda k: (0, 0)),
        ],
        out_specs=(out_spec(c_prompt), out_spec(c_sample)),
        compiler_params=pltpu.CompilerParams(dimension_semantics=("arbitrary",)),
        name="adaln_mod",
    )(c_prompt, c_sample, w_mod, b_mod.reshape(1, -1))


def _mod_rows(mod_ref, seq0, col, r0, rc, tt):
    if tt >= rc:
        return mod_ref[pl.ds(seq0 + r0 // tt, 1), col:col + D_MODEL]
    parts = [jnp.broadcast_to(mod_ref[pl.ds(seq0 + b, 1), col:col + D_MODEL], (tt, D_MODEL))
             for b in range(r0 // tt, (r0 + rc) // tt)]
    return jnp.concatenate(parts, axis=0)


def _pre_norm(x_ref, mod_ref, seq0, g_ref, h_ref, rows, tt):
    rc = min(rows, ROW_CHUNK)
    for r0 in range(0, rows, rc):
        x = x_ref[r0:r0 + rc, :]
        ms = jnp.mean(x * x, axis=-1, keepdims=True)
        gain = g_ref[...] * (1.0 + _mod_rows(mod_ref, seq0, D_MODEL, r0, rc, tt))
        h = x * lax.rsqrt(ms + RMS_EPS) * gain + _mod_rows(mod_ref, seq0, 0, r0, rc, tt)
        h_ref[r0:r0 + rc, :] = h.astype(BF16)


def _out_proj_task(mix_ref, wout_ref, o_ref, lo, n):
    def task():
        o_ref[lo:lo + n, :] = jnp.dot(mix_ref[lo:lo + n, :], wout_ref[...], preferred_element_type=F32)
    return task


def _residual_tasks(x_ref, mod_ref, seq0, g_ref, o_ref, y_ref, lo, n, tt):
    rc = min(n, ROW_CHUNK)

    def chunk(r0):
        o = o_ref[r0:r0 + rc, :]
        ms = jnp.mean(o * o, axis=-1, keepdims=True)
        gain = g_ref[...] * _mod_rows(mod_ref, seq0, 2 * D_MODEL, r0, rc, tt)
        y_ref[r0:r0 + rc, :] = x_ref[r0:r0 + rc, :] + o * lax.rsqrt(ms + RMS_EPS) * gain
    return [functools.partial(chunk, r0) for r0 in range(lo, lo + n, rc)]


def _seq_chunks(lo, n, tt):
    rc = min(tt, n, WIN_CHUNK)
    return [(r0 // tt, r0 % tt, r0, rc) for r0 in range(lo, lo + n, rc)]


def _even_gating(z_ref, ca_ref, cb_ref, cbb_ref, lng_ref, lnb_ref, mix_ref, rows):
    rc = min(rows, ROW_CHUNK)
    for r0 in range(0, rows, rc):
        def zs(c):
            return z_ref[r0:r0 + rc, MIX_W * c:MIX_W * (c + 1)]

        def all_lanes(c_ref):
            return jnp.concatenate([c_ref[g, r0:r0 + rc, :] for g in range(N_LG)], axis=1)
        ya = zs(1) * all_lanes(ca_ref) * _silu(zs(3))
        mix_ref[r0:r0 + rc, 0:MIX_W] = ya.astype(BF16)
        yb = all_lanes(cb_ref) + cbb_ref[...]
        mu = jnp.mean(yb, axis=-1, keepdims=True)
        yc = yb - mu
        var = jnp.mean(yc * yc, axis=-1, keepdims=True)
        ln = yc * lax.rsqrt(var + LN_EPS) * lng_ref[...] + lnb_ref[...]
        mix_ref[r0:r0 + rc, MIX_W:2 * MIX_W] = (_silu(ln) * _silu(zs(6))).astype(BF16)


def _even_kernel(x_ref, mod_ref, sa_ref, sb_ref, gpre_ref, win_ref, cwa_ref, cwb_ref, cbb_ref,
                 lng_ref, lnb_ref, wout_ref, gpost_ref,
                 y_ref, na_ref, nb_ref,
                 h_ref, z_ref, o_ref, fa_ref, fb_ref, ca_ref, cb_ref, mix_ref,
                 *, nb, tt, nt, time_major_state):
    rows = nb * tt
    j = pl.program_id(1)
    seq0 = pl.program_id(0) * nb
    assert not time_major_state or nt == 1

    @pl.when(j == 0)
    def _load_state():
        for b in range(0 if time_major_state else nb):
            for g in range(N_LG):
                lg = slice(LANES * g, LANES * (g + 1))
                fa_ref[b * N_LG + g, 0:HIST_A, :] = jnp.zeros((HIST_A, LANES), F32)
                fa_ref[b * N_LG + g, HIST_A - (CONV_A - 1):HIST_A, :] = sa_ref[b, :, lg]
                fb_ref[b * N_LG + g, 0:8, :] = jnp.zeros((8, LANES), F32)
                fb_ref[b * N_LG + g, HIST_B - (CONV_B - 1):HIST_B, :] = sb_ref[b, :, lg]

    _pre_norm(x_ref, mod_ref, seq0, gpre_ref, h_ref, rows, tt)

    split_order = (4, 5, 0, 2, 1, 3, 6)
    for c0 in [MIX_W * s + d for s in split_order for d in range(0, MIX_W, MXU_COLS)]:
        z_ref[:, c0:c0 + MXU_COLS] = jnp.dot(h_ref[...], win_ref[:, c0:c0 + MXU_COLS], preferred_element_type=F32)

    if time_major_state:
        rc = min(rows, WIN_CHUNK)
        for r0 in range(0, rows, rc):
            for g in range(N_LG):
                def zc(c):
                    return z_ref[r0:r0 + rc, MIX_W * c + LANES * g:MIX_W * c + LANES * (g + 1)]
                fa_ref[g, r0:r0 + rc, :] = zc(2) * zc(0)
                fb_ref[g, r0:r0 + rc, :] = zc(4) * _sigmoid(zc(5))

        def conv(w_ref, state_ref, u_ref, out_ref, new_state_ref, taps):
            hist = taps - 1

            def slab(j_in, g):
                if j_in < hist:
                    return state_ref[j_in, :, LANES * g:LANES * (g + 1)]
                return u_ref[g, pl.ds(j_in - hist, nb, stride=tt), :]

            for t in range(tt):
                for g in range(N_LG):
                    acc = None
                    for k in range(taps):
                        term = w_ref[k:k + 1, LANES * g:LANES * (g + 1)] * slab(t + k, g)
                        acc = term if acc is None else acc + term
                    out_ref[g, pl.ds(t, nb, stride=tt), :] = acc
            for j_out in range(hist):
                for g in range(N_LG):
                    new_state_ref[j_out, :, LANES * g:LANES * (g + 1)] = slab(j_out + tt, g)

        conv(cwa_ref, sa_ref, fa_ref, ca_ref, na_ref, CONV_A)
        conv(cwb_ref, sb_ref, fb_ref, cb_ref, nb_ref, CONV_B)

    for b, c0, r0, rc in ([] if time_major_state else _seq_chunks(0, rows, tt)):
        for g in range(N_LG):
            def zc(c):
                return z_ref[r0:r0 + rc, MIX_W * c + LANES * g:MIX_W * c + LANES * (g + 1)]
            fa_ref[b * N_LG + g, HIST_A + c0:HIST_A + c0 + rc, :] = zc(2) * zc(0)
            fb_ref[b * N_LG + g, HIST_B + c0:HIST_B + c0 + rc, :] = zc(4) * _sigmoid(zc(5))

    for b, c0, r0, rc in ([] if time_major_state else _seq_chunks(0, rows, tt)):
        for g in range(N_LG):
            lg = slice(LANES * g, LANES * (g + 1))
            acc = None
            for k in range(CONV_A):
                off = HIST_A - (CONV_A - 1) + k + c0
                t = cwa_ref[k:k + 1, lg] * fa_ref[b * N_LG + g, off:off + rc, :]
                acc = t if acc is None else acc + t
            ca_ref[g, r0:r0 + rc, :] = acc
            acc = None
            for k in range(CONV_B):
                off = HIST_B - (CONV_B - 1) + k + c0
                t = cwb_ref[k:k + 1, lg] * fb_ref[b * N_LG + g, off:off + rc, :]
                acc = t if acc is None else acc + t
            cb_ref[g, r0:r0 + rc, :] = acc

    for b in range(0 if time_major_state else nb):
        for g in range(N_LG):
            lg = slice(LANES * g, LANES * (g + 1))
            na_ref[b, :, lg] = fa_ref[b * N_LG + g, HIST_A + tt - (CONV_A - 1):HIST_A + tt, :]
            nb_ref[b, :, lg] = fb_ref[b * N_LG + g, HIST_B + tt - (CONV_B - 1):HIST_B + tt, :]
            if nt > 1:
                fa_ref[b * N_LG + g, 0:HIST_A, :] = fa_ref[b * N_LG + g, tt:tt + HIST_A, :]
                fb_ref[b * N_LG + g, 0:HIST_B, :] = fb_ref[b * N_LG + g, tt:tt + HIST_B, :]

    _even_gating(z_ref, ca_ref, cb_ref, cbb_ref, lng_ref, lnb_ref, mix_ref, rows)
    _out_proj_task(mix_ref, wout_ref, o_ref, 0, rows)()
    for task in _residual_tasks(x_ref, mod_ref, seq0, gpost_ref, o_ref, y_ref, 0, rows, tt):
        task()


def _even_layer(x2d, mod, st_a, st_b, g_pre, w_in, cw_a, cw_b, cb_b, ln_g, ln_b, w_out, g_post,
                *, n, t, nb, tt, time_major_state):
    rows = nb * tt
    nt = t // tt
    kern = functools.partial(_even_kernel, nb=nb, tt=tt, nt=nt, time_major_state=time_major_state)
    row_spec = pl.BlockSpec((rows, D_MODEL), lambda i, j: (i * nt + j, 0))

    def seq_spec(shape):
        if time_major_state:
            return pl.BlockSpec((shape[0], nb, shape[1]), lambda i, j: (0, i, 0))
        return pl.BlockSpec((nb,) + shape, lambda i, j: (i, 0, 0))

    def window_scratch(hist):
        if time_major_state:
            return pltpu.VMEM((N_LG, rows, LANES), F32)
        return pltpu.VMEM((nb * N_LG, hist + tt, LANES), F32)

    def const_spec(shape):
        return pl.BlockSpec(shape, lambda i, j: (0,) * len(shape))

    return pl.pallas_call(
        kern,
        out_shape=(
            jax.ShapeDtypeStruct((n * t, D_MODEL), F32),
            jax.ShapeDtypeStruct(st_a.shape, F32),
            jax.ShapeDtypeStruct(st_b.shape, F32),
        ),
        grid=(n // nb, nt),
        in_specs=[
            row_spec,
            const_spec(mod.shape),
            seq_spec((CONV_A - 1, MIX_W)),
            seq_spec((CONV_B - 1, MIX_W)),
            const_spec((1, D_MODEL)),
            const_spec((D_MODEL, EVEN_IN)),
            const_spec((CONV_A, MIX_W)),
            const_spec((CONV_B, MIX_W)),
            const_spec((1, MIX_W)),
            const_spec((1, MIX_W)),
            const_spec((1, MIX_W)),
            const_spec((2 * MIX_W, D_MODEL)),
            const_spec((1, D_MODEL)),
        ],
        out_specs=(row_spec, seq_spec((CONV_A - 1, MIX_W)), seq_spec((CONV_B - 1, MIX_W))),
        scratch_shapes=[
            pltpu.VMEM((rows, D_MODEL), BF16),
            pltpu.VMEM((rows, EVEN_IN + ROW_PAD), F32),
            pltpu.VMEM((rows, D_MODEL), F32),
            window_scratch(HIST_A),
            window_scratch(HIST_B),
            pltpu.VMEM((N_LG, rows, LANES), F32),
            pltpu.VMEM((N_LG, rows, LANES), F32),
            pltpu.VMEM((rows, 2 * MIX_W), BF16),
        ],
        compiler_params=pltpu.CompilerParams(
            dimension_semantics=("arbitrary", "arbitrary"), vmem_limit_bytes=VMEM_LIMIT_BYTES),
        name="even_layer",
    )(x2d, mod, st_a, st_b, g_pre, w_in, cw_a, cw_b, cb_b, ln_g, ln_b, w_out, g_post)


def _odd_kernel(sinks_ref, x_ref, mod_ref, sc_ref, sk_ref, sv_ref, gpre_ref, win_ref, pw_ref, ps_ref,
                wout_ref, gpost_ref,
                y_ref, nc_ref, nk_ref, nv_ref,
                h_ref, z_ref, o_ref, fc_ref, kf_ref, ks_ref, vf_ref, vs_ref, pc_ref, od_ref, mix_ref,
                *, nb, tt, nt, pos0, time_major_pool):
    rows = nb * tt
    j = pl.program_id(1)
    seq0 = pl.program_id(0) * nb
    tile_pos = pos0 + j * tt
    assert not time_major_pool or nt == 1

    @pl.when(j == 0)
    def _load_state():
        for b in range(nb):
            for g in range(0 if time_major_pool else N_LG):
                fc_ref[b * N_LG + g, 0:8, :] = jnp.zeros((8, LANES), F32)
                fc_ref[b * N_LG + g, HIST_C - POOL_PAST:HIST_C, :] = sc_ref[b, :, LANES * g:LANES * (g + 1)]
            k_past, v_past = sk_ref[b].T, sv_ref[b].T
            kf_ref[b, 0:WINDOW, :] = k_past
            vf_ref[b, 0:WINDOW, :] = v_past
            ks_ref[b, 0:WINDOW, :] = pltpu.roll(k_past, HEAD_DIM, axis=1)
            vs_ref[b, 0:WINDOW, :] = pltpu.roll(v_past, HEAD_DIM, axis=1)

    qs = min(tt, WINDOW)
    gs = max(1, ATT_MIN_Q_ROWS // qs)
    kw = WINDOW + qs
    n_q, n_keys = gs * qs, gs * kw
    assert N_REP == 4 and 2 * HEAD_DIM == LANES and KV_W == LANES
    assert qs & (qs - 1) == 0 and gs & (gs - 1) == 0 and nb % gs == 0 and (gs == 1 or tt == qs)
    shape = (2 * n_q, n_keys)
    ri = lax.broadcasted_iota(jnp.int32, shape, 0)
    ci = lax.broadcasted_iota(jnp.int32, shape, 1)
    q_seq = (ri >> (qs.bit_length() - 1)) & (gs - 1)
    k_seq = jnp.zeros(shape, jnp.int32)
    for s_i in range(1, gs):
        k_seq = k_seq + (ci >= s_i * kw).astype(jnp.int32)
    k_c = ci - k_seq * kw
    dist = (ri & (qs - 1)) + WINDOW - k_c
    band = (q_seq == k_seq) & (dist >= 0) & (dist < WINDOW)
    dist_f = dist.astype(F32)
    upper_grp = ri >= n_q
    upper_grp_col = lax.broadcasted_iota(jnp.int32, (2 * n_q, 1), 0) >= n_q
    low_lanes = lax.broadcasted_iota(jnp.int32, (n_keys, LANES), 1) < HEAD_DIM
    ones_cols = jnp.ones((n_keys, LANES), BF16)

    def head_of(g, grp, parity):
        return g * N_REP + 2 * grp + parity

    def alibi_bias(g, parity):
        slope = jnp.where(upper_grp, ALIBI_SLOPES[head_of(g, 1, parity)], ALIBI_SLOPES[head_of(g, 0, parity)])
        return jnp.where(band, -(slope * dist_f), -jnp.inf)

    biases = {(g, parity): alibi_bias(g, parity) for g in range(N_KV) for parity in range(2)}

    def attention(b0, sb, g):
        r0 = b0 * tt + sb * qs
        p0 = tile_pos + sb * qs
        in_range = (k_c >= WINDOW - p0) if pos0 + sb * qs < WINDOW else None
        if gs == 1:
            win = (b0, slice(sb * qs, sb * qs + kw))
            k_nat, k_swp, v_nat, v_swp = kf_ref[win], ks_ref[win], vf_ref[win], vs_ref[win]
        else:
            k_nat, k_swp, v_nat, v_swp = (r[b0:b0 + gs].reshape(n_keys, KV_W)
                                          for r in (kf_ref, ks_ref, vf_ref, vs_ref))
        k_lo, k_hi = (k_nat, k_swp) if g == 0 else (k_swp, k_nat)
        v_lo, v_hi = (v_nat, v_swp) if g == 0 else (v_swp, v_nat)
        v_ext = jnp.concatenate([jnp.where(low_lanes, v_lo, v_hi).astype(BF16), ones_cols], axis=1)
        qg = jnp.concatenate([z_ref[r0:r0 + n_q, Q_COL + 2 * LANES * g + LANES * grp:
                                    Q_COL + 2 * LANES * g + LANES * (grp + 1)] for grp in range(2)], axis=0)
        qg = (qg * (HEAD_DIM ** -0.5)).astype(BF16)
        for parity in range(2):
            k_ext = (jnp.where(low_lanes, k_lo, 0.0) if parity == 0 else jnp.where(low_lanes, 0.0, k_hi)).astype(BF16)
            sink = jnp.where(upper_grp_col, sinks_ref[head_of(g, 1, parity)], sinks_ref[head_of(g, 0, parity)])
            sc = lax.dot_general(qg, k_ext, (((1,), (1,)), ((), ())), preferred_element_type=F32)
            bias = biases[g, parity]
            sc = sc + (bias if in_range is None else jnp.where(in_range, bias, -jnp.inf))
            m = jnp.maximum(jnp.max(sc, axis=-1, keepdims=True), sink)
            p = jnp.exp(sc - m)
            o_ext = jnp.dot(p.astype(BF16), v_ext, preferred_element_type=F32)
            o = o_ext[:, 0:LANES] / (o_ext[:, LANES:2 * LANES] + jnp.exp(sink - m))
            for grp in range(2):
                h = head_of(g, grp, parity)
                od_ref[r0:r0 + n_q, HEAD_DIM * h:HEAD_DIM * (h + 1)] = (
                    o[grp * n_q:(grp + 1) * n_q, HEAD_DIM * parity:HEAD_DIM * (parity + 1)])

    def window_inputs(b, c0, r0, rc):
        for g in range(0 if time_major_pool else N_LG):
            fc_ref[b * N_LG + g, HIST_C + c0:HIST_C + c0 + rc, :] = z_ref[r0:r0 + rc, LANES * g:LANES * (g + 1)]
        k = z_ref[r0:r0 + rc, K_COL:K_COL + KV_W]
        v = z_ref[r0:r0 + rc, V_COL:V_COL + KV_W]
        kf_ref[b, WINDOW + c0:WINDOW + c0 + rc, :] = k
        vf_ref[b, WINDOW + c0:WINDOW + c0 + rc, :] = v
        ks_ref[b, WINDOW + c0:WINDOW + c0 + rc, :] = pltpu.roll(k, HEAD_DIM, axis=1)
        vs_ref[b, WINDOW + c0:WINDOW + c0 + rc, :] = pltpu.roll(v, HEAD_DIM, axis=1)

    def pooling(b, c0, r0, rc):
        pos = tile_pos + c0 + lax.broadcasted_iota(jnp.int32, (rc, LANES), 0)
        for g, w in enumerate(POOL_WINDOWS):
            u = fc_ref[b * N_LG + g, HIST_C + c0:HIST_C + c0 + rc, :]
            acc = u
            for d in range(1, w):
                acc = acc + fc_ref[b * N_LG + g, HIST_C + c0 - d:HIST_C + c0 - d + rc, :]
            cnt = jnp.minimum(w, pos + 1).astype(F32)
            pc_ref[g, r0:r0 + rc, :] = acc / cnt - u

    def pooling_time_major(lo, n):
        b_lo, n_seq = lo // tt, n // tt

        for g in range(N_LG):
            fc_ref[g, lo:lo + n, :] = z_ref[lo:lo + n, LANES * g:LANES * (g + 1)]

        def slab(j_in, g):
            if j_in < POOL_PAST:
                return sc_ref[j_in, b_lo:b_lo + n_seq, LANES * g:LANES * (g + 1)]
            return fc_ref[g, pl.ds(lo + j_in - POOL_PAST, n_seq, stride=tt), :]

        for t in range(tt):
            for g, w in enumerate(POOL_WINDOWS):
                u = slab(POOL_PAST + t, g)
                acc = u
                for d in range(1, w):
                    acc = acc + slab(POOL_PAST + t - d, g)
                pc_ref[g, pl.ds(lo + t, n_seq, stride=tt), :] = acc / float(min(w, pos0 + t + 1)) - u
        for j_out in range(POOL_PAST):
            for g in range(N_LG):
                nc_ref[j_out, b_lo:b_lo + n_seq, LANES * g:LANES * (g + 1)] = slab(j_out + tt, g)

    def gating(r0, rc):
        for g in range(N_LG):
            lg = slice(LANES * g, LANES * (g + 1))
            mixed = jnp.dot(pc_ref[g, r0:r0 + rc, :].astype(BF16), pw_ref[g], preferred_element_type=F32)
            yc = mixed * ps_ref[:, lg] * _silu(z_ref[r0:r0 + rc, MIX_W + LANES * g:MIX_W + LANES * (g + 1)])
            mix_ref[r0:r0 + rc, lg] = yc.astype(BF16)
        yd = od_ref[r0:r0 + rc, 0:MIX_W] * _silu(z_ref[r0:r0 + rc, DG_COL:DG_COL + MIX_W])
        mix_ref[r0:r0 + rc, MIX_W:2 * MIX_W] = yd.astype(BF16)

    def mixer_tasks(lo, n):
        chunks = _seq_chunks(lo, n, tt)
        if gs == 1:
            blocks = [(r0 // tt, (r0 % tt) // qs) for r0 in range(lo, lo + n, qs)]
        else:
            blocks = [(b0, 0) for b0 in range(lo // tt, (lo + n) // tt, gs)]
        pool_tasks = ([functools.partial(pooling_time_major, lo, n)] if time_major_pool
                      else [functools.partial(pooling, *ch) for ch in chunks])
        return ([functools.partial(window_inputs, *ch) for ch in chunks]
                + pool_tasks
                + [functools.partial(attention, b0, sb, g) for b0, sb in blocks for g in range(N_KV)]
                + [functools.partial(gating, lo, n)])

    _pre_norm(x_ref, mod_ref, seq0, gpre_ref, h_ref, rows, tt)
    for c0 in range(0, ODD_IN, MXU_COLS):
        z_ref[:, c0:c0 + MXU_COLS] = jnp.dot(h_ref[...], win_ref[:, c0:c0 + MXU_COLS], preferred_element_type=F32)

    n_chunks = max(1, rows // WIN_CHUNK)
    n = rows // n_chunks
    for c in range(n_chunks + 1):
        if c > 0:
            _out_proj_task(mix_ref, wout_ref, o_ref, (c - 1) * n, n)()
        if c < n_chunks:
            for task in mixer_tasks(c * n, n):
                task()
        if c > 0:
            for task in _residual_tasks(x_ref, mod_ref, seq0, gpost_ref, o_ref, y_ref, (c - 1) * n, n, tt):
                task()

    for b in range(nb):
        for g in range(0 if time_major_pool else N_LG):
            nc_ref[b, :, LANES * g:LANES * (g + 1)] = fc_ref[b * N_LG + g, HIST_C + tt - POOL_PAST:HIST_C + tt, :]
            if nt > 1:
                fc_ref[b * N_LG + g, 0:HIST_C, :] = fc_ref[b * N_LG + g, tt:tt + HIST_C, :]
        nk_ref[b] = kf_ref[b, tt:tt + WINDOW, :].T
        nv_ref[b] = vf_ref[b, tt:tt + WINDOW, :].T
        if nt > 1:
            for r in (kf_ref, ks_ref, vf_ref, vs_ref):
                r[b, 0:WINDOW, :] = r[b, tt:tt + WINDOW, :]


def _odd_layer(x2d, mod, st_c, st_k, st_v, g_pre, w_in, pool_w, pool_scale, sinks, w_out, g_post,
               *, n, t, nb, tt, pos0, time_major_pool):
    rows = nb * tt
    nt = t // tt
    kern = functools.partial(_odd_kernel, nb=nb, tt=tt, nt=nt, pos0=pos0, time_major_pool=time_major_pool)
    row_spec = pl.BlockSpec((rows, D_MODEL), lambda i, j: (i * nt + j, 0))

    def seq_spec(shape):
        return pl.BlockSpec((nb,) + shape, lambda i, j: (i, 0, 0))

    def const_spec(shape):
        return pl.BlockSpec(shape, lambda i, j: (0,) * len(shape))

    if time_major_pool:
        pool_spec = pl.BlockSpec((POOL_PAST, nb, MIX_W), lambda i, j: (0, i, 0))
    else:
        pool_spec = seq_spec((POOL_PAST, MIX_W))

    return pl.pallas_call(
        kern,
        out_shape=(
            jax.ShapeDtypeStruct((n * t, D_MODEL), F32),
            jax.ShapeDtypeStruct(st_c.shape, F32),
            jax.ShapeDtypeStruct((n, KV_W, WINDOW), F32),
            jax.ShapeDtypeStruct((n, KV_W, WINDOW), F32),
        ),
        grid=(n // nb, nt),
        in_specs=[
            pl.BlockSpec(memory_space=pltpu.SMEM),
            row_spec,
            const_spec(mod.shape),
            pool_spec,
            seq_spec((KV_W, WINDOW)),
            seq_spec((KV_W, WINDOW)),
            const_spec((1, D_MODEL)),
            const_spec((D_MODEL, ODD_IN)),
            const_spec((N_LG, LANES, LANES)),
            const_spec((1, MIX_W)),
            const_spec((2 * MIX_W, D_MODEL)),
            const_spec((1, D_MODEL)),
        ],
        out_specs=(row_spec, pool_spec, seq_spec((KV_W, WINDOW)), seq_spec((KV_W, WINDOW))),
        scratch_shapes=[
            pltpu.VMEM((rows, D_MODEL), BF16),
            pltpu.VMEM((rows, ODD_IN + ROW_PAD), F32),
            pltpu.VMEM((rows, D_MODEL), F32),
            pltpu.VMEM((N_LG, rows, LANES) if time_major_pool else (nb * N_LG, HIST_C + tt, LANES), F32),
            pltpu.VMEM((nb, WINDOW + tt, KV_W), F32),
            pltpu.VMEM((nb, WINDOW + tt, KV_W), F32),
            pltpu.VMEM((nb, WINDOW + tt, KV_W), F32),
            pltpu.VMEM((nb, WINDOW + tt, KV_W), F32),
            pltpu.VMEM((N_LG, rows, LANES), F32),
            pltpu.VMEM((rows, MIX_W + ROW_PAD), F32),
            pltpu.VMEM((rows, 2 * MIX_W), BF16),
        ],
        compiler_params=pltpu.CompilerParams(
            dimension_semantics=("arbitrary", "arbitrary"), vmem_limit_bytes=VMEM_LIMIT_BYTES),
        name="odd_layer",
    )(sinks, x2d, mod, st_c, st_k, st_v, g_pre, w_in, pool_w, pool_scale, w_out, g_post)


def _tiling(n, t):
    if t >= TILE_ROWS:
        return 1, TILE_ROWS
    return min(n, SHORT_TILE_ROWS // t), t


def _trunk(x, mod_e, mod_o, st_a, st_b, st_c, st_k, st_v, pos0, we, wo):
    n, t, _ = x.shape
    nb, tt = _tiling(n, t)
    short = t == tt and nb > 1

    def swap(st):
        return jnp.transpose(st, (1, 0, 2)) if short else st

    x2d, na, nbs = _even_layer(x.reshape(n * t, D_MODEL), mod_e, swap(st_a), swap(st_b), *we, n=n, t=t, nb=nb,
                               tt=tt, time_major_state=short)
    def channel_major(cache):
        return jnp.transpose(cache, (0, 2, 3, 1)).reshape(n, KV_W, WINDOW)

    def position_major(cache):
        return jnp.transpose(cache.reshape(n, N_KV, HEAD_DIM, WINDOW), (0, 3, 1, 2))

    x2d, nc, nk, nv = _odd_layer(x2d, mod_o, swap(st_c), channel_major(st_k), channel_major(st_v), *wo,
                                 n=n, t=t, nb=nb, tt=tt, pos0=pos0, time_major_pool=short)
    na, nbs, nc = swap(na), swap(nbs), swap(nc)
    return (x2d.reshape(n, t, D_MODEL), na[None], nbs[None], nc[None],
            position_major(nk)[None], position_major(nv)[None])


def kernel(x_prompt, x_sample, state_conv_a, state_conv_b, state_pool_c, cache_win_k, cache_win_v, c_prompt, c_sample, w_mod_e, b_mod_e, g_pre_e, g_post_e, w_in_e, conv_a_w, conv_b_w, conv_b_b, ln_b_g, ln_b_b, w_out_e, w_mod_o, b_mod_o, g_pre_o, g_post_o, w_in_o, pool_w, pool_scale, sinks, w_out_o):
    n_p = x_prompt.shape[0]
    assert w_in_e.shape[0] == 1 and w_in_o.shape[0] == 1, "one even and one odd layer"

    mod_e_p, mod_e_s = _adaln_mod(c_prompt, c_sample, w_mod_e[0], b_mod_e[0])
    mod_o_p, mod_o_s = _adaln_mod(c_prompt, c_sample, w_mod_o[0], b_mod_o[0])

    we = (g_pre_e, w_in_e[0].astype(BF16), conv_a_w[0], conv_b_w[0], conv_b_b, ln_b_g, ln_b_b,
          w_out_e[0].astype(BF16), g_post_e)
    wo = (g_pre_o, w_in_o[0].astype(BF16), pool_w[0].astype(BF16), pool_scale, sinks[0],
          w_out_o[0].astype(BF16), g_post_o)

    dt = x_prompt.dtype
    z_a = jnp.zeros((n_p, CONV_A - 1, MIX_W), dt)
    z_b = jnp.zeros((n_p, CONV_B - 1, MIX_W), dt)
    z_c = jnp.zeros((n_p, POOL_PAST, MIX_W), dt)
    z_kv = jnp.zeros((n_p, WINDOW, N_KV, HEAD_DIM), dt)
    y_p, pa, pb, pc, pk, pv = _trunk(x_prompt, mod_e_p, mod_o_p, z_a, z_b, z_c, z_kv, z_kv, 0, we, wo)
    y_s, sa, sb, sc, sk, sv = _trunk(x_sample, mod_e_s, mod_o_s, state_conv_a[0], state_conv_b[0],
                                     state_pool_c[0], cache_win_k[0], cache_win_v[0], PAST_LEN, we, wo)
    return (y_p, y_s, pa, sa, pb, sb, pc, sc, pk, sk, pv, sv)
```

```python
import functools

import jax
import jax.numpy as jnp
from jax import lax
from jax.experimental import pallas as pl
from jax.experimental.pallas import tpu as pltpu

F32 = jnp.float32
BF16 = jnp.bfloat16

D_MODEL = 1024
MIX_W = 512
LANES = 128
MXU_COLS = 256
N_LG = MIX_W // LANES
HEAD_DIM = 64
N_HEADS = 8
N_KV = 2
N_REP = N_HEADS // N_KV
KV_W = N_KV * HEAD_DIM
CONV_A = 3
CONV_B = 31
POOL_WINDOWS = (2, 4, 8, 16)
POOL_PAST = 15
WINDOW = 128
PAST_LEN = 8192
EVEN_IN = 7 * MIX_W
ODD_IN = 4 * MIX_W + 2 * KV_W
Q_COL, K_COL, V_COL, DG_COL = 2 * MIX_W, 3 * MIX_W, 3 * MIX_W + KV_W, 3 * MIX_W + 2 * KV_W
RMS_EPS = 1e-6
LN_EPS = 1e-5
HIST_A = 8
HIST_B = 32
HIST_C = 16
ATT_MIN_Q_ROWS = 32
TILE_ROWS = 1024
SHORT_TILE_ROWS = 256
ROW_CHUNK = 64
WIN_CHUNK = 128
MOD_K_BLOCK = 256
ROW_PAD = LANES
VMEM_LIMIT_BYTES = 56 * 1024 * 1024
NEG_LOG2_E = -1.4426950408889634
ALIBI_SLOPES = tuple(float(2.0 ** (-8.0 * (h + 1) / N_HEADS)) for h in range(N_HEADS))


def _sigmoid(x):
    return 1.0 / (1.0 + jnp.exp2(x * NEG_LOG2_E))


def _silu(x):
    return x * _sigmoid(x)


def _mod_kernel(cp_ref, cs_ref, w_ref, b_ref, op_ref, os_ref):
    k = pl.program_id(0)
    n_s = cs_ref.shape[0]
    c = jnp.concatenate([cs_ref[...], cp_ref[...]], axis=0)
    part = jnp.dot(_silu(c).astype(BF16), w_ref[...].astype(BF16), preferred_element_type=F32)
    for o_ref, rows in ((os_ref, slice(0, n_s)), (op_ref, slice(n_s, c.shape[0]))):
        @pl.when(k == 0)
        def _first():
            o_ref[...] = part[rows] + b_ref[...]

        @pl.when(k > 0)
        def _rest():
            o_ref[...] += part[rows]


def _adaln_mod(c_prompt, c_sample, w_mod, b_mod):
    def c_spec(c):
        return pl.BlockSpec((c.shape[0], MOD_K_BLOCK), lambda k: (0, k))

    def out_spec(c):
        return pl.BlockSpec((c.shape[0], 3 * D_MODEL), lambda k: (0, 0))

    return pl.pallas_call(
        _mod_kernel,
        out_shape=tuple(jax.ShapeDtypeStruct((c.shape[0], 3 * D_MODEL), F32) for c in (c_prompt, c_sample)),
        grid=(D_MODEL // MOD_K_BLOCK,),
        in_specs=[
            c_spec(c_prompt),
            c_spec(c_sample),
            pl.BlockSpec((MOD_K_BLOCK, 3 * D_MODEL), lambda k: (k, 0)),
            pl.BlockSpec((1, 3 * D_MODEL), lambda k: (0, 0)),
        ],
        out_specs=(out_spec(c_prompt), out_spec(c_sample)),
        compiler_params=pltpu.CompilerParams(dimension_semantics=("arbitrary",)),
        name="adaln_mod",
    )(c_prompt, c_sample, w_mod, b_mod.reshape(1, -1))


def _mod_rows(mod_ref, seq0, col, r0, rc, tt):
    if tt >= rc:
        return mod_ref[pl.ds(seq0 + r0 // tt, 1), col:col + D_MODEL]
    parts = [jnp.broadcast_to(mod_ref[pl.ds(seq0 + b, 1), col:col + D_MODEL], (tt, D_MODEL))
             for b in range(r0 // tt, (r0 + rc) // tt)]
    return jnp.concatenate(parts, axis=0)


def _pre_norm(x_ref, mod_ref, seq0, g_ref, h_ref, rows, tt):
    rc = min(rows, ROW_CHUNK)
    for r0 in range(0, rows, rc):
        x = x_ref[r0:r0 + rc, :]
        ms = jnp.mean(x * x, axis=-1, keepdims=True)
        gain = g_ref[...] * (1.0 + _mod_rows(mod_ref, seq0, D_MODEL, r0, rc, tt))
        h = x * lax.rsqrt(ms + RMS_EPS) * gain + _mod_rows(mod_ref, seq0, 0, r0, rc, tt)
        h_ref[r0:r0 + rc, :] = h.astype(BF16)


def _out_proj_task(mix_ref, wout_ref, o_ref, lo, n):
    def task():
        o_ref[lo:lo + n, :] = jnp.dot(mix_ref[lo:lo + n, :], wout_ref[...], preferred_element_type=F32)
    return task


def _residual_tasks(x_ref, mod_ref, seq0, g_ref, o_ref, y_ref, lo, n, tt):
    rc = min(n, ROW_CHUNK)

    def chunk(r0):
        o = o_ref[r0:r0 + rc, :]
        ms = jnp.mean(o * o, axis=-1, keepdims=True)
        gain = g_ref[...] * _mod_rows(mod_ref, seq0, 2 * D_MODEL, r0, rc, tt)
        y_ref[r0:r0 + rc, :] = x_ref[r0:r0 + rc, :] + o * lax.rsqrt(ms + RMS_EPS) * gain
    return [functools.partial(chunk, r0) for r0 in range(lo, lo + n, rc)]


def _seq_chunks(lo, n, tt):
    rc = min(tt, n, WIN_CHUNK)
    return [(r0 // tt, r0 % tt, r0, rc) for r0 in range(lo, lo + n, rc)]


def _even_gating(z_ref, ca_ref, cb_ref, cbb_ref, lng_ref, lnb_ref, mix_ref, rows):
    rc = min(rows, ROW_CHUNK)
    for r0 in range(0, rows, rc):
        def zs(c):
            return z_ref[r0:r0 + rc, MIX_W * c:MIX_W * (c + 1)]

        def all_lanes(c_ref):
            return jnp.concatenate([c_ref[g, r0:r0 + rc, :] for g in range(N_LG)], axis=1)
        ya = zs(1) * all_lanes(ca_ref) * _silu(zs(3))
        mix_ref[r0:r0 + rc, 0:MIX_W] = ya.astype(BF16)
        yb = all_lanes(cb_ref) + cbb_ref[...]
        mu = jnp.mean(yb, axis=-1, keepdims=True)
        yc = yb - mu
        var = jnp.mean(yc * yc, axis=-1, keepdims=True)
        ln = yc * lax.rsqrt(var + LN_EPS) * lng_ref[...] + lnb_ref[...]
        mix_ref[r0:r0 + rc, MIX_W:2 * MIX_W] = (_silu(ln) * _silu(zs(6))).astype(BF16)


def _even_kernel(x_ref, mod_ref, sa_ref, sb_ref, gpre_ref, win_ref, cwa_ref, cwb_ref, cbb_ref,
                 lng_ref, lnb_ref, wout_ref, gpost_ref,
                 y_ref, na_ref, nb_ref,
                 h_ref, z_ref, o_ref, fa_ref, fb_ref, ca_ref, cb_ref, mix_ref,
                 *, nb, tt, nt, time_major_state):
    rows = nb * tt
    j = pl.program_id(1)
    seq0 = pl.program_id(0) * nb
    assert not time_major_state or nt == 1

    @pl.when(j == 0)
    def _load_state():
        for b in range(0 if time_major_state else nb):
            for g in range(N_LG):
                lg = slice(LANES * g, LANES * (g + 1))
                fa_ref[b * N_LG + g, 0:HIST_A, :] = jnp.zeros((HIST_A, LANES), F32)
                fa_ref[b * N_LG + g, HIST_A - (CONV_A - 1):HIST_A, :] = sa_ref[b, :, lg]
                fb_ref[b * N_LG + g, 0:8, :] = jnp.zeros((8, LANES), F32)
                fb_ref[b * N_LG + g, HIST_B - (CONV_B - 1):HIST_B, :] = sb_ref[b, :, lg]

    _pre_norm(x_ref, mod_ref, seq0, gpre_ref, h_ref, rows, tt)

    split_order = (4, 5, 0, 2, 1, 3, 6)
    for c0 in [MIX_W * s + d for s in split_order for d in range(0, MIX_W, MXU_COLS)]:
        z_ref[:, c0:c0 + MXU_COLS] = jnp.dot(h_ref[...], win_ref[:, c0:c0 + MXU_COLS].astype(BF16),
                                             preferred_element_type=F32)

    if time_major_state:
        rc = min(rows, WIN_CHUNK)
        for r0 in range(0, rows, rc):
            for g in range(N_LG):
                def zc(c):
                    return z_ref[r0:r0 + rc, MIX_W * c + LANES * g:MIX_W * c + LANES * (g + 1)]
                fa_ref[g, r0:r0 + rc, :] = zc(2) * zc(0)
                fb_ref[g, r0:r0 + rc, :] = zc(4) * _sigmoid(zc(5))

        def conv(w_ref, state_ref, u_ref, out_ref, new_state_ref, taps):
            hist = taps - 1

            def slab(j_in, g):
                if j_in < hist:
                    return state_ref[j_in, :, LANES * g:LANES * (g + 1)]
                return u_ref[g, pl.ds(j_in - hist, nb, stride=tt), :]

            for t in range(tt):
                for g in range(N_LG):
                    acc = None
                    for k in range(taps):
                        term = w_ref[k:k + 1, LANES * g:LANES * (g + 1)] * slab(t + k, g)
                        acc = term if acc is None else acc + term
                    out_ref[g, pl.ds(t, nb, stride=tt), :] = acc
            for j_out in range(hist):
                for g in range(N_LG):
                    new_state_ref[j_out, :, LANES * g:LANES * (g + 1)] = slab(j_out + tt, g)

        conv(cwa_ref, sa_ref, fa_ref, ca_ref, na_ref, CONV_A)
        conv(cwb_ref, sb_ref, fb_ref, cb_ref, nb_ref, CONV_B)

    for b, c0, r0, rc in ([] if time_major_state else _seq_chunks(0, rows, tt)):
        for g in range(N_LG):
            def zc(c):
                return z_ref[r0:r0 + rc, MIX_W * c + LANES * g:MIX_W * c + LANES * (g + 1)]
            fa_ref[b * N_LG + g, HIST_A + c0:HIST_A + c0 + rc, :] = zc(2) * zc(0)
            fb_ref[b * N_LG + g, HIST_B + c0:HIST_B + c0 + rc, :] = zc(4) * _sigmoid(zc(5))

    for b, c0, r0, rc in ([] if time_major_state else _seq_chunks(0, rows, tt)):
        for g in range(N_LG):
            lg = slice(LANES * g, LANES * (g + 1))
            acc = None
            for k in range(CONV_A):
                off = HIST_A - (CONV_A - 1) + k + c0
                t = cwa_ref[k:k + 1, lg] * fa_ref[b * N_LG + g, off:off + rc, :]
                acc = t if acc is None else acc + t
            ca_ref[g, r0:r0 + rc, :] = acc
            acc = None
            for k in range(CONV_B):
                off = HIST_B - (CONV_B - 1) + k + c0
                t = cwb_ref[k:k + 1, lg] * fb_ref[b * N_LG + g, off:off + rc, :]
                acc = t if acc is None else acc + t
            cb_ref[g, r0:r0 + rc, :] = acc

    for b in range(0 if time_major_state else nb):
        for g in range(N_LG):
            lg = slice(LANES * g, LANES * (g + 1))
            na_ref[b, :, lg] = fa_ref[b * N_LG + g, HIST_A + tt - (CONV_A - 1):HIST_A + tt, :]
            nb_ref[b, :, lg] = fb_ref[b * N_LG + g, HIST_B + tt - (CONV_B - 1):HIST_B + tt, :]
            if nt > 1:
                fa_ref[b * N_LG + g, 0:HIST_A, :] = fa_ref[b * N_LG + g, tt:tt + HIST_A, :]
                fb_ref[b * N_LG + g, 0:HIST_B, :] = fb_ref[b * N_LG + g, tt:tt + HIST_B, :]

    _even_gating(z_ref, ca_ref, cb_ref, cbb_ref, lng_ref, lnb_ref, mix_ref, rows)
    _out_proj_task(mix_ref, wout_ref, o_ref, 0, rows)()
    for task in _residual_tasks(x_ref, mod_ref, seq0, gpost_ref, o_ref, y_ref, 0, rows, tt):
        task()


def _even_layer(x2d, mod, st_a, st_b, g_pre, w_in, cw_a, cw_b, cb_b, ln_g, ln_b, w_out, g_post,
                *, n, t, nb, tt, time_major_state):
    rows = nb * tt
    nt = t // tt
    kern = functools.partial(_even_kernel, nb=nb, tt=tt, nt=nt, time_major_state=time_major_state)
    row_spec = pl.BlockSpec((rows, D_MODEL), lambda i, j: (i * nt + j, 0))

    def seq_spec(shape):
        if time_major_state:
            return pl.BlockSpec((shape[0], nb, shape[1]), lambda i, j: (0, i, 0))
        return pl.BlockSpec((nb,) + shape, lambda i, j: (i, 0, 0))

    def window_scratch(hist):
        if time_major_state:
            return pltpu.VMEM((N_LG, rows, LANES), F32)
        return pltpu.VMEM((nb * N_LG, hist + tt, LANES), F32)

    def const_spec(shape):
        return pl.BlockSpec(shape, lambda i, j: (0,) * len(shape))

    return pl.pallas_call(
        kern,
        out_shape=(
            jax.ShapeDtypeStruct((n * t, D_MODEL), F32),
            jax.ShapeDtypeStruct(st_a.shape, F32),
            jax.ShapeDtypeStruct(st_b.shape, F32),
        ),
        grid=(n // nb, nt),
        in_specs=[
            row_spec,
            const_spec(mod.shape),
            seq_spec((CONV_A - 1, MIX_W)),
            seq_spec((CONV_B - 1, MIX_W)),
            const_spec((1, D_MODEL)),
            const_spec((D_MODEL, EVEN_IN)),
            const_spec((CONV_A, MIX_W)),
            const_spec((CONV_B, MIX_W)),
            const_spec((1, MIX_W)),
            const_spec((1, MIX_W)),
            const_spec((1, MIX_W)),
            const_spec((2 * MIX_W, D_MODEL)),
            const_spec((1, D_MODEL)),
        ],
        out_specs=(row_spec, seq_spec((CONV_A - 1, MIX_W)), seq_spec((CONV_B - 1, MIX_W))),
        scratch_shapes=[
            pltpu.VMEM((rows, D_MODEL), BF16),
            pltpu.VMEM((rows, EVEN_IN + ROW_PAD), F32),
            pltpu.VMEM((rows, D_MODEL), F32),
            window_scratch(HIST_A),
            window_scratch(HIST_B),
            pltpu.VMEM((N_LG, rows, LANES), F32),
            pltpu.VMEM((N_LG, rows, LANES), F32),
            pltpu.VMEM((rows, 2 * MIX_W), BF16),
        ],
        compiler_params=pltpu.CompilerParams(
            dimension_semantics=("arbitrary", "arbitrary"), vmem_limit_bytes=VMEM_LIMIT_BYTES),
        name="even_layer",
    )(x2d, mod, st_a, st_b, g_pre, w_in, cw_a, cw_b, cb_b, ln_g, ln_b, w_out, g_post)


def _odd_kernel(sinks_ref, x_ref, mod_ref, sc_ref, sk_ref, sv_ref, gpre_ref, win_ref, pw_ref, ps_ref,
                wout_ref, gpost_ref,
                y_ref, nc_ref, nk_ref, nv_ref,
                h_ref, z_ref, o_ref, fc_ref, kf_ref, ks_ref, vf_ref, vs_ref, pc_ref, od_ref, mix_ref,
                *, nb, tt, nt, pos0, time_major_pool):
    rows = nb * tt
    j = pl.program_id(1)
    seq0 = pl.program_id(0) * nb
    tile_pos = pos0 + j * tt
    assert not time_major_pool or nt == 1

    @pl.when(j == 0)
    def _load_state():
        for b in range(nb):
            for g in range(0 if time_major_pool else N_LG):
                fc_ref[b * N_LG + g, 0:8, :] = jnp.zeros((8, LANES), F32)
                fc_ref[b * N_LG + g, HIST_C - POOL_PAST:HIST_C, :] = sc_ref[b, :, LANES * g:LANES * (g + 1)]
            k_past, v_past = sk_ref[b].T, sv_ref[b].T
            kf_ref[b, 0:WINDOW, :] = k_past
            vf_ref[b, 0:WINDOW, :] = v_past
            ks_ref[b, 0:WINDOW, :] = pltpu.roll(k_past, HEAD_DIM, axis=1)
            vs_ref[b, 0:WINDOW, :] = pltpu.roll(v_past, HEAD_DIM, axis=1)

    qs = min(tt, WINDOW)
    gs = max(1, ATT_MIN_Q_ROWS // qs)
    kw = WINDOW + qs
    n_q, n_keys = gs * qs, gs * kw
    assert N_REP == 4 and 2 * HEAD_DIM == LANES and KV_W == LANES
    assert qs & (qs - 1) == 0 and gs & (gs - 1) == 0 and nb % gs == 0 and (gs == 1 or tt == qs)
    shape = (2 * n_q, n_keys)
    ri = lax.broadcasted_iota(jnp.int32, shape, 0)
    ci = lax.broadcasted_iota(jnp.int32, shape, 1)
    q_seq = (ri >> (qs.bit_length() - 1)) & (gs - 1)
    k_seq = jnp.zeros(shape, jnp.int32)
    for s_i in range(1, gs):
        k_seq = k_seq + (ci >= s_i * kw).astype(jnp.int32)
    k_c = ci - k_seq * kw
    dist = (ri & (qs - 1)) + WINDOW - k_c
    band = (q_seq == k_seq) & (dist >= 0) & (dist < WINDOW)
    dist_f = dist.astype(F32)
    upper_grp = ri >= n_q
    upper_grp_col = lax.broadcasted_iota(jnp.int32, (2 * n_q, 1), 0) >= n_q
    low_lanes = lax.broadcasted_iota(jnp.int32, (n_keys, LANES), 1) < HEAD_DIM
    ones_cols = jnp.ones((n_keys, LANES), BF16)

    def head_of(g, grp, parity):
        return g * N_REP + 2 * grp + parity

    def alibi_bias(g, parity):
        slope = jnp.where(upper_grp, ALIBI_SLOPES[head_of(g, 1, parity)], ALIBI_SLOPES[head_of(g, 0, parity)])
        return jnp.where(band, -(slope * dist_f), -jnp.inf)

    biases = {(g, parity): alibi_bias(g, parity) for g in range(N_KV) for parity in range(2)}

    def attention(b0, sb, g):
        r0 = b0 * tt + sb * qs
        p0 = tile_pos + sb * qs
        in_range = (k_c >= WINDOW - p0) if pos0 + sb * qs < WINDOW else None
        if gs == 1:
            win = (b0, slice(sb * qs, sb * qs + kw))
            k_nat, k_swp, v_nat, v_swp = kf_ref[win], ks_ref[win], vf_ref[win], vs_ref[win]
        else:
            k_nat, k_swp, v_nat, v_swp = (r[b0:b0 + gs].reshape(n_keys, KV_W)
                                          for r in (kf_ref, ks_ref, vf_ref, vs_ref))
        k_lo, k_hi = (k_nat, k_swp) if g == 0 else (k_swp, k_nat)
        v_lo, v_hi = (v_nat, v_swp) if g == 0 else (v_swp, v_nat)
        v_ext = jnp.concatenate([jnp.where(low_lanes, v_lo, v_hi).astype(BF16), ones_cols], axis=1)
        qg = jnp.concatenate([z_ref[r0:r0 + n_q, Q_COL + 2 * LANES * g + LANES * grp:
                                    Q_COL + 2 * LANES * g + LANES * (grp + 1)] for grp in range(2)], axis=0)
        qg = (qg * (HEAD_DIM ** -0.5)).astype(BF16)
        for parity in range(2):
            k_ext = (jnp.where(low_lanes, k_lo, 0.0) if parity == 0 else jnp.where(low_lanes, 0.0, k_hi)).astype(BF16)
            sink = jnp.where(upper_grp_col, sinks_ref[head_of(g, 1, parity)], sinks_ref[head_of(g, 0, parity)])
            sc = lax.dot_general(qg, k_ext, (((1,), (1,)), ((), ())), preferred_element_type=F32)
            bias = biases[g, parity]
            sc = sc + (bias if in_range is None else jnp.where(in_range, bias, -jnp.inf))
            m = jnp.maximum(jnp.max(sc, axis=-1, keepdims=True), sink)
            p = jnp.exp(sc - m)
            o_ext = jnp.dot(p.astype(BF16), v_ext, preferred_element_type=F32)
            o = o_ext[:, 0:LANES] / (o_ext[:, LANES:2 * LANES] + jnp.exp(sink - m))
            for grp in range(2):
                h = head_of(g, grp, parity)
                od_ref[r0:r0 + n_q, HEAD_DIM * h:HEAD_DIM * (h + 1)] = (
                    o[grp * n_q:(grp + 1) * n_q, HEAD_DIM * parity:HEAD_DIM * (parity + 1)])

    def window_inputs(b, c0, r0, rc):
        for g in range(0 if time_major_pool else N_LG):
            fc_ref[b * N_LG + g, HIST_C + c0:HIST_C + c0 + rc, :] = z_ref[r0:r0 + rc, LANES * g:LANES * (g + 1)]
        k = z_ref[r0:r0 + rc, K_COL:K_COL + KV_W]
        v = z_ref[r0:r0 + rc, V_COL:V_COL + KV_W]
        kf_ref[b, WINDOW + c0:WINDOW + c0 + rc, :] = k
        vf_ref[b, WINDOW + c0:WINDOW + c0 + rc, :] = v
        ks_ref[b, WINDOW + c0:WINDOW + c0 + rc, :] = pltpu.roll(k, HEAD_DIM, axis=1)
        vs_ref[b, WINDOW + c0:WINDOW + c0 + rc, :] = pltpu.roll(v, HEAD_DIM, axis=1)

    def pooling(b, c0, r0, rc):
        pos = tile_pos + c0 + lax.broadcasted_iota(jnp.int32, (rc, LANES), 0)
        for g, w in enumerate(POOL_WINDOWS):
            u = fc_ref[b * N_LG + g, HIST_C + c0:HIST_C + c0 + rc, :]
            acc = u
            for d in range(1, w):
                acc = acc + fc_ref[b * N_LG + g, HIST_C + c0 - d:HIST_C + c0 - d + rc, :]
            cnt = jnp.minimum(w, pos + 1).astype(F32)
            pc_ref[g, r0:r0 + rc, :] = acc / cnt - u

    def pooling_time_major(lo, n):
        b_lo, n_seq = lo // tt, n // tt

        for g in range(N_LG):
            fc_ref[g, lo:lo + n, :] = z_ref[lo:lo + n, LANES * g:LANES * (g + 1)]

        def slab(j_in, g):
            if j_in < POOL_PAST:
                return sc_ref[j_in, b_lo:b_lo + n_seq, LANES * g:LANES * (g + 1)]
            return fc_ref[g, pl.ds(lo + j_in - POOL_PAST, n_seq, stride=tt), :]

        for t in range(tt):
            for g, w in enumerate(POOL_WINDOWS):
                u = slab(POOL_PAST + t, g)
                acc = u
                for d in range(1, w):
                    acc = acc + slab(POOL_PAST + t - d, g)
                pc_ref[g, pl.ds(lo + t, n_seq, stride=tt), :] = acc / float(min(w, pos0 + t + 1)) - u
        for j_out in range(POOL_PAST):
            for g in range(N_LG):
                nc_ref[j_out, b_lo:b_lo + n_seq, LANES * g:LANES * (g + 1)] = slab(j_out + tt, g)

    def gating(r0, rc):
        for g in range(N_LG):
            lg = slice(LANES * g, LANES * (g + 1))
            mixed = jnp.dot(pc_ref[g, r0:r0 + rc, :].astype(BF16), pw_ref[g], preferred_element_type=F32)
            yc = mixed * ps_ref[:, lg] * _silu(z_ref[r0:r0 + rc, MIX_W + LANES * g:MIX_W + LANES * (g + 1)])
            mix_ref[r0:r0 + rc, lg] = yc.astype(BF16)
        yd = od_ref[r0:r0 + rc, 0:MIX_W] * _silu(z_ref[r0:r0 + rc, DG_COL:DG_COL + MIX_W])
        mix_ref[r0:r0 + rc, MIX_W:2 * MIX_W] = yd.astype(BF16)

    def mixer_tasks(lo, n):
        chunks = _seq_chunks(lo, n, tt)
        if gs == 1:
            blocks = [(r0 // tt, (r0 % tt) // qs) for r0 in range(lo, lo + n, qs)]
        else:
            blocks = [(b0, 0) for b0 in range(lo // tt, (lo + n) // tt, gs)]
        pool_tasks = ([functools.partial(pooling_time_major, lo, n)] if time_major_pool
                      else [functools.partial(pooling, *ch) for ch in chunks])
        return ([functools.partial(window_inputs, *ch) for ch in chunks]
                + pool_tasks
                + [functools.partial(attention, b0, sb, g) for b0, sb in blocks for g in range(N_KV)]
                + [functools.partial(gating, lo, n)])

    _pre_norm(x_ref, mod_ref, seq0, gpre_ref, h_ref, rows, tt)
    for c0 in range(0, ODD_IN, MXU_COLS):
        z_ref[:, c0:c0 + MXU_COLS] = jnp.dot(h_ref[...], win_ref[:, c0:c0 + MXU_COLS].astype(BF16),
                                             preferred_element_type=F32)

    n_chunks = max(1, rows // WIN_CHUNK)
    n = rows // n_chunks
    for c in range(n_chunks + 1):
        if c > 0:
            _out_proj_task(mix_ref, wout_ref, o_ref, (c - 1) * n, n)()
        if c < n_chunks:
            for task in mixer_tasks(c * n, n):
                task()
        if c > 0:
            for task in _residual_tasks(x_ref, mod_ref, seq0, gpost_ref, o_ref, y_ref, (c - 1) * n, n, tt):
                task()

    for b in range(nb):
        for g in range(0 if time_major_pool else N_LG):
            nc_ref[b, :, LANES * g:LANES * (g + 1)] = fc_ref[b * N_LG + g, HIST_C + tt - POOL_PAST:HIST_C + tt, :]
            if nt > 1:
                fc_ref[b * N_LG + g, 0:HIST_C, :] = fc_ref[b * N_LG + g, tt:tt + HIST_C, :]
        nk_ref[b] = kf_ref[b, tt:tt + WINDOW, :].T
        nv_ref[b] = vf_ref[b, tt:tt + WINDOW, :].T
        if nt > 1:
            for r in (kf_ref, ks_ref, vf_ref, vs_ref):
                r[b, 0:WINDOW, :] = r[b, tt:tt + WINDOW, :]


def _odd_layer(x2d, mod, st_c, st_k, st_v, g_pre, w_in, pool_w, pool_scale, sinks, w_out, g_post,
               *, n, t, nb, tt, pos0, time_major_pool):
    rows = nb * tt
    nt = t // tt
    kern = functools.partial(_odd_kernel, nb=nb, tt=tt, nt=nt, pos0=pos0, time_major_pool=time_major_pool)
    row_spec = pl.BlockSpec((rows, D_MODEL), lambda i, j: (i * nt + j, 0))

    def seq_spec(shape):
        return pl.BlockSpec((nb,) + shape, lambda i, j: (i, 0, 0))

    def const_spec(shape):
        return pl.BlockSpec(shape, lambda i, j: (0,) * len(shape))

    if time_major_pool:
        pool_spec = pl.BlockSpec((POOL_PAST, nb, MIX_W), lambda i, j: (0, i, 0))
    else:
        pool_spec = seq_spec((POOL_PAST, MIX_W))

    return pl.pallas_call(
        kern,
        out_shape=(
            jax.ShapeDtypeStruct((n * t, D_MODEL), F32),
            jax.ShapeDtypeStruct(st_c.shape, F32),
            jax.ShapeDtypeStruct((n, KV_W, WINDOW), F32),
            jax.ShapeDtypeStruct((n, KV_W, WINDOW), F32),
        ),
        grid=(n // nb, nt),
        in_specs=[
            pl.BlockSpec(memory_space=pltpu.SMEM),
            row_spec,
            const_spec(mod.shape),
            pool_spec,
            seq_spec((KV_W, WINDOW)),
            seq_spec((KV_W, WINDOW)),
            const_spec((1, D_MODEL)),
            const_spec((D_MODEL, ODD_IN)),
            const_spec((N_LG, LANES, LANES)),
            const_spec((1, MIX_W)),
            const_spec((2 * MIX_W, D_MODEL)),
            const_spec((1, D_MODEL)),
        ],
        out_specs=(row_spec, pool_spec, seq_spec((KV_W, WINDOW)), seq_spec((KV_W, WINDOW))),
        scratch_shapes=[
            pltpu.VMEM((rows, D_MODEL), BF16),
            pltpu.VMEM((rows, ODD_IN + ROW_PAD), F32),
            pltpu.VMEM((rows, D_MODEL), F32),
            pltpu.VMEM((N_LG, rows, LANES) if time_major_pool else (nb * N_LG, HIST_C + tt, LANES), F32),
            pltpu.VMEM((nb, WINDOW + tt, KV_W), F32),
            pltpu.VMEM((nb, WINDOW + tt, KV_W), F32),
            pltpu.VMEM((nb, WINDOW + tt, KV_W), F32),
            pltpu.VMEM((nb, WINDOW + tt, KV_W), F32),
            pltpu.VMEM((N_LG, rows, LANES), F32),
            pltpu.VMEM((rows, MIX_W + ROW_PAD), F32),
            pltpu.VMEM((rows, 2 * MIX_W), BF16),
        ],
        compiler_params=pltpu.CompilerParams(
            dimension_semantics=("arbitrary", "arbitrary"), vmem_limit_bytes=VMEM_LIMIT_BYTES),
        name="odd_layer",
    )(sinks, x2d, mod, st_c, st_k, st_v, g_pre, w_in, pool_w, pool_scale, w_out, g_post)


def _tiling(n, t):
    if t >= TILE_ROWS:
        return 1, TILE_ROWS
    return min(n, SHORT_TILE_ROWS // t), t


def _trunk(x, mod_e, mod_o, st_a, st_b, st_c, st_k, st_v, pos0, we, wo):
    n, t, _ = x.shape
    nb, tt = _tiling(n, t)
    short = t == tt and nb > 1

    def swap(st):
        return jnp.transpose(st, (1, 0, 2)) if short else st

    x2d, na, nbs = _even_layer(x.reshape(n * t, D_MODEL), mod_e, swap(st_a), swap(st_b), *we, n=n, t=t, nb=nb,
                               tt=tt, time_major_state=short)
    def channel_major(cache):
        return jnp.transpose(cache, (0, 2, 3, 1)).reshape(n, KV_W, WINDOW)

    def position_major(cache):
        return jnp.transpose(cache.reshape(n, N_KV, HEAD_DIM, WINDOW), (0, 3, 1, 2))

    x2d, nc, nk, nv = _odd_layer(x2d, mod_o, swap(st_c), channel_major(st_k), channel_major(st_v), *wo,
                                 n=n, t=t, nb=nb, tt=tt, pos0=pos0, time_major_pool=short)
    na, nbs, nc = swap(na), swap(nbs), swap(nc)
    return (x2d.reshape(n, t, D_MODEL), na[None], nbs[None], nc[None],
            position_major(nk)[None], position_major(nv)[None])


def kernel(x_prompt, x_sample, state_conv_a, state_conv_b, state_pool_c, cache_win_k, cache_win_v, c_prompt, c_sample, w_mod_e, b_mod_e, g_pre_e, g_post_e, w_in_e, conv_a_w, conv_b_w, conv_b_b, ln_b_g, ln_b_b, w_out_e, w_mod_o, b_mod_o, g_pre_o, g_post_o, w_in_o, pool_w, pool_scale, sinks, w_out_o):
    n_p = x_prompt.shape[0]
    assert w_in_e.shape[0] == 1 and w_in_o.shape[0] == 1, "one even and one odd layer"

    mod_e_p, mod_e_s = _adaln_mod(c_prompt, c_sample, w_mod_e[0], b_mod_e[0])
    mod_o_p, mod_o_s = _adaln_mod(c_prompt, c_sample, w_mod_o[0], b_mod_o[0])

    we = (g_pre_e, w_in_e[0], conv_a_w[0], conv_b_w[0], conv_b_b, ln_b_g, ln_b_b,
          w_out_e[0].astype(BF16), g_post_e)
    wo = (g_pre_o, w_in_o[0], pool_w[0].astype(BF16), pool_scale, sinks[0],
          w_out_o[0].astype(BF16), g_post_o)

    dt = x_prompt.dtype
    z_a = jnp.zeros((n_p, CONV_A - 1, MIX_W), dt)
    z_b = jnp.zeros((n_p, CONV_B - 1, MIX_W), dt)
    z_c = jnp.zeros((n_p, POOL_PAST, MIX_W), dt)
    z_kv = jnp.zeros((n_p, WINDOW, N_KV, HEAD_DIM), dt)
    y_p, pa, pb, pc, pk, pv = _trunk(x_prompt, mod_e_p, mod_o_p, z_a, z_b, z_c, z_kv, z_kv, 0, we, wo)
    y_s, sa, sb, sc, sk, sv = _trunk(x_sample, mod_e_s, mod_o_s, state_conv_a[0], state_conv_b[0],
                                     state_pool_c[0], cache_win_k[0], cache_win_v[0], PAST_LEN, we, wo)
    return (y_p, y_s, pa, sa, pb, sb, pc, sc, pk, sk, pv, sv)
```

```python
import functools

import jax
import jax.numpy as jnp
from jax import lax
from jax.experimental import pallas as pl
from jax.experimental.pallas import tpu as pltpu

F32 = jnp.float32
BF16 = jnp.bfloat16

D_MODEL = 1024
MIX_W = 512
LANES = 128
MXU_COLS = 256
N_LG = MIX_W // LANES
HEAD_DIM = 64
N_HEADS = 8
N_KV = 2
N_REP = N_HEADS // N_KV
KV_W = N_KV * HEAD_DIM
CONV_A = 3
CONV_B = 31
POOL_WINDOWS = (2, 4, 8, 16)
POOL_PAST = 15
WINDOW = 128
PAST_LEN = 8192
EVEN_IN = 7 * MIX_W
ODD_IN = 4 * MIX_W + 2 * KV_W
Q_COL, K_COL, V_COL, DG_COL = 2 * MIX_W, 3 * MIX_W, 3 * MIX_W + KV_W, 3 * MIX_W + 2 * KV_W
RMS_EPS = 1e-6
LN_EPS = 1e-5
HIST_A = 8
HIST_B = 32
HIST_C = 16
ATT_MIN_Q_ROWS = 32
TILE_ROWS = 1024
SHORT_TILE_ROWS = 256
ROW_CHUNK = 64
WIN_CHUNK = 128
MOD_K_BLOCK = 256
ROW_PAD = LANES
VMEM_LIMIT_BYTES = 56 * 1024 * 1024
NEG_LOG2_E = -1.4426950408889634
ALIBI_SLOPES = tuple(float(2.0 ** (-8.0 * (h + 1) / N_HEADS)) for h in range(N_HEADS))


def _sigmoid(x):
    return 1.0 / (1.0 + jnp.exp2(x * NEG_LOG2_E))


def _silu(x):
    return x * _sigmoid(x)


def _mod_kernel(cp_ref, cs_ref, w_ref, b_ref, op_ref, os_ref):
    k = pl.program_id(0)
    n_s = cs_ref.shape[0]
    c = jnp.concatenate([cs_ref[...], cp_ref[...]], axis=0)
    part = jnp.dot(_silu(c).astype(BF16), w_ref[...].astype(BF16), preferred_element_type=F32)
    for o_ref, rows in ((os_ref, slice(0, n_s)), (op_ref, slice(n_s, c.shape[0]))):
        @pl.when(k == 0)
        def _first():
            o_ref[...] = part[rows] + b_ref[...]

        @pl.when(k > 0)
        def _rest():
            o_ref[...] += part[rows]


def _adaln_mod(c_prompt, c_sample, w_mod, b_mod):
    def c_spec(c):
        return pl.BlockSpec((c.shape[0], MOD_K_BLOCK), lambda k: (0, k))

    def out_spec(c):
        return pl.BlockSpec((c.shape[0], 3 * D_MODEL), lambda k: (0, 0))

    return pl.pallas_call(
        _mod_kernel,
        out_shape=tuple(jax.ShapeDtypeStruct((c.shape[0], 3 * D_MODEL), F32) for c in (c_prompt, c_sample)),
        grid=(D_MODEL // MOD_K_BLOCK,),
        in_specs=[
            c_spec(c_prompt),
            c_spec(c_sample),
            pl.BlockSpec((MOD_K_BLOCK, 3 * D_MODEL), lambda k: (k, 0)),
            pl.BlockSpec((1, 3 * D_MODEL), lambda k: (0, 0)),
        ],
        out_specs=(out_spec(c_prompt), out_spec(c_sample)),
        compiler_params=pltpu.CompilerParams(dimension_semantics=("arbitrary",)),
        name="adaln_mod",
    )(c_prompt, c_sample, w_mod, b_mod.reshape(1, -1))


def _mod_rows(mod_ref, seq0, col, r0, rc, tt):
    if tt >= rc:
        return mod_ref[pl.ds(seq0 + r0 // tt, 1), col:col + D_MODEL]
    parts = [jnp.broadcast_to(mod_ref[pl.ds(seq0 + b, 1), col:col + D_MODEL], (tt, D_MODEL))
             for b in range(r0 // tt, (r0 + rc) // tt)]
    return jnp.concatenate(parts, axis=0)


def _pre_norm(x_ref, mod_ref, seq0, g_ref, h_ref, rows, tt):
    rc = min(rows, ROW_CHUNK)
    for r0 in range(0, rows, rc):
        x = x_ref[r0:r0 + rc, :]
        ms = jnp.mean(x * x, axis=-1, keepdims=True)
        gain = g_ref[...] * (1.0 + _mod_rows(mod_ref, seq0, D_MODEL, r0, rc, tt))
        h = x * lax.rsqrt(ms + RMS_EPS) * gain + _mod_rows(mod_ref, seq0, 0, r0, rc, tt)
        h_ref[r0:r0 + rc, :] = h.astype(BF16)


def _out_proj_task(mix_ref, wout_ref, o_ref, lo, n):
    def task():
        o_ref[lo:lo + n, :] = jnp.dot(mix_ref[lo:lo + n, :], wout_ref[...], preferred_element_type=F32)
    return task


def _residual_tasks(x_ref, mod_ref, seq0, g_ref, o_ref, y_ref, lo, n, tt):
    rc = min(n, ROW_CHUNK)

    def chunk(r0):
        o = o_ref[r0:r0 + rc, :]
        ms = jnp.mean(o * o, axis=-1, keepdims=True)
        gain = g_ref[...] * _mod_rows(mod_ref, seq0, 2 * D_MODEL, r0, rc, tt)
        y_ref[r0:r0 + rc, :] = x_ref[r0:r0 + rc, :] + o * lax.rsqrt(ms + RMS_EPS) * gain
    return [functools.partial(chunk, r0) for r0 in range(lo, lo + n, rc)]


def _seq_chunks(lo, n, tt):
    rc = min(tt, n, WIN_CHUNK)
    return [(r0 // tt, r0 % tt, r0, rc) for r0 in range(lo, lo + n, rc)]


def _even_gating(z_ref, ca_ref, cb_ref, cbb_ref, lng_ref, lnb_ref, mix_ref, rows):
    rc = min(rows, ROW_CHUNK)
    for r0 in range(0, rows, rc):
        def zs(c):
            return z_ref[r0:r0 + rc, MIX_W * c:MIX_W * (c + 1)]

        def all_lanes(c_ref):
            return jnp.concatenate([c_ref[g, r0:r0 + rc, :] for g in range(N_LG)], axis=1)
        ya = zs(1) * all_lanes(ca_ref) * _silu(zs(3))
        mix_ref[r0:r0 + rc, 0:MIX_W] = ya.astype(BF16)
        yb = all_lanes(cb_ref) + cbb_ref[...]
        mu = jnp.mean(yb, axis=-1, keepdims=True)
        yc = yb - mu
        var = jnp.mean(yc * yc, axis=-1, keepdims=True)
        ln = yc * lax.rsqrt(var + LN_EPS) * lng_ref[...] + lnb_ref[...]
        mix_ref[r0:r0 + rc, MIX_W:2 * MIX_W] = (_silu(ln) * _silu(zs(6))).astype(BF16)


def _even_kernel(x_ref, mod_ref, sa_ref, sb_ref, gpre_ref, win_ref, cwa_ref, cwb_ref, cbb_ref,
                 lng_ref, lnb_ref, wout_ref, gpost_ref,
                 y_ref, na_ref, nb_ref,
                 h_ref, z_ref, o_ref, fa_ref, fb_ref, ca_ref, cb_ref, mix_ref, wob_ref,
                 *, nb, tt, nt, time_major_state):
    rows = nb * tt
    j = pl.program_id(1)
    seq0 = pl.program_id(0) * nb
    assert not time_major_state or nt == 1

    @pl.when(j == 0)
    def _load_state():
        for b in range(0 if time_major_state else nb):
            for g in range(N_LG):
                lg = slice(LANES * g, LANES * (g + 1))
                fa_ref[b * N_LG + g, 0:HIST_A, :] = jnp.zeros((HIST_A, LANES), F32)
                fa_ref[b * N_LG + g, HIST_A - (CONV_A - 1):HIST_A, :] = sa_ref[b, :, lg]
                fb_ref[b * N_LG + g, 0:8, :] = jnp.zeros((8, LANES), F32)
                fb_ref[b * N_LG + g, HIST_B - (CONV_B - 1):HIST_B, :] = sb_ref[b, :, lg]

    _pre_norm(x_ref, mod_ref, seq0, gpre_ref, h_ref, rows, tt)

    split_order = (4, 5, 0, 2, 1, 3, 6)
    for c0 in [MIX_W * s + d for s in split_order for d in range(0, MIX_W, MXU_COLS)]:
        z_ref[:, c0:c0 + MXU_COLS] = jnp.dot(h_ref[...], win_ref[:, c0:c0 + MXU_COLS].astype(BF16),
                                             preferred_element_type=F32)
    wob_ref[...] = wout_ref[...].astype(BF16)

    if time_major_state:
        rc = min(rows, WIN_CHUNK)
        for r0 in range(0, rows, rc):
            for g in range(N_LG):
                def zc(c):
                    return z_ref[r0:r0 + rc, MIX_W * c + LANES * g:MIX_W * c + LANES * (g + 1)]
                fa_ref[g, r0:r0 + rc, :] = zc(2) * zc(0)
                fb_ref[g, r0:r0 + rc, :] = zc(4) * _sigmoid(zc(5))

        def conv(w_ref, state_ref, u_ref, out_ref, new_state_ref, taps):
            hist = taps - 1

            def slab(j_in, g):
                if j_in < hist:
                    return state_ref[j_in, :, LANES * g:LANES * (g + 1)]
                return u_ref[g, pl.ds(j_in - hist, nb, stride=tt), :]

            for t in range(tt):
                for g in range(N_LG):
                    acc = None
                    for k in range(taps):
                        term = w_ref[k:k + 1, LANES * g:LANES * (g + 1)] * slab(t + k, g)
                        acc = term if acc is None else acc + term
                    out_ref[g, pl.ds(t, nb, stride=tt), :] = acc
            for j_out in range(hist):
                for g in range(N_LG):
                    new_state_ref[j_out, :, LANES * g:LANES * (g + 1)] = slab(j_out + tt, g)

        conv(cwa_ref, sa_ref, fa_ref, ca_ref, na_ref, CONV_A)
        conv(cwb_ref, sb_ref, fb_ref, cb_ref, nb_ref, CONV_B)

    for b, c0, r0, rc in ([] if time_major_state else _seq_chunks(0, rows, tt)):
        for g in range(N_LG):
            def zc(c):
                return z_ref[r0:r0 + rc, MIX_W * c + LANES * g:MIX_W * c + LANES * (g + 1)]
            fa_ref[b * N_LG + g, HIST_A + c0:HIST_A + c0 + rc, :] = zc(2) * zc(0)
            fb_ref[b * N_LG + g, HIST_B + c0:HIST_B + c0 + rc, :] = zc(4) * _sigmoid(zc(5))

    for b, c0, r0, rc in ([] if time_major_state else _seq_chunks(0, rows, tt)):
        for g in range(N_LG):
            lg = slice(LANES * g, LANES * (g + 1))
            acc = None
            for k in range(CONV_A):
                off = HIST_A - (CONV_A - 1) + k + c0
                t = cwa_ref[k:k + 1, lg] * fa_ref[b * N_LG + g, off:off + rc, :]
                acc = t if acc is None else acc + t
            ca_ref[g, r0:r0 + rc, :] = acc
            acc = None
            for k in range(CONV_B):
                off = HIST_B - (CONV_B - 1) + k + c0
                t = cwb_ref[k:k + 1, lg] * fb_ref[b * N_LG + g, off:off + rc, :]
                acc = t if acc is None else acc + t
            cb_ref[g, r0:r0 + rc, :] = acc

    for b in range(0 if time_major_state else nb):
        for g in range(N_LG):
            lg = slice(LANES * g, LANES * (g + 1))
            na_ref[b, :, lg] = fa_ref[b * N_LG + g, HIST_A + tt - (CONV_A - 1):HIST_A + tt, :]
            nb_ref[b, :, lg] = fb_ref[b * N_LG + g, HIST_B + tt - (CONV_B - 1):HIST_B + tt, :]
            if nt > 1:
                fa_ref[b * N_LG + g, 0:HIST_A, :] = fa_ref[b * N_LG + g, tt:tt + HIST_A, :]
                fb_ref[b * N_LG + g, 0:HIST_B, :] = fb_ref[b * N_LG + g, tt:tt + HIST_B, :]

    _even_gating(z_ref, ca_ref, cb_ref, cbb_ref, lng_ref, lnb_ref, mix_ref, rows)
    _out_proj_task(mix_ref, wob_ref, o_ref, 0, rows)()
    for task in _residual_tasks(x_ref, mod_ref, seq0, gpost_ref, o_ref, y_ref, 0, rows, tt):
        task()


def _even_layer(x2d, mod, st_a, st_b, g_pre, w_in, cw_a, cw_b, cb_b, ln_g, ln_b, w_out, g_post,
                *, n, t, nb, tt, time_major_state):
    rows = nb * tt
    nt = t // tt
    kern = functools.partial(_even_kernel, nb=nb, tt=tt, nt=nt, time_major_state=time_major_state)
    row_spec = pl.BlockSpec((rows, D_MODEL), lambda i, j: (i * nt + j, 0))

    def seq_spec(shape):
        if time_major_state:
            return pl.BlockSpec((shape[0], nb, shape[1]), lambda i, j: (0, i, 0))
        return pl.BlockSpec((nb,) + shape, lambda i, j: (i, 0, 0))

    def window_scratch(hist):
        if time_major_state:
            return pltpu.VMEM((N_LG, rows, LANES), F32)
        return pltpu.VMEM((nb * N_LG, hist + tt, LANES), F32)

    def const_spec(shape):
        return pl.BlockSpec(shape, lambda i, j: (0,) * len(shape))

    return pl.pallas_call(
        kern,
        out_shape=(
            jax.ShapeDtypeStruct((n * t, D_MODEL), F32),
            jax.ShapeDtypeStruct(st_a.shape, F32),
            jax.ShapeDtypeStruct(st_b.shape, F32),
        ),
        grid=(n // nb, nt),
        in_specs=[
            row_spec,
            const_spec(mod.shape),
            seq_spec((CONV_A - 1, MIX_W)),
            seq_spec((CONV_B - 1, MIX_W)),
            const_spec((1, D_MODEL)),
            const_spec((D_MODEL, EVEN_IN)),
            const_spec((CONV_A, MIX_W)),
            const_spec((CONV_B, MIX_W)),
            const_spec((1, MIX_W)),
            const_spec((1, MIX_W)),
            const_spec((1, MIX_W)),
            const_spec((2 * MIX_W, D_MODEL)),
            const_spec((1, D_MODEL)),
        ],
        out_specs=(row_spec, seq_spec((CONV_A - 1, MIX_W)), seq_spec((CONV_B - 1, MIX_W))),
        scratch_shapes=[
            pltpu.VMEM((rows, D_MODEL), BF16),
            pltpu.VMEM((rows, EVEN_IN + ROW_PAD), F32),
            pltpu.VMEM((rows, D_MODEL), F32),
            window_scratch(HIST_A),
            window_scratch(HIST_B),
            pltpu.VMEM((N_LG, rows, LANES), F32),
            pltpu.VMEM((N_LG, rows, LANES), F32),
            pltpu.VMEM((rows, 2 * MIX_W), BF16),
            pltpu.VMEM((2 * MIX_W, D_MODEL), BF16),
        ],
        compiler_params=pltpu.CompilerParams(
            dimension_semantics=("arbitrary", "arbitrary"), vmem_limit_bytes=VMEM_LIMIT_BYTES),
        name="even_layer",
    )(x2d, mod, st_a, st_b, g_pre, w_in, cw_a, cw_b, cb_b, ln_g, ln_b, w_out, g_post)


def _odd_kernel(sinks_ref, x_ref, mod_ref, sc_ref, sk_ref, sv_ref, gpre_ref, win_ref, pw_ref, ps_ref,
                wout_ref, gpost_ref,
                y_ref, nc_ref, nk_ref, nv_ref,
                h_ref, z_ref, o_ref, fc_ref, kf_ref, ks_ref, vf_ref, vs_ref, pc_ref, od_ref, mix_ref, wob_ref,
                *, nb, tt, nt, pos0, time_major_pool):
    rows = nb * tt
    j = pl.program_id(1)
    seq0 = pl.program_id(0) * nb
    tile_pos = pos0 + j * tt
    assert not time_major_pool or nt == 1

    @pl.when(j == 0)
    def _load_state():
        for b in range(nb):
            for g in range(0 if time_major_pool else N_LG):
                fc_ref[b * N_LG + g, 0:8, :] = jnp.zeros((8, LANES), F32)
                fc_ref[b * N_LG + g, HIST_C - POOL_PAST:HIST_C, :] = sc_ref[b, :, LANES * g:LANES * (g + 1)]
            k_past, v_past = sk_ref[b].T, sv_ref[b].T
            kf_ref[b, 0:WINDOW, :] = k_past
            vf_ref[b, 0:WINDOW, :] = v_past
            ks_ref[b, 0:WINDOW, :] = pltpu.roll(k_past, HEAD_DIM, axis=1)
            vs_ref[b, 0:WINDOW, :] = pltpu.roll(v_past, HEAD_DIM, axis=1)

    qs = min(tt, WINDOW)
    gs = max(1, ATT_MIN_Q_ROWS // qs)
    kw = WINDOW + qs
    n_q, n_keys = gs * qs, gs * kw
    assert N_REP == 4 and 2 * HEAD_DIM == LANES and KV_W == LANES
    assert qs & (qs - 1) == 0 and gs & (gs - 1) == 0 and nb % gs == 0 and (gs == 1 or tt == qs)
    shape = (2 * n_q, n_keys)
    ri = lax.broadcasted_iota(jnp.int32, shape, 0)
    ci = lax.broadcasted_iota(jnp.int32, shape, 1)
    q_seq = (ri >> (qs.bit_length() - 1)) & (gs - 1)
    k_seq = jnp.zeros(shape, jnp.int32)
    for s_i in range(1, gs):
        k_seq = k_seq + (ci >= s_i * kw).astype(jnp.int32)
    k_c = ci - k_seq * kw
    dist = (ri & (qs - 1)) + WINDOW - k_c
    band = (q_seq == k_seq) & (dist >= 0) & (dist < WINDOW)
    dist_f = dist.astype(F32)
    upper_grp = ri >= n_q
    upper_grp_col = lax.broadcasted_iota(jnp.int32, (2 * n_q, 1), 0) >= n_q
    low_lanes = lax.broadcasted_iota(jnp.int32, (n_keys, LANES), 1) < HEAD_DIM
    ones_cols = jnp.ones((n_keys, LANES), BF16)

    def head_of(g, grp, parity):
        return g * N_REP + 2 * grp + parity

    def alibi_bias(g, parity):
        slope = jnp.where(upper_grp, ALIBI_SLOPES[head_of(g, 1, parity)], ALIBI_SLOPES[head_of(g, 0, parity)])
        return jnp.where(band, -(slope * dist_f), -jnp.inf)

    biases = {(g, parity): alibi_bias(g, parity) for g in range(N_KV) for parity in range(2)}

    def attention(b0, sb, g):
        r0 = b0 * tt + sb * qs
        p0 = tile_pos + sb * qs
        in_range = (k_c >= WINDOW - p0) if pos0 + sb * qs < WINDOW else None
        if gs == 1:
            win = (b0, slice(sb * qs, sb * qs + kw))
            k_nat, k_swp, v_nat, v_swp = kf_ref[win], ks_ref[win], vf_ref[win], vs_ref[win]
        else:
            k_nat, k_swp, v_nat, v_swp = (r[b0:b0 + gs].reshape(n_keys, KV_W)
                                          for r in (kf_ref, ks_ref, vf_ref, vs_ref))
        k_lo, k_hi = (k_nat, k_swp) if g == 0 else (k_swp, k_nat)
        v_lo, v_hi = (v_nat, v_swp) if g == 0 else (v_swp, v_nat)
        v_ext = jnp.concatenate([jnp.where(low_lanes, v_lo, v_hi).astype(BF16), ones_cols], axis=1)
        qg = jnp.concatenate([z_ref[r0:r0 + n_q, Q_COL + 2 * LANES * g + LANES * grp:
                                    Q_COL + 2 * LANES * g + LANES * (grp + 1)] for grp in range(2)], axis=0)
        qg = (qg * (HEAD_DIM ** -0.5)).astype(BF16)
        for parity in range(2):
            k_ext = (jnp.where(low_lanes, k_lo, 0.0) if parity == 0 else jnp.where(low_lanes, 0.0, k_hi)).astype(BF16)
            sink = jnp.where(upper_grp_col, sinks_ref[head_of(g, 1, parity)], sinks_ref[head_of(g, 0, parity)])
            sc = lax.dot_general(qg, k_ext, (((1,), (1,)), ((), ())), preferred_element_type=F32)
            bias = biases[g, parity]
            sc = sc + (bias if in_range is None else jnp.where(in_range, bias, -jnp.inf))
            m = jnp.maximum(jnp.max(sc, axis=-1, keepdims=True), sink)
            p = jnp.exp(sc - m)
            o_ext = jnp.dot(p.astype(BF16), v_ext, preferred_element_type=F32)
            o = o_ext[:, 0:LANES] / (o_ext[:, LANES:2 * LANES] + jnp.exp(sink - m))
            for grp in range(2):
                h = head_of(g, grp, parity)
                od_ref[r0:r0 + n_q, HEAD_DIM * h:HEAD_DIM * (h + 1)] = (
                    o[grp * n_q:(grp + 1) * n_q, HEAD_DIM * parity:HEAD_DIM * (parity + 1)])

    def window_inputs(b, c0, r0, rc):
        for g in range(0 if time_major_pool else N_LG):
            fc_ref[b * N_LG + g, HIST_C + c0:HIST_C + c0 + rc, :] = z_ref[r0:r0 + rc, LANES * g:LANES * (g + 1)]
        k = z_ref[r0:r0 + rc, K_COL:K_COL + KV_W]
        v = z_ref[r0:r0 + rc, V_COL:V_COL + KV_W]
        kf_ref[b, WINDOW + c0:WINDOW + c0 + rc, :] = k
        vf_ref[b, WINDOW + c0:WINDOW + c0 + rc, :] = v
        ks_ref[b, WINDOW + c0:WINDOW + c0 + rc, :] = pltpu.roll(k, HEAD_DIM, axis=1)
        vs_ref[b, WINDOW + c0:WINDOW + c0 + rc, :] = pltpu.roll(v, HEAD_DIM, axis=1)

    def pooling(b, c0, r0, rc):
        pos = tile_pos + c0 + lax.broadcasted_iota(jnp.int32, (rc, LANES), 0)
        for g, w in enumerate(POOL_WINDOWS):
            u = fc_ref[b * N_LG + g, HIST_C + c0:HIST_C + c0 + rc, :]
            acc = u
            for d in range(1, w):
                acc = acc + fc_ref[b * N_LG + g, HIST_C + c0 - d:HIST_C + c0 - d + rc, :]
            cnt = jnp.minimum(w, pos + 1).astype(F32)
            pc_ref[g, r0:r0 + rc, :] = acc / cnt - u

    def pooling_time_major(lo, n):
        b_lo, n_seq = lo // tt, n // tt

        for g in range(N_LG):
            fc_ref[g, lo:lo + n, :] = z_ref[lo:lo + n, LANES * g:LANES * (g + 1)]

        def slab(j_in, g):
            if j_in < POOL_PAST:
                return sc_ref[j_in, b_lo:b_lo + n_seq, LANES * g:LANES * (g + 1)]
            return fc_ref[g, pl.ds(lo + j_in - POOL_PAST, n_seq, stride=tt), :]

        for t in range(tt):
            for g, w in enumerate(POOL_WINDOWS):
                u = slab(POOL_PAST + t, g)
                acc = u
                for d in range(1, w):
                    acc = acc + slab(POOL_PAST + t - d, g)
                pc_ref[g, pl.ds(lo + t, n_seq, stride=tt), :] = acc / float(min(w, pos0 + t + 1)) - u
        for j_out in range(POOL_PAST):
            for g in range(N_LG):
                nc_ref[j_out, b_lo:b_lo + n_seq, LANES * g:LANES * (g + 1)] = slab(j_out + tt, g)

    def gating(r0, rc):
        for g in range(N_LG):
            lg = slice(LANES * g, LANES * (g + 1))
            mixed = jnp.dot(pc_ref[g, r0:r0 + rc, :].astype(BF16), pw_ref[g], preferred_element_type=F32)
            yc = mixed * ps_ref[:, lg] * _silu(z_ref[r0:r0 + rc, MIX_W + LANES * g:MIX_W + LANES * (g + 1)])
            mix_ref[r0:r0 + rc, lg] = yc.astype(BF16)
        yd = od_ref[r0:r0 + rc, 0:MIX_W] * _silu(z_ref[r0:r0 + rc, DG_COL:DG_COL + MIX_W])
        mix_ref[r0:r0 + rc, MIX_W:2 * MIX_W] = yd.astype(BF16)

    def mixer_tasks(lo, n):
        chunks = _seq_chunks(lo, n, tt)
        if gs == 1:
            blocks = [(r0 // tt, (r0 % tt) // qs) for r0 in range(lo, lo + n, qs)]
        else:
            blocks = [(b0, 0) for b0 in range(lo // tt, (lo + n) // tt, gs)]
        pool_tasks = ([functools.partial(pooling_time_major, lo, n)] if time_major_pool
                      else [functools.partial(pooling, *ch) for ch in chunks])
        return ([functools.partial(window_inputs, *ch) for ch in chunks]
                + pool_tasks
                + [functools.partial(attention, b0, sb, g) for b0, sb in blocks for g in range(N_KV)]
                + [functools.partial(gating, lo, n)])

    _pre_norm(x_ref, mod_ref, seq0, gpre_ref, h_ref, rows, tt)
    for c0 in range(0, ODD_IN, MXU_COLS):
        z_ref[:, c0:c0 + MXU_COLS] = jnp.dot(h_ref[...], win_ref[:, c0:c0 + MXU_COLS].astype(BF16),
                                             preferred_element_type=F32)
    wob_ref[...] = wout_ref[...].astype(BF16)

    n_chunks = max(1, rows // WIN_CHUNK)
    n = rows // n_chunks
    for c in range(n_chunks + 1):
        if c > 0:
            _out_proj_task(mix_ref, wob_ref, o_ref, (c - 1) * n, n)()
        if c < n_chunks:
            for task in mixer_tasks(c * n, n):
                task()
        if c > 0:
            for task in _residual_tasks(x_ref, mod_ref, seq0, gpost_ref, o_ref, y_ref, (c - 1) * n, n, tt):
                task()

    for b in range(nb):
        for g in range(0 if time_major_pool else N_LG):
            nc_ref[b, :, LANES * g:LANES * (g + 1)] = fc_ref[b * N_LG + g, HIST_C + tt - POOL_PAST:HIST_C + tt, :]
            if nt > 1:
                fc_ref[b * N_LG + g, 0:HIST_C, :] = fc_ref[b * N_LG + g, tt:tt + HIST_C, :]
        nk_ref[b] = kf_ref[b, tt:tt + WINDOW, :].T
        nv_ref[b] = vf_ref[b, tt:tt + WINDOW, :].T
        if nt > 1:
            for r in (kf_ref, ks_ref, vf_ref, vs_ref):
                r[b, 0:WINDOW, :] = r[b, tt:tt + WINDOW, :]


def _odd_layer(x2d, mod, st_c, st_k, st_v, g_pre, w_in, pool_w, pool_scale, sinks, w_out, g_post,
               *, n, t, nb, tt, pos0, time_major_pool):
    rows = nb * tt
    nt = t // tt
    kern = functools.partial(_odd_kernel, nb=nb, tt=tt, nt=nt, pos0=pos0, time_major_pool=time_major_pool)
    row_spec = pl.BlockSpec((rows, D_MODEL), lambda i, j: (i * nt + j, 0))

    def seq_spec(shape):
        return pl.BlockSpec((nb,) + shape, lambda i, j: (i, 0, 0))

    def const_spec(shape):
        return pl.BlockSpec(shape, lambda i, j: (0,) * len(shape))

    if time_major_pool:
        pool_spec = pl.BlockSpec((POOL_PAST, nb, MIX_W), lambda i, j: (0, i, 0))
    else:
        pool_spec = seq_spec((POOL_PAST, MIX_W))

    return pl.pallas_call(
        kern,
        out_shape=(
            jax.ShapeDtypeStruct((n * t, D_MODEL), F32),
            jax.ShapeDtypeStruct(st_c.shape, F32),
            jax.ShapeDtypeStruct((n, KV_W, WINDOW), F32),
            jax.ShapeDtypeStruct((n, KV_W, WINDOW), F32),
        ),
        grid=(n // nb, nt),
        in_specs=[
            pl.BlockSpec(memory_space=pltpu.SMEM),
            row_spec,
            const_spec(mod.shape),
            pool_spec,
            seq_spec((KV_W, WINDOW)),
            seq_spec((KV_W, WINDOW)),
            const_spec((1, D_MODEL)),
            const_spec((D_MODEL, ODD_IN)),
            const_spec((N_LG, LANES, LANES)),
            const_spec((1, MIX_W)),
            const_spec((2 * MIX_W, D_MODEL)),
            const_spec((1, D_MODEL)),
        ],
        out_specs=(row_spec, pool_spec, seq_spec((KV_W, WINDOW)), seq_spec((KV_W, WINDOW))),
        scratch_shapes=[
            pltpu.VMEM((rows, D_MODEL), BF16),
            pltpu.VMEM((rows, ODD_IN + ROW_PAD), F32),
            pltpu.VMEM((rows, D_MODEL), F32),
            pltpu.VMEM((N_LG, rows, LANES) if time_major_pool else (nb * N_LG, HIST_C + tt, LANES), F32),
            pltpu.VMEM((nb, WINDOW + tt, KV_W), F32),
            pltpu.VMEM((nb, WINDOW + tt, KV_W), F32),
            pltpu.VMEM((nb, WINDOW + tt, KV_W), F32),
            pltpu.VMEM((nb, WINDOW + tt, KV_W), F32),
            pltpu.VMEM((N_LG, rows, LANES), F32),
            pltpu.VMEM((rows, MIX_W + ROW_PAD), F32),
            pltpu.VMEM((rows, 2 * MIX_W), BF16),
            pltpu.VMEM((2 * MIX_W, D_MODEL), BF16),
        ],
        compiler_params=pltpu.CompilerParams(
            dimension_semantics=("arbitrary", "arbitrary"), vmem_limit_bytes=VMEM_LIMIT_BYTES),
        name="odd_layer",
    )(sinks, x2d, mod, st_c, st_k, st_v, g_pre, w_in, pool_w, pool_scale, w_out, g_post)


def _tiling(n, t):
    if t >= TILE_ROWS:
        return 1, TILE_ROWS
    return min(n, SHORT_TILE_ROWS // t), t


def _trunk(x, mod_e, mod_o, st_a, st_b, st_c, st_k, st_v, pos0, we, wo):
    n, t, _ = x.shape
    nb, tt = _tiling(n, t)
    short = t == tt and nb > 1

    def swap(st):
        return jnp.transpose(st, (1, 0, 2)) if short else st

    x2d, na, nbs = _even_layer(x.reshape(n * t, D_MODEL), mod_e, swap(st_a), swap(st_b), *we, n=n, t=t, nb=nb,
                               tt=tt, time_major_state=short)
    def channel_major(cache):
        return jnp.transpose(cache, (0, 2, 3, 1)).reshape(n, KV_W, WINDOW)

    def position_major(cache):
        return jnp.transpose(cache.reshape(n, N_KV, HEAD_DIM, WINDOW), (0, 3, 1, 2))

    x2d, nc, nk, nv = _odd_layer(x2d, mod_o, swap(st_c), channel_major(st_k), channel_major(st_v), *wo,
                                 n=n, t=t, nb=nb, tt=tt, pos0=pos0, time_major_pool=short)
    na, nbs, nc = swap(na), swap(nbs), swap(nc)
    return (x2d.reshape(n, t, D_MODEL), na[None], nbs[None], nc[None],
            position_major(nk)[None], position_major(nv)[None])


def kernel(x_prompt, x_sample, state_conv_a, state_conv_b, state_pool_c, cache_win_k, cache_win_v, c_prompt, c_sample, w_mod_e, b_mod_e, g_pre_e, g_post_e, w_in_e, conv_a_w, conv_b_w, conv_b_b, ln_b_g, ln_b_b, w_out_e, w_mod_o, b_mod_o, g_pre_o, g_post_o, w_in_o, pool_w, pool_scale, sinks, w_out_o):
    n_p = x_prompt.shape[0]
    assert w_in_e.shape[0] == 1 and w_in_o.shape[0] == 1, "one even and one odd layer"

    mod_e_p, mod_e_s = _adaln_mod(c_prompt, c_sample, w_mod_e[0], b_mod_e[0])
    mod_o_p, mod_o_s = _adaln_mod(c_prompt, c_sample, w_mod_o[0], b_mod_o[0])

    we = (g_pre_e, w_in_e[0], conv_a_w[0], conv_b_w[0], conv_b_b, ln_b_g, ln_b_b,
          w_out_e[0], g_post_e)
    wo = (g_pre_o, w_in_o[0], pool_w[0].astype(BF16), pool_scale, sinks[0],
          w_out_o[0], g_post_o)

    dt = x_prompt.dtype
    z_a = jnp.zeros((n_p, CONV_A - 1, MIX_W), dt)
    z_b = jnp.zeros((n_p, CONV_B - 1, MIX_W), dt)
    z_c = jnp.zeros((n_p, POOL_PAST, MIX_W), dt)
    z_kv = jnp.zeros((n_p, WINDOW, N_KV, HEAD_DIM), dt)
    y_p, pa, pb, pc, pk, pv = _trunk(x_prompt, mod_e_p, mod_o_p, z_a, z_b, z_c, z_kv, z_kv, 0, we, wo)
    y_s, sa, sb, sc, sk, sv = _trunk(x_sample, mod_e_s, mod_o_s, state_conv_a[0], state_conv_b[0],
                                     state_pool_c[0], cache_win_k[0], cache_win_v[0], PAST_LEN, we, wo)
    return (y_p, y_s, pa, sa, pb, sb, pc, sc, pk, sk, pv, sv)
```

```python
import functools

import jax
import jax.numpy as jnp
from jax import lax
from jax.experimental import pallas as pl
from jax.experimental.pallas import tpu as pltpu

F32 = jnp.float32
BF16 = jnp.bfloat16

D_MODEL = 1024
MIX_W = 512
LANES = 128
MXU_COLS = 256
N_LG = MIX_W // LANES
HEAD_DIM = 64
N_HEADS = 8
N_KV = 2
N_REP = N_HEADS // N_KV
KV_W = N_KV * HEAD_DIM
CONV_A = 3
CONV_B = 31
POOL_WINDOWS = (2, 4, 8, 16)
POOL_PAST = 15
WINDOW = 128
PAST_LEN = 8192
EVEN_IN = 7 * MIX_W
ODD_IN = 4 * MIX_W + 2 * KV_W
Q_COL, K_COL, V_COL, DG_COL = 2 * MIX_W, 3 * MIX_W, 3 * MIX_W + KV_W, 3 * MIX_W + 2 * KV_W
RMS_EPS = 1e-6
LN_EPS = 1e-5
HIST_A = 8
HIST_B = 32
HIST_C = 16
ATT_MIN_Q_ROWS = 32
TILE_ROWS = 1024
SHORT_TILE_ROWS = 256
ROW_CHUNK = 64
WIN_CHUNK = 128
MOD_K_BLOCK = 256
VMEM_LIMIT_BYTES = 56 * 1024 * 1024
NEG_LOG2_E = -1.4426950408889634
ALIBI_SLOPES = tuple(float(2.0 ** (-8.0 * (h + 1) / N_HEADS)) for h in range(N_HEADS))


def _sigmoid(x):
    return 1.0 / (1.0 + jnp.exp2(x * NEG_LOG2_E))


def _silu(x):
    return x * _sigmoid(x)


def _mod_kernel(cp_ref, cs_ref, w_ref, b_ref, op_ref, os_ref):
    k = pl.program_id(0)
    n_s = cs_ref.shape[0]
    c = jnp.concatenate([cs_ref[...], cp_ref[...]], axis=0)
    part = jnp.dot(_silu(c).astype(BF16), w_ref[...].astype(BF16), preferred_element_type=F32)
    for o_ref, rows in ((os_ref, slice(0, n_s)), (op_ref, slice(n_s, c.shape[0]))):
        @pl.when(k == 0)
        def _first():
            o_ref[...] = part[rows] + b_ref[...]

        @pl.when(k > 0)
        def _rest():
            o_ref[...] += part[rows]


def _adaln_mod(c_prompt, c_sample, w_mod, b_mod):
    def c_spec(c):
        return pl.BlockSpec((c.shape[0], MOD_K_BLOCK), lambda k: (0, k))

    def out_spec(c):
        return pl.BlockSpec((c.shape[0], 3 * D_MODEL), lambda k: (0, 0))

    return pl.pallas_call(
        _mod_kernel,
        out_shape=tuple(jax.ShapeDtypeStruct((c.shape[0], 3 * D_MODEL), F32) for c in (c_prompt, c_sample)),
        grid=(D_MODEL // MOD_K_BLOCK,),
        in_specs=[
            c_spec(c_prompt),
            c_spec(c_sample),
            pl.BlockSpec((MOD_K_BLOCK, 3 * D_MODEL), lambda k: (k, 0)),
            pl.BlockSpec((1, 3 * D_MODEL), lambda k: (0, 0)),
        ],
        out_specs=(out_spec(c_prompt), out_spec(c_sample)),
        compiler_params=pltpu.CompilerParams(dimension_semantics=("arbitrary",)),
        name="adaln_mod",
    )(c_prompt, c_sample, w_mod, b_mod.reshape(1, -1))


def _mod_rows(mod_ref, seq0, col, r0, rc, tt):
    if tt >= rc:
        return mod_ref[pl.ds(seq0 + r0 // tt, 1), col:col + D_MODEL]
    parts = [jnp.broadcast_to(mod_ref[pl.ds(seq0 + b, 1), col:col + D_MODEL], (tt, D_MODEL))
             for b in range(r0 // tt, (r0 + rc) // tt)]
    return jnp.concatenate(parts, axis=0)


def _pre_norm(x_ref, mod_ref, seq0, g_ref, h_ref, rows, tt):
    rc = min(rows, ROW_CHUNK)
    for r0 in range(0, rows, rc):
        x = x_ref[r0:r0 + rc, :]
        ms = jnp.mean(x * x, axis=-1, keepdims=True)
        gain = g_ref[...] * (1.0 + _mod_rows(mod_ref, seq0, D_MODEL, r0, rc, tt))
        h = x * lax.rsqrt(ms + RMS_EPS) * gain + _mod_rows(mod_ref, seq0, 0, r0, rc, tt)
        h_ref[r0:r0 + rc, :] = h.astype(BF16)


def _out_proj_task(mix_ref, wout_ref, o_ref, lo, n):
    def task():
        o_ref[lo:lo + n, :] = jnp.dot(mix_ref[lo:lo + n, :], wout_ref[...], preferred_element_type=F32)
    return task


def _residual_tasks(x_ref, mod_ref, seq0, g_ref, o_ref, y_ref, lo, n, tt):
    rc = min(n, ROW_CHUNK)

    def chunk(r0):
        o = o_ref[r0:r0 + rc, :]
        ms = jnp.mean(o * o, axis=-1, keepdims=True)
        gain = g_ref[...] * _mod_rows(mod_ref, seq0, 2 * D_MODEL, r0, rc, tt)
        y_ref[r0:r0 + rc, :] = x_ref[r0:r0 + rc, :] + o * lax.rsqrt(ms + RMS_EPS) * gain
    return [functools.partial(chunk, r0) for r0 in range(lo, lo + n, rc)]


def _seq_chunks(lo, n, tt):
    rc = min(tt, n, WIN_CHUNK)
    return [(r0 // tt, r0 % tt, r0, rc) for r0 in range(lo, lo + n, rc)]


def _even_gating(z_ref, ca_ref, cb_ref, cbb_ref, lng_ref, lnb_ref, mix_ref, rows):
    rc = min(rows, ROW_CHUNK)
    for r0 in range(0, rows, rc):
        def zs(c):
            return z_ref[r0:r0 + rc, MIX_W * c:MIX_W * (c + 1)]

        def all_lanes(c_ref):
            return jnp.concatenate([c_ref[g, r0:r0 + rc, :] for g in range(N_LG)], axis=1)
        ya = zs(1) * all_lanes(ca_ref) * _silu(zs(3))
        mix_ref[r0:r0 + rc, 0:MIX_W] = ya.astype(BF16)
        yb = all_lanes(cb_ref) + cbb_ref[...]
        mu = jnp.mean(yb, axis=-1, keepdims=True)
        yc = yb - mu
        var = jnp.mean(yc * yc, axis=-1, keepdims=True)
        ln = yc * lax.rsqrt(var + LN_EPS) * lng_ref[...] + lnb_ref[...]
        mix_ref[r0:r0 + rc, MIX_W:2 * MIX_W] = (_silu(ln) * _silu(zs(6))).astype(BF16)


def _even_kernel(x_ref, mod_ref, sa_ref, sb_ref, gpre_ref, win_ref, cwa_ref, cwb_ref, cbb_ref,
                 lng_ref, lnb_ref, wout_ref, gpost_ref,
                 y_ref, na_ref, nb_ref,
                 h_ref, z_ref, o_ref, fa_ref, fb_ref, ca_ref, cb_ref, mix_ref, wob_ref,
                 *, nb, tt, nt, time_major_state):
    rows = nb * tt
    j = pl.program_id(1)
    seq0 = pl.program_id(0) * nb
    assert not time_major_state or nt == 1

    @pl.when(j == 0)
    def _load_state():
        for b in range(0 if time_major_state else nb):
            for g in range(N_LG):
                lg = slice(LANES * g, LANES * (g + 1))
                fa_ref[b * N_LG + g, 0:HIST_A, :] = jnp.zeros((HIST_A, LANES), F32)
                fa_ref[b * N_LG + g, HIST_A - (CONV_A - 1):HIST_A, :] = sa_ref[b, :, lg]
                fb_ref[b * N_LG + g, 0:8, :] = jnp.zeros((8, LANES), F32)
                fb_ref[b * N_LG + g, HIST_B - (CONV_B - 1):HIST_B, :] = sb_ref[b, :, lg]

    _pre_norm(x_ref, mod_ref, seq0, gpre_ref, h_ref, rows, tt)

    split_order = (4, 5, 0, 2, 1, 3, 6)
    for c0 in [MIX_W * s + d for s in split_order for d in range(0, MIX_W, MXU_COLS)]:
        z_ref[:, c0:c0 + MXU_COLS] = jnp.dot(h_ref[...], win_ref[:, c0:c0 + MXU_COLS].astype(BF16),
                                             preferred_element_type=F32)
    wob_ref[...] = wout_ref[...].astype(BF16)

    if time_major_state:
        rc = min(rows, WIN_CHUNK)
        for r0 in range(0, rows, rc):
            for g in range(N_LG):
                def zc(c):
                    return z_ref[r0:r0 + rc, MIX_W * c + LANES * g:MIX_W * c + LANES * (g + 1)]
                fa_ref[g, r0:r0 + rc, :] = zc(2) * zc(0)
                fb_ref[g, r0:r0 + rc, :] = zc(4) * _sigmoid(zc(5))

        def conv(w_ref, state_ref, u_ref, out_ref, new_state_ref, taps):
            hist = taps - 1

            def slab(j_in, g):
                if j_in < hist:
                    return state_ref[j_in, :, LANES * g:LANES * (g + 1)]
                return u_ref[g, pl.ds(j_in - hist, nb, stride=tt), :]

            for t in range(tt):
                for g in range(N_LG):
                    acc = None
                    for k in range(taps):
                        term = w_ref[k:k + 1, LANES * g:LANES * (g + 1)] * slab(t + k, g)
                        acc = term if acc is None else acc + term
                    out_ref[g, pl.ds(t, nb, stride=tt), :] = acc
            for j_out in range(hist):
                for g in range(N_LG):
                    new_state_ref[j_out, :, LANES * g:LANES * (g + 1)] = slab(j_out + tt, g)

        conv(cwa_ref, sa_ref, fa_ref, ca_ref, na_ref, CONV_A)
        conv(cwb_ref, sb_ref, fb_ref, cb_ref, nb_ref, CONV_B)

    for b, c0, r0, rc in ([] if time_major_state else _seq_chunks(0, rows, tt)):
        for g in range(N_LG):
            def zc(c):
                return z_ref[r0:r0 + rc, MIX_W * c + LANES * g:MIX_W * c + LANES * (g + 1)]
            fa_ref[b * N_LG + g, HIST_A + c0:HIST_A + c0 + rc, :] = zc(2) * zc(0)
            fb_ref[b * N_LG + g, HIST_B + c0:HIST_B + c0 + rc, :] = zc(4) * _sigmoid(zc(5))

    for b, c0, r0, rc in ([] if time_major_state else _seq_chunks(0, rows, tt)):
        for g in range(N_LG):
            lg = slice(LANES * g, LANES * (g + 1))
            acc = None
            for k in range(CONV_A):
                off = HIST_A - (CONV_A - 1) + k + c0
                t = cwa_ref[k:k + 1, lg] * fa_ref[b * N_LG + g, off:off + rc, :]
                acc = t if acc is None else acc + t
            ca_ref[g, r0:r0 + rc, :] = acc
            acc = None
            for k in range(CONV_B):
                off = HIST_B - (CONV_B - 1) + k + c0
                t = cwb_ref[k:k + 1, lg] * fb_ref[b * N_LG + g, off:off + rc, :]
                acc = t if acc is None else acc + t
            cb_ref[g, r0:r0 + rc, :] = acc

    for b in range(0 if time_major_state else nb):
        for g in range(N_LG):
            lg = slice(LANES * g, LANES * (g + 1))
            na_ref[b, :, lg] = fa_ref[b * N_LG + g, HIST_A + tt - (CONV_A - 1):HIST_A + tt, :]
            nb_ref[b, :, lg] = fb_ref[b * N_LG + g, HIST_B + tt - (CONV_B - 1):HIST_B + tt, :]
            if nt > 1:
                fa_ref[b * N_LG + g, 0:HIST_A, :] = fa_ref[b * N_LG + g, tt:tt + HIST_A, :]
                fb_ref[b * N_LG + g, 0:HIST_B, :] = fb_ref[b * N_LG + g, tt:tt + HIST_B, :]

    _even_gating(z_ref, ca_ref, cb_ref, cbb_ref, lng_ref, lnb_ref, mix_ref, rows)
    _out_proj_task(mix_ref, wob_ref, o_ref, 0, rows)()
    for task in _residual_tasks(x_ref, mod_ref, seq0, gpost_ref, o_ref, y_ref, 0, rows, tt):
        task()


def _even_layer(x2d, mod, st_a, st_b, g_pre, w_in, cw_a, cw_b, cb_b, ln_g, ln_b, w_out, g_post,
                *, n, t, nb, tt, time_major_state):
    rows = nb * tt
    nt = t // tt
    kern = functools.partial(_even_kernel, nb=nb, tt=tt, nt=nt, time_major_state=time_major_state)
    row_spec = pl.BlockSpec((rows, D_MODEL), lambda i, j: (i * nt + j, 0))

    def seq_spec(shape):
        if time_major_state:
            return pl.BlockSpec((shape[0], nb, shape[1]), lambda i, j: (0, i, 0))
        return pl.BlockSpec((nb,) + shape, lambda i, j: (i, 0, 0))

    def window_scratch(hist):
        if time_major_state:
            return pltpu.VMEM((N_LG, rows, LANES), F32)
        return pltpu.VMEM((nb * N_LG, hist + tt, LANES), F32)

    def const_spec(shape):
        return pl.BlockSpec(shape, lambda i, j: (0,) * len(shape))

    return pl.pallas_call(
        kern,
        out_shape=(
            jax.ShapeDtypeStruct((n * t, D_MODEL), F32),
            jax.ShapeDtypeStruct(st_a.shape, F32),
            jax.ShapeDtypeStruct(st_b.shape, F32),
        ),
        grid=(n // nb, nt),
        in_specs=[
            row_spec,
            const_spec(mod.shape),
            seq_spec((CONV_A - 1, MIX_W)),
            seq_spec((CONV_B - 1, MIX_W)),
            const_spec((1, D_MODEL)),
            const_spec((D_MODEL, EVEN_IN)),
            const_spec((CONV_A, MIX_W)),
            const_spec((CONV_B, MIX_W)),
            const_spec((1, MIX_W)),
            const_spec((1, MIX_W)),
            const_spec((1, MIX_W)),
            const_spec((2 * MIX_W, D_MODEL)),
            const_spec((1, D_MODEL)),
        ],
        out_specs=(row_spec, seq_spec((CONV_A - 1, MIX_W)), seq_spec((CONV_B - 1, MIX_W))),
        scratch_shapes=[
            pltpu.VMEM((rows, D_MODEL), BF16),
            pltpu.VMEM((rows, EVEN_IN), F32),
            pltpu.VMEM((rows, D_MODEL), F32),
            window_scratch(HIST_A),
            window_scratch(HIST_B),
            pltpu.VMEM((N_LG, rows, LANES), F32),
            pltpu.VMEM((N_LG, rows, LANES), F32),
            pltpu.VMEM((rows, 2 * MIX_W), BF16),
            pltpu.VMEM((2 * MIX_W, D_MODEL), BF16),
        ],
        compiler_params=pltpu.CompilerParams(
            dimension_semantics=("arbitrary", "arbitrary"), vmem_limit_bytes=VMEM_LIMIT_BYTES),
        name="even_layer",
    )(x2d, mod, st_a, st_b, g_pre, w_in, cw_a, cw_b, cb_b, ln_g, ln_b, w_out, g_post)


def _odd_kernel(sinks_ref, x_ref, mod_ref, sc_ref, sk_ref, sv_ref, gpre_ref, win_ref, pw_ref, ps_ref,
                wout_ref, gpost_ref,
                y_ref, nc_ref, nk_ref, nv_ref,
                h_ref, z_ref, o_ref, fc_ref, kf_ref, ks_ref, vf_ref, vs_ref, pc_ref, od_ref, mix_ref, wob_ref,
                *, nb, tt, nt, pos0, time_major_pool):
    rows = nb * tt
    j = pl.program_id(1)
    seq0 = pl.program_id(0) * nb
    tile_pos = pos0 + j * tt
    assert not time_major_pool or nt == 1

    @pl.when(j == 0)
    def _load_state():
        for b in range(nb):
            for g in range(0 if time_major_pool else N_LG):
                fc_ref[b * N_LG + g, 0:8, :] = jnp.zeros((8, LANES), F32)
                fc_ref[b * N_LG + g, HIST_C - POOL_PAST:HIST_C, :] = sc_ref[b, :, LANES * g:LANES * (g + 1)]
            k_past, v_past = sk_ref[b].T, sv_ref[b].T
            kf_ref[b, 0:WINDOW, :] = k_past
            vf_ref[b, 0:WINDOW, :] = v_past
            ks_ref[b, 0:WINDOW, :] = pltpu.roll(k_past, HEAD_DIM, axis=1)
            vs_ref[b, 0:WINDOW, :] = pltpu.roll(v_past, HEAD_DIM, axis=1)

    qs = min(tt, WINDOW)
    gs = max(1, ATT_MIN_Q_ROWS // qs)
    kw = WINDOW + qs
    n_q, n_keys = gs * qs, gs * kw
    assert N_REP == 4 and 2 * HEAD_DIM == LANES and KV_W == LANES
    assert qs & (qs - 1) == 0 and gs & (gs - 1) == 0 and nb % gs == 0 and (gs == 1 or tt == qs)
    shape = (2 * n_q, n_keys)
    ri = lax.broadcasted_iota(jnp.int32, shape, 0)
    ci = lax.broadcasted_iota(jnp.int32, shape, 1)
    q_seq = (ri >> (qs.bit_length() - 1)) & (gs - 1)
    k_seq = jnp.zeros(shape, jnp.int32)
    for s_i in range(1, gs):
        k_seq = k_seq + (ci >= s_i * kw).astype(jnp.int32)
    k_c = ci - k_seq * kw
    dist = (ri & (qs - 1)) + WINDOW - k_c
    band = (q_seq == k_seq) & (dist >= 0) & (dist < WINDOW)
    dist_f = dist.astype(F32)
    upper_grp = ri >= n_q
    upper_grp_col = lax.broadcasted_iota(jnp.int32, (2 * n_q, 1), 0) >= n_q
    low_lanes = lax.broadcasted_iota(jnp.int32, (n_keys, LANES), 1) < HEAD_DIM
    ones_cols = jnp.ones((n_keys, LANES), BF16)

    def head_of(g, grp, parity):
        return g * N_REP + 2 * grp + parity

    def alibi_bias(g, parity):
        slope = jnp.where(upper_grp, ALIBI_SLOPES[head_of(g, 1, parity)], ALIBI_SLOPES[head_of(g, 0, parity)])
        return jnp.where(band, -(slope * dist_f), -jnp.inf)

    biases = {(g, parity): alibi_bias(g, parity) for g in range(N_KV) for parity in range(2)}

    def attention(b0, sb, g):
        r0 = b0 * tt + sb * qs
        p0 = tile_pos + sb * qs
        in_range = (k_c >= WINDOW - p0) if pos0 + sb * qs < WINDOW else None
        if gs == 1:
            win = (b0, slice(sb * qs, sb * qs + kw))
            k_nat, k_swp, v_nat, v_swp = kf_ref[win], ks_ref[win], vf_ref[win], vs_ref[win]
        else:
            k_nat, k_swp, v_nat, v_swp = (r[b0:b0 + gs].reshape(n_keys, KV_W)
                                          for r in (kf_ref, ks_ref, vf_ref, vs_ref))
        k_lo, k_hi = (k_nat, k_swp) if g == 0 else (k_swp, k_nat)
        v_lo, v_hi = (v_nat, v_swp) if g == 0 else (v_swp, v_nat)
        v_ext = jnp.concatenate([jnp.where(low_lanes, v_lo, v_hi).astype(BF16), ones_cols], axis=1)
        qg = jnp.concatenate([z_ref[r0:r0 + n_q, Q_COL + 2 * LANES * g + LANES * grp:
                                    Q_COL + 2 * LANES * g + LANES * (grp + 1)] for grp in range(2)], axis=0)
        qg = (qg * (HEAD_DIM ** -0.5)).astype(BF16)
        for parity in range(2):
            k_ext = (jnp.where(low_lanes, k_lo, 0.0) if parity == 0 else jnp.where(low_lanes, 0.0, k_hi)).astype(BF16)
            sink = jnp.where(upper_grp_col, sinks_ref[head_of(g, 1, parity)], sinks_ref[head_of(g, 0, parity)])
            sc = lax.dot_general(qg, k_ext, (((1,), (1,)), ((), ())), preferred_element_type=F32)
            bias = biases[g, parity]
            sc = sc + (bias if in_range is None else jnp.where(in_range, bias, -jnp.inf))
            m = jnp.maximum(jnp.max(sc, axis=-1, keepdims=True), sink)
            p = jnp.exp(sc - m)
            o_ext = jnp.dot(p.astype(BF16), v_ext, preferred_element_type=F32)
            o = o_ext[:, 0:LANES] / (o_ext[:, LANES:2 * LANES] + jnp.exp(sink - m))
            for grp in range(2):
                h = head_of(g, grp, parity)
                od_ref[r0:r0 + n_q, HEAD_DIM * h:HEAD_DIM * (h + 1)] = (
                    o[grp * n_q:(grp + 1) * n_q, HEAD_DIM * parity:HEAD_DIM * (parity + 1)])

    def window_inputs(b, c0, r0, rc):
        for g in range(0 if time_major_pool else N_LG):
            fc_ref[b * N_LG + g, HIST_C + c0:HIST_C + c0 + rc, :] = z_ref[r0:r0 + rc, LANES * g:LANES * (g + 1)]
        k = z_ref[r0:r0 + rc, K_COL:K_COL + KV_W]
        v = z_ref[r0:r0 + rc, V_COL:V_COL + KV_W]
        kf_ref[b, WINDOW + c0:WINDOW + c0 + rc, :] = k
        vf_ref[b, WINDOW + c0:WINDOW + c0 + rc, :] = v
        ks_ref[b, WINDOW + c0:WINDOW + c0 + rc, :] = pltpu.roll(k, HEAD_DIM, axis=1)
        vs_ref[b, WINDOW + c0:WINDOW + c0 + rc, :] = pltpu.roll(v, HEAD_DIM, axis=1)

    def pooling(b, c0, r0, rc):
        pos = tile_pos + c0 + lax.broadcasted_iota(jnp.int32, (rc, LANES), 0)
        for g, w in enumerate(POOL_WINDOWS):
            u = fc_ref[b * N_LG + g, HIST_C + c0:HIST_C + c0 + rc, :]
            acc = u
            for d in range(1, w):
                acc = acc + fc_ref[b * N_LG + g, HIST_C + c0 - d:HIST_C + c0 - d + rc, :]
            cnt = jnp.minimum(w, pos + 1).astype(F32)
            pc_ref[g, r0:r0 + rc, :] = acc / cnt - u

    def pooling_time_major(lo, n):
        b_lo, n_seq = lo // tt, n // tt

        for g in range(N_LG):
            fc_ref[g, lo:lo + n, :] = z_ref[lo:lo + n, LANES * g:LANES * (g + 1)]

        def slab(j_in, g):
            if j_in < POOL_PAST:
                return sc_ref[j_in, b_lo:b_lo + n_seq, LANES * g:LANES * (g + 1)]
            return fc_ref[g, pl.ds(lo + j_in - POOL_PAST, n_seq, stride=tt), :]

        for t in range(tt):
            for g, w in enumerate(POOL_WINDOWS):
                u = slab(POOL_PAST + t, g)
                acc = u
                for d in range(1, w):
                    acc = acc + slab(POOL_PAST + t - d, g)
                pc_ref[g, pl.ds(lo + t, n_seq, stride=tt), :] = acc / float(min(w, pos0 + t + 1)) - u
        for j_out in range(POOL_PAST):
            for g in range(N_LG):
                nc_ref[j_out, b_lo:b_lo + n_seq, LANES * g:LANES * (g + 1)] = slab(j_out + tt, g)

    def gating(r0, rc):
        for g in range(N_LG):
            lg = slice(LANES * g, LANES * (g + 1))
            mixed = jnp.dot(pc_ref[g, r0:r0 + rc, :].astype(BF16), pw_ref[g], preferred_element_type=F32)
            yc = mixed * ps_ref[:, lg] * _silu(z_ref[r0:r0 + rc, MIX_W + LANES * g:MIX_W + LANES * (g + 1)])
            mix_ref[r0:r0 + rc, lg] = yc.astype(BF16)
        yd = od_ref[r0:r0 + rc, 0:MIX_W] * _silu(z_ref[r0:r0 + rc, DG_COL:DG_COL + MIX_W])
        mix_ref[r0:r0 + rc, MIX_W:2 * MIX_W] = yd.astype(BF16)

    def mixer_tasks(lo, n):
        chunks = _seq_chunks(lo, n, tt)
        if gs == 1:
            blocks = [(r0 // tt, (r0 % tt) // qs) for r0 in range(lo, lo + n, qs)]
        else:
            blocks = [(b0, 0) for b0 in range(lo // tt, (lo + n) // tt, gs)]
        pool_tasks = ([functools.partial(pooling_time_major, lo, n)] if time_major_pool
                      else [functools.partial(pooling, *ch) for ch in chunks])
        return ([functools.partial(window_inputs, *ch) for ch in chunks]
                + pool_tasks
                + [functools.partial(attention, b0, sb, g) for b0, sb in blocks for g in range(N_KV)]
                + [functools.partial(gating, lo, n)])

    _pre_norm(x_ref, mod_ref, seq0, gpre_ref, h_ref, rows, tt)
    for c0 in range(0, ODD_IN, MXU_COLS):
        z_ref[:, c0:c0 + MXU_COLS] = jnp.dot(h_ref[...], win_ref[:, c0:c0 + MXU_COLS].astype(BF16),
                                             preferred_element_type=F32)
    wob_ref[...] = wout_ref[...].astype(BF16)

    n_chunks = max(1, rows // WIN_CHUNK)
    n = rows // n_chunks
    for c in range(n_chunks + 1):
        if c > 0:
            _out_proj_task(mix_ref, wob_ref, o_ref, (c - 1) * n, n)()
        if c < n_chunks:
            for task in mixer_tasks(c * n, n):
                task()
        if c > 0:
            for task in _residual_tasks(x_ref, mod_ref, seq0, gpost_ref, o_ref, y_ref, (c - 1) * n, n, tt):
                task()

    for b in range(nb):
        for g in range(0 if time_major_pool else N_LG):
            nc_ref[b, :, LANES * g:LANES * (g + 1)] = fc_ref[b * N_LG + g, HIST_C + tt - POOL_PAST:HIST_C + tt, :]
            if nt > 1:
                fc_ref[b * N_LG + g, 0:HIST_C, :] = fc_ref[b * N_LG + g, tt:tt + HIST_C, :]
        nk_ref[b] = kf_ref[b, tt:tt + WINDOW, :].T
        nv_ref[b] = vf_ref[b, tt:tt + WINDOW, :].T
        if nt > 1:
            for r in (kf_ref, ks_ref, vf_ref, vs_ref):
                r[b, 0:WINDOW, :] = r[b, tt:tt + WINDOW, :]


def _odd_layer(x2d, mod, st_c, st_k, st_v, g_pre, w_in, pool_w, pool_scale, sinks, w_out, g_post,
               *, n, t, nb, tt, pos0, time_major_pool):
    rows = nb * tt
    nt = t // tt
    kern = functools.partial(_odd_kernel, nb=nb, tt=tt, nt=nt, pos0=pos0, time_major_pool=time_major_pool)
    row_spec = pl.BlockSpec((rows, D_MODEL), lambda i, j: (i * nt + j, 0))

    def seq_spec(shape):
        return pl.BlockSpec((nb,) + shape, lambda i, j: (i, 0, 0))

    def const_spec(shape):
        return pl.BlockSpec(shape, lambda i, j: (0,) * len(shape))

    if time_major_pool:
        pool_spec = pl.BlockSpec((POOL_PAST, nb, MIX_W), lambda i, j: (0, i, 0))
    else:
        pool_spec = seq_spec((POOL_PAST, MIX_W))

    return pl.pallas_call(
        kern,
        out_shape=(
            jax.ShapeDtypeStruct((n * t, D_MODEL), F32),
            jax.ShapeDtypeStruct(st_c.shape, F32),
            jax.ShapeDtypeStruct((n, KV_W, WINDOW), F32),
            jax.ShapeDtypeStruct((n, KV_W, WINDOW), F32),
        ),
        grid=(n // nb, nt),
        in_specs=[
            pl.BlockSpec(memory_space=pltpu.SMEM),
            row_spec,
            const_spec(mod.shape),
            pool_spec,
            seq_spec((KV_W, WINDOW)),
            seq_spec((KV_W, WINDOW)),
            const_spec((1, D_MODEL)),
            const_spec((D_MODEL, ODD_IN)),
            const_spec((N_LG, LANES, LANES)),
            const_spec((1, MIX_W)),
            const_spec((2 * MIX_W, D_MODEL)),
            const_spec((1, D_MODEL)),
        ],
        out_specs=(row_spec, pool_spec, seq_spec((KV_W, WINDOW)), seq_spec((KV_W, WINDOW))),
        scratch_shapes=[
            pltpu.VMEM((rows, D_MODEL), BF16),
            pltpu.VMEM((rows, ODD_IN), F32),
            pltpu.VMEM((rows, D_MODEL), F32),
            pltpu.VMEM((N_LG, rows, LANES) if time_major_pool else (nb * N_LG, HIST_C + tt, LANES), F32),
            pltpu.VMEM((nb, WINDOW + tt, KV_W), F32),
            pltpu.VMEM((nb, WINDOW + tt, KV_W), F32),
            pltpu.VMEM((nb, WINDOW + tt, KV_W), F32),
            pltpu.VMEM((nb, WINDOW + tt, KV_W), F32),
            pltpu.VMEM((N_LG, rows, LANES), F32),
            pltpu.VMEM((rows, MIX_W), F32),
            pltpu.VMEM((rows, 2 * MIX_W), BF16),
            pltpu.VMEM((2 * MIX_W, D_MODEL), BF16),
        ],
        compiler_params=pltpu.CompilerParams(
            dimension_semantics=("arbitrary", "arbitrary"), vmem_limit_bytes=VMEM_LIMIT_BYTES),
        name="odd_layer",
    )(sinks, x2d, mod, st_c, st_k, st_v, g_pre, w_in, pool_w, pool_scale, w_out, g_post)


def _tiling(n, t):
    if t >= TILE_ROWS:
        return 1, TILE_ROWS
    return min(n, SHORT_TILE_ROWS // t), t


def _trunk(x, mod_e, mod_o, st_a, st_b, st_c, st_k, st_v, pos0, we, wo):
    n, t, _ = x.shape
    nb, tt = _tiling(n, t)
    short = t == tt and nb > 1

    def swap(st):
        return jnp.transpose(st, (1, 0, 2)) if short else st

    x2d, na, nbs = _even_layer(x.reshape(n * t, D_MODEL), mod_e, swap(st_a), swap(st_b), *we, n=n, t=t, nb=nb,
                               tt=tt, time_major_state=short)
    def channel_major(cache):
        return jnp.transpose(cache, (0, 2, 3, 1)).reshape(n, KV_W, WINDOW)

    def position_major(cache):
        return jnp.transpose(cache.reshape(n, N_KV, HEAD_DIM, WINDOW), (0, 3, 1, 2))

    x2d, nc, nk, nv = _odd_layer(x2d, mod_o, swap(st_c), channel_major(st_k), channel_major(st_v), *wo,
                                 n=n, t=t, nb=nb, tt=tt, pos0=pos0, time_major_pool=short)
    na, nbs, nc = swap(na), swap(nbs), swap(nc)
    return (x2d.reshape(n, t, D_MODEL), na[None], nbs[None], nc[None],
            position_major(nk)[None], position_major(nv)[None])


def kernel(x_prompt, x_sample, state_conv_a, state_conv_b, state_pool_c, cache_win_k, cache_win_v, c_prompt, c_sample, w_mod_e, b_mod_e, g_pre_e, g_post_e, w_in_e, conv_a_w, conv_b_w, conv_b_b, ln_b_g, ln_b_b, w_out_e, w_mod_o, b_mod_o, g_pre_o, g_post_o, w_in_o, pool_w, pool_scale, sinks, w_out_o):
    n_p = x_prompt.shape[0]
    assert w_in_e.shape[0] == 1 and w_in_o.shape[0] == 1, "one even and one odd layer"

    mod_e_p, mod_e_s = _adaln_mod(c_prompt, c_sample, w_mod_e[0], b_mod_e[0])
    mod_o_p, mod_o_s = _adaln_mod(c_prompt, c_sample, w_mod_o[0], b_mod_o[0])

    we = (g_pre_e, w_in_e[0], conv_a_w[0], conv_b_w[0], conv_b_b, ln_b_g, ln_b_b,
          w_out_e[0], g_post_e)
    wo = (g_pre_o, w_in_o[0], pool_w[0].astype(BF16), pool_scale, sinks[0],
          w_out_o[0], g_post_o)

    dt = x_prompt.dtype
    z_a = jnp.zeros((n_p, CONV_A - 1, MIX_W), dt)
    z_b = jnp.zeros((n_p, CONV_B - 1, MIX_W), dt)
    z_c = jnp.zeros((n_p, POOL_PAST, MIX_W), dt)
    z_kv = jnp.zeros((n_p, WINDOW, N_KV, HEAD_DIM), dt)
    y_p, pa, pb, pc, pk, pv = _trunk(x_prompt, mod_e_p, mod_o_p, z_a, z_b, z_c, z_kv, z_kv, 0, we, wo)
    y_s, sa, sb, sc, sk, sv = _trunk(x_sample, mod_e_s, mod_o_s, state_conv_a[0], state_conv_b[0],
                                     state_pool_c[0], cache_win_k[0], cache_win_v[0], PAST_LEN, we, wo)
    return (y_p, y_s, pa, sa, pb, sb, pc, sc, pk, sk, pv, sv)
```

```python
import functools

import jax
import jax.numpy as jnp
from jax import lax
from jax.experimental import pallas as pl
from jax.experimental.pallas import tpu as pltpu

F32 = jnp.float32
BF16 = jnp.bfloat16

D_MODEL = 1024
MIX_W = 512
LANES = 128
MXU_COLS = 256
N_LG = MIX_W // LANES
HEAD_DIM = 64
N_HEADS = 8
N_KV = 2
N_REP = N_HEADS // N_KV
KV_W = N_KV * HEAD_DIM
CONV_A = 3
CONV_B = 31
POOL_WINDOWS = (2, 4, 8, 16)
POOL_PAST = 15
WINDOW = 128
PAST_LEN = 8192
EVEN_IN = 7 * MIX_W
ODD_IN = 4 * MIX_W + 2 * KV_W
Q_COL, K_COL, V_COL, DG_COL = 2 * MIX_W, 3 * MIX_W, 3 * MIX_W + KV_W, 3 * MIX_W + 2 * KV_W
RMS_EPS = 1e-6
LN_EPS = 1e-5
HIST_A = 8
HIST_B = 32
HIST_C = 16
ATT_MIN_Q_ROWS = 32
TILE_ROWS = 1024
SHORT_TILE_ROWS = 256
ROW_CHUNK = 64
WIN_CHUNK = 128
MOD_K_BLOCK = 512
VMEM_LIMIT_BYTES = 56 * 1024 * 1024
NEG_LOG2_E = -1.4426950408889634
ALIBI_SLOPES = tuple(float(2.0 ** (-8.0 * (h + 1) / N_HEADS)) for h in range(N_HEADS))


def _sigmoid(x):
    return 1.0 / (1.0 + jnp.exp2(x * NEG_LOG2_E))


def _silu(x):
    return x * _sigmoid(x)


def _mod_kernel(cp_ref, cs_ref, w_ref, b_ref, op_ref, os_ref):
    k = pl.program_id(0)
    n_s = cs_ref.shape[0]
    c = jnp.concatenate([cs_ref[...], cp_ref[...]], axis=0)
    part = jnp.dot(_silu(c).astype(BF16), w_ref[...].astype(BF16), preferred_element_type=F32)
    for o_ref, rows in ((os_ref, slice(0, n_s)), (op_ref, slice(n_s, c.shape[0]))):
        @pl.when(k == 0)
        def _first():
            o_ref[...] = part[rows] + b_ref[...]

        @pl.when(k > 0)
        def _rest():
            o_ref[...] += part[rows]


def _adaln_mod(c_prompt, c_sample, w_mod, b_mod):
    def c_spec(c):
        return pl.BlockSpec((c.shape[0], MOD_K_BLOCK), lambda k: (0, k))

    def out_spec(c):
        return pl.BlockSpec((c.shape[0], 3 * D_MODEL), lambda k: (0, 0))

    return pl.pallas_call(
        _mod_kernel,
        out_shape=tuple(jax.ShapeDtypeStruct((c.shape[0], 3 * D_MODEL), F32) for c in (c_prompt, c_sample)),
        grid=(D_MODEL // MOD_K_BLOCK,),
        in_specs=[
            c_spec(c_prompt),
            c_spec(c_sample),
            pl.BlockSpec((MOD_K_BLOCK, 3 * D_MODEL), lambda k: (k, 0)),
            pl.BlockSpec((1, 3 * D_MODEL), lambda k: (0, 0)),
        ],
        out_specs=(out_spec(c_prompt), out_spec(c_sample)),
        compiler_params=pltpu.CompilerParams(dimension_semantics=("arbitrary",)),
        name="adaln_mod",
    )(c_prompt, c_sample, w_mod, b_mod.reshape(1, -1))


def _mod_rows(mod_ref, seq0, col, r0, rc, tt):
    if tt >= rc:
        return mod_ref[pl.ds(seq0 + r0 // tt, 1), col:col + D_MODEL]
    parts = [jnp.broadcast_to(mod_ref[pl.ds(seq0 + b, 1), col:col + D_MODEL], (tt, D_MODEL))
             for b in range(r0 // tt, (r0 + rc) // tt)]
    return jnp.concatenate(parts, axis=0)


def _pre_norm(x_ref, mod_ref, seq0, g_ref, h_ref, rows, tt):
    rc = min(rows, ROW_CHUNK)
    for r0 in range(0, rows, rc):
        x = x_ref[r0:r0 + rc, :]
        ms = jnp.mean(x * x, axis=-1, keepdims=True)
        gain = g_ref[...] * (1.0 + _mod_rows(mod_ref, seq0, D_MODEL, r0, rc, tt))
        h = x * lax.rsqrt(ms + RMS_EPS) * gain + _mod_rows(mod_ref, seq0, 0, r0, rc, tt)
        h_ref[r0:r0 + rc, :] = h.astype(BF16)


def _out_proj_task(mix_ref, wout_ref, o_ref, lo, n):
    def task():
        o_ref[lo:lo + n, :] = jnp.dot(mix_ref[lo:lo + n, :], wout_ref[...], preferred_element_type=F32)
    return task


def _residual_tasks(x_ref, mod_ref, seq0, g_ref, o_ref, y_ref, lo, n, tt):
    rc = min(n, ROW_CHUNK)

    def chunk(r0):
        o = o_ref[r0:r0 + rc, :]
        ms = jnp.mean(o * o, axis=-1, keepdims=True)
        gain = g_ref[...] * _mod_rows(mod_ref, seq0, 2 * D_MODEL, r0, rc, tt)
        y_ref[r0:r0 + rc, :] = x_ref[r0:r0 + rc, :] + o * lax.rsqrt(ms + RMS_EPS) * gain
    return [functools.partial(chunk, r0) for r0 in range(lo, lo + n, rc)]


def _seq_chunks(lo, n, tt):
    rc = min(tt, n, WIN_CHUNK)
    return [(r0 // tt, r0 % tt, r0, rc) for r0 in range(lo, lo + n, rc)]


def _even_gating(z_ref, ca_ref, cb_ref, cbb_ref, lng_ref, lnb_ref, mix_ref, rows):
    rc = min(rows, ROW_CHUNK)
    for r0 in range(0, rows, rc):
        def zs(c):
            return z_ref[r0:r0 + rc, MIX_W * c:MIX_W * (c + 1)]

        def all_lanes(c_ref):
            return jnp.concatenate([c_ref[g, r0:r0 + rc, :] for g in range(N_LG)], axis=1)
        ya = zs(1) * all_lanes(ca_ref) * _silu(zs(3))
        mix_ref[r0:r0 + rc, 0:MIX_W] = ya.astype(BF16)
        yb = all_lanes(cb_ref) + cbb_ref[...]
        mu = jnp.mean(yb, axis=-1, keepdims=True)
        yc = yb - mu
        var = jnp.mean(yc * yc, axis=-1, keepdims=True)
        ln = yc * lax.rsqrt(var + LN_EPS) * lng_ref[...] + lnb_ref[...]
        mix_ref[r0:r0 + rc, MIX_W:2 * MIX_W] = (_silu(ln) * _silu(zs(6))).astype(BF16)


def _even_kernel(x_ref, mod_ref, sa_ref, sb_ref, gpre_ref, win_ref, cwa_ref, cwb_ref, cbb_ref,
                 lng_ref, lnb_ref, wout_ref, gpost_ref,
                 y_ref, na_ref, nb_ref,
                 h_ref, z_ref, o_ref, fa_ref, fb_ref, ca_ref, cb_ref, mix_ref, wob_ref,
                 *, nb, tt, nt, time_major_state):
    rows = nb * tt
    j = pl.program_id(1)
    seq0 = pl.program_id(0) * nb
    assert not time_major_state or nt == 1

    @pl.when(j == 0)
    def _load_state():
        for b in range(0 if time_major_state else nb):
            for g in range(N_LG):
                lg = slice(LANES * g, LANES * (g + 1))
                fa_ref[b * N_LG + g, 0:HIST_A, :] = jnp.zeros((HIST_A, LANES), F32)
                fa_ref[b * N_LG + g, HIST_A - (CONV_A - 1):HIST_A, :] = sa_ref[b, :, lg]
                fb_ref[b * N_LG + g, 0:8, :] = jnp.zeros((8, LANES), F32)
                fb_ref[b * N_LG + g, HIST_B - (CONV_B - 1):HIST_B, :] = sb_ref[b, :, lg]

    _pre_norm(x_ref, mod_ref, seq0, gpre_ref, h_ref, rows, tt)

    split_order = (4, 5, 0, 2, 1, 3, 6)
    for c0 in [MIX_W * s + d for s in split_order for d in range(0, MIX_W, MXU_COLS)]:
        z_ref[:, c0:c0 + MXU_COLS] = jnp.dot(h_ref[...], win_ref[:, c0:c0 + MXU_COLS].astype(BF16),
                                             preferred_element_type=F32)
    wob_ref[...] = wout_ref[...].astype(BF16)

    if time_major_state:
        rc = min(rows, WIN_CHUNK)
        for r0 in range(0, rows, rc):
            for g in range(N_LG):
                def zc(c):
                    return z_ref[r0:r0 + rc, MIX_W * c + LANES * g:MIX_W * c + LANES * (g + 1)]
                fa_ref[g, r0:r0 + rc, :] = zc(2) * zc(0)
                fb_ref[g, r0:r0 + rc, :] = zc(4) * _sigmoid(zc(5))

        def conv(w_ref, state_ref, u_ref, out_ref, new_state_ref, taps):
            hist = taps - 1

            def slab(j_in, g):
                if j_in < hist:
                    return state_ref[j_in, :, LANES * g:LANES * (g + 1)]
                return u_ref[g, pl.ds(j_in - hist, nb, stride=tt), :]

            for t in range(tt):
                for g in range(N_LG):
                    acc = None
                    for k in range(taps):
                        term = w_ref[k:k + 1, LANES * g:LANES * (g + 1)] * slab(t + k, g)
                        acc = term if acc is None else acc + term
                    out_ref[g, pl.ds(t, nb, stride=tt), :] = acc
            for j_out in range(hist):
                for g in range(N_LG):
                    new_state_ref[j_out, :, LANES * g:LANES * (g + 1)] = slab(j_out + tt, g)

        conv(cwa_ref, sa_ref, fa_ref, ca_ref, na_ref, CONV_A)
        conv(cwb_ref, sb_ref, fb_ref, cb_ref, nb_ref, CONV_B)

    for b, c0, r0, rc in ([] if time_major_state else _seq_chunks(0, rows, tt)):
        for g in range(N_LG):
            def zc(c):
                return z_ref[r0:r0 + rc, MIX_W * c + LANES * g:MIX_W * c + LANES * (g + 1)]
            fa_ref[b * N_LG + g, HIST_A + c0:HIST_A + c0 + rc, :] = zc(2) * zc(0)
            fb_ref[b * N_LG + g, HIST_B + c0:HIST_B + c0 + rc, :] = zc(4) * _sigmoid(zc(5))

    for b, c0, r0, rc in ([] if time_major_state else _seq_chunks(0, rows, tt)):
        for g in range(N_LG):
            lg = slice(LANES * g, LANES * (g + 1))
            acc = None
            for k in range(CONV_A):
                off = HIST_A - (CONV_A - 1) + k + c0
                t = cwa_ref[k:k + 1, lg] * fa_ref[b * N_LG + g, off:off + rc, :]
                acc = t if acc is None else acc + t
            ca_ref[g, r0:r0 + rc, :] = acc
            acc = None
            for k in range(CONV_B):
                off = HIST_B - (CONV_B - 1) + k + c0
                t = cwb_ref[k:k + 1, lg] * fb_ref[b * N_LG + g, off:off + rc, :]
                acc = t if acc is None else acc + t
            cb_ref[g, r0:r0 + rc, :] = acc

    for b in range(0 if time_major_state else nb):
        for g in range(N_LG):
            lg = slice(LANES * g, LANES * (g + 1))
            na_ref[b, :, lg] = fa_ref[b * N_LG + g, HIST_A + tt - (CONV_A - 1):HIST_A + tt, :]
            nb_ref[b, :, lg] = fb_ref[b * N_LG + g, HIST_B + tt - (CONV_B - 1):HIST_B + tt, :]
            if nt > 1:
                fa_ref[b * N_LG + g, 0:HIST_A, :] = fa_ref[b * N_LG + g, tt:tt + HIST_A, :]
                fb_ref[b * N_LG + g, 0:HIST_B, :] = fb_ref[b * N_LG + g, tt:tt + HIST_B, :]

    _even_gating(z_ref, ca_ref, cb_ref, cbb_ref, lng_ref, lnb_ref, mix_ref, rows)
    _out_proj_task(mix_ref, wob_ref, o_ref, 0, rows)()
    for task in _residual_tasks(x_ref, mod_ref, seq0, gpost_ref, o_ref, y_ref, 0, rows, tt):
        task()


def _even_layer(x2d, mod, st_a, st_b, g_pre, w_in, cw_a, cw_b, cb_b, ln_g, ln_b, w_out, g_post,
                *, n, t, nb, tt, time_major_state):
    rows = nb * tt
    nt = t // tt
    kern = functools.partial(_even_kernel, nb=nb, tt=tt, nt=nt, time_major_state=time_major_state)
    row_spec = pl.BlockSpec((rows, D_MODEL), lambda i, j: (i * nt + j, 0))

    def seq_spec(shape):
        if time_major_state:
            return pl.BlockSpec((shape[0], nb, shape[1]), lambda i, j: (0, i, 0))
        return pl.BlockSpec((nb,) + shape, lambda i, j: (i, 0, 0))

    def window_scratch(hist):
        if time_major_state:
            return pltpu.VMEM((N_LG, rows, LANES), F32)
        return pltpu.VMEM((nb * N_LG, hist + tt, LANES), F32)

    def const_spec(shape):
        return pl.BlockSpec(shape, lambda i, j: (0,) * len(shape))

    return pl.pallas_call(
        kern,
        out_shape=(
            jax.ShapeDtypeStruct((n * t, D_MODEL), F32),
            jax.ShapeDtypeStruct(st_a.shape, F32),
            jax.ShapeDtypeStruct(st_b.shape, F32),
        ),
        grid=(n // nb, nt),
        in_specs=[
            row_spec,
            const_spec(mod.shape),
            seq_spec((CONV_A - 1, MIX_W)),
            seq_spec((CONV_B - 1, MIX_W)),
            const_spec((1, D_MODEL)),
            const_spec((D_MODEL, EVEN_IN)),
            const_spec((CONV_A, MIX_W)),
            const_spec((CONV_B, MIX_W)),
            const_spec((1, MIX_W)),
            const_spec((1, MIX_W)),
            const_spec((1, MIX_W)),
            const_spec((2 * MIX_W, D_MODEL)),
            const_spec((1, D_MODEL)),
        ],
        out_specs=(row_spec, seq_spec((CONV_A - 1, MIX_W)), seq_spec((CONV_B - 1, MIX_W))),
        scratch_shapes=[
            pltpu.VMEM((rows, D_MODEL), BF16),
            pltpu.VMEM((rows, EVEN_IN), F32),
            pltpu.VMEM((rows, D_MODEL), F32),
            window_scratch(HIST_A),
            window_scratch(HIST_B),
            pltpu.VMEM((N_LG, rows, LANES), F32),
            pltpu.VMEM((N_LG, rows, LANES), F32),
            pltpu.VMEM((rows, 2 * MIX_W), BF16),
            pltpu.VMEM((2 * MIX_W, D_MODEL), BF16),
        ],
        compiler_params=pltpu.CompilerParams(
            dimension_semantics=("arbitrary", "arbitrary"), vmem_limit_bytes=VMEM_LIMIT_BYTES),
        name="even_layer",
    )(x2d, mod, st_a, st_b, g_pre, w_in, cw_a, cw_b, cb_b, ln_g, ln_b, w_out, g_post)


def _odd_kernel(sinks_ref, x_ref, mod_ref, sc_ref, sk_ref, sv_ref, gpre_ref, win_ref, pw_ref, ps_ref,
                wout_ref, gpost_ref,
                y_ref, nc_ref, nk_ref, nv_ref,
                h_ref, z_ref, o_ref, fc_ref, kf_ref, ks_ref, vf_ref, vs_ref, pc_ref, od_ref, mix_ref, wob_ref,
                *, nb, tt, nt, pos0, time_major_pool):
    rows = nb * tt
    j = pl.program_id(1)
    seq0 = pl.program_id(0) * nb
    tile_pos = pos0 + j * tt
    assert not time_major_pool or nt == 1

    @pl.when(j == 0)
    def _load_state():
        for b in range(nb):
            for g in range(0 if time_major_pool else N_LG):
                fc_ref[b * N_LG + g, 0:8, :] = jnp.zeros((8, LANES), F32)
                fc_ref[b * N_LG + g, HIST_C - POOL_PAST:HIST_C, :] = sc_ref[b, :, LANES * g:LANES * (g + 1)]
            k_past, v_past = sk_ref[b].T, sv_ref[b].T
            kf_ref[b, 0:WINDOW, :] = k_past
            vf_ref[b, 0:WINDOW, :] = v_past
            ks_ref[b, 0:WINDOW, :] = pltpu.roll(k_past, HEAD_DIM, axis=1)
            vs_ref[b, 0:WINDOW, :] = pltpu.roll(v_past, HEAD_DIM, axis=1)

    qs = min(tt, WINDOW)
    gs = max(1, ATT_MIN_Q_ROWS // qs)
    kw = WINDOW + qs
    n_q, n_keys = gs * qs, gs * kw
    assert N_REP == 4 and 2 * HEAD_DIM == LANES and KV_W == LANES
    assert qs & (qs - 1) == 0 and gs & (gs - 1) == 0 and nb % gs == 0 and (gs == 1 or tt == qs)
    shape = (2 * n_q, n_keys)
    ri = lax.broadcasted_iota(jnp.int32, shape, 0)
    ci = lax.broadcasted_iota(jnp.int32, shape, 1)
    q_seq = (ri >> (qs.bit_length() - 1)) & (gs - 1)
    k_seq = jnp.zeros(shape, jnp.int32)
    for s_i in range(1, gs):
        k_seq = k_seq + (ci >= s_i * kw).astype(jnp.int32)
    k_c = ci - k_seq * kw
    dist = (ri & (qs - 1)) + WINDOW - k_c
    band = (q_seq == k_seq) & (dist >= 0) & (dist < WINDOW)
    dist_f = dist.astype(F32)
    upper_grp = ri >= n_q
    upper_grp_col = lax.broadcasted_iota(jnp.int32, (2 * n_q, 1), 0) >= n_q
    low_lanes = lax.broadcasted_iota(jnp.int32, (n_keys, LANES), 1) < HEAD_DIM
    ones_cols = jnp.ones((n_keys, LANES), BF16)

    def head_of(g, grp, parity):
        return g * N_REP + 2 * grp + parity

    def alibi_bias(g, parity):
        slope = jnp.where(upper_grp, ALIBI_SLOPES[head_of(g, 1, parity)], ALIBI_SLOPES[head_of(g, 0, parity)])
        return jnp.where(band, -(slope * dist_f), -jnp.inf)

    biases = {(g, parity): alibi_bias(g, parity) for g in range(N_KV) for parity in range(2)}

    def attention(b0, sb, g):
        r0 = b0 * tt + sb * qs
        p0 = tile_pos + sb * qs
        in_range = (k_c >= WINDOW - p0) if pos0 + sb * qs < WINDOW else None
        if gs == 1:
            win = (b0, slice(sb * qs, sb * qs + kw))
            k_nat, k_swp, v_nat, v_swp = kf_ref[win], ks_ref[win], vf_ref[win], vs_ref[win]
        else:
            k_nat, k_swp, v_nat, v_swp = (r[b0:b0 + gs].reshape(n_keys, KV_W)
                                          for r in (kf_ref, ks_ref, vf_ref, vs_ref))
        k_lo, k_hi = (k_nat, k_swp) if g == 0 else (k_swp, k_nat)
        v_lo, v_hi = (v_nat, v_swp) if g == 0 else (v_swp, v_nat)
        v_ext = jnp.concatenate([jnp.where(low_lanes, v_lo, v_hi).astype(BF16), ones_cols], axis=1)
        qg = jnp.concatenate([z_ref[r0:r0 + n_q, Q_COL + 2 * LANES * g + LANES * grp:
                                    Q_COL + 2 * LANES * g + LANES * (grp + 1)] for grp in range(2)], axis=0)
        qg = (qg * (HEAD_DIM ** -0.5)).astype(BF16)
        for parity in range(2):
            k_ext = (jnp.where(low_lanes, k_lo, 0.0) if parity == 0 else jnp.where(low_lanes, 0.0, k_hi)).astype(BF16)
            sink = jnp.where(upper_grp_col, sinks_ref[head_of(g, 1, parity)], sinks_ref[head_of(g, 0, parity)])
            sc = lax.dot_general(qg, k_ext, (((1,), (1,)), ((), ())), preferred_element_type=F32)
            bias = biases[g, parity]
            sc = sc + (bias if in_range is None else jnp.where(in_range, bias, -jnp.inf))
            m = jnp.maximum(jnp.max(sc, axis=-1, keepdims=True), sink)
            p = jnp.exp(sc - m)
            o_ext = jnp.dot(p.astype(BF16), v_ext, preferred_element_type=F32)
            o = o_ext[:, 0:LANES] / (o_ext[:, LANES:2 * LANES] + jnp.exp(sink - m))
            for grp in range(2):
                h = head_of(g, grp, parity)
                od_ref[r0:r0 + n_q, HEAD_DIM * h:HEAD_DIM * (h + 1)] = (
                    o[grp * n_q:(grp + 1) * n_q, HEAD_DIM * parity:HEAD_DIM * (parity + 1)])

    def window_inputs(b, c0, r0, rc):
        for g in range(0 if time_major_pool else N_LG):
            fc_ref[b * N_LG + g, HIST_C + c0:HIST_C + c0 + rc, :] = z_ref[r0:r0 + rc, LANES * g:LANES * (g + 1)]
        k = z_ref[r0:r0 + rc, K_COL:K_COL + KV_W]
        v = z_ref[r0:r0 + rc, V_COL:V_COL + KV_W]
        kf_ref[b, WINDOW + c0:WINDOW + c0 + rc, :] = k
        vf_ref[b, WINDOW + c0:WINDOW + c0 + rc, :] = v
        ks_ref[b, WINDOW + c0:WINDOW + c0 + rc, :] = pltpu.roll(k, HEAD_DIM, axis=1)
        vs_ref[b, WINDOW + c0:WINDOW + c0 + rc, :] = pltpu.roll(v, HEAD_DIM, axis=1)

    def pooling(b, c0, r0, rc):
        pos = tile_pos + c0 + lax.broadcasted_iota(jnp.int32, (rc, LANES), 0)
        for g, w in enumerate(POOL_WINDOWS):
            u = fc_ref[b * N_LG + g, HIST_C + c0:HIST_C + c0 + rc, :]
            acc = u
            for d in range(1, w):
                acc = acc + fc_ref[b * N_LG + g, HIST_C + c0 - d:HIST_C + c0 - d + rc, :]
            cnt = jnp.minimum(w, pos + 1).astype(F32)
            pc_ref[g, r0:r0 + rc, :] = acc / cnt - u

    def pooling_time_major(lo, n):
        b_lo, n_seq = lo // tt, n // tt

        for g in range(N_LG):
            fc_ref[g, lo:lo + n, :] = z_ref[lo:lo + n, LANES * g:LANES * (g + 1)]

        def slab(j_in, g):
            if j_in < POOL_PAST:
                return sc_ref[j_in, b_lo:b_lo + n_seq, LANES * g:LANES * (g + 1)]
            return fc_ref[g, pl.ds(lo + j_in - POOL_PAST, n_seq, stride=tt), :]

        for t in range(tt):
            for g, w in enumerate(POOL_WINDOWS):
                u = slab(POOL_PAST + t, g)
                acc = u
                for d in range(1, w):
                    acc = acc + slab(POOL_PAST + t - d, g)
                pc_ref[g, pl.ds(lo + t, n_seq, stride=tt), :] = acc / float(min(w, pos0 + t + 1)) - u
        for j_out in range(POOL_PAST):
            for g in range(N_LG):
                nc_ref[j_out, b_lo:b_lo + n_seq, LANES * g:LANES * (g + 1)] = slab(j_out + tt, g)

    def gating(r0, rc):
        for g in range(N_LG):
            lg = slice(LANES * g, LANES * (g + 1))
            mixed = jnp.dot(pc_ref[g, r0:r0 + rc, :].astype(BF16), pw_ref[g], preferred_element_type=F32)
            yc = mixed * ps_ref[:, lg] * _silu(z_ref[r0:r0 + rc, MIX_W + LANES * g:MIX_W + LANES * (g + 1)])
            mix_ref[r0:r0 + rc, lg] = yc.astype(BF16)
        yd = od_ref[r0:r0 + rc, 0:MIX_W] * _silu(z_ref[r0:r0 + rc, DG_COL:DG_COL + MIX_W])
        mix_ref[r0:r0 + rc, MIX_W:2 * MIX_W] = yd.astype(BF16)

    def mixer_tasks(lo, n):
        chunks = _seq_chunks(lo, n, tt)
        if gs == 1:
            blocks = [(r0 // tt, (r0 % tt) // qs) for r0 in range(lo, lo + n, qs)]
        else:
            blocks = [(b0, 0) for b0 in range(lo // tt, (lo + n) // tt, gs)]
        pool_tasks = ([functools.partial(pooling_time_major, lo, n)] if time_major_pool
                      else [functools.partial(pooling, *ch) for ch in chunks])
        return ([functools.partial(window_inputs, *ch) for ch in chunks]
                + pool_tasks
                + [functools.partial(attention, b0, sb, g) for b0, sb in blocks for g in range(N_KV)]
                + [functools.partial(gating, lo, n)])

    _pre_norm(x_ref, mod_ref, seq0, gpre_ref, h_ref, rows, tt)
    for c0 in range(0, ODD_IN, MXU_COLS):
        z_ref[:, c0:c0 + MXU_COLS] = jnp.dot(h_ref[...], win_ref[:, c0:c0 + MXU_COLS].astype(BF16),
                                             preferred_element_type=F32)
    wob_ref[...] = wout_ref[...].astype(BF16)

    n_chunks = max(1, rows // WIN_CHUNK)
    n = rows // n_chunks
    for c in range(n_chunks + 1):
        if c > 0:
            _out_proj_task(mix_ref, wob_ref, o_ref, (c - 1) * n, n)()
        if c < n_chunks:
            for task in mixer_tasks(c * n, n):
                task()
        if c > 0:
            for task in _residual_tasks(x_ref, mod_ref, seq0, gpost_ref, o_ref, y_ref, (c - 1) * n, n, tt):
                task()

    for b in range(nb):
        for g in range(0 if time_major_pool else N_LG):
            nc_ref[b, :, LANES * g:LANES * (g + 1)] = fc_ref[b * N_LG + g, HIST_C + tt - POOL_PAST:HIST_C + tt, :]
            if nt > 1:
                fc_ref[b * N_LG + g, 0:HIST_C, :] = fc_ref[b * N_LG + g, tt:tt + HIST_C, :]
        nk_ref[b] = kf_ref[b, tt:tt + WINDOW, :].T
        nv_ref[b] = vf_ref[b, tt:tt + WINDOW, :].T
        if nt > 1:
            for r in (kf_ref, ks_ref, vf_ref, vs_ref):
                r[b, 0:WINDOW, :] = r[b, tt:tt + WINDOW, :]


def _odd_layer(x2d, mod, st_c, st_k, st_v, g_pre, w_in, pool_w, pool_scale, sinks, w_out, g_post,
               *, n, t, nb, tt, pos0, time_major_pool):
    rows = nb * tt
    nt = t // tt
    kern = functools.partial(_odd_kernel, nb=nb, tt=tt, nt=nt, pos0=pos0, time_major_pool=time_major_pool)
    row_spec = pl.BlockSpec((rows, D_MODEL), lambda i, j: (i * nt + j, 0))

    def seq_spec(shape):
        return pl.BlockSpec((nb,) + shape, lambda i, j: (i, 0, 0))

    def const_spec(shape):
        return pl.BlockSpec(shape, lambda i, j: (0,) * len(shape))

    if time_major_pool:
        pool_spec = pl.BlockSpec((POOL_PAST, nb, MIX_W), lambda i, j: (0, i, 0))
    else:
        pool_spec = seq_spec((POOL_PAST, MIX_W))

    return pl.pallas_call(
        kern,
        out_shape=(
            jax.ShapeDtypeStruct((n * t, D_MODEL), F32),
            jax.ShapeDtypeStruct(st_c.shape, F32),
            jax.ShapeDtypeStruct((n, KV_W, WINDOW), F32),
            jax.ShapeDtypeStruct((n, KV_W, WINDOW), F32),
        ),
        grid=(n // nb, nt),
        in_specs=[
            pl.BlockSpec(memory_space=pltpu.SMEM),
            row_spec,
            const_spec(mod.shape),
            pool_spec,
            seq_spec((KV_W, WINDOW)),
            seq_spec((KV_W, WINDOW)),
            const_spec((1, D_MODEL)),
            const_spec((D_MODEL, ODD_IN)),
            const_spec((N_LG, LANES, LANES)),
            const_spec((1, MIX_W)),
            const_spec((2 * MIX_W, D_MODEL)),
            const_spec((1, D_MODEL)),
        ],
        out_specs=(row_spec, pool_spec, seq_spec((KV_W, WINDOW)), seq_spec((KV_W, WINDOW))),
        scratch_shapes=[
            pltpu.VMEM((rows, D_MODEL), BF16),
            pltpu.VMEM((rows, ODD_IN), F32),
            pltpu.VMEM((rows, D_MODEL), F32),
            pltpu.VMEM((N_LG, rows, LANES) if time_major_pool else (nb * N_LG, HIST_C + tt, LANES), F32),
            pltpu.VMEM((nb, WINDOW + tt, KV_W), F32),
            pltpu.VMEM((nb, WINDOW + tt, KV_W), F32),
            pltpu.VMEM((nb, WINDOW + tt, KV_W), F32),
            pltpu.VMEM((nb, WINDOW + tt, KV_W), F32),
            pltpu.VMEM((N_LG, rows, LANES), F32),
            pltpu.VMEM((rows, MIX_W), F32),
            pltpu.VMEM((rows, 2 * MIX_W), BF16),
            pltpu.VMEM((2 * MIX_W, D_MODEL), BF16),
        ],
        compiler_params=pltpu.CompilerParams(
            dimension_semantics=("arbitrary", "arbitrary"), vmem_limit_bytes=VMEM_LIMIT_BYTES),
        name="odd_layer",
    )(sinks, x2d, mod, st_c, st_k, st_v, g_pre, w_in, pool_w, pool_scale, w_out, g_post)


def _tiling(n, t):
    if t >= TILE_ROWS:
        return 1, TILE_ROWS
    return min(n, SHORT_TILE_ROWS // t), t


def _trunk(x, mod_e, mod_o, st_a, st_b, st_c, st_k, st_v, pos0, we, wo):
    n, t, _ = x.shape
    nb, tt = _tiling(n, t)
    short = t == tt and nb > 1

    def swap(st):
        return jnp.transpose(st, (1, 0, 2)) if short else st

    x2d, na, nbs = _even_layer(x.reshape(n * t, D_MODEL), mod_e, swap(st_a), swap(st_b), *we, n=n, t=t, nb=nb,
                               tt=tt, time_major_state=short)
    def channel_major(cache):
        return jnp.transpose(cache, (0, 2, 3, 1)).reshape(n, KV_W, WINDOW)

    def position_major(cache):
        return jnp.transpose(cache.reshape(n, N_KV, HEAD_DIM, WINDOW), (0, 3, 1, 2))

    x2d, nc, nk, nv = _odd_layer(x2d, mod_o, swap(st_c), channel_major(st_k), channel_major(st_v), *wo,
                                 n=n, t=t, nb=nb, tt=tt, pos0=pos0, time_major_pool=short)
    na, nbs, nc = swap(na), swap(nbs), swap(nc)
    return (x2d.reshape(n, t, D_MODEL), na[None], nbs[None], nc[None],
            position_major(nk)[None], position_major(nv)[None])


def kernel(x_prompt, x_sample, state_conv_a, state_conv_b, state_pool_c, cache_win_k, cache_win_v, c_prompt, c_sample, w_mod_e, b_mod_e, g_pre_e, g_post_e, w_in_e, conv_a_w, conv_b_w, conv_b_b, ln_b_g, ln_b_b, w_out_e, w_mod_o, b_mod_o, g_pre_o, g_post_o, w_in_o, pool_w, pool_scale, sinks, w_out_o):
    n_p = x_prompt.shape[0]
    assert w_in_e.shape[0] == 1 and w_in_o.shape[0] == 1, "one even and one odd layer"

    mod_e_p, mod_e_s = _adaln_mod(c_prompt, c_sample, w_mod_e[0], b_mod_e[0])
    mod_o_p, mod_o_s = _adaln_mod(c_prompt, c_sample, w_mod_o[0], b_mod_o[0])

    we = (g_pre_e, w_in_e[0], conv_a_w[0], conv_b_w[0], conv_b_b, ln_b_g, ln_b_b,
          w_out_e[0], g_post_e)
    wo = (g_pre_o, w_in_o[0], pool_w[0].astype(BF16), pool_scale, sinks[0],
          w_out_o[0], g_post_o)

    dt = x_prompt.dtype
    z_a = jnp.zeros((n_p, CONV_A - 1, MIX_W), dt)
    z_b = jnp.zeros((n_p, CONV_B - 1, MIX_W), dt)
    z_c = jnp.zeros((n_p, POOL_PAST, MIX_W), dt)
    z_kv = jnp.zeros((n_p, WINDOW, N_KV, HEAD_DIM), dt)
    y_p, pa, pb, pc, pk, pv = _trunk(x_prompt, mod_e_p, mod_o_p, z_a, z_b, z_c, z_kv, z_kv, 0, we, wo)
    y_s, sa, sb, sc, sk, sv = _trunk(x_sample, mod_e_s, mod_o_s, state_conv_a[0], state_conv_b[0],
                                     state_pool_c[0], cache_win_k[0], cache_win_v[0], PAST_LEN, we, wo)
    return (y_p, y_s, pa, sa, pb, sb, pc, sc, pk, sk, pv, sv)
```

```python
import functools

import jax
import jax.numpy as jnp
from jax import lax
from jax.experimental import pallas as pl
from jax.experimental.pallas import tpu as pltpu

F32 = jnp.float32
BF16 = jnp.bfloat16

D_MODEL = 1024
MIX_W = 512
LANES = 128
MXU_COLS = 256
N_LG = MIX_W // LANES
HEAD_DIM = 64
N_HEADS = 8
N_KV = 2
N_REP = N_HEADS // N_KV
KV_W = N_KV * HEAD_DIM
CONV_A = 3
CONV_B = 31
POOL_WINDOWS = (2, 4, 8, 16)
POOL_PAST = 15
WINDOW = 128
PAST_LEN = 8192
EVEN_IN = 7 * MIX_W
ODD_IN = 4 * MIX_W + 2 * KV_W
Q_COL, K_COL, V_COL, DG_COL = 2 * MIX_W, 3 * MIX_W, 3 * MIX_W + KV_W, 3 * MIX_W + 2 * KV_W
RMS_EPS = 1e-6
LN_EPS = 1e-5
HIST_A = 8
HIST_B = 32
HIST_C = 16
ATT_MIN_Q_ROWS = 32
TILE_ROWS = 1024
SHORT_TILE_ROWS = 256
ROW_CHUNK = 64
WIN_CHUNK = 128
MOD_K_BLOCK = 256
VMEM_LIMIT_BYTES = 56 * 1024 * 1024
NEG_LOG2_E = -1.4426950408889634
ALIBI_SLOPES = tuple(float(2.0 ** (-8.0 * (h + 1) / N_HEADS)) for h in range(N_HEADS))


def _sigmoid(x):
    return 1.0 / (1.0 + jnp.exp2(x * NEG_LOG2_E))


def _silu(x):
    return x * _sigmoid(x)


def _mod_kernel(cp_ref, cs_ref, w_ref, b_ref, op_ref, os_ref):
    k = pl.program_id(0)
    n_s = cs_ref.shape[0]
    c = jnp.concatenate([cs_ref[...], cp_ref[...]], axis=0)
    part = jnp.dot(_silu(c).astype(BF16), w_ref[...].astype(BF16), preferred_element_type=F32)
    for o_ref, rows in ((os_ref, slice(0, n_s)), (op_ref, slice(n_s, c.shape[0]))):
        @pl.when(k == 0)
        def _first():
            o_ref[...] = part[rows] + b_ref[...]

        @pl.when(k > 0)
        def _rest():
            o_ref[...] += part[rows]


def _adaln_mod(c_prompt, c_sample, w_mod, b_mod):
    def c_spec(c):
        return pl.BlockSpec((c.shape[0], MOD_K_BLOCK), lambda k: (0, k))

    def out_spec(c):
        return pl.BlockSpec((c.shape[0], 3 * D_MODEL), lambda k: (0, 0))

    return pl.pallas_call(
        _mod_kernel,
        out_shape=tuple(jax.ShapeDtypeStruct((c.shape[0], 3 * D_MODEL), F32) for c in (c_prompt, c_sample)),
        grid=(D_MODEL // MOD_K_BLOCK,),
        in_specs=[
            c_spec(c_prompt),
            c_spec(c_sample),
            pl.BlockSpec((MOD_K_BLOCK, 3 * D_MODEL), lambda k: (k, 0)),
            pl.BlockSpec((1, 3 * D_MODEL), lambda k: (0, 0)),
        ],
        out_specs=(out_spec(c_prompt), out_spec(c_sample)),
        compiler_params=pltpu.CompilerParams(dimension_semantics=("arbitrary",)),
        name="adaln_mod",
    )(c_prompt, c_sample, w_mod, b_mod.reshape(1, -1))


def _mod_rows(mod_ref, seq0, col, r0, rc, tt):
    if tt >= rc:
        return mod_ref[pl.ds(seq0 + r0 // tt, 1), col:col + D_MODEL]
    parts = [jnp.broadcast_to(mod_ref[pl.ds(seq0 + b, 1), col:col + D_MODEL], (tt, D_MODEL))
             for b in range(r0 // tt, (r0 + rc) // tt)]
    return jnp.concatenate(parts, axis=0)


def _pre_norm(x_ref, mod_ref, seq0, g_ref, h_ref, rows, tt):
    rc = min(rows, ROW_CHUNK)
    for r0 in range(0, rows, rc):
        x = x_ref[r0:r0 + rc, :]
        ms = jnp.mean(x * x, axis=-1, keepdims=True)
        gain = g_ref[...] * (1.0 + _mod_rows(mod_ref, seq0, D_MODEL, r0, rc, tt))
        h = x * lax.rsqrt(ms + RMS_EPS) * gain + _mod_rows(mod_ref, seq0, 0, r0, rc, tt)
        h_ref[r0:r0 + rc, :] = h.astype(BF16)


def _out_proj_task(mix_ref, wout_ref, o_ref, lo, n):
    def task():
        o_ref[lo:lo + n, :] = jnp.dot(mix_ref[lo:lo + n, :], wout_ref[...], preferred_element_type=F32)
    return task


def _residual_tasks(x_ref, mod_ref, seq0, g_ref, o_ref, y_ref, lo, n, tt):
    rc = min(n, ROW_CHUNK)

    def chunk(r0):
        o = o_ref[r0:r0 + rc, :]
        ms = jnp.mean(o * o, axis=-1, keepdims=True)
        gain = g_ref[...] * _mod_rows(mod_ref, seq0, 2 * D_MODEL, r0, rc, tt)
        y_ref[r0:r0 + rc, :] = x_ref[r0:r0 + rc, :] + o * lax.rsqrt(ms + RMS_EPS) * gain
    return [functools.partial(chunk, r0) for r0 in range(lo, lo + n, rc)]


def _seq_chunks(lo, n, tt):
    rc = min(tt, n, WIN_CHUNK)
    return [(r0 // tt, r0 % tt, r0, rc) for r0 in range(lo, lo + n, rc)]


def _even_gating(z_ref, ca_ref, cb_ref, cbb_ref, lng_ref, lnb_ref, mix_ref, rows):
    rc = min(rows, ROW_CHUNK)
    for r0 in range(0, rows, rc):
        def zs(c):
            return z_ref[r0:r0 + rc, MIX_W * c:MIX_W * (c + 1)]

        def all_lanes(c_ref):
            return jnp.concatenate([c_ref[g, r0:r0 + rc, :] for g in range(N_LG)], axis=1)
        ya = zs(1) * all_lanes(ca_ref) * _silu(zs(3))
        mix_ref[r0:r0 + rc, 0:MIX_W] = ya.astype(BF16)
        yb = all_lanes(cb_ref) + cbb_ref[...]
        mu = jnp.mean(yb, axis=-1, keepdims=True)
        yc = yb - mu
        var = jnp.mean(yc * yc, axis=-1, keepdims=True)
        ln = yc * lax.rsqrt(var + LN_EPS) * lng_ref[...] + lnb_ref[...]
        mix_ref[r0:r0 + rc, MIX_W:2 * MIX_W] = (_silu(ln) * _silu(zs(6))).astype(BF16)


def _even_kernel(x_ref, mod_ref, sa_ref, sb_ref, gpre_ref, win_ref, cwa_ref, cwb_ref, cbb_ref,
                 lng_ref, lnb_ref, wout_ref, gpost_ref,
                 y_ref, na_ref, nb_ref,
                 h_ref, z_ref, o_ref, fa_ref, fb_ref, ca_ref, cb_ref, mix_ref, wob_ref,
                 *, nb, tt, nt, time_major_state):
    rows = nb * tt
    j = pl.program_id(1)
    seq0 = pl.program_id(0) * nb
    assert not time_major_state or nt == 1

    @pl.when(j == 0)
    def _load_state():
        for b in range(0 if time_major_state else nb):
            for g in range(N_LG):
                lg = slice(LANES * g, LANES * (g + 1))
                fa_ref[b * N_LG + g, 0:HIST_A, :] = jnp.zeros((HIST_A, LANES), F32)
                fa_ref[b * N_LG + g, HIST_A - (CONV_A - 1):HIST_A, :] = sa_ref[b, :, lg]
                fb_ref[b * N_LG + g, 0:8, :] = jnp.zeros((8, LANES), F32)
                fb_ref[b * N_LG + g, HIST_B - (CONV_B - 1):HIST_B, :] = sb_ref[b, :, lg]

    _pre_norm(x_ref, mod_ref, seq0, gpre_ref, h_ref, rows, tt)

    split_order = (4, 5, 0, 2, 1, 3, 6)
    for c0 in [MIX_W * s + d for s in split_order for d in range(0, MIX_W, MXU_COLS)]:
        z_ref[:, c0:c0 + MXU_COLS] = jnp.dot(h_ref[...], win_ref[:, c0:c0 + MXU_COLS].astype(BF16),
                                             preferred_element_type=F32)
    wob_ref[...] = wout_ref[...].astype(BF16)

    if time_major_state:
        rc = min(rows, WIN_CHUNK)
        for r0 in range(0, rows, rc):
            for g in range(N_LG):
                def zc(c):
                    return z_ref[r0:r0 + rc, MIX_W * c + LANES * g:MIX_W * c + LANES * (g + 1)]
                fa_ref[g, r0:r0 + rc, :] = zc(2) * zc(0)
                fb_ref[g, r0:r0 + rc, :] = zc(4) * _sigmoid(zc(5))

        def conv(w_ref, state_ref, u_ref, out_ref, new_state_ref, taps):
            hist = taps - 1

            def slab(j_in, g):
                if j_in < hist:
                    return state_ref[j_in, :, LANES * g:LANES * (g + 1)]
                return u_ref[g, pl.ds(j_in - hist, nb, stride=tt), :]

            for t in range(tt):
                for g in range(N_LG):
                    acc = None
                    for k in range(taps):
                        term = w_ref[k:k + 1, LANES * g:LANES * (g + 1)] * slab(t + k, g)
                        acc = term if acc is None else acc + term
                    out_ref[g, pl.ds(t, nb, stride=tt), :] = acc
            for j_out in range(hist):
                for g in range(N_LG):
                    new_state_ref[j_out, :, LANES * g:LANES * (g + 1)] = slab(j_out + tt, g)

        conv(cwa_ref, sa_ref, fa_ref, ca_ref, na_ref, CONV_A)
        conv(cwb_ref, sb_ref, fb_ref, cb_ref, nb_ref, CONV_B)

    for b, c0, r0, rc in ([] if time_major_state else _seq_chunks(0, rows, tt)):
        for g in range(N_LG):
            def zc(c):
                return z_ref[r0:r0 + rc, MIX_W * c + LANES * g:MIX_W * c + LANES * (g + 1)]
            fa_ref[b * N_LG + g, HIST_A + c0:HIST_A + c0 + rc, :] = zc(2) * zc(0)
            fb_ref[b * N_LG + g, HIST_B + c0:HIST_B + c0 + rc, :] = zc(4) * _sigmoid(zc(5))

    for b, c0, r0, rc in ([] if time_major_state else _seq_chunks(0, rows, tt)):
        for g in range(N_LG):
            lg = slice(LANES * g, LANES * (g + 1))
            acc = None
            for k in range(CONV_A):
                off = HIST_A - (CONV_A - 1) + k + c0
                t = cwa_ref[k:k + 1, lg] * fa_ref[b * N_LG + g, off:off + rc, :]
                acc = t if acc is None else acc + t
            ca_ref[g, r0:r0 + rc, :] = acc
            acc = None
            for k in range(CONV_B):
                off = HIST_B - (CONV_B - 1) + k + c0
                t = cwb_ref[k:k + 1, lg] * fb_ref[b * N_LG + g, off:off + rc, :]
                acc = t if acc is None else acc + t
            cb_ref[g, r0:r0 + rc, :] = acc

    for b in range(0 if time_major_state else nb):
        for g in range(N_LG):
            lg = slice(LANES * g, LANES * (g + 1))
            na_ref[b, :, lg] = fa_ref[b * N_LG + g, HIST_A + tt - (CONV_A - 1):HIST_A + tt, :]
            nb_ref[b, :, lg] = fb_ref[b * N_LG + g, HIST_B + tt - (CONV_B - 1):HIST_B + tt, :]
            if nt > 1:
                fa_ref[b * N_LG + g, 0:HIST_A, :] = fa_ref[b * N_LG + g, tt:tt + HIST_A, :]
                fb_ref[b * N_LG + g, 0:HIST_B, :] = fb_ref[b * N_LG + g, tt:tt + HIST_B, :]

    _even_gating(z_ref, ca_ref, cb_ref, cbb_ref, lng_ref, lnb_ref, mix_ref, rows)
    _out_proj_task(mix_ref, wob_ref, o_ref, 0, rows)()
    for task in _residual_tasks(x_ref, mod_ref, seq0, gpost_ref, o_ref, y_ref, 0, rows, tt):
        task()


def _even_layer(x2d, mod, st_a, st_b, g_pre, w_in, cw_a, cw_b, cb_b, ln_g, ln_b, w_out, g_post,
                *, n, t, nb, tt, time_major_state):
    rows = nb * tt
    nt = t // tt
    kern = functools.partial(_even_kernel, nb=nb, tt=tt, nt=nt, time_major_state=time_major_state)
    row_spec = pl.BlockSpec((rows, D_MODEL), lambda i, j: (i * nt + j, 0))

    def seq_spec(shape):
        if time_major_state:
            return pl.BlockSpec((shape[0], nb, shape[1]), lambda i, j: (0, i, 0))
        return pl.BlockSpec((nb,) + shape, lambda i, j: (i, 0, 0))

    def window_scratch(hist):
        if time_major_state:
            return pltpu.VMEM((N_LG, rows, LANES), F32)
        return pltpu.VMEM((nb * N_LG, hist + tt, LANES), F32)

    def const_spec(shape):
        return pl.BlockSpec(shape, lambda i, j: (0,) * len(shape))

    return pl.pallas_call(
        kern,
        out_shape=(
            jax.ShapeDtypeStruct((n * t, D_MODEL), F32),
            jax.ShapeDtypeStruct(st_a.shape, F32),
            jax.ShapeDtypeStruct(st_b.shape, F32),
        ),
        grid=(n // nb, nt),
        in_specs=[
            row_spec,
            const_spec(mod.shape),
            seq_spec((CONV_A - 1, MIX_W)),
            seq_spec((CONV_B - 1, MIX_W)),
            const_spec((1, D_MODEL)),
            const_spec((D_MODEL, EVEN_IN)),
            const_spec((CONV_A, MIX_W)),
            const_spec((CONV_B, MIX_W)),
            const_spec((1, MIX_W)),
            const_spec((1, MIX_W)),
            const_spec((1, MIX_W)),
            const_spec((2 * MIX_W, D_MODEL)),
            const_spec((1, D_MODEL)),
        ],
        out_specs=(row_spec, seq_spec((CONV_A - 1, MIX_W)), seq_spec((CONV_B - 1, MIX_W))),
        scratch_shapes=[
            pltpu.VMEM((rows, D_MODEL), BF16),
            pltpu.VMEM((rows, EVEN_IN), F32),
            pltpu.VMEM((rows, D_MODEL), F32),
            window_scratch(HIST_A),
            window_scratch(HIST_B),
            pltpu.VMEM((N_LG, rows, LANES), F32),
            pltpu.VMEM((N_LG, rows, LANES), F32),
            pltpu.VMEM((rows, 2 * MIX_W), BF16),
            pltpu.VMEM((2 * MIX_W, D_MODEL), BF16),
        ],
        compiler_params=pltpu.CompilerParams(
            dimension_semantics=("arbitrary", "arbitrary"), vmem_limit_bytes=VMEM_LIMIT_BYTES),
        name="even_layer",
    )(x2d, mod, st_a, st_b, g_pre, w_in, cw_a, cw_b, cb_b, ln_g, ln_b, w_out, g_post)


def _odd_kernel(sinks_ref, x_ref, mod_ref, sc_ref, sk_ref, sv_ref, gpre_ref, win_ref, pw_ref, ps_ref,
                wout_ref, gpost_ref,
                y_ref, nc_ref, nk_ref, nv_ref,
                h_ref, z_ref, o_ref, fc_ref, kf_ref, ks_ref, vf_ref, vs_ref, pc_ref, od_ref, mix_ref, wob_ref,
                *, nb, tt, nt, pos0, time_major_pool):
    rows = nb * tt
    j = pl.program_id(1)
    seq0 = pl.program_id(0) * nb
    tile_pos = pos0 + j * tt
    assert not time_major_pool or nt == 1

    @pl.when(j == 0)
    def _load_state():
        for b in range(nb):
            for g in range(0 if time_major_pool else N_LG):
                fc_ref[b * N_LG + g, 0:8, :] = jnp.zeros((8, LANES), F32)
                fc_ref[b * N_LG + g, HIST_C - POOL_PAST:HIST_C, :] = sc_ref[b, :, LANES * g:LANES * (g + 1)]
            k_past, v_past = sk_ref[b].T, sv_ref[b].T
            kf_ref[b, 0:WINDOW, :] = k_past
            vf_ref[b, 0:WINDOW, :] = v_past
            ks_ref[b, 0:WINDOW, :] = pltpu.roll(k_past, HEAD_DIM, axis=1)
            vs_ref[b, 0:WINDOW, :] = pltpu.roll(v_past, HEAD_DIM, axis=1)

    qs = min(tt, WINDOW)
    gs = max(1, ATT_MIN_Q_ROWS // qs)
    kw = WINDOW + qs
    n_q, n_keys = gs * qs, gs * kw
    assert N_REP == 4 and 2 * HEAD_DIM == LANES and KV_W == LANES
    assert qs & (qs - 1) == 0 and gs & (gs - 1) == 0 and nb % gs == 0 and (gs == 1 or tt == qs)
    shape = (2 * n_q, n_keys)
    ri = lax.broadcasted_iota(jnp.int32, shape, 0)
    ci = lax.broadcasted_iota(jnp.int32, shape, 1)
    q_seq = (ri >> (qs.bit_length() - 1)) & (gs - 1)
    k_seq = jnp.zeros(shape, jnp.int32)
    for s_i in range(1, gs):
        k_seq = k_seq + (ci >= s_i * kw).astype(jnp.int32)
    k_c = ci - k_seq * kw
    dist = (ri & (qs - 1)) + WINDOW - k_c
    band = (q_seq == k_seq) & (dist >= 0) & (dist < WINDOW)
    dist_f = dist.astype(F32)
    upper_grp = ri >= n_q
    upper_grp_col = lax.broadcasted_iota(jnp.int32, (2 * n_q, 1), 0) >= n_q
    low_lanes = lax.broadcasted_iota(jnp.int32, (n_keys, LANES), 1) < HEAD_DIM
    ones_cols = jnp.ones((n_keys, LANES), BF16)

    def head_of(g, grp, parity):
        return g * N_REP + 2 * grp + parity

    def alibi_bias(g, parity):
        slope = jnp.where(upper_grp, ALIBI_SLOPES[head_of(g, 1, parity)], ALIBI_SLOPES[head_of(g, 0, parity)])
        return jnp.where(band, -(slope * dist_f), -jnp.inf)

    biases = {(g, parity): alibi_bias(g, parity) for g in range(N_KV) for parity in range(2)}

    def attention(b0, sb, g):
        r0 = b0 * tt + sb * qs
        p0 = tile_pos + sb * qs
        in_range = (k_c >= WINDOW - p0) if pos0 + sb * qs < WINDOW else None
        if gs == 1:
            win = (b0, slice(sb * qs, sb * qs + kw))
            k_nat, k_swp, v_nat, v_swp = kf_ref[win], ks_ref[win], vf_ref[win], vs_ref[win]
        else:
            k_nat, k_swp, v_nat, v_swp = (r[b0:b0 + gs].reshape(n_keys, KV_W)
                                          for r in (kf_ref, ks_ref, vf_ref, vs_ref))
        k_lo, k_hi = (k_nat, k_swp) if g == 0 else (k_swp, k_nat)
        v_lo, v_hi = (v_nat, v_swp) if g == 0 else (v_swp, v_nat)
        v_ext = jnp.concatenate([jnp.where(low_lanes, v_lo, v_hi).astype(BF16), ones_cols], axis=1)
        qg = jnp.concatenate([z_ref[r0:r0 + n_q, Q_COL + 2 * LANES * g + LANES * grp:
                                    Q_COL + 2 * LANES * g + LANES * (grp + 1)] for grp in range(2)], axis=0)
        qg = (qg * (HEAD_DIM ** -0.5)).astype(BF16)
        for parity in range(2):
            k_ext = (jnp.where(low_lanes, k_lo, 0.0) if parity == 0 else jnp.where(low_lanes, 0.0, k_hi)).astype(BF16)
            sink = jnp.where(upper_grp_col, sinks_ref[head_of(g, 1, parity)], sinks_ref[head_of(g, 0, parity)])
            sc = lax.dot_general(qg, k_ext, (((1,), (1,)), ((), ())), preferred_element_type=F32)
            bias = biases[g, parity]
            sc = sc + (bias if in_range is None else jnp.where(in_range, bias, -jnp.inf))
            m = jnp.maximum(jnp.max(sc, axis=-1, keepdims=True), sink)
            p = jnp.exp(sc - m)
            o_ext = jnp.dot(p.astype(BF16), v_ext, preferred_element_type=F32)
            o = o_ext[:, 0:LANES] / (o_ext[:, LANES:2 * LANES] + jnp.exp(sink - m))
            for grp in range(2):
                h = head_of(g, grp, parity)
                od_ref[r0:r0 + n_q, HEAD_DIM * h:HEAD_DIM * (h + 1)] = (
                    o[grp * n_q:(grp + 1) * n_q, HEAD_DIM * parity:HEAD_DIM * (parity + 1)])

    def window_inputs(b, c0, r0, rc):
        for g in range(0 if time_major_pool else N_LG):
            fc_ref[b * N_LG + g, HIST_C + c0:HIST_C + c0 + rc, :] = z_ref[r0:r0 + rc, LANES * g:LANES * (g + 1)]
        k = z_ref[r0:r0 + rc, K_COL:K_COL + KV_W]
        v = z_ref[r0:r0 + rc, V_COL:V_COL + KV_W]
        kf_ref[b, WINDOW + c0:WINDOW + c0 + rc, :] = k
        vf_ref[b, WINDOW + c0:WINDOW + c0 + rc, :] = v
        ks_ref[b, WINDOW + c0:WINDOW + c0 + rc, :] = pltpu.roll(k, HEAD_DIM, axis=1)
        vs_ref[b, WINDOW + c0:WINDOW + c0 + rc, :] = pltpu.roll(v, HEAD_DIM, axis=1)

    def pooling(b, c0, r0, rc):
        pos = tile_pos + c0 + lax.broadcasted_iota(jnp.int32, (rc, LANES), 0)
        for g, w in enumerate(POOL_WINDOWS):
            u = fc_ref[b * N_LG + g, HIST_C + c0:HIST_C + c0 + rc, :]
            acc = u
            for d in range(1, w):
                acc = acc + fc_ref[b * N_LG + g, HIST_C + c0 - d:HIST_C + c0 - d + rc, :]
            cnt = jnp.minimum(w, pos + 1).astype(F32)
            pc_ref[g, r0:r0 + rc, :] = acc / cnt - u

    def pooling_time_major(lo, n):
        b_lo, n_seq = lo // tt, n // tt

        for g in range(N_LG):
            fc_ref[g, lo:lo + n, :] = z_ref[lo:lo + n, LANES * g:LANES * (g + 1)]

        def slab(j_in, g):
            if j_in < POOL_PAST:
                return sc_ref[j_in, b_lo:b_lo + n_seq, LANES * g:LANES * (g + 1)]
            return fc_ref[g, pl.ds(lo + j_in - POOL_PAST, n_seq, stride=tt), :]

        for t in range(tt):
            for g, w in enumerate(POOL_WINDOWS):
                u = slab(POOL_PAST + t, g)
                acc = u
                for d in range(1, w):
                    acc = acc + slab(POOL_PAST + t - d, g)
                pc_ref[g, pl.ds(lo + t, n_seq, stride=tt), :] = acc / float(min(w, pos0 + t + 1)) - u
        for j_out in range(POOL_PAST):
            for g in range(N_LG):
                nc_ref[j_out, b_lo:b_lo + n_seq, LANES * g:LANES * (g + 1)] = slab(j_out + tt, g)

    def gating(r0, rc):
        for g in range(N_LG):
            lg = slice(LANES * g, LANES * (g + 1))
            mixed = jnp.dot(pc_ref[g, r0:r0 + rc, :].astype(BF16), pw_ref[g], preferred_element_type=F32)
            yc = mixed * ps_ref[:, lg] * _silu(z_ref[r0:r0 + rc, MIX_W + LANES * g:MIX_W + LANES * (g + 1)])
            mix_ref[r0:r0 + rc, lg] = yc.astype(BF16)
        yd = od_ref[r0:r0 + rc, 0:MIX_W] * _silu(z_ref[r0:r0 + rc, DG_COL:DG_COL + MIX_W])
        mix_ref[r0:r0 + rc, MIX_W:2 * MIX_W] = yd.astype(BF16)

    def mixer_tasks(lo, n):
        chunks = _seq_chunks(lo, n, tt)
        if gs == 1:
            blocks = [(r0 // tt, (r0 % tt) // qs) for r0 in range(lo, lo + n, qs)]
        else:
            blocks = [(b0, 0) for b0 in range(lo // tt, (lo + n) // tt, gs)]
        pool_tasks = ([functools.partial(pooling_time_major, lo, n)] if time_major_pool
                      else [functools.partial(pooling, *ch) for ch in chunks])
        return ([functools.partial(window_inputs, *ch) for ch in chunks]
                + pool_tasks
                + [functools.partial(attention, b0, sb, g) for b0, sb in blocks for g in range(N_KV)]
                + [functools.partial(gating, lo, n)])

    _pre_norm(x_ref, mod_ref, seq0, gpre_ref, h_ref, rows, tt)
    for c0 in range(0, ODD_IN, MXU_COLS):
        z_ref[:, c0:c0 + MXU_COLS] = jnp.dot(h_ref[...], win_ref[:, c0:c0 + MXU_COLS].astype(BF16),
                                             preferred_element_type=F32)
    wob_ref[...] = wout_ref[...].astype(BF16)

    for task in mixer_tasks(0, rows):
        task()
    _out_proj_task(mix_ref, wob_ref, o_ref, 0, rows)()
    for task in _residual_tasks(x_ref, mod_ref, seq0, gpost_ref, o_ref, y_ref, 0, rows, tt):
        task()

    for b in range(nb):
        for g in range(0 if time_major_pool else N_LG):
            nc_ref[b, :, LANES * g:LANES * (g + 1)] = fc_ref[b * N_LG + g, HIST_C + tt - POOL_PAST:HIST_C + tt, :]
            if nt > 1:
                fc_ref[b * N_LG + g, 0:HIST_C, :] = fc_ref[b * N_LG + g, tt:tt + HIST_C, :]
        nk_ref[b] = kf_ref[b, tt:tt + WINDOW, :].T
        nv_ref[b] = vf_ref[b, tt:tt + WINDOW, :].T
        if nt > 1:
            for r in (kf_ref, ks_ref, vf_ref, vs_ref):
                r[b, 0:WINDOW, :] = r[b, tt:tt + WINDOW, :]


def _odd_layer(x2d, mod, st_c, st_k, st_v, g_pre, w_in, pool_w, pool_scale, sinks, w_out, g_post,
               *, n, t, nb, tt, pos0, time_major_pool):
    rows = nb * tt
    nt = t // tt
    kern = functools.partial(_odd_kernel, nb=nb, tt=tt, nt=nt, pos0=pos0, time_major_pool=time_major_pool)
    row_spec = pl.BlockSpec((rows, D_MODEL), lambda i, j: (i * nt + j, 0))

    def seq_spec(shape):
        return pl.BlockSpec((nb,) + shape, lambda i, j: (i, 0, 0))

    def const_spec(shape):
        return pl.BlockSpec(shape, lambda i, j: (0,) * len(shape))

    if time_major_pool:
        pool_spec = pl.BlockSpec((POOL_PAST, nb, MIX_W), lambda i, j: (0, i, 0))
    else:
        pool_spec = seq_spec((POOL_PAST, MIX_W))

    return pl.pallas_call(
        kern,
        out_shape=(
            jax.ShapeDtypeStruct((n * t, D_MODEL), F32),
            jax.ShapeDtypeStruct(st_c.shape, F32),
            jax.ShapeDtypeStruct((n, KV_W, WINDOW), F32),
            jax.ShapeDtypeStruct((n, KV_W, WINDOW), F32),
        ),
        grid=(n // nb, nt),
        in_specs=[
            pl.BlockSpec(memory_space=pltpu.SMEM),
            row_spec,
            const_spec(mod.shape),
            pool_spec,
            seq_spec((KV_W, WINDOW)),
            seq_spec((KV_W, WINDOW)),
            const_spec((1, D_MODEL)),
            const_spec((D_MODEL, ODD_IN)),
            const_spec((N_LG, LANES, LANES)),
            const_spec((1, MIX_W)),
            const_spec((2 * MIX_W, D_MODEL)),
            const_spec((1, D_MODEL)),
        ],
        out_specs=(row_spec, pool_spec, seq_spec((KV_W, WINDOW)), seq_spec((KV_W, WINDOW))),
        scratch_shapes=[
            pltpu.VMEM((rows, D_MODEL), BF16),
            pltpu.VMEM((rows, ODD_IN), F32),
            pltpu.VMEM((rows, D_MODEL), F32),
            pltpu.VMEM((N_LG, rows, LANES) if time_major_pool else (nb * N_LG, HIST_C + tt, LANES), F32),
            pltpu.VMEM((nb, WINDOW + tt, KV_W), F32),
            pltpu.VMEM((nb, WINDOW + tt, KV_W), F32),
            pltpu.VMEM((nb, WINDOW + tt, KV_W), F32),
            pltpu.VMEM((nb, WINDOW + tt, KV_W), F32),
            pltpu.VMEM((N_LG, rows, LANES), F32),
            pltpu.VMEM((rows, MIX_W), F32),
            pltpu.VMEM((rows, 2 * MIX_W), BF16),
            pltpu.VMEM((2 * MIX_W, D_MODEL), BF16),
        ],
        compiler_params=pltpu.CompilerParams(
            dimension_semantics=("arbitrary", "arbitrary"), vmem_limit_bytes=VMEM_LIMIT_BYTES),
        name="odd_layer",
    )(sinks, x2d, mod, st_c, st_k, st_v, g_pre, w_in, pool_w, pool_scale, w_out, g_post)


def _tiling(n, t):
    if t >= TILE_ROWS:
        return 1, TILE_ROWS
    return min(n, SHORT_TILE_ROWS // t), t


def _trunk(x, mod_e, mod_o, st_a, st_b, st_c, st_k, st_v, pos0, we, wo):
    n, t, _ = x.shape
    nb, tt = _tiling(n, t)
    short = t == tt and nb > 1

    def swap(st):
        return jnp.transpose(st, (1, 0, 2)) if short else st

    x2d, na, nbs = _even_layer(x.reshape(n * t, D_MODEL), mod_e, swap(st_a), swap(st_b), *we, n=n, t=t, nb=nb,
                               tt=tt, time_major_state=short)
    def channel_major(cache):
        return jnp.transpose(cache, (0, 2, 3, 1)).reshape(n, KV_W, WINDOW)

    def position_major(cache):
        return jnp.transpose(cache.reshape(n, N_KV, HEAD_DIM, WINDOW), (0, 3, 1, 2))

    x2d, nc, nk, nv = _odd_layer(x2d, mod_o, swap(st_c), channel_major(st_k), channel_major(st_v), *wo,
                                 n=n, t=t, nb=nb, tt=tt, pos0=pos0, time_major_pool=short)
    na, nbs, nc = swap(na), swap(nbs), swap(nc)
    return (x2d.reshape(n, t, D_MODEL), na[None], nbs[None], nc[None],
            position_major(nk)[None], position_major(nv)[None])


def kernel(x_prompt, x_sample, state_conv_a, state_conv_b, state_pool_c, cache_win_k, cache_win_v, c_prompt, c_sample, w_mod_e, b_mod_e, g_pre_e, g_post_e, w_in_e, conv_a_w, conv_b_w, conv_b_b, ln_b_g, ln_b_b, w_out_e, w_mod_o, b_mod_o, g_pre_o, g_post_o, w_in_o, pool_w, pool_scale, sinks, w_out_o):
    n_p = x_prompt.shape[0]
    assert w_in_e.shape[0] == 1 and w_in_o.shape[0] == 1, "one even and one odd layer"

    mod_e_p, mod_e_s = _adaln_mod(c_prompt, c_sample, w_mod_e[0], b_mod_e[0])
    mod_o_p, mod_o_s = _adaln_mod(c_prompt, c_sample, w_mod_o[0], b_mod_o[0])

    we = (g_pre_e, w_in_e[0], conv_a_w[0], conv_b_w[0], conv_b_b, ln_b_g, ln_b_b,
          w_out_e[0], g_post_e)
    wo = (g_pre_o, w_in_o[0], pool_w[0].astype(BF16), pool_scale, sinks[0],
          w_out_o[0], g_post_o)

    dt = x_prompt.dtype
    z_a = jnp.zeros((n_p, CONV_A - 1, MIX_W), dt)
    z_b = jnp.zeros((n_p, CONV_B - 1, MIX_W), dt)
    z_c = jnp.zeros((n_p, POOL_PAST, MIX_W), dt)
    z_kv = jnp.zeros((n_p, WINDOW, N_KV, HEAD_DIM), dt)
    y_p, pa, pb, pc, pk, pv = _trunk(x_prompt, mod_e_p, mod_o_p, z_a, z_b, z_c, z_kv, z_kv, 0, we, wo)
    y_s, sa, sb, sc, sk, sv = _trunk(x_sample, mod_e_s, mod_o_s, state_conv_a[0], state_conv_b[0],
                                     state_pool_c[0], cache_win_k[0], cache_win_v[0], PAST_LEN, we, wo)
    return (y_p, y_s, pa, sa, pb, sb, pc, sc, pk, sk, pv, sv)
```

```python
import functools

import jax
import jax.numpy as jnp
from jax import lax
from jax.experimental import pallas as pl
from jax.experimental.pallas import tpu as pltpu

F32 = jnp.float32
BF16 = jnp.bfloat16

D_MODEL = 1024
MIX_W = 512
LANES = 128
MXU_COLS = 256
N_LG = MIX_W // LANES
HEAD_DIM = 64
N_HEADS = 8
N_KV = 2
N_REP = N_HEADS // N_KV
KV_W = N_KV * HEAD_DIM
CONV_A = 3
CONV_B = 31
POOL_WINDOWS = (2, 4, 8, 16)
POOL_PAST = 15
WINDOW = 128
PAST_LEN = 8192
EVEN_IN = 7 * MIX_W
ODD_IN = 4 * MIX_W + 2 * KV_W
Q_COL, K_COL, V_COL, DG_COL = 2 * MIX_W, 3 * MIX_W, 3 * MIX_W + KV_W, 3 * MIX_W + 2 * KV_W
RMS_EPS = 1e-6
LN_EPS = 1e-5
HIST_A = 8
HIST_B = 32
HIST_C = 16
ATT_MIN_Q_ROWS = 32
TILE_ROWS = 1024
SHORT_TILE_ROWS = 256
ROW_CHUNK = 64
WIN_CHUNK = 128
MOD_K_BLOCK = 256
VMEM_LIMIT_BYTES = 62 * 1024 * 1024
NEG_LOG2_E = -1.4426950408889634
ALIBI_SLOPES = tuple(float(2.0 ** (-8.0 * (h + 1) / N_HEADS)) for h in range(N_HEADS))


def _sigmoid(x):
    return 1.0 / (1.0 + jnp.exp2(x * NEG_LOG2_E))


def _silu(x):
    return x * _sigmoid(x)


def _mod_kernel(cp_ref, cs_ref, w_ref, b_ref, op_ref, os_ref):
    k = pl.program_id(0)
    n_s = cs_ref.shape[0]
    c = jnp.concatenate([cs_ref[...], cp_ref[...]], axis=0)
    part = jnp.dot(_silu(c).astype(BF16), w_ref[...].astype(BF16), preferred_element_type=F32)
    for o_ref, rows in ((os_ref, slice(0, n_s)), (op_ref, slice(n_s, c.shape[0]))):
        @pl.when(k == 0)
        def _first():
            o_ref[...] = part[rows] + b_ref[...]

        @pl.when(k > 0)
        def _rest():
            o_ref[...] += part[rows]


def _adaln_mod(c_prompt, c_sample, w_mod, b_mod):
    def c_spec(c):
        return pl.BlockSpec((c.shape[0], MOD_K_BLOCK), lambda k: (0, k))

    def out_spec(c):
        return pl.BlockSpec((c.shape[0], 3 * D_MODEL), lambda k: (0, 0))

    return pl.pallas_call(
        _mod_kernel,
        out_shape=tuple(jax.ShapeDtypeStruct((c.shape[0], 3 * D_MODEL), F32) for c in (c_prompt, c_sample)),
        grid=(D_MODEL // MOD_K_BLOCK,),
        in_specs=[
            c_spec(c_prompt),
            c_spec(c_sample),
            pl.BlockSpec((MOD_K_BLOCK, 3 * D_MODEL), lambda k: (k, 0)),
            pl.BlockSpec((1, 3 * D_MODEL), lambda k: (0, 0)),
        ],
        out_specs=(out_spec(c_prompt), out_spec(c_sample)),
        compiler_params=pltpu.CompilerParams(dimension_semantics=("arbitrary",)),
        name="adaln_mod",
    )(c_prompt, c_sample, w_mod, b_mod.reshape(1, -1))


def _mod_rows(mod_ref, seq0, col, r0, rc, tt):
    if tt >= rc:
        return mod_ref[pl.ds(seq0 + r0 // tt, 1), col:col + D_MODEL]
    parts = [jnp.broadcast_to(mod_ref[pl.ds(seq0 + b, 1), col:col + D_MODEL], (tt, D_MODEL))
             for b in range(r0 // tt, (r0 + rc) // tt)]
    return jnp.concatenate(parts, axis=0)


def _pre_norm(x_ref, mod_ref, seq0, g_ref, h_ref, rows, tt):
    rc = min(rows, ROW_CHUNK)
    for r0 in range(0, rows, rc):
        x = x_ref[r0:r0 + rc, :]
        ms = jnp.mean(x * x, axis=-1, keepdims=True)
        gain = g_ref[...] * (1.0 + _mod_rows(mod_ref, seq0, D_MODEL, r0, rc, tt))
        h = x * lax.rsqrt(ms + RMS_EPS) * gain + _mod_rows(mod_ref, seq0, 0, r0, rc, tt)
        h_ref[r0:r0 + rc, :] = h.astype(BF16)


def _out_proj_task(mix_ref, wout_ref, o_ref, lo, n):
    def task():
        o_ref[lo:lo + n, :] = jnp.dot(mix_ref[lo:lo + n, :], wout_ref[...], preferred_element_type=F32)
    return task


def _residual_tasks(x_ref, mod_ref, seq0, g_ref, o_ref, y_ref, lo, n, tt):
    rc = min(n, ROW_CHUNK)

    def chunk(r0):
        o = o_ref[r0:r0 + rc, :]
        ms = jnp.mean(o * o, axis=-1, keepdims=True)
        gain = g_ref[...] * _mod_rows(mod_ref, seq0, 2 * D_MODEL, r0, rc, tt)
        y_ref[r0:r0 + rc, :] = x_ref[r0:r0 + rc, :] + o * lax.rsqrt(ms + RMS_EPS) * gain
    return [functools.partial(chunk, r0) for r0 in range(lo, lo + n, rc)]


def _stage_weights(win_hbm, wout_hbm, win_ref, wout_ref, sem, groups):
    first = (pl.program_id(0) == 0) & (pl.program_id(1) == 0)
    spans = [span for group in groups for span in group]
    copies = [pltpu.make_async_copy(win_hbm.at[:, c0:c0 + n], win_ref.at[:, c0:c0 + n], sem.at[k])
              for k, (c0, n) in enumerate(spans)]
    copies.append(pltpu.make_async_copy(wout_hbm, wout_ref, sem.at[len(spans)]))
    starts = [sum(len(group) for group in groups[:g]) for g in range(len(groups) + 1)] + [len(copies)]

    @pl.when(first)
    def _start():
        for copy in copies:
            copy.start()

    def wait(g):
        @pl.when(first)
        def _wait():
            for copy in copies[starts[g]:starts[g + 1]]:
                copy.wait()
    return wait


def _seq_chunks(lo, n, tt):
    rc = min(tt, n, WIN_CHUNK)
    return [(r0 // tt, r0 % tt, r0, rc) for r0 in range(lo, lo + n, rc)]


def _even_gating(z_ref, ca_ref, cb_ref, cbb_ref, lng_ref, lnb_ref, mix_ref, rows):
    rc = min(rows, ROW_CHUNK)
    for r0 in range(0, rows, rc):
        def zs(c):
            return z_ref[r0:r0 + rc, MIX_W * c:MIX_W * (c + 1)]

        def all_lanes(c_ref):
            return jnp.concatenate([c_ref[g, r0:r0 + rc, :] for g in range(N_LG)], axis=1)
        ya = zs(1) * all_lanes(ca_ref) * _silu(zs(3))
        mix_ref[r0:r0 + rc, 0:MIX_W] = ya.astype(BF16)
        yb = all_lanes(cb_ref) + cbb_ref[...]
        mu = jnp.mean(yb, axis=-1, keepdims=True)
        yc = yb - mu
        var = jnp.mean(yc * yc, axis=-1, keepdims=True)
        ln = yc * lax.rsqrt(var + LN_EPS) * lng_ref[...] + lnb_ref[...]
        mix_ref[r0:r0 + rc, MIX_W:2 * MIX_W] = (_silu(ln) * _silu(zs(6))).astype(BF16)


def _even_kernel(x_ref, mod_ref, sa_ref, sb_ref, gpre_ref, win_hbm, cwa_ref, cwb_ref, cbb_ref,
                 lng_ref, lnb_ref, wout_hbm, gpost_ref,
                 y_ref, na_ref, nb_ref,
                 h_ref, z_ref, o_ref, fa_ref, fb_ref, ca_ref, cb_ref, mix_ref, wob_ref, win_ref, wout_ref, sem,
                 *, nb, tt, nt, time_major_state):
    rows = nb * tt
    j = pl.program_id(1)
    seq0 = pl.program_id(0) * nb
    assert not time_major_state or nt == 1

    split_order = (4, 5, 0, 2, 1, 3, 6)
    span = {s: (MIX_W * s, MIX_W) for s in split_order}
    weight_groups = [[span[4]], [span[5], span[0]], [span[2], span[1], span[3], span[6]]]
    wait_weights = _stage_weights(win_hbm, wout_hbm, win_ref, wout_ref, sem, weight_groups)

    @pl.when(j == 0)
    def _load_state():
        for b in range(0 if time_major_state else nb):
            for g in range(N_LG):
                lg = slice(LANES * g, LANES * (g + 1))
                fa_ref[b * N_LG + g, 0:HIST_A, :] = jnp.zeros((HIST_A, LANES), F32)
                fa_ref[b * N_LG + g, HIST_A - (CONV_A - 1):HIST_A, :] = sa_ref[b, :, lg]
                fb_ref[b * N_LG + g, 0:8, :] = jnp.zeros((8, LANES), F32)
                fb_ref[b * N_LG + g, HIST_B - (CONV_B - 1):HIST_B, :] = sb_ref[b, :, lg]

    _pre_norm(x_ref, mod_ref, seq0, gpre_ref, h_ref, rows, tt)

    for g, group in enumerate(weight_groups):
        wait_weights(g)
        for c0 in [s0 + d for s0, _ in group for d in range(0, MIX_W, MXU_COLS)]:
            z_ref[:, c0:c0 + MXU_COLS] = jnp.dot(h_ref[...], win_ref[:, c0:c0 + MXU_COLS].astype(BF16),
                                                 preferred_element_type=F32)
    wait_weights(len(weight_groups))
    wob_ref[...] = wout_ref[...].astype(BF16)

    if time_major_state:
        rc = min(rows, WIN_CHUNK)
        for r0 in range(0, rows, rc):
            for g in range(N_LG):
                def zc(c):
                    return z_ref[r0:r0 + rc, MIX_W * c + LANES * g:MIX_W * c + LANES * (g + 1)]
                fa_ref[g, r0:r0 + rc, :] = zc(2) * zc(0)
                fb_ref[g, r0:r0 + rc, :] = zc(4) * _sigmoid(zc(5))

        def conv(w_ref, state_ref, u_ref, out_ref, new_state_ref, taps):
            hist = taps - 1

            def slab(j_in, g):
                if j_in < hist:
                    return state_ref[j_in, :, LANES * g:LANES * (g + 1)]
                return u_ref[g, pl.ds(j_in - hist, nb, stride=tt), :]

            for t in range(tt):
                for g in range(N_LG):
                    acc = None
                    for k in range(taps):
                        term = w_ref[k:k + 1, LANES * g:LANES * (g + 1)] * slab(t + k, g)
                        acc = term if acc is None else acc + term
                    out_ref[g, pl.ds(t, nb, stride=tt), :] = acc
            for j_out in range(hist):
                for g in range(N_LG):
                    new_state_ref[j_out, :, LANES * g:LANES * (g + 1)] = slab(j_out + tt, g)

        conv(cwa_ref, sa_ref, fa_ref, ca_ref, na_ref, CONV_A)
        conv(cwb_ref, sb_ref, fb_ref, cb_ref, nb_ref, CONV_B)

    for b, c0, r0, rc in ([] if time_major_state else _seq_chunks(0, rows, tt)):
        for g in range(N_LG):
            def zc(c):
                return z_ref[r0:r0 + rc, MIX_W * c + LANES * g:MIX_W * c + LANES * (g + 1)]
            fa_ref[b * N_LG + g, HIST_A + c0:HIST_A + c0 + rc, :] = zc(2) * zc(0)
            fb_ref[b * N_LG + g, HIST_B + c0:HIST_B + c0 + rc, :] = zc(4) * _sigmoid(zc(5))

    for b, c0, r0, rc in ([] if time_major_state else _seq_chunks(0, rows, tt)):
        for g in range(N_LG):
            lg = slice(LANES * g, LANES * (g + 1))
            acc = None
            for k in range(CONV_A):
                off = HIST_A - (CONV_A - 1) + k + c0
                t = cwa_ref[k:k + 1, lg] * fa_ref[b * N_LG + g, off:off + rc, :]
                acc = t if acc is None else acc + t
            ca_ref[g, r0:r0 + rc, :] = acc
            acc = None
            for k in range(CONV_B):
                off = HIST_B - (CONV_B - 1) + k + c0
                t = cwb_ref[k:k + 1, lg] * fb_ref[b * N_LG + g, off:off + rc, :]
                acc = t if acc is None else acc + t
            cb_ref[g, r0:r0 + rc, :] = acc

    for b in range(0 if time_major_state else nb):
        for g in range(N_LG):
            lg = slice(LANES * g, LANES * (g + 1))
            na_ref[b, :, lg] = fa_ref[b * N_LG + g, HIST_A + tt - (CONV_A - 1):HIST_A + tt, :]
            nb_ref[b, :, lg] = fb_ref[b * N_LG + g, HIST_B + tt - (CONV_B - 1):HIST_B + tt, :]
            if nt > 1:
                fa_ref[b * N_LG + g, 0:HIST_A, :] = fa_ref[b * N_LG + g, tt:tt + HIST_A, :]
                fb_ref[b * N_LG + g, 0:HIST_B, :] = fb_ref[b * N_LG + g, tt:tt + HIST_B, :]

    _even_gating(z_ref, ca_ref, cb_ref, cbb_ref, lng_ref, lnb_ref, mix_ref, rows)
    _out_proj_task(mix_ref, wob_ref, o_ref, 0, rows)()
    for task in _residual_tasks(x_ref, mod_ref, seq0, gpost_ref, o_ref, y_ref, 0, rows, tt):
        task()


def _even_layer(x2d, mod, st_a, st_b, g_pre, w_in, cw_a, cw_b, cb_b, ln_g, ln_b, w_out, g_post,
                *, n, t, nb, tt, time_major_state):
    rows = nb * tt
    nt = t // tt
    kern = functools.partial(_even_kernel, nb=nb, tt=tt, nt=nt, time_major_state=time_major_state)
    row_spec = pl.BlockSpec((rows, D_MODEL), lambda i, j: (i * nt + j, 0))

    def seq_spec(shape):
        if time_major_state:
            return pl.BlockSpec((shape[0], nb, shape[1]), lambda i, j: (0, i, 0))
        return pl.BlockSpec((nb,) + shape, lambda i, j: (i, 0, 0))

    def window_scratch(hist):
        if time_major_state:
            return pltpu.VMEM((N_LG, rows, LANES), F32)
        return pltpu.VMEM((nb * N_LG, hist + tt, LANES), F32)

    def const_spec(shape):
        return pl.BlockSpec(shape, lambda i, j: (0,) * len(shape))

    return pl.pallas_call(
        kern,
        out_shape=(
            jax.ShapeDtypeStruct((n * t, D_MODEL), F32),
            jax.ShapeDtypeStruct(st_a.shape, F32),
            jax.ShapeDtypeStruct(st_b.shape, F32),
        ),
        grid=(n // nb, nt),
        in_specs=[
            row_spec,
            const_spec(mod.shape),
            seq_spec((CONV_A - 1, MIX_W)),
            seq_spec((CONV_B - 1, MIX_W)),
            const_spec((1, D_MODEL)),
            pl.BlockSpec(memory_space=pl.ANY),
            const_spec((CONV_A, MIX_W)),
            const_spec((CONV_B, MIX_W)),
            const_spec((1, MIX_W)),
            const_spec((1, MIX_W)),
            const_spec((1, MIX_W)),
            pl.BlockSpec(memory_space=pl.ANY),
            const_spec((1, D_MODEL)),
        ],
        out_specs=(row_spec, seq_spec((CONV_A - 1, MIX_W)), seq_spec((CONV_B - 1, MIX_W))),
        scratch_shapes=[
            pltpu.VMEM((rows, D_MODEL), BF16),
            pltpu.VMEM((rows, EVEN_IN), F32),
            pltpu.VMEM((rows, D_MODEL), F32),
            window_scratch(HIST_A),
            window_scratch(HIST_B),
            pltpu.VMEM((N_LG, rows, LANES), F32),
            pltpu.VMEM((N_LG, rows, LANES), F32),
            pltpu.VMEM((rows, 2 * MIX_W), BF16),
            pltpu.VMEM((2 * MIX_W, D_MODEL), BF16),
            pltpu.VMEM((D_MODEL, EVEN_IN), F32),
            pltpu.VMEM((2 * MIX_W, D_MODEL), F32),
            pltpu.SemaphoreType.DMA((8,)),
        ],
        compiler_params=pltpu.CompilerParams(
            dimension_semantics=("arbitrary", "arbitrary"), vmem_limit_bytes=VMEM_LIMIT_BYTES),
        name="even_layer",
    )(x2d, mod, st_a, st_b, g_pre, w_in, cw_a, cw_b, cb_b, ln_g, ln_b, w_out, g_post)


def _odd_kernel(sinks_ref, x_ref, mod_ref, sc_ref, sk_ref, sv_ref, gpre_ref, win_hbm, pw_ref, ps_ref,
                wout_hbm, gpost_ref,
                y_ref, nc_ref, nk_ref, nv_ref,
                h_ref, z_ref, o_ref, fc_ref, kf_ref, ks_ref, vf_ref, vs_ref, pc_ref, od_ref, mix_ref, wob_ref,
                win_ref, wout_ref, sem,
                *, nb, tt, nt, pos0, time_major_pool):
    rows = nb * tt
    j = pl.program_id(1)
    seq0 = pl.program_id(0) * nb
    tile_pos = pos0 + j * tt
    assert not time_major_pool or nt == 1
    weight_groups = [[(0, 2 * MXU_COLS)], [(2 * MXU_COLS, 4 * MXU_COLS)], [(6 * MXU_COLS, ODD_IN - 6 * MXU_COLS)]]
    wait_weights = _stage_weights(win_hbm, wout_hbm, win_ref, wout_ref, sem, weight_groups)

    @pl.when(j == 0)
    def _load_state():
        for b in range(nb):
            for g in range(0 if time_major_pool else N_LG):
                fc_ref[b * N_LG + g, 0:8, :] = jnp.zeros((8, LANES), F32)
                fc_ref[b * N_LG + g, HIST_C - POOL_PAST:HIST_C, :] = sc_ref[b, :, LANES * g:LANES * (g + 1)]
            k_past, v_past = sk_ref[b].T, sv_ref[b].T
            kf_ref[b, 0:WINDOW, :] = k_past
            vf_ref[b, 0:WINDOW, :] = v_past
            ks_ref[b, 0:WINDOW, :] = pltpu.roll(k_past, HEAD_DIM, axis=1)
            vs_ref[b, 0:WINDOW, :] = pltpu.roll(v_past, HEAD_DIM, axis=1)

    qs = min(tt, WINDOW)
    gs = max(1, ATT_MIN_Q_ROWS // qs)
    kw = WINDOW + qs
    n_q, n_keys = gs * qs, gs * kw
    assert N_REP == 4 and 2 * HEAD_DIM == LANES and KV_W == LANES
    assert qs & (qs - 1) == 0 and gs & (gs - 1) == 0 and nb % gs == 0 and (gs == 1 or tt == qs)
    shape = (2 * n_q, n_keys)
    ri = lax.broadcasted_iota(jnp.int32, shape, 0)
    ci = lax.broadcasted_iota(jnp.int32, shape, 1)
    q_seq = (ri >> (qs.bit_length() - 1)) & (gs - 1)
    k_seq = jnp.zeros(shape, jnp.int32)
    for s_i in range(1, gs):
        k_seq = k_seq + (ci >= s_i * kw).astype(jnp.int32)
    k_c = ci - k_seq * kw
    dist = (ri & (qs - 1)) + WINDOW - k_c
    band = (q_seq == k_seq) & (dist >= 0) & (dist < WINDOW)
    dist_f = dist.astype(F32)
    upper_grp = ri >= n_q
    upper_grp_col = lax.broadcasted_iota(jnp.int32, (2 * n_q, 1), 0) >= n_q
    low_lanes = lax.broadcasted_iota(jnp.int32, (n_keys, LANES), 1) < HEAD_DIM
    ones_cols = jnp.ones((n_keys, LANES), BF16)

    def head_of(g, grp, parity):
        return g * N_REP + 2 * grp + parity

    def alibi_bias(g, parity):
        slope = jnp.where(upper_grp, ALIBI_SLOPES[head_of(g, 1, parity)], ALIBI_SLOPES[head_of(g, 0, parity)])
        return jnp.where(band, -(slope * dist_f), -jnp.inf)

    biases = {(g, parity): alibi_bias(g, parity) for g in range(N_KV) for parity in range(2)}

    def attention(b0, sb, g):
        r0 = b0 * tt + sb * qs
        p0 = tile_pos + sb * qs
        in_range = (k_c >= WINDOW - p0) if pos0 + sb * qs < WINDOW else None
        if gs == 1:
            win = (b0, slice(sb * qs, sb * qs + kw))
            k_nat, k_swp, v_nat, v_swp = kf_ref[win], ks_ref[win], vf_ref[win], vs_ref[win]
        else:
            k_nat, k_swp, v_nat, v_swp = (r[b0:b0 + gs].reshape(n_keys, KV_W)
                                          for r in (kf_ref, ks_ref, vf_ref, vs_ref))
        k_lo, k_hi = (k_nat, k_swp) if g == 0 else (k_swp, k_nat)
        v_lo, v_hi = (v_nat, v_swp) if g == 0 else (v_swp, v_nat)
        v_ext = jnp.concatenate([jnp.where(low_lanes, v_lo, v_hi).astype(BF16), ones_cols], axis=1)
        qg = jnp.concatenate([z_ref[r0:r0 + n_q, Q_COL + 2 * LANES * g + LANES * grp:
                                    Q_COL + 2 * LANES * g + LANES * (grp + 1)] for grp in range(2)], axis=0)
        qg = (qg * (HEAD_DIM ** -0.5)).astype(BF16)
        for parity in range(2):
            k_ext = (jnp.where(low_lanes, k_lo, 0.0) if parity == 0 else jnp.where(low_lanes, 0.0, k_hi)).astype(BF16)
            sink = jnp.where(upper_grp_col, sinks_ref[head_of(g, 1, parity)], sinks_ref[head_of(g, 0, parity)])
            sc = lax.dot_general(qg, k_ext, (((1,), (1,)), ((), ())), preferred_element_type=F32)
            bias = biases[g, parity]
            sc = sc + (bias if in_range is None else jnp.where(in_range, bias, -jnp.inf))
            m = jnp.maximum(jnp.max(sc, axis=-1, keepdims=True), sink)
            p = jnp.exp(sc - m)
            o_ext = jnp.dot(p.astype(BF16), v_ext, preferred_element_type=F32)
            o = o_ext[:, 0:LANES] / (o_ext[:, LANES:2 * LANES] + jnp.exp(sink - m))
            for grp in range(2):
                h = head_of(g, grp, parity)
                od_ref[r0:r0 + n_q, HEAD_DIM * h:HEAD_DIM * (h + 1)] = (
                    o[grp * n_q:(grp + 1) * n_q, HEAD_DIM * parity:HEAD_DIM * (parity + 1)])

    def window_inputs(b, c0, r0, rc):
        for g in range(0 if time_major_pool else N_LG):
            fc_ref[b * N_LG + g, HIST_C + c0:HIST_C + c0 + rc, :] = z_ref[r0:r0 + rc, LANES * g:LANES * (g + 1)]
        k = z_ref[r0:r0 + rc, K_COL:K_COL + KV_W]
        v = z_ref[r0:r0 + rc, V_COL:V_COL + KV_W]
        kf_ref[b, WINDOW + c0:WINDOW + c0 + rc, :] = k
        vf_ref[b, WINDOW + c0:WINDOW + c0 + rc, :] = v
        ks_ref[b, WINDOW + c0:WINDOW + c0 + rc, :] = pltpu.roll(k, HEAD_DIM, axis=1)
        vs_ref[b, WINDOW + c0:WINDOW + c0 + rc, :] = pltpu.roll(v, HEAD_DIM, axis=1)

    def pooling(b, c0, r0, rc):
        pos = tile_pos + c0 + lax.broadcasted_iota(jnp.int32, (rc, LANES), 0)
        for g, w in enumerate(POOL_WINDOWS):
            u = fc_ref[b * N_LG + g, HIST_C + c0:HIST_C + c0 + rc, :]
            acc = u
            for d in range(1, w):
                acc = acc + fc_ref[b * N_LG + g, HIST_C + c0 - d:HIST_C + c0 - d + rc, :]
            cnt = jnp.minimum(w, pos + 1).astype(F32)
            pc_ref[g, r0:r0 + rc, :] = acc / cnt - u

    def pooling_time_major(lo, n):
        b_lo, n_seq = lo // tt, n // tt

        for g in range(N_LG):
            fc_ref[g, lo:lo + n, :] = z_ref[lo:lo + n, LANES * g:LANES * (g + 1)]

        def slab(j_in, g):
            if j_in < POOL_PAST:
                return sc_ref[j_in, b_lo:b_lo + n_seq, LANES * g:LANES * (g + 1)]
            return fc_ref[g, pl.ds(lo + j_in - POOL_PAST, n_seq, stride=tt), :]

        for t in range(tt):
            for g, w in enumerate(POOL_WINDOWS):
                u = slab(POOL_PAST + t, g)
                acc = u
                for d in range(1, w):
                    acc = acc + slab(POOL_PAST + t - d, g)
                pc_ref[g, pl.ds(lo + t, n_seq, stride=tt), :] = acc / float(min(w, pos0 + t + 1)) - u
        for j_out in range(POOL_PAST):
            for g in range(N_LG):
                nc_ref[j_out, b_lo:b_lo + n_seq, LANES * g:LANES * (g + 1)] = slab(j_out + tt, g)

    def gating(r0, rc):
        for g in range(N_LG):
            lg = slice(LANES * g, LANES * (g + 1))
            mixed = jnp.dot(pc_ref[g, r0:r0 + rc, :].astype(BF16), pw_ref[g], preferred_element_type=F32)
            yc = mixed * ps_ref[:, lg] * _silu(z_ref[r0:r0 + rc, MIX_W + LANES * g:MIX_W + LANES * (g + 1)])
            mix_ref[r0:r0 + rc, lg] = yc.astype(BF16)
        yd = od_ref[r0:r0 + rc, 0:MIX_W] * _silu(z_ref[r0:r0 + rc, DG_COL:DG_COL + MIX_W])
        mix_ref[r0:r0 + rc, MIX_W:2 * MIX_W] = yd.astype(BF16)

    def mixer_tasks(lo, n):
        chunks = _seq_chunks(lo, n, tt)
        if gs == 1:
            blocks = [(r0 // tt, (r0 % tt) // qs) for r0 in range(lo, lo + n, qs)]
        else:
            blocks = [(b0, 0) for b0 in range(lo // tt, (lo + n) // tt, gs)]
        pool_tasks = ([functools.partial(pooling_time_major, lo, n)] if time_major_pool
                      else [functools.partial(pooling, *ch) for ch in chunks])
        return ([functools.partial(window_inputs, *ch) for ch in chunks]
                + pool_tasks
                + [functools.partial(attention, b0, sb, g) for b0, sb in blocks for g in range(N_KV)]
                + [functools.partial(gating, lo, n)])

    _pre_norm(x_ref, mod_ref, seq0, gpre_ref, h_ref, rows, tt)
    for g, ((s0, n),) in enumerate(weight_groups):
        wait_weights(g)
        for c0 in range(s0, s0 + n, MXU_COLS):
            z_ref[:, c0:c0 + MXU_COLS] = jnp.dot(h_ref[...], win_ref[:, c0:c0 + MXU_COLS].astype(BF16),
                                                 preferred_element_type=F32)
    wait_weights(len(weight_groups))
    wob_ref[...] = wout_ref[...].astype(BF16)

    for task in mixer_tasks(0, rows):
        task()
    _out_proj_task(mix_ref, wob_ref, o_ref, 0, rows)()
    for task in _residual_tasks(x_ref, mod_ref, seq0, gpost_ref, o_ref, y_ref, 0, rows, tt):
        task()

    for b in range(nb):
        for g in range(0 if time_major_pool else N_LG):
            nc_ref[b, :, LANES * g:LANES * (g + 1)] = fc_ref[b * N_LG + g, HIST_C + tt - POOL_PAST:HIST_C + tt, :]
            if nt > 1:
                fc_ref[b * N_LG + g, 0:HIST_C, :] = fc_ref[b * N_LG + g, tt:tt + HIST_C, :]
        nk_ref[b] = kf_ref[b, tt:tt + WINDOW, :].T
        nv_ref[b] = vf_ref[b, tt:tt + WINDOW, :].T
        if nt > 1:
            for r in (kf_ref, ks_ref, vf_ref, vs_ref):
                r[b, 0:WINDOW, :] = r[b, tt:tt + WINDOW, :]


def _odd_layer(x2d, mod, st_c, st_k, st_v, g_pre, w_in, pool_w, pool_scale, sinks, w_out, g_post,
               *, n, t, nb, tt, pos0, time_major_pool):
    rows = nb * tt
    nt = t // tt
    kern = functools.partial(_odd_kernel, nb=nb, tt=tt, nt=nt, pos0=pos0, time_major_pool=time_major_pool)
    row_spec = pl.BlockSpec((rows, D_MODEL), lambda i, j: (i * nt + j, 0))

    def seq_spec(shape):
        return pl.BlockSpec((nb,) + shape, lambda i, j: (i, 0, 0))

    def const_spec(shape):
        return pl.BlockSpec(shape, lambda i, j: (0,) * len(shape))

    if time_major_pool:
        pool_spec = pl.BlockSpec((POOL_PAST, nb, MIX_W), lambda i, j: (0, i, 0))
    else:
        pool_spec = seq_spec((POOL_PAST, MIX_W))

    return pl.pallas_call(
        kern,
        out_shape=(
            jax.ShapeDtypeStruct((n * t, D_MODEL), F32),
            jax.ShapeDtypeStruct(st_c.shape, F32),
            jax.ShapeDtypeStruct((n, KV_W, WINDOW), F32),
            jax.ShapeDtypeStruct((n, KV_W, WINDOW), F32),
        ),
        grid=(n // nb, nt),
        in_specs=[
            pl.BlockSpec(memory_space=pltpu.SMEM),
            row_spec,
            const_spec(mod.shape),
            pool_spec,
            seq_spec((KV_W, WINDOW)),
            seq_spec((KV_W, WINDOW)),
            const_spec((1, D_MODEL)),
            pl.BlockSpec(memory_space=pl.ANY),
            const_spec((N_LG, LANES, LANES)),
            const_spec((1, MIX_W)),
            pl.BlockSpec(memory_space=pl.ANY),
            const_spec((1, D_MODEL)),
        ],
        out_specs=(row_spec, pool_spec, seq_spec((KV_W, WINDOW)), seq_spec((KV_W, WINDOW))),
        scratch_shapes=[
            pltpu.VMEM((rows, D_MODEL), BF16),
            pltpu.VMEM((rows, ODD_IN), F32),
            pltpu.VMEM((rows, D_MODEL), F32),
            pltpu.VMEM((N_LG, rows, LANES) if time_major_pool else (nb * N_LG, HIST_C + tt, LANES), F32),
            pltpu.VMEM((nb, WINDOW + tt, KV_W), F32),
            pltpu.VMEM((nb, WINDOW + tt, KV_W), F32),
            pltpu.VMEM((nb, WINDOW + tt, KV_W), F32),
            pltpu.VMEM((nb, WINDOW + tt, KV_W), F32),
            pltpu.VMEM((N_LG, rows, LANES), F32),
            pltpu.VMEM((rows, MIX_W), F32),
            pltpu.VMEM((rows, 2 * MIX_W), BF16),
            pltpu.VMEM((2 * MIX_W, D_MODEL), BF16),
            pltpu.VMEM((D_MODEL, ODD_IN), F32),
            pltpu.VMEM((2 * MIX_W, D_MODEL), F32),
            pltpu.SemaphoreType.DMA((4,)),
        ],
        compiler_params=pltpu.CompilerParams(
            dimension_semantics=("arbitrary", "arbitrary"), vmem_limit_bytes=VMEM_LIMIT_BYTES),
        name="odd_layer",
    )(sinks, x2d, mod, st_c, st_k, st_v, g_pre, w_in, pool_w, pool_scale, w_out, g_post)


def _tiling(n, t):
    if t >= TILE_ROWS:
        return 1, TILE_ROWS
    return min(n, SHORT_TILE_ROWS // t), t


def _trunk(x, mod_e, mod_o, st_a, st_b, st_c, st_k, st_v, pos0, we, wo):
    n, t, _ = x.shape
    nb, tt = _tiling(n, t)
    short = t == tt and nb > 1

    def swap(st):
        return jnp.transpose(st, (1, 0, 2)) if short else st

    x2d, na, nbs = _even_layer(x.reshape(n * t, D_MODEL), mod_e, swap(st_a), swap(st_b), *we, n=n, t=t, nb=nb,
                               tt=tt, time_major_state=short)
    def channel_major(cache):
        return jnp.transpose(cache, (0, 2, 3, 1)).reshape(n, KV_W, WINDOW)

    def position_major(cache):
        return jnp.transpose(cache.reshape(n, N_KV, HEAD_DIM, WINDOW), (0, 3, 1, 2))

    x2d, nc, nk, nv = _odd_layer(x2d, mod_o, swap(st_c), channel_major(st_k), channel_major(st_v), *wo,
                                 n=n, t=t, nb=nb, tt=tt, pos0=pos0, time_major_pool=short)
    na, nbs, nc = swap(na), swap(nbs), swap(nc)
    return (x2d.reshape(n, t, D_MODEL), na[None], nbs[None], nc[None],
            position_major(nk)[None], position_major(nv)[None])


def kernel(x_prompt, x_sample, state_conv_a, state_conv_b, state_pool_c, cache_win_k, cache_win_v, c_prompt, c_sample, w_mod_e, b_mod_e, g_pre_e, g_post_e, w_in_e, conv_a_w, conv_b_w, conv_b_b, ln_b_g, ln_b_b, w_out_e, w_mod_o, b_mod_o, g_pre_o, g_post_o, w_in_o, pool_w, pool_scale, sinks, w_out_o):
    n_p = x_prompt.shape[0]
    assert w_in_e.shape[0] == 1 and w_in_o.shape[0] == 1, "one even and one odd layer"

    mod_e_p, mod_e_s = _adaln_mod(c_prompt, c_sample, w_mod_e[0], b_mod_e[0])
    mod_o_p, mod_o_s = _adaln_mod(c_prompt, c_sample, w_mod_o[0], b_mod_o[0])

    we = (g_pre_e, w_in_e[0], conv_a_w[0], conv_b_w[0], conv_b_b, ln_b_g, ln_b_b,
          w_out_e[0], g_post_e)
    wo = (g_pre_o, w_in_o[0], pool_w[0].astype(BF16), pool_scale, sinks[0],
          w_out_o[0], g_post_o)

    dt = x_prompt.dtype
    z_a = jnp.zeros((n_p, CONV_A - 1, MIX_W), dt)
    z_b = jnp.zeros((n_p, CONV_B - 1, MIX_W), dt)
    z_c = jnp.zeros((n_p, POOL_PAST, MIX_W), dt)
    z_kv = jnp.zeros((n_p, WINDOW, N_KV, HEAD_DIM), dt)
    y_p, pa, pb, pc, pk, pv = _trunk(x_prompt, mod_e_p, mod_o_p, z_a, z_b, z_c, z_kv, z_kv, 0, we, wo)
    y_s, sa, sb, sc, sk, sv = _trunk(x_sample, mod_e_s, mod_o_s, state_conv_a[0], state_conv_b[0],
                                     state_pool_c[0], cache_win_k[0], cache_win_v[0], PAST_LEN, we, wo)
    return (y_p, y_s, pa, sa, pb, sb, pc, sc, pk, sk, pv, sv)
```

```python
import functools

import jax
import jax.numpy as jnp
from jax import lax
from jax.experimental import pallas as pl
from jax.experimental.pallas import tpu as pltpu

F32 = jnp.float32
BF16 = jnp.bfloat16

D_MODEL = 1024
MIX_W = 512
LANES = 128
MXU_COLS = 256
N_LG = MIX_W // LANES
HEAD_DIM = 64
N_HEADS = 8
N_KV = 2
N_REP = N_HEADS // N_KV
KV_W = N_KV * HEAD_DIM
CONV_A = 3
CONV_B = 31
POOL_WINDOWS = (2, 4, 8, 16)
POOL_PAST = 15
WINDOW = 128
PAST_LEN = 8192
EVEN_IN = 7 * MIX_W
ODD_IN = 4 * MIX_W + 2 * KV_W
Q_COL, K_COL, V_COL, DG_COL = 2 * MIX_W, 3 * MIX_W, 3 * MIX_W + KV_W, 3 * MIX_W + 2 * KV_W
RMS_EPS = 1e-6
LN_EPS = 1e-5
HIST_A = 8
HIST_B = 32
HIST_C = 16
ATT_MIN_Q_ROWS = 32
TILE_ROWS = 1024
SHORT_TILE_ROWS = 256
ROW_CHUNK = 64
WIN_CHUNK = 128
MOD_K_BLOCK = 256
VMEM_LIMIT_BYTES = 62 * 1024 * 1024
NEG_LOG2_E = -1.4426950408889634
ALIBI_SLOPES = tuple(float(2.0 ** (-8.0 * (h + 1) / N_HEADS)) for h in range(N_HEADS))


def _sigmoid(x):
    return 1.0 / (1.0 + jnp.exp2(x * NEG_LOG2_E))


def _silu(x):
    return x * _sigmoid(x)


def _mod_kernel(cp_ref, cs_ref, w_ref, b_ref, op_ref, os_ref):
    k = pl.program_id(0)
    n_s = cs_ref.shape[0]
    c = jnp.concatenate([cs_ref[...], cp_ref[...]], axis=0)
    part = jnp.dot(_silu(c).astype(BF16), w_ref[...].astype(BF16), preferred_element_type=F32)
    for o_ref, rows in ((os_ref, slice(0, n_s)), (op_ref, slice(n_s, c.shape[0]))):
        @pl.when(k == 0)
        def _first():
            o_ref[...] = part[rows] + b_ref[...]

        @pl.when(k > 0)
        def _rest():
            o_ref[...] += part[rows]


def _adaln_mod(c_prompt, c_sample, w_mod, b_mod):
    def c_spec(c):
        return pl.BlockSpec((c.shape[0], MOD_K_BLOCK), lambda k: (0, k))

    def out_spec(c):
        return pl.BlockSpec((c.shape[0], 3 * D_MODEL), lambda k: (0, 0))

    return pl.pallas_call(
        _mod_kernel,
        out_shape=tuple(jax.ShapeDtypeStruct((c.shape[0], 3 * D_MODEL), F32) for c in (c_prompt, c_sample)),
        grid=(D_MODEL // MOD_K_BLOCK,),
        in_specs=[
            c_spec(c_prompt),
            c_spec(c_sample),
            pl.BlockSpec((MOD_K_BLOCK, 3 * D_MODEL), lambda k: (k, 0)),
            pl.BlockSpec((1, 3 * D_MODEL), lambda k: (0, 0)),
        ],
        out_specs=(out_spec(c_prompt), out_spec(c_sample)),
        compiler_params=pltpu.CompilerParams(dimension_semantics=("arbitrary",)),
        name="adaln_mod",
    )(c_prompt, c_sample, w_mod, b_mod.reshape(1, -1))


def _mod_rows(mod_ref, seq0, col, r0, rc, tt):
    if tt >= rc:
        return mod_ref[pl.ds(seq0 + r0 // tt, 1), col:col + D_MODEL]
    parts = [jnp.broadcast_to(mod_ref[pl.ds(seq0 + b, 1), col:col + D_MODEL], (tt, D_MODEL))
             for b in range(r0 // tt, (r0 + rc) // tt)]
    return jnp.concatenate(parts, axis=0)


def _pre_norm(x_ref, mod_ref, seq0, g_ref, h_ref, rows, tt):
    rc = min(rows, ROW_CHUNK)
    for r0 in range(0, rows, rc):
        x = x_ref[r0:r0 + rc, :]
        ms = jnp.mean(x * x, axis=-1, keepdims=True)
        gain = g_ref[...] * (1.0 + _mod_rows(mod_ref, seq0, D_MODEL, r0, rc, tt))
        h = x * lax.rsqrt(ms + RMS_EPS) * gain + _mod_rows(mod_ref, seq0, 0, r0, rc, tt)
        h_ref[r0:r0 + rc, :] = h.astype(BF16)


def _out_proj_task(mix_ref, wout_ref, o_ref, lo, n):
    def task():
        o_ref[lo:lo + n, :] = jnp.dot(mix_ref[lo:lo + n, :], wout_ref[...], preferred_element_type=F32)
    return task


def _residual_tasks(x_ref, mod_ref, seq0, g_ref, o_ref, y_ref, lo, n, tt):
    rc = min(n, ROW_CHUNK)

    def chunk(r0):
        o = o_ref[r0:r0 + rc, :]
        ms = jnp.mean(o * o, axis=-1, keepdims=True)
        gain = g_ref[...] * _mod_rows(mod_ref, seq0, 2 * D_MODEL, r0, rc, tt)
        y_ref[r0:r0 + rc, :] = x_ref[r0:r0 + rc, :] + o * lax.rsqrt(ms + RMS_EPS) * gain
    return [functools.partial(chunk, r0) for r0 in range(lo, lo + n, rc)]


def _stage_weights(win_hbm, wout_hbm, win_ref, wout_ref, sem, groups):
    spans = [span for group in groups for span in group]
    copies = [pltpu.make_async_copy(win_hbm.at[:, c0:c0 + n], win_ref.at[:, c0:c0 + n], sem.at[k])
              for k, (c0, n) in enumerate(spans)]
    copies.append(pltpu.make_async_copy(wout_hbm, wout_ref, sem.at[len(spans)]))
    starts = [sum(len(group) for group in groups[:g]) for g in range(len(groups) + 1)] + [len(copies)]
    for copy in copies:
        copy.start()

    def wait(g):
        for copy in copies[starts[g]:starts[g + 1]]:
            copy.wait()
    return wait


def _seq_chunks(lo, n, tt):
    rc = min(tt, n, WIN_CHUNK)
    return [(r0 // tt, r0 % tt, r0, rc) for r0 in range(lo, lo + n, rc)]


def _even_gating(z_ref, ca_ref, cb_ref, cbb_ref, lng_ref, lnb_ref, mix_ref, rows):
    rc = min(rows, ROW_CHUNK)
    for r0 in range(0, rows, rc):
        def zs(c):
            return z_ref[r0:r0 + rc, MIX_W * c:MIX_W * (c + 1)]

        def all_lanes(c_ref):
            return jnp.concatenate([c_ref[g, r0:r0 + rc, :] for g in range(N_LG)], axis=1)
        ya = zs(1) * all_lanes(ca_ref) * _silu(zs(3))
        mix_ref[r0:r0 + rc, 0:MIX_W] = ya.astype(BF16)
        yb = all_lanes(cb_ref) + cbb_ref[...]
        mu = jnp.mean(yb, axis=-1, keepdims=True)
        yc = yb - mu
        var = jnp.mean(yc * yc, axis=-1, keepdims=True)
        ln = yc * lax.rsqrt(var + LN_EPS) * lng_ref[...] + lnb_ref[...]
        mix_ref[r0:r0 + rc, MIX_W:2 * MIX_W] = (_silu(ln) * _silu(zs(6))).astype(BF16)


def _even_kernel(x_ref, mod_ref, sa_ref, sb_ref, gpre_ref, win_hbm, cwa_ref, cwb_ref, cbb_ref,
                 lng_ref, lnb_ref, wout_hbm, gpost_ref,
                 y_ref, na_ref, nb_ref,
                 h_ref, z_ref, o_ref, fa_ref, fb_ref, ca_ref, cb_ref, mix_ref, wob_ref, win_ref, wout_ref, sem,
                 *, nb, tt, nt, time_major_state):
    rows = nb * tt
    j = pl.program_id(1)
    seq0 = pl.program_id(0) * nb
    assert not time_major_state or nt == 1

    split_order = (4, 5, 0, 2, 1, 3, 6)
    span = {s: (MIX_W * s, MIX_W) for s in split_order}
    weight_groups = [[span[4]], [span[5], span[0]], [span[2], span[1], span[3], span[6]]]
    wait_weights = _stage_weights(win_hbm, wout_hbm, win_ref, wout_ref, sem, weight_groups)

    @pl.when(j == 0)
    def _load_state():
        for b in range(0 if time_major_state else nb):
            for g in range(N_LG):
                lg = slice(LANES * g, LANES * (g + 1))
                fa_ref[b * N_LG + g, 0:HIST_A, :] = jnp.zeros((HIST_A, LANES), F32)
                fa_ref[b * N_LG + g, HIST_A - (CONV_A - 1):HIST_A, :] = sa_ref[b, :, lg]
                fb_ref[b * N_LG + g, 0:8, :] = jnp.zeros((8, LANES), F32)
                fb_ref[b * N_LG + g, HIST_B - (CONV_B - 1):HIST_B, :] = sb_ref[b, :, lg]

    _pre_norm(x_ref, mod_ref, seq0, gpre_ref, h_ref, rows, tt)

    for g, group in enumerate(weight_groups):
        wait_weights(g)
        for c0 in [s0 + d for s0, _ in group for d in range(0, MIX_W, MXU_COLS)]:
            z_ref[:, c0:c0 + MXU_COLS] = jnp.dot(h_ref[...], win_ref[:, c0:c0 + MXU_COLS].astype(BF16),
                                                 preferred_element_type=F32)
    wait_weights(len(weight_groups))
    wob_ref[...] = wout_ref[...].astype(BF16)

    if time_major_state:
        rc = min(rows, WIN_CHUNK)
        for r0 in range(0, rows, rc):
            for g in range(N_LG):
                def zc(c):
                    return z_ref[r0:r0 + rc, MIX_W * c + LANES * g:MIX_W * c + LANES * (g + 1)]
                fa_ref[g, r0:r0 + rc, :] = zc(2) * zc(0)
                fb_ref[g, r0:r0 + rc, :] = zc(4) * _sigmoid(zc(5))

        def conv(w_ref, state_ref, u_ref, out_ref, new_state_ref, taps):
            hist = taps - 1

            def slab(j_in, g):
                if j_in < hist:
                    return state_ref[j_in, :, LANES * g:LANES * (g + 1)]
                return u_ref[g, pl.ds(j_in - hist, nb, stride=tt), :]

            for t in range(tt):
                for g in range(N_LG):
                    acc = None
                    for k in range(taps):
                        term = w_ref[k:k + 1, LANES * g:LANES * (g + 1)] * slab(t + k, g)
                        acc = term if acc is None else acc + term
                    out_ref[g, pl.ds(t, nb, stride=tt), :] = acc
            for j_out in range(hist):
                for g in range(N_LG):
                    new_state_ref[j_out, :, LANES * g:LANES * (g + 1)] = slab(j_out + tt, g)

        conv(cwa_ref, sa_ref, fa_ref, ca_ref, na_ref, CONV_A)
        conv(cwb_ref, sb_ref, fb_ref, cb_ref, nb_ref, CONV_B)

    for b, c0, r0, rc in ([] if time_major_state else _seq_chunks(0, rows, tt)):
        for g in range(N_LG):
            def zc(c):
                return z_ref[r0:r0 + rc, MIX_W * c + LANES * g:MIX_W * c + LANES * (g + 1)]
            fa_ref[b * N_LG + g, HIST_A + c0:HIST_A + c0 + rc, :] = zc(2) * zc(0)
            fb_ref[b * N_LG + g, HIST_B + c0:HIST_B + c0 + rc, :] = zc(4) * _sigmoid(zc(5))

    for b, c0, r0, rc in ([] if time_major_state else _seq_chunks(0, rows, tt)):
        for g in range(N_LG):
            lg = slice(LANES * g, LANES * (g + 1))
            acc = None
            for k in range(CONV_A):
                off = HIST_A - (CONV_A - 1) + k + c0
                t = cwa_ref[k:k + 1, lg] * fa_ref[b * N_LG + g, off:off + rc, :]
                acc = t if acc is None else acc + t
            ca_ref[g, r0:r0 + rc, :] = acc
            acc = None
            for k in range(CONV_B):
                off = HIST_B - (CONV_B - 1) + k + c0
                t = cwb_ref[k:k + 1, lg] * fb_ref[b * N_LG + g, off:off + rc, :]
                acc = t if acc is None else acc + t
            cb_ref[g, r0:r0 + rc, :] = acc

    for b in range(0 if time_major_state else nb):
        for g in range(N_LG):
            lg = slice(LANES * g, LANES * (g + 1))
            na_ref[b, :, lg] = fa_ref[b * N_LG + g, HIST_A + tt - (CONV_A - 1):HIST_A + tt, :]
            nb_ref[b, :, lg] = fb_ref[b * N_LG + g, HIST_B + tt - (CONV_B - 1):HIST_B + tt, :]
            if nt > 1:
                fa_ref[b * N_LG + g, 0:HIST_A, :] = fa_ref[b * N_LG + g, tt:tt + HIST_A, :]
                fb_ref[b * N_LG + g, 0:HIST_B, :] = fb_ref[b * N_LG + g, tt:tt + HIST_B, :]

    _even_gating(z_ref, ca_ref, cb_ref, cbb_ref, lng_ref, lnb_ref, mix_ref, rows)
    _out_proj_task(mix_ref, wob_ref, o_ref, 0, rows)()
    for task in _residual_tasks(x_ref, mod_ref, seq0, gpost_ref, o_ref, y_ref, 0, rows, tt):
        task()


def _even_layer(x2d, mod, st_a, st_b, g_pre, w_in, cw_a, cw_b, cb_b, ln_g, ln_b, w_out, g_post,
                *, n, t, nb, tt, time_major_state):
    rows = nb * tt
    nt = t // tt
    kern = functools.partial(_even_kernel, nb=nb, tt=tt, nt=nt, time_major_state=time_major_state)
    row_spec = pl.BlockSpec((rows, D_MODEL), lambda i, j: (i * nt + j, 0))

    def seq_spec(shape):
        if time_major_state:
            return pl.BlockSpec((shape[0], nb, shape[1]), lambda i, j: (0, i, 0))
        return pl.BlockSpec((nb,) + shape, lambda i, j: (i, 0, 0))

    def window_scratch(hist):
        if time_major_state:
            return pltpu.VMEM((N_LG, rows, LANES), F32)
        return pltpu.VMEM((nb * N_LG, hist + tt, LANES), F32)

    def const_spec(shape):
        return pl.BlockSpec(shape, lambda i, j: (0,) * len(shape))

    return pl.pallas_call(
        kern,
        out_shape=(
            jax.ShapeDtypeStruct((n * t, D_MODEL), F32),
            jax.ShapeDtypeStruct(st_a.shape, F32),
            jax.ShapeDtypeStruct(st_b.shape, F32),
        ),
        grid=(n // nb, nt),
        in_specs=[
            row_spec,
            const_spec(mod.shape),
            seq_spec((CONV_A - 1, MIX_W)),
            seq_spec((CONV_B - 1, MIX_W)),
            const_spec((1, D_MODEL)),
            pl.BlockSpec(memory_space=pl.ANY),
            const_spec((CONV_A, MIX_W)),
            const_spec((CONV_B, MIX_W)),
            const_spec((1, MIX_W)),
            const_spec((1, MIX_W)),
            const_spec((1, MIX_W)),
            pl.BlockSpec(memory_space=pl.ANY),
            const_spec((1, D_MODEL)),
        ],
        out_specs=(row_spec, seq_spec((CONV_A - 1, MIX_W)), seq_spec((CONV_B - 1, MIX_W))),
        scratch_shapes=[
            pltpu.VMEM((rows, D_MODEL), BF16),
            pltpu.VMEM((rows, EVEN_IN), F32),
            pltpu.VMEM((rows, D_MODEL), F32),
            window_scratch(HIST_A),
            window_scratch(HIST_B),
            pltpu.VMEM((N_LG, rows, LANES), F32),
            pltpu.VMEM((N_LG, rows, LANES), F32),
            pltpu.VMEM((rows, 2 * MIX_W), BF16),
            pltpu.VMEM((2 * MIX_W, D_MODEL), BF16),
            pltpu.VMEM((D_MODEL, EVEN_IN), F32),
            pltpu.VMEM((2 * MIX_W, D_MODEL), F32),
            pltpu.SemaphoreType.DMA((8,)),
        ],
        compiler_params=pltpu.CompilerParams(
            dimension_semantics=("arbitrary", "arbitrary"), vmem_limit_bytes=VMEM_LIMIT_BYTES),
        name="even_layer",
    )(x2d, mod, st_a, st_b, g_pre, w_in, cw_a, cw_b, cb_b, ln_g, ln_b, w_out, g_post)


def _odd_kernel(sinks_ref, x_ref, mod_ref, sc_ref, sk_ref, sv_ref, gpre_ref, win_hbm, pw_ref, ps_ref,
                wout_hbm, gpost_ref,
                y_ref, nc_ref, nk_ref, nv_ref,
                h_ref, z_ref, o_ref, fc_ref, kf_ref, ks_ref, vf_ref, vs_ref, pc_ref, od_ref, mix_ref, wob_ref,
                win_ref, wout_ref, sem,
                *, nb, tt, nt, pos0, time_major_pool):
    rows = nb * tt
    j = pl.program_id(1)
    seq0 = pl.program_id(0) * nb
    tile_pos = pos0 + j * tt
    assert not time_major_pool or nt == 1
    weight_groups = [[(0, 2 * MXU_COLS)], [(2 * MXU_COLS, 4 * MXU_COLS)], [(6 * MXU_COLS, ODD_IN - 6 * MXU_COLS)]]
    wait_weights = _stage_weights(win_hbm, wout_hbm, win_ref, wout_ref, sem, weight_groups)

    @pl.when(j == 0)
    def _load_state():
        for b in range(nb):
            for g in range(0 if time_major_pool else N_LG):
                fc_ref[b * N_LG + g, 0:8, :] = jnp.zeros((8, LANES), F32)
                fc_ref[b * N_LG + g, HIST_C - POOL_PAST:HIST_C, :] = sc_ref[b, :, LANES * g:LANES * (g + 1)]
            k_past, v_past = sk_ref[b].T, sv_ref[b].T
            kf_ref[b, 0:WINDOW, :] = k_past
            vf_ref[b, 0:WINDOW, :] = v_past
            ks_ref[b, 0:WINDOW, :] = pltpu.roll(k_past, HEAD_DIM, axis=1)
            vs_ref[b, 0:WINDOW, :] = pltpu.roll(v_past, HEAD_DIM, axis=1)

    qs = min(tt, WINDOW)
    gs = max(1, ATT_MIN_Q_ROWS // qs)
    kw = WINDOW + qs
    n_q, n_keys = gs * qs, gs * kw
    assert N_REP == 4 and 2 * HEAD_DIM == LANES and KV_W == LANES
    assert qs & (qs - 1) == 0 and gs & (gs - 1) == 0 and nb % gs == 0 and (gs == 1 or tt == qs)
    shape = (2 * n_q, n_keys)
    ri = lax.broadcasted_iota(jnp.int32, shape, 0)
    ci = lax.broadcasted_iota(jnp.int32, shape, 1)
    q_seq = (ri >> (qs.bit_length() - 1)) & (gs - 1)
    k_seq = jnp.zeros(shape, jnp.int32)
    for s_i in range(1, gs):
        k_seq = k_seq + (ci >= s_i * kw).astype(jnp.int32)
    k_c = ci - k_seq * kw
    dist = (ri & (qs - 1)) + WINDOW - k_c
    band = (q_seq == k_seq) & (dist >= 0) & (dist < WINDOW)
    dist_f = dist.astype(F32)
    upper_grp = ri >= n_q
    upper_grp_col = lax.broadcasted_iota(jnp.int32, (2 * n_q, 1), 0) >= n_q
    low_lanes = lax.broadcasted_iota(jnp.int32, (n_keys, LANES), 1) < HEAD_DIM
    ones_cols = jnp.ones((n_keys, LANES), BF16)

    def head_of(g, grp, parity):
        return g * N_REP + 2 * grp + parity

    def alibi_bias(g, parity):
        slope = jnp.where(upper_grp, ALIBI_SLOPES[head_of(g, 1, parity)], ALIBI_SLOPES[head_of(g, 0, parity)])
        return jnp.where(band, -(slope * dist_f), -jnp.inf)

    biases = {(g, parity): alibi_bias(g, parity) for g in range(N_KV) for parity in range(2)}

    def attention(b0, sb, g):
        r0 = b0 * tt + sb * qs
        p0 = tile_pos + sb * qs
        in_range = (k_c >= WINDOW - p0) if pos0 + sb * qs < WINDOW else None
        if gs == 1:
            win = (b0, slice(sb * qs, sb * qs + kw))
            k_nat, k_swp, v_nat, v_swp = kf_ref[win], ks_ref[win], vf_ref[win], vs_ref[win]
        else:
            k_nat, k_swp, v_nat, v_swp = (r[b0:b0 + gs].reshape(n_keys, KV_W)
                                          for r in (kf_ref, ks_ref, vf_ref, vs_ref))
        k_lo, k_hi = (k_nat, k_swp) if g == 0 else (k_swp, k_nat)
        v_lo, v_hi = (v_nat, v_swp) if g == 0 else (v_swp, v_nat)
        v_ext = jnp.concatenate([jnp.where(low_lanes, v_lo, v_hi).astype(BF16), ones_cols], axis=1)
        qg = jnp.concatenate([z_ref[r0:r0 + n_q, Q_COL + 2 * LANES * g + LANES * grp:
                                    Q_COL + 2 * LANES * g + LANES * (grp + 1)] for grp in range(2)], axis=0)
        qg = (qg * (HEAD_DIM ** -0.5)).astype(BF16)
        for parity in range(2):
            k_ext = (jnp.where(low_lanes, k_lo, 0.0) if parity == 0 else jnp.where(low_lanes, 0.0, k_hi)).astype(BF16)
            sink = jnp.where(upper_grp_col, sinks_ref[head_of(g, 1, parity)], sinks_ref[head_of(g, 0, parity)])
            sc = lax.dot_general(qg, k_ext, (((1,), (1,)), ((), ())), preferred_element_type=F32)
            bias = biases[g, parity]
            sc = sc + (bias if in_range is None else jnp.where(in_range, bias, -jnp.inf))
            m = jnp.maximum(jnp.max(sc, axis=-1, keepdims=True), sink)
            p = jnp.exp(sc - m)
            o_ext = jnp.dot(p.astype(BF16), v_ext, preferred_element_type=F32)
            o = o_ext[:, 0:LANES] / (o_ext[:, LANES:2 * LANES] + jnp.exp(sink - m))
            for grp in range(2):
                h = head_of(g, grp, parity)
                od_ref[r0:r0 + n_q, HEAD_DIM * h:HEAD_DIM * (h + 1)] = (
                    o[grp * n_q:(grp + 1) * n_q, HEAD_DIM * parity:HEAD_DIM * (parity + 1)])

    def window_inputs(b, c0, r0, rc):
        for g in range(0 if time_major_pool else N_LG):
            fc_ref[b * N_LG + g, HIST_C + c0:HIST_C + c0 + rc, :] = z_ref[r0:r0 + rc, LANES * g:LANES * (g + 1)]
        k = z_ref[r0:r0 + rc, K_COL:K_COL + KV_W]
        v = z_ref[r0:r0 + rc, V_COL:V_COL + KV_W]
        kf_ref[b, WINDOW + c0:WINDOW + c0 + rc, :] = k
        vf_ref[b, WINDOW + c0:WINDOW + c0 + rc, :] = v
        ks_ref[b, WINDOW + c0:WINDOW + c0 + rc, :] = pltpu.roll(k, HEAD_DIM, axis=1)
        vs_ref[b, WINDOW + c0:WINDOW + c0 + rc, :] = pltpu.roll(v, HEAD_DIM, axis=1)

    def pooling(b, c0, r0, rc):
        pos = tile_pos + c0 + lax.broadcasted_iota(jnp.int32, (rc, LANES), 0)
        for g, w in enumerate(POOL_WINDOWS):
            u = fc_ref[b * N_LG + g, HIST_C + c0:HIST_C + c0 + rc, :]
            acc = u
            for d in range(1, w):
                acc = acc + fc_ref[b * N_LG + g, HIST_C + c0 - d:HIST_C + c0 - d + rc, :]
            cnt = jnp.minimum(w, pos + 1).astype(F32)
            pc_ref[g, r0:r0 + rc, :] = acc / cnt - u

    def pooling_time_major(lo, n):
        b_lo, n_seq = lo // tt, n // tt

        for g in range(N_LG):
            fc_ref[g, lo:lo + n, :] = z_ref[lo:lo + n, LANES * g:LANES * (g + 1)]

        def slab(j_in, g):
            if j_in < POOL_PAST:
                return sc_ref[j_in, b_lo:b_lo + n_seq, LANES * g:LANES * (g + 1)]
            return fc_ref[g, pl.ds(lo + j_in - POOL_PAST, n_seq, stride=tt), :]

        for t in range(tt):
            for g, w in enumerate(POOL_WINDOWS):
                u = slab(POOL_PAST + t, g)
                acc = u
                for d in range(1, w):
                    acc = acc + slab(POOL_PAST + t - d, g)
                pc_ref[g, pl.ds(lo + t, n_seq, stride=tt), :] = acc / float(min(w, pos0 + t + 1)) - u
        for j_out in range(POOL_PAST):
            for g in range(N_LG):
                nc_ref[j_out, b_lo:b_lo + n_seq, LANES * g:LANES * (g + 1)] = slab(j_out + tt, g)

    def gating(r0, rc):
        for g in range(N_LG):
            lg = slice(LANES * g, LANES * (g + 1))
            mixed = jnp.dot(pc_ref[g, r0:r0 + rc, :].astype(BF16), pw_ref[g], preferred_element_type=F32)
            yc = mixed * ps_ref[:, lg] * _silu(z_ref[r0:r0 + rc, MIX_W + LANES * g:MIX_W + LANES * (g + 1)])
            mix_ref[r0:r0 + rc, lg] = yc.astype(BF16)
        yd = od_ref[r0:r0 + rc, 0:MIX_W] * _silu(z_ref[r0:r0 + rc, DG_COL:DG_COL + MIX_W])
        mix_ref[r0:r0 + rc, MIX_W:2 * MIX_W] = yd.astype(BF16)

    def mixer_tasks(lo, n):
        chunks = _seq_chunks(lo, n, tt)
        if gs == 1:
            blocks = [(r0 // tt, (r0 % tt) // qs) for r0 in range(lo, lo + n, qs)]
        else:
            blocks = [(b0, 0) for b0 in range(lo // tt, (lo + n) // tt, gs)]
        pool_tasks = ([functools.partial(pooling_time_major, lo, n)] if time_major_pool
                      else [functools.partial(pooling, *ch) for ch in chunks])
        return ([functools.partial(window_inputs, *ch) for ch in chunks]
                + pool_tasks
                + [functools.partial(attention, b0, sb, g) for b0, sb in blocks for g in range(N_KV)]
                + [functools.partial(gating, lo, n)])

    _pre_norm(x_ref, mod_ref, seq0, gpre_ref, h_ref, rows, tt)
    for g, ((s0, n),) in enumerate(weight_groups):
        wait_weights(g)
        for c0 in range(s0, s0 + n, MXU_COLS):
            z_ref[:, c0:c0 + MXU_COLS] = jnp.dot(h_ref[...], win_ref[:, c0:c0 + MXU_COLS].astype(BF16),
                                                 preferred_element_type=F32)
    wait_weights(len(weight_groups))
    wob_ref[...] = wout_ref[...].astype(BF16)

    for task in mixer_tasks(0, rows):
        task()
    _out_proj_task(mix_ref, wob_ref, o_ref, 0, rows)()
    for task in _residual_tasks(x_ref, mod_ref, seq0, gpost_ref, o_ref, y_ref, 0, rows, tt):
        task()

    for b in range(nb):
        for g in range(0 if time_major_pool else N_LG):
            nc_ref[b, :, LANES * g:LANES * (g + 1)] = fc_ref[b * N_LG + g, HIST_C + tt - POOL_PAST:HIST_C + tt, :]
            if nt > 1:
                fc_ref[b * N_LG + g, 0:HIST_C, :] = fc_ref[b * N_LG + g, tt:tt + HIST_C, :]
        nk_ref[b] = kf_ref[b, tt:tt + WINDOW, :].T
        nv_ref[b] = vf_ref[b, tt:tt + WINDOW, :].T
        if nt > 1:
            for r in (kf_ref, ks_ref, vf_ref, vs_ref):
                r[b, 0:WINDOW, :] = r[b, tt:tt + WINDOW, :]


def _odd_layer(x2d, mod, st_c, st_k, st_v, g_pre, w_in, pool_w, pool_scale, sinks, w_out, g_post,
               *, n, t, nb, tt, pos0, time_major_pool):
    rows = nb * tt
    nt = t // tt
    kern = functools.partial(_odd_kernel, nb=nb, tt=tt, nt=nt, pos0=pos0, time_major_pool=time_major_pool)
    row_spec = pl.BlockSpec((rows, D_MODEL), lambda i, j: (i * nt + j, 0))

    def seq_spec(shape):
        return pl.BlockSpec((nb,) + shape, lambda i, j: (i, 0, 0))

    def const_spec(shape):
        return pl.BlockSpec(shape, lambda i, j: (0,) * len(shape))

    if time_major_pool:
        pool_spec = pl.BlockSpec((POOL_PAST, nb, MIX_W), lambda i, j: (0, i, 0))
    else:
        pool_spec = seq_spec((POOL_PAST, MIX_W))

    return pl.pallas_call(
        kern,
        out_shape=(
            jax.ShapeDtypeStruct((n * t, D_MODEL), F32),
            jax.ShapeDtypeStruct(st_c.shape, F32),
            jax.ShapeDtypeStruct((n, KV_W, WINDOW), F32),
            jax.ShapeDtypeStruct((n, KV_W, WINDOW), F32),
        ),
        grid=(n // nb, nt),
        in_specs=[
            pl.BlockSpec(memory_space=pltpu.SMEM),
            row_spec,
            const_spec(mod.shape),
            pool_spec,
            seq_spec((KV_W, WINDOW)),
            seq_spec((KV_W, WINDOW)),
            const_spec((1, D_MODEL)),
            pl.BlockSpec(memory_space=pl.ANY),
            const_spec((N_LG, LANES, LANES)),
            const_spec((1, MIX_W)),
            pl.BlockSpec(memory_space=pl.ANY),
            const_spec((1, D_MODEL)),
        ],
        out_specs=(row_spec, pool_spec, seq_spec((KV_W, WINDOW)), seq_spec((KV_W, WINDOW))),
        scratch_shapes=[
            pltpu.VMEM((rows, D_MODEL), BF16),
            pltpu.VMEM((rows, ODD_IN), F32),
            pltpu.VMEM((rows, D_MODEL), F32),
            pltpu.VMEM((N_LG, rows, LANES) if time_major_pool else (nb * N_LG, HIST_C + tt, LANES), F32),
            pltpu.VMEM((nb, WINDOW + tt, KV_W), F32),
            pltpu.VMEM((nb, WINDOW + tt, KV_W), F32),
            pltpu.VMEM((nb, WINDOW + tt, KV_W), F32),
            pltpu.VMEM((nb, WINDOW + tt, KV_W), F32),
            pltpu.VMEM((N_LG, rows, LANES), F32),
            pltpu.VMEM((rows, MIX_W), F32),
            pltpu.VMEM((rows, 2 * MIX_W), BF16),
            pltpu.VMEM((2 * MIX_W, D_MODEL), BF16),
            pltpu.VMEM((D_MODEL, ODD_IN), F32),
            pltpu.VMEM((2 * MIX_W, D_MODEL), F32),
            pltpu.SemaphoreType.DMA((4,)),
        ],
        compiler_params=pltpu.CompilerParams(
            dimension_semantics=("arbitrary", "arbitrary"), vmem_limit_bytes=VMEM_LIMIT_BYTES),
        name="odd_layer",
    )(sinks, x2d, mod, st_c, st_k, st_v, g_pre, w_in, pool_w, pool_scale, w_out, g_post)


def _tiling(n, t):
    if t >= TILE_ROWS:
        return 1, TILE_ROWS
    return min(n, SHORT_TILE_ROWS // t), t


def _trunk(x, mod_e, mod_o, st_a, st_b, st_c, st_k, st_v, pos0, we, wo):
    n, t, _ = x.shape
    nb, tt = _tiling(n, t)
    short = t == tt and nb > 1

    def swap(st):
        return jnp.transpose(st, (1, 0, 2)) if short else st

    x2d, na, nbs = _even_layer(x.reshape(n * t, D_MODEL), mod_e, swap(st_a), swap(st_b), *we, n=n, t=t, nb=nb,
                               tt=tt, time_major_state=short)
    def channel_major(cache):
        return jnp.transpose(cache, (0, 2, 3, 1)).reshape(n, KV_W, WINDOW)

    def position_major(cache):
        return jnp.transpose(cache.reshape(n, N_KV, HEAD_DIM, WINDOW), (0, 3, 1, 2))

    x2d, nc, nk, nv = _odd_layer(x2d, mod_o, swap(st_c), channel_major(st_k), channel_major(st_v), *wo,
                                 n=n, t=t, nb=nb, tt=tt, pos0=pos0, time_major_pool=short)
    na, nbs, nc = swap(na), swap(nbs), swap(nc)
    return (x2d.reshape(n, t, D_MODEL), na[None], nbs[None], nc[None],
            position_major(nk)[None], position_major(nv)[None])


def kernel(x_prompt, x_sample, state_conv_a, state_conv_b, state_pool_c, cache_win_k, cache_win_v, c_prompt, c_sample, w_mod_e, b_mod_e, g_pre_e, g_post_e, w_in_e, conv_a_w, conv_b_w, conv_b_b, ln_b_g, ln_b_b, w_out_e, w_mod_o, b_mod_o, g_pre_o, g_post_o, w_in_o, pool_w, pool_scale, sinks, w_out_o):
    n_p = x_prompt.shape[0]
    assert w_in_e.shape[0] == 1 and w_in_o.shape[0] == 1, "one even and one odd layer"

    mod_e_p, mod_e_s = _adaln_mod(c_prompt, c_sample, w_mod_e[0], b_mod_e[0])
    mod_o_p, mod_o_s = _adaln_mod(c_prompt, c_sample, w_mod_o[0], b_mod_o[0])

    we = (g_pre_e, w_in_e[0], conv_a_w[0], conv_b_w[0], conv_b_b, ln_b_g, ln_b_b,
          w_out_e[0], g_post_e)
    wo = (g_pre_o, w_in_o[0], pool_w[0].astype(BF16), pool_scale, sinks[0],
          w_out_o[0], g_post_o)

    dt = x_prompt.dtype
    z_a = jnp.zeros((n_p, CONV_A - 1, MIX_W), dt)
    z_b = jnp.zeros((n_p, CONV_B - 1, MIX_W), dt)
    z_c = jnp.zeros((n_p, POOL_PAST, MIX_W), dt)
    z_kv = jnp.zeros((n_p, WINDOW, N_KV, HEAD_DIM), dt)
    y_p, pa, pb, pc, pk, pv = _trunk(x_prompt, mod_e_p, mod_o_p, z_a, z_b, z_c, z_kv, z_kv, 0, we, wo)
    y_s, sa, sb, sc, sk, sv = _trunk(x_sample, mod_e_s, mod_o_s, state_conv_a[0], state_conv_b[0],
                                     state_pool_c[0], cache_win_k[0], cache_win_v[0], PAST_LEN, we, wo)
    return (y_p, y_s, pa, sa, pb, sb, pc, sc, pk, sk, pv, sv)
```
